```python
import math
import jax
import jax.numpy as jnp
from jax import lax
import numpy as np

D_MODEL = 1024
BATCH = 8
SEQ = 2048
DEPTH = 1

HG_DK = 128
HG_DV = 128
HG_HEADS = (D_MODEL // 2) // HG_DK
HG_KW = HG_HEADS * HG_DK
HG_OUT = HG_HEADS * HG_DV
HG_CHUNK = 32
DA_DH = 64
DA_HEADS = (D_MODEL // 2) // (2 * DA_DH)
DA_WIDTH = DA_HEADS * 2 * DA_DH
Q_BLOCK = 128
MIX_WIDTH = HG_OUT + DA_WIDTH
N_EXPERTS = 32
TOP_K = 4
D_FF = D_MODEL
SWIGLU_ALPHA = 1.702
SWIGLU_LIMIT = 7.0
MOE_BLOCK = 128
NORM_EPS = 1e-6
N_MOD = 6

IN_SIZES = (HG_KW, HG_KW, HG_OUT, HG_OUT, DA_WIDTH, DA_WIDTH, DA_WIDTH, D_MODEL, D_MODEL)
IN_WIDTH = sum(IN_SIZES)
IN_SPLITS = tuple(sum(IN_SIZES[:i + 1]) for i in range(len(IN_SIZES) - 1))

kernel_name = "hybrid_hgrn2_diffattn_moe_block"


def _rmsnorm(x, g):
    xf = x.astype(jnp.float32)
    y = xf * lax.rsqrt(jnp.mean(xf * xf, axis=-1, keepdims=True) + NORM_EPS)
    return y * g.astype(jnp.float32)


def _modulate(xn, shift, scale):
    return xn * (1.0 + scale[:, None, :]) + shift[:, None, :]


def _hgrn2(q_raw, f_raw, i_raw, og_raw, lb, norm_g):
    B, S, _ = q_raw.shape
    N = S // HG_CHUNK

    def heads(t, d):
        t = t.astype(jnp.float32).reshape(B, N, HG_CHUNK, HG_HEADS, d)
        return t.transpose(0, 3, 1, 2, 4)

    q = jax.nn.silu(heads(q_raw, HG_DK))
    lbh = lb.astype(jnp.float32).reshape(1, HG_HEADS, 1, 1, HG_DK)
    f = lbh + (1.0 - lbh) * jax.nn.sigmoid(heads(f_raw, HG_DK))
    k = 1.0 - f
    v = heads(i_raw, HG_DV)
    bcum = jnp.cumsum(jnp.log(f), axis=3)
    q_t = q * jnp.exp(bcum)
    k_t = k * jnp.exp(-bcum)
    causal = jnp.tril(jnp.ones((HG_CHUNK, HG_CHUNK), dtype=bool))
    scores = jnp.where(causal, jnp.einsum('bhncd,bhnsd->bhncs', q_t, k_t), 0.0)
    o_intra = jnp.einsum('bhncs,bhnsv->bhncv', scores, v)
    b_last = bcum[:, :, :, -1:, :]
    kv = jnp.einsum('bhnsd,bhnsv->bhndv', k * jnp.exp(b_last - bcum), v)
    decay = jnp.exp(b_last[:, :, :, 0, :])

    def step(state, inp):
        dec, upd = inp
        return dec[..., None] * state + upd, state

    s0 = jnp.zeros((B, HG_HEADS, HG_DK, HG_DV), jnp.float32)
    _, s_starts = lax.scan(step, s0, (jnp.moveaxis(decay, 2, 0), jnp.moveaxis(kv, 2, 0)))
    s_starts = jnp.moveaxis(s_starts, 0, 2)
    o_inter = jnp.einsum('bhncd,bhndv->bhncv', q_t, s_starts)
    o = (o_intra + o_inter).transpose(0, 2, 3, 1, 4).reshape(B, S, HG_HEADS, HG_DV)
    og = og_raw.astype(jnp.float32).reshape(B, S, HG_HEADS, HG_DV)
    o = _rmsnorm(o, norm_g) * jax.nn.silu(og)
    return o.reshape(B, S, HG_OUT)


def _diff_attention(q_raw, k_raw, v_raw, qn_g, kn_g, lq1, lk1, lq2, lk2, subln_g, lambda_init):
    B, S, _ = q_raw.shape
    q = _rmsnorm(q_raw.reshape(B, S, DA_HEADS, 2, DA_DH), qn_g) * (DA_DH ** -0.5)
    k = _rmsnorm(k_raw.reshape(B, S, DA_HEADS, 2, DA_DH), kn_g)
    v = v_raw.astype(jnp.float32).reshape(B, S, DA_HEADS, 2 * DA_DH)
    lam = (jnp.exp(jnp.sum(lq1.astype(jnp.float32) * lk1.astype(jnp.float32)))
           - jnp.exp(jnp.sum(lq2.astype(jnp.float32) * lk2.astype(jnp.float32))) + lambda_init)
    outs = []
    for blk in range(S // Q_BLOCK):
        q0 = blk * Q_BLOCK
        kv_len = q0 + Q_BLOCK
        s = jnp.einsum('bqhcd,bkhcd->bhcqk', q[:, q0:kv_len], k[:, :kv_len])
        mask = (q0 + jnp.arange(Q_BLOCK))[:, None] >= jnp.arange(kv_len)[None, :]
        p = jax.nn.softmax(jnp.where(mask, s, -jnp.inf), axis=-1)
        a = p[:, :, 0] - lam * p[:, :, 1]
        outs.append(jnp.einsum('bhqk,bkhv->bqhv', a, v[:, :kv_len]))
    o = jnp.concatenate(outs, axis=1)
    o = _rmsnorm(o, subln_g) * (1.0 - lambda_init)
    return o.reshape(B, S, DA_WIDTH)


def _clamped_swiglu(u):
    x_glu, x_lin = u[..., ::2], u[..., 1::2]
    x_glu = jnp.minimum(x_glu, SWIGLU_LIMIT)
    x_lin = jnp.clip(x_lin, -SWIGLU_LIMIT, SWIGLU_LIMIT)
    return x_glu * jax.nn.sigmoid(SWIGLU_ALPHA * x_glu) * (x_lin + 1.0)


def _moe(h, w_router, b_router, w1, b1, w2, b2):
    B, S, D = h.shape
    T = B * S
    A = T * TOP_K
    hf = h.reshape(T, D)
    logits = (hf @ w_router + b_router).astype(jnp.float32)
    top_vals, top_idx = lax.top_k(logits, TOP_K)
    gates = jax.nn.softmax(top_vals, axis=-1)
    e_flat = top_idx.reshape(-1).astype(jnp.int32)
    w_flat = gates.reshape(-1)
    tok_flat = jnp.arange(A, dtype=jnp.int32) // TOP_K
    order = jnp.argsort(e_flat)
    e_sorted, tok_sorted, w_sorted = e_flat[order], tok_flat[order], w_flat[order]
    counts = jax.ops.segment_sum(jnp.ones_like(e_flat), e_flat, num_segments=N_EXPERTS)
    starts = jnp.cumsum(counts) - counts
    padded = (counts + MOE_BLOCK - 1) // MOE_BLOCK * MOE_BLOCK
    pad_ends = jnp.cumsum(padded)
    pad_starts = pad_ends - padded
    dest = pad_starts[e_sorted] + (jnp.arange(A, dtype=jnp.int32) - starts[e_sorted])
    n_blocks = -(-(A + N_EXPERTS * (MOE_BLOCK - 1)) // MOE_BLOCK)
    P = n_blocks * MOE_BLOCK
    row_tok = jnp.full((P,), T, jnp.int32).at[dest].set(tok_sorted)
    row_w = jnp.zeros((P,), jnp.float32).at[dest].set(w_sorted)
    block_expert = jnp.minimum(
        jnp.searchsorted(pad_ends, jnp.arange(n_blocks, dtype=jnp.int32) * MOE_BLOCK, side='right'),
        N_EXPERTS - 1).astype(jnp.int32)
    h_pad = jnp.concatenate([hf, jnp.zeros((1, D), hf.dtype)], axis=0)
    xb = h_pad[row_tok].reshape(n_blocks, MOE_BLOCK, D)

    def expert_block(args):
        xblk, e = args
        u = (xblk @ w1[e] + b1[e]).astype(jnp.float32)
        return _clamped_swiglu(u) @ w2[e] + b2[e]

    yb = lax.map(expert_block, (xb, block_expert))
    y = yb.reshape(P, D).astype(jnp.float32) * row_w[:, None]
    out = jax.ops.segment_sum(y, row_tok, num_segments=T + 1)[:T]
    return out.reshape(B, S, D)


def setup_inputs(seed: int = 0) -> dict:
    key = jax.random.key(seed)
    ks = jax.random.split(key, 24)
    f32 = jnp.float32

    def nrm(k, shape, scale):
        return jax.random.normal(k, shape, f32) * scale

    def gain(k, shape):
        return 1.0 + 0.05 * jax.random.normal(k, shape, f32)

    L = DEPTH
    return {
        "x": nrm(ks[0], (BATCH, SEQ, D_MODEL), 1.0),
        "c": nrm(ks[1], (BATCH, D_MODEL), 1.0),
        "w_ada": nrm(ks[2], (L, D_MODEL, N_MOD * D_MODEL), 0.5 * D_MODEL ** -0.5),
        "b_ada": nrm(ks[3], (L, N_MOD * D_MODEL), 0.02),
        "mix_norm_g": gain(ks[4], (L, D_MODEL)),
        "ffn_norm_g": gain(ks[5], (L, D_MODEL)),
        "w_in": nrm(ks[6], (L, D_MODEL, IN_WIDTH), D_MODEL ** -0.5),
        "hg_lower_bound_logits": nrm(ks[7], (L + 1, HG_KW), 0.1),
        "hg_out_norm_g": gain(ks[8], (L, HG_DV)),
        "da_q_norm_g": gain(ks[9], (L, DA_DH)),
        "da_k_norm_g": gain(ks[10], (L, DA_DH)),
        "da_lambda_q1": nrm(ks[11], (L, DA_DH), 0.1),
        "da_lambda_k1": nrm(ks[12], (L, DA_DH), 0.1),
        "da_lambda_q2": nrm(ks[13], (L, DA_DH), 0.1),
        "da_lambda_k2": nrm(ks[14], (L, DA_DH), 0.1),
        "da_subln_g": gain(ks[15], (L, 2 * DA_DH)),
        "w_out": nrm(ks[16], (L, MIX_WIDTH, D_MODEL), MIX_WIDTH ** -0.5),
        "w_router": nrm(ks[17], (L, D_MODEL, N_EXPERTS), D_MODEL ** -0.5),
        "b_router": nrm(ks[18], (L, N_EXPERTS), 0.01),
        "w1": nrm(ks[19], (L, N_EXPERTS, D_MODEL, 2 * D_FF), D_MODEL ** -0.5),
        "b1": nrm(ks[20], (L, N_EXPERTS, 2 * D_FF), 0.01),
        "w2": nrm(ks[21], (L, N_EXPERTS, D_FF, D_MODEL), D_FF ** -0.5),
        "b2": nrm(ks[22], (L, N_EXPERTS, D_MODEL), 0.01),
    }


def reference(x, c, w_ada, b_ada, mix_norm_g, ffn_norm_g, w_in, hg_lower_bound_logits, hg_out_norm_g,
              da_q_norm_g, da_k_norm_g, da_lambda_q1, da_lambda_k1, da_lambda_q2, da_lambda_k2, da_subln_g,
              w_out, w_router, b_router, w1, b1, w2, b2):
    out_dtype = x.dtype
    lb_table = jnp.cumsum(jax.nn.softmax(hg_lower_bound_logits.astype(jnp.float32), axis=0), axis=0)
    c_act = jax.nn.silu(c.astype(jnp.float32))
    for l in range(DEPTH):
        ada = c_act @ w_ada[l] + b_ada[l]
        sh1, sc1, g1, sh2, sc2, g2 = jnp.split(ada, N_MOD, axis=-1)
        h = _modulate(_rmsnorm(x, mix_norm_g[l]), sh1, sc1)
        proj = h @ w_in[l]
        q_a, f_a, i_a, og_a, q_d, k_d, v_d, gate_a, gate_d = jnp.split(proj, IN_SPLITS, axis=-1)
        o_a = _hgrn2(q_a, f_a, i_a, og_a, lb_table[l], hg_out_norm_g[l])
        lambda_init = 0.8 - 0.6 * math.exp(-0.3 * l)
        o_d = _diff_attention(q_d, k_d, v_d, da_q_norm_g[l], da_k_norm_g[l], da_lambda_q1[l], da_lambda_k1[l],
                              da_lambda_q2[l], da_lambda_k2[l], da_subln_g[l], lambda_init)
        wo = w_out[l]
        y = (jax.nn.sigmoid(gate_a.astype(jnp.float32)) * (o_a @ wo[:HG_OUT])
             + jax.nn.sigmoid(gate_d.astype(jnp.float32)) * (o_d @ wo[HG_OUT:]))
        x = (x + g1[:, None, :] * y).astype(out_dtype)
        h2 = _modulate(_rmsnorm(x, ffn_norm_g[l]), sh2, sc2)
        m = _moe(h2, w_router[l], b_router[l], w1[l], b1[l], w2[l], b2[l])
        x = (x + g2[:, None, :] * m).astype(out_dtype)
    return x
```

```python
import functools
import math

import jax
import jax.numpy as jnp
from jax import lax
from jax.experimental import pallas as pl
from jax.experimental.pallas import tpu as pltpu

F32 = jnp.float32
BF16 = jnp.bfloat16
I32 = jnp.int32

HG_HEADS = 4
HG_DK = 128
HG_DV = 128
HG_CHUNK = 32
DA_HEADS = 4
DA_DH = 64
N_EXPERTS = 32
TOP_K = 4
SWIGLU_ALPHA = 1.702
SWIGLU_LIMIT = 7.0
NORM_EPS = 1e-6
N_MOD = 6

LANES = 128
VMEM_LIMIT = 56 * 1024 * 1024

HG_ROWS = 256
ATTN_TILE = 256
FFN_BLOCK = 256


def _sigmoid(x):
    return 1.0 / (1.0 + jnp.exp(-x))


def _dot(a, b):
    return jnp.dot(a, b, preferred_element_type=F32)


def _dot_nt(a, b):
    return lax.dot_general(a, b, (((1,), (1,)), ((), ())), preferred_element_type=F32)


def _split_bf16(x):
    hi = x.astype(BF16)
    lo = (x - hi.astype(F32)).astype(BF16)
    return hi, lo


def _ada_body(c_ref, w_ref, b_ref, o_ref):
    c = c_ref[...].astype(F32)
    ca = c * _sigmoid(c)
    o_ref[...] = jnp.dot(ca, w_ref[...], preferred_element_type=F32,
                         precision=lax.Precision.HIGHEST) + b_ref[...]


def _ada(c, w_ada, b_ada):
    bsz, d = c.shape
    n = w_ada.shape[1]
    tn = d
    return pl.pallas_call(
        _ada_body,
        grid=(n // tn,),
        in_specs=[
            pl.BlockSpec((bsz, d), lambda j: (0, 0)),
            pl.BlockSpec((d, tn), lambda j: (0, j)),
            pl.BlockSpec((1, tn), lambda j: (0, j)),
        ],
        out_specs=pl.BlockSpec((bsz, tn), lambda j: (0, j)),
        out_shape=jax.ShapeDtypeStruct((bsz, n), F32),
        name="ada",
    )(c, w_ada, b_ada.reshape(1, n))


def _norm_mod(x, g, shift, scale):
    ms = jnp.mean(x * x, axis=-1, keepdims=True)
    return (x * lax.rsqrt(ms + NORM_EPS) * g) * (1.0 + scale) + shift


def _inproj_body(x_ref, g_ref, sh_ref, sc_ref, w_ref, o_ref):
    h = _norm_mod(x_ref[...], g_ref[...], sh_ref[0], sc_ref[0])
    o_ref[...] = _dot(h.astype(BF16), w_ref[...]).astype(BF16)


def _inproj(x2, g, ada3, w_bf16, seq):
    t, d = x2.shape
    n = w_bf16.shape[1]
    tm = min(512, seq)
    nj = 2
    tn = n // nj
    per_b = seq // tm
    return pl.pallas_call(
        _inproj_body,
        grid=(nj, t // tm),
        in_specs=[
            pl.BlockSpec((tm, d), lambda j, i: (i, 0)),
            pl.BlockSpec((1, d), lambda j, i: (0, 0)),
            pl.BlockSpec((1, 1, d), lambda j, i: (i // per_b, 0, 0)),
            pl.BlockSpec((1, 1, d), lambda j, i: (i // per_b, 0, 1)),
            pl.BlockSpec((d, tn), lambda j, i: (0, j)),
        ],
        out_specs=pl.BlockSpec((tm, tn), lambda j, i: (i, j)),
        out_shape=jax.ShapeDtypeStruct((t, n), BF16),
        compiler_params=pltpu.CompilerParams(
            dimension_semantics=("arbitrary", "arbitrary"), vmem_limit_bytes=VMEM_LIMIT),
        name="inproj",
    )(x2, g, ada3, ada3, w_bf16)


def _hgrn_body(q_ref, f_ref, i_ref, og_ref, lbl_ref, g_ref, o_ref, *, seq, layer):
    rows, chunk = HG_ROWS, HG_CHUNK
    nchunk = rows // chunk
    lbl = lbl_ref[...].astype(F32)
    e = jnp.exp(lbl - jnp.max(lbl, axis=0, keepdims=True))
    lb = jnp.sum(e[: layer + 1], axis=0, keepdims=True) / jnp.sum(e, axis=0, keepdims=True)
    r_i = lax.broadcasted_iota(I32, (rows, rows), 0)
    c_i = lax.broadcasted_iota(I32, (rows, rows), 1)
    tri = ((r_i // chunk) == (c_i // chunk)) & (r_i >= c_i)
    tri_b = jnp.where(tri, 1.0, 0.0).astype(BF16)
    lane_chunk = lax.broadcasted_iota(I32, (HG_DV, rows), 1) // chunk
    g = g_ref[...].astype(F32)

    def block(r, st):
        sl = pl.ds(pl.multiple_of(r * rows, rows), rows)
        qr = q_ref[sl, :].astype(F32)
        fr = f_ref[sl, :].astype(F32)
        v = i_ref[sl, :].astype(F32)
        og = og_ref[sl, :].astype(F32)
        q = qr * _sigmoid(qr)
        f = lb + (1.0 - lb) * _sigmoid(fr)
        k = 1.0 - f
        logf = jnp.log(f)
        lhi, llo = _split_bf16(logf)
        bcum = _dot(tri_b, lhi) + _dot(tri_b, llo)
        b3 = bcum.reshape(nchunk, chunk, HG_DK)
        bl = b3[:, chunk - 1:chunk, :]
        qt = (q * jnp.exp(bcum)).astype(BF16)
        kt = (k * jnp.exp(-bcum)).astype(BF16)
        kd = (k.reshape(nchunk, chunk, HG_DK) * jnp.exp(bl - b3)).reshape(rows, HG_DK).astype(BF16)
        dec = jnp.exp(bl)
        vb = v.astype(BF16)
        scores = _dot_nt(qt, kt)
        a = jnp.where(tri, scores, 0.0).astype(BF16)
        o = _dot(a, vb)
        vt = v.T
        inter = []
        for c in range(nchunk):
            qc = qt[c * chunk:(c + 1) * chunk, :]
            inter.append(_dot_nt(qc, st.astype(BF16)))
            vtc = jnp.where(lane_chunk == c, vt, 0.0).astype(BF16)
            kvt = _dot(vtc, kd)
            st = st * dec[c] + kvt
        o = o + jnp.concatenate(inter, axis=0)
        ms = jnp.mean(o * o, axis=-1, keepdims=True)
        o = o * lax.rsqrt(ms + NORM_EPS) * g
        o_ref[sl, :] = (o * (og * _sigmoid(og))).astype(o_ref.dtype)
        return st

    lax.fori_loop(0, seq // rows, block, jnp.zeros((HG_DV, HG_DK), F32))


def _hgrn(proj, lb_logits, norm_g, bsz, seq, col0, layer):
    t = proj.shape[0]
    blk = lambda off: pl.BlockSpec((seq, LANES), lambda b, h, off=off: (b, col0 + off + h))
    return pl.pallas_call(
        functools.partial(_hgrn_body, seq=seq, layer=layer),
        grid=(bsz, HG_HEADS),
        in_specs=[
            blk(0), blk(HG_HEADS), blk(2 * HG_HEADS), blk(3 * HG_HEADS),
            pl.BlockSpec((lb_logits.shape[0], HG_DK), lambda b, h: (0, h)),
            pl.BlockSpec((1, HG_DV), lambda b, h: (0, 0)),
        ],
        out_specs=pl.BlockSpec((seq, HG_DV), lambda b, h: (b, h)),
        out_shape=jax.ShapeDtypeStruct((t, HG_HEADS * HG_DV), BF16),
        compiler_params=pltpu.CompilerParams(
            dimension_semantics=("arbitrary", "arbitrary"), vmem_limit_bytes=VMEM_LIMIT),
        name="hgrn",
    )(proj, proj, proj, proj, lb_logits, norm_g)


def _group_norm(x, gsum_b, gain):
    hi, lo = _split_bf16(x * x)
    ss = _dot(hi, gsum_b) + _dot(lo, gsum_b)
    return x * lax.rsqrt(ss * (1.0 / DA_DH) + NORM_EPS) * gain


def _attn_body(q_ref, k_ref, v_ref, qg_ref, kg_ref, lam_ref, sg_ref, o_ref, kn_scr, *, seq, lambda_init):
    tq = ATTN_TILE
    qi = pl.program_id(2)
    width = 2 * DA_DH
    r_l = lax.broadcasted_iota(I32, (width, width), 0) // DA_DH
    c_l = lax.broadcasted_iota(I32, (width, width), 1) // DA_DH
    gsum_b = jnp.where(r_l == c_l, 1.0, 0.0).astype(BF16)
    lane = lax.broadcasted_iota(I32, (1, width), 1)

    @pl.when(qi == 0)
    def _():
        kg = kg_ref[...].astype(F32)

        def kblock(r, carry):
            sl = pl.ds(pl.multiple_of(r * tq, tq), tq)
            kn_scr[sl, :] = _group_norm(k_ref[sl, :].astype(F32), gsum_b, kg).astype(BF16)
            return carry

        lax.fori_loop(0, seq // tq, kblock, 0)

    qn = _group_norm(q_ref[...].astype(F32), gsum_b, qg_ref[...].astype(F32)) * (DA_DH ** -0.5)
    q1 = jnp.where(lane < DA_DH, qn, 0.0).astype(BF16)
    q2 = jnp.where(lane >= DA_DH, qn, 0.0).astype(BF16)

    def update(s, m, l, acc, vblk):
        m_new = jnp.maximum(m, jnp.max(s, axis=-1, keepdims=True))
        p = jnp.exp(s - m_new)
        alpha = jnp.exp(m - m_new)
        l = alpha * l + jnp.sum(p, axis=-1, keepdims=True)
        acc = alpha * acc + _dot(p.astype(BF16), vblk)
        return m_new, l, acc

    def step(kb, carry, masked):
        m1, l1, a1, m2, l2, a2 = carry
        sl = pl.ds(pl.multiple_of(kb * tq, tq), tq)
        kblk = kn_scr[sl, :]
        vblk = v_ref[sl, :]
        s1 = _dot_nt(q1, kblk)
        s2 = _dot_nt(q2, kblk)
        if masked:
            row = lax.broadcasted_iota(I32, (tq, tq), 0)
            col = lax.broadcasted_iota(I32, (tq, tq), 1)
            keep = row >= col
            s1 = jnp.where(keep, s1, -jnp.inf)
            s2 = jnp.where(keep, s2, -jnp.inf)
        m1, l1, a1 = update(s1, m1, l1, a1, vblk)
        m2, l2, a2 = update(s2, m2, l2, a2, vblk)
        return m1, l1, a1, m2, l2, a2

    neg = jnp.full((tq, 1), -jnp.inf, F32)
    zero = jnp.zeros((tq, 1), F32)
    zacc = jnp.zeros((tq, width), F32)
    carry = lax.fori_loop(0, qi, lambda kb, c: step(kb, c, False), (neg, zero, zacc, neg, zero, zacc))
    m1, l1, a1, m2, l2, a2 = step(qi, carry, True)

    lam_v = lam_ref[...].astype(F32)
    lam = (jnp.exp(jnp.sum(lam_v[0:1] * lam_v[1:2], axis=-1, keepdims=True))
           - jnp.exp(jnp.sum(lam_v[2:3] * lam_v[3:4], axis=-1, keepdims=True)) + lambda_init)
    o = a1 / l1 - lam * (a2 / l2)
    ms = jnp.mean(o * o, axis=-1, keepdims=True)
    o = o * lax.rsqrt(ms + NORM_EPS) * sg_ref[...].astype(F32) * (1.0 - lambda_init)
    o_ref[...] = o.astype(o_ref.dtype)


def _attn(proj, qg2, kg2, lam4, subln_g, bsz, seq, col0, lambda_init):
    t = proj.shape[0]
    tq = ATTN_TILE
    nq = seq // tq
    width = 2 * DA_DH
    return pl.pallas_call(
        functools.partial(_attn_body, seq=seq, lambda_init=lambda_init),
        grid=(bsz, DA_HEADS, nq),
        in_specs=[
            pl.BlockSpec((tq, width), lambda b, h, i: (b * nq + i, col0 + h)),
            pl.BlockSpec((seq, width), lambda b, h, i: (b, col0 + DA_HEADS + h)),
            pl.BlockSpec((seq, width), lambda b, h, i: (b, col0 + 2 * DA_HEADS + h)),
            pl.BlockSpec((1, width), lambda b, h, i: (0, 0)),
            pl.BlockSpec((1, width), lambda b, h, i: (0, 0)),
            pl.BlockSpec((4, DA_DH), lambda b, h, i: (0, 0)),
            pl.BlockSpec((1, width), lambda b, h, i: (0, 0)),
        ],
        out_specs=pl.BlockSpec((tq, width), lambda b, h, i: (b * nq + i, h)),
        out_shape=jax.ShapeDtypeStruct((t, DA_HEADS * width), BF16),
        scratch_shapes=[pltpu.VMEM((seq, width), BF16)],
        compiler_params=pltpu.CompilerParams(
            dimension_semantics=("arbitrary", "arbitrary", "arbitrary"), vmem_limit_bytes=VMEM_LIMIT),
        name="attn",
    )(proj, proj, proj, qg2, kg2, lam4, subln_g)


def _mixout_body(x_ref, oa_ref, od_ref, ga_ref, gd_ref, woa_ref, wod_ref, g1_ref, g_ref, sh_ref, sc_ref,
                 wr_ref, br_ref,
                 x1_ref, h2_ref, idx_ref, rank_ref, gcol_ref, cnt_ref, carry_scr):
    i = pl.program_id(0)
    tm = x_ref.shape[0]

    @pl.when(i == 0)
    def _():
        carry_scr[...] = jnp.zeros_like(carry_scr)

    ya = _dot(oa_ref[...], woa_ref[...])
    yd = _dot(od_ref[...], wod_ref[...])
    y = _sigmoid(ga_ref[...].astype(F32)) * ya + _sigmoid(gd_ref[...].astype(F32)) * yd
    x1 = x_ref[...] + g1_ref[0] * y
    x1_ref[...] = x1
    h2 = _norm_mod(x1, g_ref[...], sh_ref[0], sc_ref[0])
    h2_ref[...] = h2

    hh, hl = _split_bf16(h2)
    wh, wl = _split_bf16(wr_ref[...])
    logits = _dot_nt(wh, hh) + _dot_nt(wl, hh) + _dot_nt(wh, hl) + br_ref[...]

    e_iota = lax.broadcasted_iota(I32, (N_EXPERTS, tm), 0).astype(F32)
    vals = logits
    tops, sels, idxs = [], [], []
    for _ in range(TOP_K):
        m = jnp.max(vals, axis=0, keepdims=True)
        idx = jnp.min(jnp.where(vals == m, e_iota, float(N_EXPERTS)), axis=0, keepdims=True)
        sel = e_iota == idx
        vals = jnp.where(sel, -jnp.inf, vals)
        tops.append(m)
        sels.append(sel)
        idxs.append(idx)
    ex = [jnp.exp(tv - tops[0]) for tv in tops]
    den = ex[0] + ex[1] + ex[2] + ex[3]
    gates = [v / den for v in ex]

    hot = jnp.where(sels[0] | sels[1] | sels[2] | sels[3], 1.0, 0.0)
    r_t = lax.broadcasted_iota(I32, (tm, tm), 0)
    c_t = lax.broadcasted_iota(I32, (tm, tm), 1)
    upper = jnp.where(r_t < c_t, 1.0, 0.0).astype(BF16)
    excl = _dot(hot.astype(BF16), upper) + carry_scr[:, 0:1]
    carry_scr[...] = carry_scr[...] + jnp.sum(hot, axis=1, keepdims=True)
    cnt_ref[...] = carry_scr[...]

    ranks = [jnp.sum(jnp.where(s, excl, 0.0), axis=0, keepdims=True) for s in sels]
    idx_ref[...] = jnp.concatenate(idxs, axis=0).astype(I32)
    rank_ref[...] = jnp.concatenate(ranks, axis=0).astype(I32)
    gpad = jnp.concatenate(gates + [jnp.zeros((LANES - TOP_K, tm), F32)], axis=0)
    gcol_ref[...] = gpad.T


def _mixout(x2, oa, od, proj, woa, wod, ada3, ffn_g, wr_t, br_col, seq):
    t, d = x2.shape
    tm = min(512, seq)
    per_b = seq // tm
    hw = oa.shape[1]
    row = lambda w: pl.BlockSpec((tm, w), lambda i: (i, 0))
    mod = lambda c: pl.BlockSpec((1, 1, d), lambda i, c=c: (i // per_b, 0, c))
    full = lambda a: pl.BlockSpec(a.shape, lambda i: (0,) * a.ndim)
    return pl.pallas_call(
        _mixout_body,
        grid=(t // tm,),
        in_specs=[
            row(d), row(hw), row(hw),
            pl.BlockSpec((tm, d), lambda i: (i, 0)),
            pl.BlockSpec((tm, d), lambda i: (i, 1)),
            full(woa), full(wod),
            mod(2),
            full(ffn_g), mod(3), mod(4),
            full(wr_t), full(br_col),
        ],
        out_specs=[
            row(d), row(d),
            pl.BlockSpec((TOP_K, tm), lambda i: (0, i)),
            pl.BlockSpec((TOP_K, tm), lambda i: (0, i)),
            pl.BlockSpec((tm, LANES), lambda i: (i, 0)),
            pl.BlockSpec((N_EXPERTS, LANES), lambda i: (0, 0)),
        ],
        out_shape=[
            jax.ShapeDtypeStruct((t, d), F32),
            jax.ShapeDtypeStruct((t, d), F32),
            jax.ShapeDtypeStruct((TOP_K, t), I32),
            jax.ShapeDtypeStruct((TOP_K, t), I32),
            jax.ShapeDtypeStruct((t, LANES), F32),
            jax.ShapeDtypeStruct((N_EXPERTS, LANES), F32),
        ],
        scratch_shapes=[pltpu.VMEM((N_EXPERTS, LANES), F32)],
        compiler_params=pltpu.CompilerParams(
            dimension_semantics=("arbitrary",), vmem_limit_bytes=VMEM_LIMIT),
        name="mixout",
    )(x2, oa, od, proj, proj, woa, wod, ada3, ffn_g, ada3, ada3, wr_t, br_col)


def _ffn_body(be_ref, nv_ref, nact_ref, asg_hbm, h2_hbm, w1_ref, b1_ref, w2_ref, b2_ref, y_hbm,
              asg_smem, xbuf, ybuf, asem, gsem, ssem, *, n_tok):
    j = pl.program_id(0)
    nact = nact_ref[0]
    bm = FFN_BLOCK
    ff = w2_ref.shape[1]

    def asg_copy(blk, slot):
        return pltpu.make_async_copy(asg_hbm.at[blk], asg_smem.at[slot], asem.at[slot])

    def gather_row(slot3, slot2, r):
        a = asg_smem[slot3, 0, r]
        tok = lax.rem(jnp.maximum(a, 0), n_tok)
        return pltpu.make_async_copy(h2_hbm.at[pl.ds(tok, 1), :], xbuf.at[slot2, pl.ds(r, 1), :], gsem.at[slot2])

    def start_gather(blk):
        slot3, slot2 = lax.rem(blk, 3), lax.rem(blk, 2)

        def body(r, c):
            gather_row(slot3, slot2, r).start()
            return c

        lax.fori_loop(0, bm, body, 0, unroll=8)

    def wait_gather(slot2):
        pltpu.make_async_copy(h2_hbm.at[pl.ds(0, bm), :], xbuf.at[slot2], gsem.at[slot2]).wait()

    def scatter_row(slot3, slot2, r):
        a = asg_smem[slot3, 0, r]
        return pltpu.make_async_copy(ybuf.at[slot2, pl.ds(r, 1), :], y_hbm.at[pl.ds(a, 1), :], ssem.at[slot2])

    def wait_scatter(blk):
        slot2, nv = lax.rem(blk, 2), nv_ref[blk]
        nv8 = pl.multiple_of((nv >> 3) << 3, 8)

        @pl.when(nv8 > 0)
        def _():
            pltpu.make_async_copy(ybuf.at[slot2, pl.ds(0, nv8), :], y_hbm.at[pl.ds(0, nv8), :], ssem.at[slot2]).wait()

        def one(r, c):
            pltpu.make_async_copy(ybuf.at[slot2, pl.ds(0, 1), :], y_hbm.at[pl.ds(0, 1), :], ssem.at[slot2]).wait()
            return c

        lax.fori_loop(0, nv - nv8, one, 0)

    @pl.when(j == 0)
    def _():
        asg_copy(0, 0).start()
        asg_copy(0, 0).wait()
        start_gather(0)

        @pl.when(nact > 1)
        def _():
            asg_copy(1, 1).start()

    @pl.when(j < nact)
    def _():
        slot3, slot2 = lax.rem(j, 3), lax.rem(j, 2)

        @pl.when(j + 1 < nact)
        def _():
            nslot = lax.rem(j + 1, 3)
            asg_copy(j + 1, nslot).wait()
            start_gather(j + 1)

        @pl.when(j + 2 < nact)
        def _():
            asg_copy(j + 2, lax.rem(j + 2, 3)).start()

        wait_gather(slot2)
        xb = xbuf[slot2].astype(BF16)
        u = _dot(xb, w1_ref[0]) + b1_ref[0]
        glu = jnp.minimum(u[:, :ff], SWIGLU_LIMIT)
        lin = jnp.clip(u[:, ff:], -SWIGLU_LIMIT, SWIGLU_LIMIT)
        act = glu * _sigmoid(SWIGLU_ALPHA * glu) * (lin + 1.0)
        yb = _dot(act.astype(BF16), w2_ref[0]) + b2_ref[0]

        @pl.when(j >= 2)
        def _():
            wait_scatter(j - 2)

        ybuf[slot2] = yb

        def sbody(r, c):
            scatter_row(slot3, slot2, r).start()
            return c

        lax.fori_loop(0, nv_ref[j], sbody, 0)

        @pl.when(j == nact - 1)
        def _():
            wait_scatter(j)

            @pl.when(j >= 1)
            def _():
                wait_scatter(j - 1)


def _ffn(block_expert, nvalid, nact, asg, h2, w1p, b1p, w2b, b2, n_asg):
    n_blocks = asg.shape[0]
    bm = FFN_BLOCK
    t, d = h2.shape
    ff2 = w1p.shape[2]
    ff = w2b.shape[1]
    grid_spec = pltpu.PrefetchScalarGridSpec(
        num_scalar_prefetch=3,
        grid=(n_blocks,),
        in_specs=[
            pl.BlockSpec(memory_space=pl.ANY),
            pl.BlockSpec(memory_space=pl.ANY),
            pl.BlockSpec((1, d, ff2), lambda j, be, nv, na: (be[j], 0, 0)),
            pl.BlockSpec((1, 1, ff2), lambda j, be, nv, na: (be[j], 0, 0)),
            pl.BlockSpec((1, ff, d), lambda j, be, nv, na: (be[j], 0, 0)),
            pl.BlockSpec((1, 1, d), lambda j, be, nv, na: (be[j], 0, 0)),
        ],
        out_specs=pl.BlockSpec(memory_space=pl.ANY),
        scratch_shapes=[
            pltpu.SMEM((3, 1, bm), I32),
            pltpu.VMEM((2, bm, d), F32),
            pltpu.VMEM((2, bm, d), F32),
            pltpu.SemaphoreType.DMA((3,)),
            pltpu.SemaphoreType.DMA((2,)),
            pltpu.SemaphoreType.DMA((2,)),
        ],
    )
    return pl.pallas_call(
        functools.partial(_ffn_body, n_tok=t),
        grid_spec=grid_spec,
        out_shape=jax.ShapeDtypeStruct((n_asg, d), F32),
        compiler_params=pltpu.CompilerParams(
            dimension_semantics=("arbitrary",), vmem_limit_bytes=VMEM_LIMIT),
        name="ffn",
    )(block_expert, nvalid, nact, asg, h2, w1p, b1p, w2b, b2)


def _combine_body(x1_ref, y0_ref, y1_ref, y2_ref, y3_ref, gcol_ref, g2_ref, o_ref):
    gc = gcol_ref[...]
    m = (gc[:, 0:1] * y0_ref[...] + gc[:, 1:2] * y1_ref[...]
         + gc[:, 2:3] * y2_ref[...] + gc[:, 3:4] * y3_ref[...])
    o_ref[...] = (x1_ref[...] + g2_ref[0] * m).astype(o_ref.dtype)


def _combine(x1, yall, gcol, ada3, seq, out_dtype):
    t, d = x1.shape
    tm = min(512, seq)
    per_b = seq // tm
    nt = t // tm
    yk = lambda k: pl.BlockSpec((tm, d), lambda i, k=k: (k * nt + i, 0))
    return pl.pallas_call(
        _combine_body,
        grid=(nt,),
        in_specs=[
            pl.BlockSpec((tm, d), lambda i: (i, 0)),
            yk(0), yk(1), yk(2), yk(3),
            pl.BlockSpec((tm, LANES), lambda i: (i, 0)),
            pl.BlockSpec((1, 1, d), lambda i: (i // per_b, 0, 5)),
        ],
        out_specs=pl.BlockSpec((tm, d), lambda i: (i, 0)),
        out_shape=jax.ShapeDtypeStruct((t, d), out_dtype),
        compiler_params=pltpu.CompilerParams(
            dimension_semantics=("arbitrary",), vmem_limit_bytes=VMEM_LIMIT),
        name="combine",
    )(x1, yall, yall, yall, yall, gcol, ada3)


def _route_tables(idx, rank, counts, n_tok):
    bm = FFN_BLOCK
    n_asg = TOP_K * n_tok
    n_blocks = -(-(n_asg + N_EXPERTS * (bm - 1)) // bm)
    padded = (counts + bm - 1) // bm * bm
    pad_ends = jnp.cumsum(padded)
    pad_starts = pad_ends - padded
    dest = (pad_starts[idx] + rank).reshape(-1)
    asg = jnp.full((n_blocks * bm,), -1, I32).at[dest].set(jnp.arange(n_asg, dtype=I32))
    nact = (pad_ends[-1] // bm).astype(I32)
    blk_start = jnp.arange(n_blocks, dtype=I32) * bm
    be = jnp.minimum(jnp.searchsorted(pad_ends, blk_start, side="right"), N_EXPERTS - 1).astype(I32)
    active = jnp.arange(n_blocks) < nact
    nvalid = jnp.where(active, jnp.clip(counts[be] - (blk_start - pad_starts[be]), 0, bm), 0).astype(I32)
    be = jnp.where(active, be, be[jnp.maximum(nact - 1, 0)])
    return be, nvalid, nact.reshape(1), asg.reshape(n_blocks, 1, bm), n_asg


def kernel(x, c, w_ada, b_ada, mix_norm_g, ffn_norm_g, w_in, hg_lower_bound_logits, hg_out_norm_g, da_q_norm_g, da_k_norm_g, da_lambda_q1, da_lambda_k1, da_lambda_q2, da_lambda_k2, da_subln_g, w_out, w_router, b_router, w1, b1, w2, b2):
    bsz, seq, d = x.shape
    t = bsz * seq
    depth = w_ada.shape[0]
    out_dtype = x.dtype
    hw = HG_HEADS * HG_DV
    xcur = x.reshape(t, d)
    for l in range(depth):
        ada = _ada(c, w_ada[l], b_ada[l])
        ada3 = ada.reshape(bsz, 1, N_MOD * d)
        wi = w_in[l]
        n_in = wi.shape[1]
        w_in_r = jnp.concatenate([wi[:, n_in - 2 * d:], wi[:, :n_in - 2 * d]], axis=1).astype(BF16)
        col_h = (2 * d) // LANES
        col_a = col_h + 4 * HG_HEADS
        proj = _inproj(xcur, mix_norm_g[l].reshape(1, d), ada3, w_in_r, seq)

        o_a = _hgrn(proj, hg_lower_bound_logits, hg_out_norm_g[l].reshape(1, HG_DV), bsz, seq, col_h, l)
        lambda_init = 0.8 - 0.6 * math.exp(-0.3 * l)
        qg2 = jnp.tile(da_q_norm_g[l], 2).reshape(1, 2 * DA_DH)
        kg2 = jnp.tile(da_k_norm_g[l], 2).reshape(1, 2 * DA_DH)
        lam4 = jnp.stack([da_lambda_q1[l], da_lambda_k1[l], da_lambda_q2[l], da_lambda_k2[l]])
        o_d = _attn(proj, qg2, kg2, lam4, da_subln_g[l].reshape(1, 2 * DA_DH), bsz, seq, col_a, lambda_init)

        wo = w_out[l].astype(BF16)
        x1, h2, idx, rank, gcol, cnt = _mixout(
            xcur, o_a, o_d, proj, wo[:hw], wo[hw:], ada3, ffn_norm_g[l].reshape(1, d),
            w_router[l].T, b_router[l].reshape(N_EXPERTS, 1), seq)

        counts = cnt[:, 0].astype(I32)
        be, nvalid, nact, asg, n_asg = _route_tables(idx, rank, counts, t)
        w1l = w1[l]
        w1p = jnp.concatenate([w1l[..., 0::2], w1l[..., 1::2]], axis=-1).astype(BF16)
        b1l = b1[l]
        b1p = jnp.concatenate([b1l[..., 0::2], b1l[..., 1::2]], axis=-1).reshape(N_EXPERTS, 1, -1)
        yall = _ffn(be, nvalid, nact, asg, h2, w1p, b1p, w2[l].astype(BF16), b2[l].reshape(N_EXPERTS, 1, d), n_asg)
        xcur = _combine(x1, yall, gcol, ada3, seq, out_dtype)
    return xcur.reshape(bsz, seq, d)
```

```python
import functools
import math

import jax
import jax.numpy as jnp
from jax import lax
from jax.experimental import pallas as pl
from jax.experimental.pallas import tpu as pltpu

F32 = jnp.float32
BF16 = jnp.bfloat16
I32 = jnp.int32

HG_HEADS = 4
HG_DK = 128
HG_DV = 128
HG_CHUNK = 32
DA_HEADS = 4
DA_DH = 64
N_EXPERTS = 32
TOP_K = 4
SWIGLU_ALPHA = 1.702
SWIGLU_LIMIT = 7.0
NORM_EPS = 1e-6
N_MOD = 6

LANES = 128
VMEM_LIMIT = 56 * 1024 * 1024

HG_ROWS = 256
ATTN_TILE = 256
FFN_BLOCK = 256


def _sigmoid(x):
    return 1.0 / (1.0 + jnp.exp(-x))


def _dot(a, b):
    return jnp.dot(a, b, preferred_element_type=F32)


def _dot_nt(a, b):
    return lax.dot_general(a, b, (((1,), (1,)), ((), ())), preferred_element_type=F32)


def _split_bf16(x):
    hi = x.astype(BF16)
    lo = (x - hi.astype(F32)).astype(BF16)
    return hi, lo


def _ada_body(c_ref, w_ref, b_ref, o_ref):
    c = c_ref[...].astype(F32)
    ca = c * _sigmoid(c)
    o_ref[...] = jnp.dot(ca, w_ref[...], preferred_element_type=F32,
                         precision=lax.Precision.HIGHEST) + b_ref[...]


def _ada(c, w_ada, b_ada):
    bsz, d = c.shape
    n = w_ada.shape[1]
    tn = d
    return pl.pallas_call(
        _ada_body,
        grid=(n // tn,),
        in_specs=[
            pl.BlockSpec((bsz, d), lambda j: (0, 0)),
            pl.BlockSpec((d, tn), lambda j: (0, j)),
            pl.BlockSpec((1, tn), lambda j: (0, j)),
        ],
        out_specs=pl.BlockSpec((bsz, tn), lambda j: (0, j)),
        out_shape=jax.ShapeDtypeStruct((bsz, n), F32),
        name="ada",
    )(c, w_ada, b_ada.reshape(1, n))


def _norm_mod(x, g, shift, scale):
    ms = jnp.mean(x * x, axis=-1, keepdims=True)
    return (x * lax.rsqrt(ms + NORM_EPS) * g) * (1.0 + scale) + shift


def _inproj_body(x_ref, g_ref, sh_ref, sc_ref, w_ref, o_ref):
    h = _norm_mod(x_ref[...], g_ref[...], sh_ref[0], sc_ref[0])
    o_ref[...] = _dot(h.astype(BF16), w_ref[...]).astype(BF16)


def _inproj(x2, g, ada3, w_bf16, seq):
    t, d = x2.shape
    n = w_bf16.shape[1]
    tm = min(512, seq)
    nj = 2
    tn = n // nj
    per_b = seq // tm
    return pl.pallas_call(
        _inproj_body,
        grid=(nj, t // tm),
        in_specs=[
            pl.BlockSpec((tm, d), lambda j, i: (i, 0)),
            pl.BlockSpec((1, d), lambda j, i: (0, 0)),
            pl.BlockSpec((1, 1, d), lambda j, i: (i // per_b, 0, 0)),
            pl.BlockSpec((1, 1, d), lambda j, i: (i // per_b, 0, 1)),
            pl.BlockSpec((d, tn), lambda j, i: (0, j)),
        ],
        out_specs=pl.BlockSpec((tm, tn), lambda j, i: (i, j)),
        out_shape=jax.ShapeDtypeStruct((t, n), BF16),
        compiler_params=pltpu.CompilerParams(
            dimension_semantics=("arbitrary", "arbitrary"), vmem_limit_bytes=VMEM_LIMIT),
        name="inproj",
    )(x2, g, ada3, ada3, w_bf16)


def _hgrn_body(q_ref, f_ref, i_ref, og_ref, lbl_ref, g_ref, o_ref, *, seq, layer):
    rows, chunk = HG_ROWS, HG_CHUNK
    nchunk = rows // chunk
    lbl = lbl_ref[...].astype(F32)
    e = jnp.exp(lbl - jnp.max(lbl, axis=0, keepdims=True))
    lb = jnp.sum(e[: layer + 1], axis=0, keepdims=True) / jnp.sum(e, axis=0, keepdims=True)
    r_i = lax.broadcasted_iota(I32, (rows, rows), 0)
    c_i = lax.broadcasted_iota(I32, (rows, rows), 1)
    tri = ((r_i // chunk) == (c_i // chunk)) & (r_i >= c_i)
    tri_b = jnp.where(tri, 1.0, 0.0).astype(BF16)
    lane_chunk = lax.broadcasted_iota(I32, (HG_DV, rows), 1) // chunk
    g = g_ref[...].astype(F32)

    def block(r, st):
        sl = pl.ds(pl.multiple_of(r * rows, rows), rows)
        qr = q_ref[sl, :].astype(F32)
        fr = f_ref[sl, :].astype(F32)
        v = i_ref[sl, :].astype(F32)
        og = og_ref[sl, :].astype(F32)
        q = qr * _sigmoid(qr)
        f = lb + (1.0 - lb) * _sigmoid(fr)
        k = 1.0 - f
        logf = jnp.log(f)
        lhi, llo = _split_bf16(logf)
        bcum = _dot(tri_b, lhi) + _dot(tri_b, llo)
        b3 = bcum.reshape(nchunk, chunk, HG_DK)
        bl = b3[:, chunk - 1:chunk, :]
        qt = (q * jnp.exp(bcum)).astype(BF16)
        kt = (k * jnp.exp(-bcum)).astype(BF16)
        kd = (k.reshape(nchunk, chunk, HG_DK) * jnp.exp(bl - b3)).reshape(rows, HG_DK).astype(BF16)
        dec = jnp.exp(bl)
        vb = v.astype(BF16)
        scores = _dot_nt(qt, kt)
        a = jnp.where(tri, scores, 0.0).astype(BF16)
        o = _dot(a, vb)
        vt = v.T
        inter = []
        for c in range(nchunk):
            qc = qt[c * chunk:(c + 1) * chunk, :]
            inter.append(_dot_nt(qc, st.astype(BF16)))
            vtc = jnp.where(lane_chunk == c, vt, 0.0).astype(BF16)
            kvt = _dot(vtc, kd)
            st = st * dec[c] + kvt
        o = o + jnp.concatenate(inter, axis=0)
        ms = jnp.mean(o * o, axis=-1, keepdims=True)
        o = o * lax.rsqrt(ms + NORM_EPS) * g
        o_ref[sl, :] = (o * (og * _sigmoid(og))).astype(o_ref.dtype)
        return st

    lax.fori_loop(0, seq // rows, block, jnp.zeros((HG_DV, HG_DK), F32))


def _hgrn(proj, lb_logits, norm_g, bsz, seq, col0, layer):
    t = proj.shape[0]
    blk = lambda off: pl.BlockSpec((seq, LANES), lambda b, h, off=off: (b, col0 + off + h))
    return pl.pallas_call(
        functools.partial(_hgrn_body, seq=seq, layer=layer),
        grid=(bsz, HG_HEADS),
        in_specs=[
            blk(0), blk(HG_HEADS), blk(2 * HG_HEADS), blk(3 * HG_HEADS),
            pl.BlockSpec((lb_logits.shape[0], HG_DK), lambda b, h: (0, h)),
            pl.BlockSpec((1, HG_DV), lambda b, h: (0, 0)),
        ],
        out_specs=pl.BlockSpec((seq, HG_DV), lambda b, h: (b, h)),
        out_shape=jax.ShapeDtypeStruct((t, HG_HEADS * HG_DV), BF16),
        compiler_params=pltpu.CompilerParams(
            dimension_semantics=("arbitrary", "arbitrary"), vmem_limit_bytes=VMEM_LIMIT),
        name="hgrn",
    )(proj, proj, proj, proj, lb_logits, norm_g)


def _group_norm(x, gsum_b, gain):
    hi, lo = _split_bf16(x * x)
    ss = _dot(hi, gsum_b) + _dot(lo, gsum_b)
    return x * lax.rsqrt(ss * (1.0 / DA_DH) + NORM_EPS) * gain


def _attn_body(q_ref, k_ref, v_ref, qg_ref, kg_ref, lam_ref, sg_ref, o_ref, kn_scr, *, seq, lambda_init):
    tq = ATTN_TILE
    qi = pl.program_id(2)
    width = 2 * DA_DH
    r_l = lax.broadcasted_iota(I32, (width, width), 0) // DA_DH
    c_l = lax.broadcasted_iota(I32, (width, width), 1) // DA_DH
    gsum_b = jnp.where(r_l == c_l, 1.0, 0.0).astype(BF16)
    lane = lax.broadcasted_iota(I32, (1, width), 1)

    @pl.when(qi == 0)
    def _():
        kg = kg_ref[...].astype(F32)

        def kblock(r, carry):
            sl = pl.ds(pl.multiple_of(r * tq, tq), tq)
            kn_scr[sl, :] = _group_norm(k_ref[sl, :].astype(F32), gsum_b, kg).astype(BF16)
            return carry

        lax.fori_loop(0, seq // tq, kblock, 0)

    qn = _group_norm(q_ref[...].astype(F32), gsum_b, qg_ref[...].astype(F32)) * (DA_DH ** -0.5)
    q1 = jnp.where(lane < DA_DH, qn, 0.0).astype(BF16)
    q2 = jnp.where(lane >= DA_DH, qn, 0.0).astype(BF16)

    def update(s, m, l, acc, vblk):
        m_new = jnp.maximum(m, jnp.max(s, axis=-1, keepdims=True))
        p = jnp.exp(s - m_new)
        alpha = jnp.exp(m - m_new)
        l = alpha * l + jnp.sum(p, axis=-1, keepdims=True)
        acc = alpha * acc + _dot(p.astype(BF16), vblk)
        return m_new, l, acc

    def step(kb, carry, masked):
        m1, l1, a1, m2, l2, a2 = carry
        sl = pl.ds(pl.multiple_of(kb * tq, tq), tq)
        kblk = kn_scr[sl, :]
        vblk = v_ref[sl, :]
        s1 = _dot_nt(q1, kblk)
        s2 = _dot_nt(q2, kblk)
        if masked:
            row = lax.broadcasted_iota(I32, (tq, tq), 0)
            col = lax.broadcasted_iota(I32, (tq, tq), 1)
            keep = row >= col
            s1 = jnp.where(keep, s1, -jnp.inf)
            s2 = jnp.where(keep, s2, -jnp.inf)
        m1, l1, a1 = update(s1, m1, l1, a1, vblk)
        m2, l2, a2 = update(s2, m2, l2, a2, vblk)
        return m1, l1, a1, m2, l2, a2

    neg = jnp.full((tq, 1), -jnp.inf, F32)
    zero = jnp.zeros((tq, 1), F32)
    zacc = jnp.zeros((tq, width), F32)
    carry = lax.fori_loop(0, qi, lambda kb, c: step(kb, c, False), (neg, zero, zacc, neg, zero, zacc))
    m1, l1, a1, m2, l2, a2 = step(qi, carry, True)

    lam_v = lam_ref[...].astype(F32)
    lam = (jnp.exp(jnp.sum(lam_v[0:1] * lam_v[1:2], axis=-1, keepdims=True))
           - jnp.exp(jnp.sum(lam_v[2:3] * lam_v[3:4], axis=-1, keepdims=True)) + lambda_init)
    o = a1 / l1 - lam * (a2 / l2)
    ms = jnp.mean(o * o, axis=-1, keepdims=True)
    o = o * lax.rsqrt(ms + NORM_EPS) * sg_ref[...].astype(F32) * (1.0 - lambda_init)
    o_ref[...] = o.astype(o_ref.dtype)


def _attn(proj, qg2, kg2, lam4, subln_g, bsz, seq, col0, lambda_init):
    t = proj.shape[0]
    tq = ATTN_TILE
    nq = seq // tq
    width = 2 * DA_DH
    return pl.pallas_call(
        functools.partial(_attn_body, seq=seq, lambda_init=lambda_init),
        grid=(bsz, DA_HEADS, nq),
        in_specs=[
            pl.BlockSpec((tq, width), lambda b, h, i: (b * nq + i, col0 + h)),
            pl.BlockSpec((seq, width), lambda b, h, i: (b, col0 + DA_HEADS + h)),
            pl.BlockSpec((seq, width), lambda b, h, i: (b, col0 + 2 * DA_HEADS + h)),
            pl.BlockSpec((1, width), lambda b, h, i: (0, 0)),
            pl.BlockSpec((1, width), lambda b, h, i: (0, 0)),
            pl.BlockSpec((4, DA_DH), lambda b, h, i: (0, 0)),
            pl.BlockSpec((1, width), lambda b, h, i: (0, 0)),
        ],
        out_specs=pl.BlockSpec((tq, width), lambda b, h, i: (b * nq + i, h)),
        out_shape=jax.ShapeDtypeStruct((t, DA_HEADS * width), BF16),
        scratch_shapes=[pltpu.VMEM((seq, width), BF16)],
        compiler_params=pltpu.CompilerParams(
            dimension_semantics=("arbitrary", "arbitrary", "arbitrary"), vmem_limit_bytes=VMEM_LIMIT),
        name="attn",
    )(proj, proj, proj, qg2, kg2, lam4, subln_g)


def _mixout_body(x_ref, oa_ref, od_ref, ga_ref, gd_ref, woa_ref, wod_ref, g1_ref, g_ref, sh_ref, sc_ref,
                 wr_ref, br_ref,
                 x1_ref, h2_ref, idx_ref, rank_ref, gcol_ref, cnt_ref, carry_scr):
    i = pl.program_id(0)
    tm = x_ref.shape[0]

    @pl.when(i == 0)
    def _():
        carry_scr[...] = jnp.zeros_like(carry_scr)

    ya = _dot(oa_ref[...], woa_ref[...])
    yd = _dot(od_ref[...], wod_ref[...])
    y = _sigmoid(ga_ref[...].astype(F32)) * ya + _sigmoid(gd_ref[...].astype(F32)) * yd
    x1 = x_ref[...] + g1_ref[0] * y
    x1_ref[...] = x1
    h2 = _norm_mod(x1, g_ref[...], sh_ref[0], sc_ref[0])
    h2_ref[...] = h2

    hh, hl = _split_bf16(h2)
    wh, wl = _split_bf16(wr_ref[...])
    logits = _dot_nt(wh, hh) + _dot_nt(wl, hh) + _dot_nt(wh, hl) + br_ref[...]

    e_iota = lax.broadcasted_iota(I32, (N_EXPERTS, tm), 0).astype(F32)
    vals = logits
    tops, sels, idxs = [], [], []
    for _ in range(TOP_K):
        m = jnp.max(vals, axis=0, keepdims=True)
        idx = jnp.min(jnp.where(vals == m, e_iota, float(N_EXPERTS)), axis=0, keepdims=True)
        sel = e_iota == idx
        vals = jnp.where(sel, -jnp.inf, vals)
        tops.append(m)
        sels.append(sel)
        idxs.append(idx)
    ex = [jnp.exp(tv - tops[0]) for tv in tops]
    den = ex[0] + ex[1] + ex[2] + ex[3]
    gates = [v / den for v in ex]

    hot = jnp.where(sels[0] | sels[1] | sels[2] | sels[3], 1.0, 0.0)
    r_t = lax.broadcasted_iota(I32, (tm, tm), 0)
    c_t = lax.broadcasted_iota(I32, (tm, tm), 1)
    upper = jnp.where(r_t < c_t, 1.0, 0.0).astype(BF16)
    excl = _dot(hot.astype(BF16), upper) + carry_scr[:, 0:1]
    carry_scr[...] = carry_scr[...] + jnp.sum(hot, axis=1, keepdims=True)
    cnt_ref[...] = carry_scr[...]

    ranks = [jnp.sum(jnp.where(s, excl, 0.0), axis=0, keepdims=True) for s in sels]
    idx_ref[...] = jnp.concatenate(idxs, axis=0).astype(I32)
    rank_ref[...] = jnp.concatenate(ranks, axis=0).astype(I32)
    gpad = jnp.concatenate(gates + [jnp.zeros((LANES - TOP_K, tm), F32)], axis=0)
    gcol_ref[...] = gpad.T


def _mixout(x2, oa, od, proj, woa, wod, ada3, ffn_g, wr_t, br_col, seq):
    t, d = x2.shape
    tm = min(512, seq)
    per_b = seq // tm
    hw = oa.shape[1]
    row = lambda w: pl.BlockSpec((tm, w), lambda i: (i, 0))
    mod = lambda c: pl.BlockSpec((1, 1, d), lambda i, c=c: (i // per_b, 0, c))
    full = lambda a: pl.BlockSpec(a.shape, lambda i: (0,) * a.ndim)
    return pl.pallas_call(
        _mixout_body,
        grid=(t // tm,),
        in_specs=[
            row(d), row(hw), row(hw),
            pl.BlockSpec((tm, d), lambda i: (i, 0)),
            pl.BlockSpec((tm, d), lambda i: (i, 1)),
            full(woa), full(wod),
            mod(2),
            full(ffn_g), mod(3), mod(4),
            full(wr_t), full(br_col),
        ],
        out_specs=[
            row(d), row(d),
            pl.BlockSpec((TOP_K, tm), lambda i: (0, i)),
            pl.BlockSpec((TOP_K, tm), lambda i: (0, i)),
            pl.BlockSpec((tm, LANES), lambda i: (i, 0)),
            pl.BlockSpec((N_EXPERTS, LANES), lambda i: (0, 0)),
        ],
        out_shape=[
            jax.ShapeDtypeStruct((t, d), F32),
            jax.ShapeDtypeStruct((t, d), F32),
            jax.ShapeDtypeStruct((TOP_K, t), I32),
            jax.ShapeDtypeStruct((TOP_K, t), I32),
            jax.ShapeDtypeStruct((t, LANES), F32),
            jax.ShapeDtypeStruct((N_EXPERTS, LANES), F32),
        ],
        scratch_shapes=[pltpu.VMEM((N_EXPERTS, LANES), F32)],
        compiler_params=pltpu.CompilerParams(
            dimension_semantics=("arbitrary",), vmem_limit_bytes=VMEM_LIMIT),
        name="mixout",
    )(x2, oa, od, proj, proj, woa, wod, ada3, ffn_g, ada3, ada3, wr_t, br_col)


def _ffn_body(be_ref, nv_ref, nact_ref, asg_hbm, h2_hbm, w1_ref, b1_ref, w2_ref, b2_ref, y_hbm,
              asg_smem, xbuf, ybuf, w1c, w2c, asem, gsem, ssem, *, n_tok):
    j = pl.program_id(0)
    nact = nact_ref[0]
    bm = FFN_BLOCK
    ff = w2_ref.shape[1]
    pair = 2 * LANES
    ngroup = (2 * ff) // pair

    @pl.when((j == 0) | (be_ref[j] != be_ref[jnp.maximum(j - 1, 0)]))
    def _():
        r_p = lax.broadcasted_iota(I32, (pair, pair), 0)
        c_p = lax.broadcasted_iota(I32, (pair, pair), 1)
        src = jnp.where(c_p < LANES, 2 * c_p, 2 * (c_p - LANES) + 1)
        perm = jnp.where(r_p == src, 1.0, 0.0).astype(BF16)
        for g in range(ngroup):
            cols = slice(g * pair, (g + 1) * pair)
            w1c[:, cols] = _dot(w1_ref[0, :, cols].astype(BF16), perm).astype(BF16)
        w2c[...] = w2_ref[0].astype(BF16)

    def asg_copy(blk, slot):
        return pltpu.make_async_copy(asg_hbm.at[blk], asg_smem.at[slot], asem.at[slot])

    def gather_row(slot3, slot2, r):
        a = asg_smem[slot3, 0, r]
        tok = lax.rem(jnp.maximum(a, 0), n_tok)
        return pltpu.make_async_copy(h2_hbm.at[pl.ds(tok, 1), :], xbuf.at[slot2, pl.ds(r, 1), :], gsem.at[slot2])

    def start_gather(blk):
        slot3, slot2 = lax.rem(blk, 3), lax.rem(blk, 2)

        def body(r, c):
            gather_row(slot3, slot2, r).start()
            return c

        lax.fori_loop(0, bm, body, 0, unroll=8)

    def wait_gather(slot2):
        pltpu.make_async_copy(h2_hbm.at[pl.ds(0, bm), :], xbuf.at[slot2], gsem.at[slot2]).wait()

    def scatter_row(slot3, slot2, r):
        a = asg_smem[slot3, 0, r]
        return pltpu.make_async_copy(ybuf.at[slot2, pl.ds(r, 1), :], y_hbm.at[pl.ds(a, 1), :], ssem.at[slot2])

    def wait_scatter(blk):
        slot2, nv = lax.rem(blk, 2), nv_ref[blk]
        nv8 = pl.multiple_of((nv >> 3) << 3, 8)

        @pl.when(nv8 > 0)
        def _():
            pltpu.make_async_copy(ybuf.at[slot2, pl.ds(0, nv8), :], y_hbm.at[pl.ds(0, nv8), :], ssem.at[slot2]).wait()

        def one(r, c):
            pltpu.make_async_copy(ybuf.at[slot2, pl.ds(0, 1), :], y_hbm.at[pl.ds(0, 1), :], ssem.at[slot2]).wait()
            return c

        lax.fori_loop(0, nv - nv8, one, 0)

    @pl.when(j == 0)
    def _():
        asg_copy(0, 0).start()
        asg_copy(0, 0).wait()
        start_gather(0)

        @pl.when(nact > 1)
        def _():
            asg_copy(1, 1).start()

    @pl.when(j < nact)
    def _():
        slot3, slot2 = lax.rem(j, 3), lax.rem(j, 2)

        @pl.when(j + 1 < nact)
        def _():
            nslot = lax.rem(j + 1, 3)
            asg_copy(j + 1, nslot).wait()
            start_gather(j + 1)

        @pl.when(j + 2 < nact)
        def _():
            asg_copy(j + 2, lax.rem(j + 2, 3)).start()

        wait_gather(slot2)
        xb = xbuf[slot2].astype(BF16)
        u = _dot(xb, w1c[...]) + b1_ref[0]
        acts = []
        for g in range(ngroup):
            glu = jnp.minimum(u[:, g * pair:g * pair + LANES], SWIGLU_LIMIT)
            lin = jnp.clip(u[:, g * pair + LANES:(g + 1) * pair], -SWIGLU_LIMIT, SWIGLU_LIMIT)
            acts.append((glu * _sigmoid(SWIGLU_ALPHA * glu) * (lin + 1.0)).astype(BF16))
        act = jnp.concatenate(acts, axis=1)
        yb = _dot(act, w2c[...]) + b2_ref[0]

        @pl.when(j >= 2)
        def _():
            wait_scatter(j - 2)

        ybuf[slot2] = yb

        def sbody(r, c):
            scatter_row(slot3, slot2, r).start()
            return c

        lax.fori_loop(0, nv_ref[j], sbody, 0)

        @pl.when(j == nact - 1)
        def _():
            wait_scatter(j)

            @pl.when(j >= 1)
            def _():
                wait_scatter(j - 1)


def _ffn(block_expert, nvalid, nact, asg, h2, w1p, b1p, w2b, b2, n_asg):
    n_blocks = asg.shape[0]
    bm = FFN_BLOCK
    t, d = h2.shape
    ff2 = w1p.shape[2]
    ff = w2b.shape[1]
    grid_spec = pltpu.PrefetchScalarGridSpec(
        num_scalar_prefetch=3,
        grid=(n_blocks,),
        in_specs=[
            pl.BlockSpec(memory_space=pl.ANY),
            pl.BlockSpec(memory_space=pl.ANY),
            pl.BlockSpec((1, d, ff2), lambda j, be, nv, na: (be[j], 0, 0)),
            pl.BlockSpec((1, 1, ff2), lambda j, be, nv, na: (be[j], 0, 0)),
            pl.BlockSpec((1, ff, d), lambda j, be, nv, na: (be[j], 0, 0)),
            pl.BlockSpec((1, 1, d), lambda j, be, nv, na: (be[j], 0, 0)),
        ],
        out_specs=pl.BlockSpec(memory_space=pl.ANY),
        scratch_shapes=[
            pltpu.SMEM((3, 1, bm), I32),
            pltpu.VMEM((2, bm, d), F32),
            pltpu.VMEM((2, bm, d), F32),
            pltpu.VMEM((d, ff2), BF16),
            pltpu.VMEM((ff, d), BF16),
            pltpu.SemaphoreType.DMA((3,)),
            pltpu.SemaphoreType.DMA((2,)),
            pltpu.SemaphoreType.DMA((2,)),
        ],
    )
    return pl.pallas_call(
        functools.partial(_ffn_body, n_tok=t),
        grid_spec=grid_spec,
        out_shape=jax.ShapeDtypeStruct((n_asg, d), F32),
        compiler_params=pltpu.CompilerParams(
            dimension_semantics=("arbitrary",), vmem_limit_bytes=VMEM_LIMIT),
        name="ffn",
    )(block_expert, nvalid, nact, asg, h2, w1p, b1p, w2b, b2)


def _combine_body(x1_ref, y0_ref, y1_ref, y2_ref, y3_ref, gcol_ref, g2_ref, o_ref):
    gc = gcol_ref[...]
    m = (gc[:, 0:1] * y0_ref[...] + gc[:, 1:2] * y1_ref[...]
         + gc[:, 2:3] * y2_ref[...] + gc[:, 3:4] * y3_ref[...])
    o_ref[...] = (x1_ref[...] + g2_ref[0] * m).astype(o_ref.dtype)


def _combine(x1, yall, gcol, ada3, seq, out_dtype):
    t, d = x1.shape
    tm = min(512, seq)
    per_b = seq // tm
    nt = t // tm
    yk = lambda k: pl.BlockSpec((tm, d), lambda i, k=k: (k * nt + i, 0))
    return pl.pallas_call(
        _combine_body,
        grid=(nt,),
        in_specs=[
            pl.BlockSpec((tm, d), lambda i: (i, 0)),
            yk(0), yk(1), yk(2), yk(3),
            pl.BlockSpec((tm, LANES), lambda i: (i, 0)),
            pl.BlockSpec((1, 1, d), lambda i: (i // per_b, 0, 5)),
        ],
        out_specs=pl.BlockSpec((tm, d), lambda i: (i, 0)),
        out_shape=jax.ShapeDtypeStruct((t, d), out_dtype),
        compiler_params=pltpu.CompilerParams(
            dimension_semantics=("arbitrary",), vmem_limit_bytes=VMEM_LIMIT),
        name="combine",
    )(x1, yall, yall, yall, yall, gcol, ada3)


def _route_tables(idx, rank, counts, n_tok):
    bm = FFN_BLOCK
    n_asg = TOP_K * n_tok
    n_blocks = -(-(n_asg + N_EXPERTS * (bm - 1)) // bm)
    padded = (counts + bm - 1) // bm * bm
    pad_ends = jnp.cumsum(padded)
    pad_starts = pad_ends - padded
    e_ids = jnp.arange(N_EXPERTS, dtype=I32)
    start_of = jnp.sum(jnp.where(idx[None] == e_ids[:, None, None], pad_starts[:, None, None], 0), axis=0)
    dest = (start_of + rank).reshape(-1)
    asg = jnp.full((n_blocks * bm,), -1, I32).at[dest].set(jnp.arange(n_asg, dtype=I32))
    nact = (pad_ends[-1] // bm).astype(I32)
    blk_start = jnp.arange(n_blocks, dtype=I32) * bm
    active = blk_start < pad_ends[-1]
    last = jnp.sum(jnp.where(pad_ends <= pad_ends[-1] - 1, 1, 0)).astype(I32)
    be = jnp.sum(jnp.where(pad_ends[None, :] <= blk_start[:, None], 1, 0), axis=1).astype(I32)
    be = jnp.where(active, be, last)
    mine = be[:, None] == e_ids[None, :]
    cnt_b = jnp.sum(jnp.where(mine, counts[None, :], 0), axis=1)
    start_b = jnp.sum(jnp.where(mine, pad_starts[None, :], 0), axis=1)
    nvalid = jnp.where(active, jnp.clip(cnt_b - (blk_start - start_b), 0, bm), 0).astype(I32)
    return be, nvalid, nact.reshape(1), asg.reshape(n_blocks, 1, bm), n_asg


def kernel(x, c, w_ada, b_ada, mix_norm_g, ffn_norm_g, w_in, hg_lower_bound_logits, hg_out_norm_g, da_q_norm_g, da_k_norm_g, da_lambda_q1, da_lambda_k1, da_lambda_q2, da_lambda_k2, da_subln_g, w_out, w_router, b_router, w1, b1, w2, b2):
    bsz, seq, d = x.shape
    t = bsz * seq
    depth = w_ada.shape[0]
    out_dtype = x.dtype
    hw = HG_HEADS * HG_DV
    xcur = x.reshape(t, d)
    for l in range(depth):
        ada = _ada(c, w_ada[l], b_ada[l])
        ada3 = ada.reshape(bsz, 1, N_MOD * d)
        wi = w_in[l]
        n_in = wi.shape[1]
        w_in_r = jnp.concatenate([wi[:, n_in - 2 * d:], wi[:, :n_in - 2 * d]], axis=1).astype(BF16)
        col_h = (2 * d) // LANES
        col_a = col_h + 4 * HG_HEADS
        proj = _inproj(xcur, mix_norm_g[l].reshape(1, d), ada3, w_in_r, seq)

        o_a = _hgrn(proj, hg_lower_bound_logits, hg_out_norm_g[l].reshape(1, HG_DV), bsz, seq, col_h, l)
        lambda_init = 0.8 - 0.6 * math.exp(-0.3 * l)
        qg2 = jnp.tile(da_q_norm_g[l], 2).reshape(1, 2 * DA_DH)
        kg2 = jnp.tile(da_k_norm_g[l], 2).reshape(1, 2 * DA_DH)
        lam4 = jnp.stack([da_lambda_q1[l], da_lambda_k1[l], da_lambda_q2[l], da_lambda_k2[l]])
        o_d = _attn(proj, qg2, kg2, lam4, da_subln_g[l].reshape(1, 2 * DA_DH), bsz, seq, col_a, lambda_init)

        wo = w_out[l].astype(BF16)
        x1, h2, idx, rank, gcol, cnt = _mixout(
            xcur, o_a, o_d, proj, wo[:hw], wo[hw:], ada3, ffn_norm_g[l].reshape(1, d),
            w_router[l].T, b_router[l].reshape(N_EXPERTS, 1), seq)

        counts = cnt[:, 0].astype(I32)
        be, nvalid, nact, asg, n_asg = _route_tables(idx, rank, counts, t)
        b1p = b1[l].reshape(N_EXPERTS, -1, LANES, 2).transpose(0, 1, 3, 2).reshape(N_EXPERTS, 1, -1)
        yall = _ffn(be, nvalid, nact, asg, h2, w1[l], b1p, w2[l], b2[l].reshape(N_EXPERTS, 1, d), n_asg)
        xcur = _combine(x1, yall, gcol, ada3, seq, out_dtype)
    return xcur.reshape(bsz, seq, d)
```

```python
import functools
import math

import jax
import jax.numpy as jnp
from jax import lax
from jax.experimental import pallas as pl
from jax.experimental.pallas import tpu as pltpu

F32 = jnp.float32
BF16 = jnp.bfloat16
I32 = jnp.int32

HG_HEADS = 4
HG_DK = 128
HG_DV = 128
HG_CHUNK = 32
DA_HEADS = 4
DA_DH = 64
N_EXPERTS = 32
TOP_K = 4
SWIGLU_ALPHA = 1.702
SWIGLU_LIMIT = 7.0
NORM_EPS = 1e-6
LOG2E = math.log2(math.e)
N_MOD = 6

LANES = 128
VMEM_LIMIT = 56 * 1024 * 1024

HG_ROWS = 256
ATTN_TILE = 256
FFN_BLOCK = 256


def _sigmoid(x):
    return 1.0 / (1.0 + jnp.exp(-x))


def _dot(a, b):
    return jnp.dot(a, b, preferred_element_type=F32)


def _dot_nt(a, b):
    return lax.dot_general(a, b, (((1,), (1,)), ((), ())), preferred_element_type=F32)


def _split_bf16(x):
    hi = x.astype(BF16)
    lo = (x - hi.astype(F32)).astype(BF16)
    return hi, lo


def _ada_body(c_ref, w_ref, b_ref, o_ref):
    c = c_ref[...].astype(F32)
    ca = c * _sigmoid(c)
    o_ref[...] = jnp.dot(ca, w_ref[...], preferred_element_type=F32,
                         precision=lax.Precision.HIGHEST) + b_ref[...]


def _ada(c, w_ada, b_ada):
    bsz, d = c.shape
    n = w_ada.shape[1]
    tn = d
    return pl.pallas_call(
        _ada_body,
        grid=(n // tn,),
        in_specs=[
            pl.BlockSpec((bsz, d), lambda j: (0, 0)),
            pl.BlockSpec((d, tn), lambda j: (0, j)),
            pl.BlockSpec((1, tn), lambda j: (0, j)),
        ],
        out_specs=pl.BlockSpec((bsz, tn), lambda j: (0, j)),
        out_shape=jax.ShapeDtypeStruct((bsz, n), F32),
        name="ada",
    )(c, w_ada, b_ada.reshape(1, n))


def _norm_mod(x, g, shift, scale):
    ms = jnp.mean(x * x, axis=-1, keepdims=True)
    return (x * lax.rsqrt(ms + NORM_EPS) * g) * (1.0 + scale) + shift


def _inproj_body(x_ref, g_ref, sh_ref, sc_ref, w_ref, o_ref):
    h = _norm_mod(x_ref[...], g_ref[...], sh_ref[0], sc_ref[0])
    o_ref[...] = _dot(h.astype(BF16), w_ref[...]).astype(BF16)


def _inproj(x2, g, ada3, w_bf16, seq):
    t, d = x2.shape
    n = w_bf16.shape[1]
    tm = min(512, seq)
    nj = 2
    tn = n // nj
    per_b = seq // tm
    return pl.pallas_call(
        _inproj_body,
        grid=(nj, t // tm),
        in_specs=[
            pl.BlockSpec((tm, d), lambda j, i: (i, 0)),
            pl.BlockSpec((1, d), lambda j, i: (0, 0)),
            pl.BlockSpec((1, 1, d), lambda j, i: (i // per_b, 0, 0)),
            pl.BlockSpec((1, 1, d), lambda j, i: (i // per_b, 0, 1)),
            pl.BlockSpec((d, tn), lambda j, i: (0, j)),
        ],
        out_specs=pl.BlockSpec((tm, tn), lambda j, i: (i, j)),
        out_shape=jax.ShapeDtypeStruct((t, n), BF16),
        compiler_params=pltpu.CompilerParams(
            dimension_semantics=("arbitrary", "arbitrary"), vmem_limit_bytes=VMEM_LIMIT),
        name="inproj",
    )(x2, g, ada3, ada3, w_bf16)


def _hgrn_body(q_ref, f_ref, i_ref, og_ref, lbl_ref, g_ref, o_ref, *, seq, layer):
    rows, chunk = HG_ROWS, HG_CHUNK
    nchunk = rows // chunk
    lbl = lbl_ref[...].astype(F32)
    e = jnp.exp(lbl - jnp.max(lbl, axis=0, keepdims=True))
    lb = jnp.sum(e[: layer + 1], axis=0, keepdims=True) / jnp.sum(e, axis=0, keepdims=True)
    r_i = lax.broadcasted_iota(I32, (rows, rows), 0)
    c_i = lax.broadcasted_iota(I32, (rows, rows), 1)
    tri = ((r_i // chunk) == (c_i // chunk)) & (r_i >= c_i)
    tri_b = jnp.where(tri, 1.0, 0.0).astype(BF16)
    lane_chunk = lax.broadcasted_iota(I32, (HG_DV, rows), 1) // chunk
    g = g_ref[...].astype(F32)

    def block(r, st):
        sl = pl.ds(pl.multiple_of(r * rows, rows), rows)
        qr = q_ref[sl, :].astype(F32)
        fr = f_ref[sl, :].astype(F32)
        v = i_ref[sl, :].astype(F32)
        og = og_ref[sl, :].astype(F32)
        q = qr * _sigmoid(qr)
        f = lb + (1.0 - lb) * _sigmoid(fr)
        k = 1.0 - f
        logf = jnp.log(f)
        lhi, llo = _split_bf16(logf)
        bcum = _dot(tri_b, lhi) + _dot(tri_b, llo)
        b3 = bcum.reshape(nchunk, chunk, HG_DK)
        bl = b3[:, chunk - 1:chunk, :]
        qt = (q * jnp.exp(bcum)).astype(BF16)
        kt = (k * jnp.exp(-bcum)).astype(BF16)
        kd = (k.reshape(nchunk, chunk, HG_DK) * jnp.exp(bl - b3)).reshape(rows, HG_DK).astype(BF16)
        dec = jnp.exp(bl)
        vb = v.astype(BF16)
        scores = _dot_nt(qt, kt)
        a = jnp.where(tri, scores, 0.0).astype(BF16)
        o = _dot(a, vb)
        vt = v.T
        inter = []
        for c in range(nchunk):
            qc = qt[c * chunk:(c + 1) * chunk, :]
            inter.append(_dot_nt(qc, st.astype(BF16)))
            vtc = jnp.where(lane_chunk == c, vt, 0.0).astype(BF16)
            kvt = _dot(vtc, kd)
            st = st * dec[c] + kvt
        o = o + jnp.concatenate(inter, axis=0)
        ms = jnp.mean(o * o, axis=-1, keepdims=True)
        o = o * lax.rsqrt(ms + NORM_EPS) * g
        o_ref[sl, :] = (o * (og * _sigmoid(og))).astype(o_ref.dtype)
        return st

    lax.fori_loop(0, seq // rows, block, jnp.zeros((HG_DV, HG_DK), F32))


def _hgrn(proj, lb_logits, norm_g, bsz, seq, col0, layer):
    t = proj.shape[0]
    blk = lambda off: pl.BlockSpec((seq, LANES), lambda b, h, off=off: (b, col0 + off + h))
    return pl.pallas_call(
        functools.partial(_hgrn_body, seq=seq, layer=layer),
        grid=(bsz, HG_HEADS),
        in_specs=[
            blk(0), blk(HG_HEADS), blk(2 * HG_HEADS), blk(3 * HG_HEADS),
            pl.BlockSpec((lb_logits.shape[0], HG_DK), lambda b, h: (0, h)),
            pl.BlockSpec((1, HG_DV), lambda b, h: (0, 0)),
        ],
        out_specs=pl.BlockSpec((seq, HG_DV), lambda b, h: (b, h)),
        out_shape=jax.ShapeDtypeStruct((t, HG_HEADS * HG_DV), BF16),
        compiler_params=pltpu.CompilerParams(
            dimension_semantics=("arbitrary", "arbitrary"), vmem_limit_bytes=VMEM_LIMIT),
        name="hgrn",
    )(proj, proj, proj, proj, lb_logits, norm_g)


def _group_norm(x, gsum_b, gain):
    hi, lo = _split_bf16(x * x)
    ss = _dot(hi, gsum_b) + _dot(lo, gsum_b)
    return x * lax.rsqrt(ss * (1.0 / DA_DH) + NORM_EPS) * gain


def _attn_body(q_ref, k_ref, v_ref, qg_ref, kg_ref, lam_ref, sg_ref, o_ref, kn_scr, v1_scr, *, seq, lambda_init):
    tq = ATTN_TILE
    qi = pl.program_id(2)
    width = 2 * DA_DH
    r_l = lax.broadcasted_iota(I32, (width, width), 0) // DA_DH
    c_l = lax.broadcasted_iota(I32, (width, width), 1) // DA_DH
    gsum_b = jnp.where(r_l == c_l, 1.0, 0.0).astype(BF16)
    lane = lax.broadcasted_iota(I32, (1, width), 1)

    @pl.when(qi == 0)
    def _():
        kg = kg_ref[...].astype(F32)

        def kblock(r, carry):
            sl = pl.ds(pl.multiple_of(r * tq, tq), tq)
            kn_scr[sl, :] = _group_norm(k_ref[sl, :].astype(F32), gsum_b, kg).astype(BF16)
            v1_scr[sl, :] = jnp.concatenate([v_ref[sl, :], jnp.ones((tq, width), BF16)], axis=1)
            return carry

        lax.fori_loop(0, seq // tq, kblock, 0)

    qn = _group_norm(q_ref[...].astype(F32), gsum_b, qg_ref[...].astype(F32)) * (DA_DH ** -0.5 * LOG2E)
    q1 = jnp.where(lane < DA_DH, qn, 0.0).astype(BF16)
    q2 = jnp.where(lane >= DA_DH, qn, 0.0).astype(BF16)
    row = lax.broadcasted_iota(I32, (tq, tq), 0)
    col = lax.broadcasted_iota(I32, (tq, tq), 1)
    keep = row >= col

    lam_v = lam_ref[...].astype(F32)
    lam = (jnp.exp(jnp.sum(lam_v[0:1] * lam_v[1:2], axis=-1, keepdims=True))
           - jnp.exp(jnp.sum(lam_v[2:3] * lam_v[3:4], axis=-1, keepdims=True)) + lambda_init)

    def softmax_v(qc, nk):
        s = _dot_nt(qc, kn_scr[0:nk, :])
        diag = jnp.where(keep, s[:, nk - tq:], -jnp.inf)
        s = diag if nk == tq else jnp.concatenate([s[:, :nk - tq], diag], axis=1)
        m = jnp.max(s, axis=-1, keepdims=True)
        return _dot(jnp.exp2(s - m).astype(BF16), v1_scr[0:nk, :])

    def tile(nblk):
        nk = nblk * tq
        a1 = softmax_v(q1, nk)
        a2 = softmax_v(q2, nk)
        o = a1[:, :width] / a1[:, width:width + 1] - lam * (a2[:, :width] / a2[:, width:width + 1])
        ms = jnp.mean(o * o, axis=-1, keepdims=True)
        o = o * lax.rsqrt(ms + NORM_EPS) * sg_ref[...].astype(F32) * (1.0 - lambda_init)
        o_ref[...] = o.astype(o_ref.dtype)

    for i in range(seq // tq):
        pl.when(qi == i)(functools.partial(tile, i + 1))


def _attn(proj, qg2, kg2, lam4, subln_g, bsz, seq, col0, lambda_init):
    t = proj.shape[0]
    tq = ATTN_TILE
    nq = seq // tq
    width = 2 * DA_DH
    return pl.pallas_call(
        functools.partial(_attn_body, seq=seq, lambda_init=lambda_init),
        grid=(bsz, DA_HEADS, nq),
        in_specs=[
            pl.BlockSpec((tq, width), lambda b, h, i: (b * nq + i, col0 + h)),
            pl.BlockSpec((seq, width), lambda b, h, i: (b, col0 + DA_HEADS + h)),
            pl.BlockSpec((seq, width), lambda b, h, i: (b, col0 + 2 * DA_HEADS + h)),
            pl.BlockSpec((1, width), lambda b, h, i: (0, 0)),
            pl.BlockSpec((1, width), lambda b, h, i: (0, 0)),
            pl.BlockSpec((4, DA_DH), lambda b, h, i: (0, 0)),
            pl.BlockSpec((1, width), lambda b, h, i: (0, 0)),
        ],
        out_specs=pl.BlockSpec((tq, width), lambda b, h, i: (b * nq + i, h)),
        out_shape=jax.ShapeDtypeStruct((t, DA_HEADS * width), BF16),
        scratch_shapes=[pltpu.VMEM((seq, width), BF16), pltpu.VMEM((seq, 2 * width), BF16)],
        compiler_params=pltpu.CompilerParams(
            dimension_semantics=("arbitrary", "arbitrary", "arbitrary"), vmem_limit_bytes=VMEM_LIMIT),
        name="attn",
    )(proj, proj, proj, qg2, kg2, lam4, subln_g)


def _mixout_body(x_ref, oa_ref, od_ref, ga_ref, gd_ref, woa_ref, wod_ref, g1_ref, g_ref, sh_ref, sc_ref,
                 wr_ref, br_ref,
                 x1_ref, h2_ref, idx_ref, rank_ref, gcol_ref, cnt_ref, carry_scr):
    i = pl.program_id(0)
    tm = x_ref.shape[0]

    @pl.when(i == 0)
    def _():
        carry_scr[...] = jnp.zeros_like(carry_scr)

    ya = _dot(oa_ref[...], woa_ref[...])
    yd = _dot(od_ref[...], wod_ref[...])
    y = _sigmoid(ga_ref[...].astype(F32)) * ya + _sigmoid(gd_ref[...].astype(F32)) * yd
    x1 = x_ref[...] + g1_ref[0] * y
    x1_ref[...] = x1
    h2 = _norm_mod(x1, g_ref[...], sh_ref[0], sc_ref[0])
    h2_ref[...] = h2

    hh, hl = _split_bf16(h2)
    wh, wl = _split_bf16(wr_ref[...])
    logits = _dot_nt(wh, hh) + _dot_nt(wl, hh) + _dot_nt(wh, hl) + br_ref[...]

    e_iota = lax.broadcasted_iota(I32, (N_EXPERTS, tm), 0).astype(F32)
    vals = logits
    tops, sels, idxs = [], [], []
    for _ in range(TOP_K):
        m = jnp.max(vals, axis=0, keepdims=True)
        idx = jnp.min(jnp.where(vals == m, e_iota, float(N_EXPERTS)), axis=0, keepdims=True)
        sel = e_iota == idx
        vals = jnp.where(sel, -jnp.inf, vals)
        tops.append(m)
        sels.append(sel)
        idxs.append(idx)
    ex = [jnp.exp(tv - tops[0]) for tv in tops]
    den = ex[0] + ex[1] + ex[2] + ex[3]
    gates = [v / den for v in ex]

    hot = jnp.where(sels[0] | sels[1] | sels[2] | sels[3], 1.0, 0.0)
    r_t = lax.broadcasted_iota(I32, (tm, tm), 0)
    c_t = lax.broadcasted_iota(I32, (tm, tm), 1)
    upper = jnp.where(r_t < c_t, 1.0, 0.0).astype(BF16)
    excl = _dot(hot.astype(BF16), upper) + carry_scr[:, 0:1]
    carry_scr[...] = carry_scr[...] + jnp.sum(hot, axis=1, keepdims=True)
    cnt_ref[...] = carry_scr[...]

    ranks = [jnp.sum(jnp.where(s, excl, 0.0), axis=0, keepdims=True) for s in sels]
    idx_ref[...] = jnp.concatenate(idxs, axis=0).astype(I32)
    rank_ref[...] = jnp.concatenate(ranks, axis=0).astype(I32)
    gpad = jnp.concatenate(gates + [jnp.zeros((LANES - TOP_K, tm), F32)], axis=0)
    gcol_ref[...] = gpad.T


def _mixout(x2, oa, od, proj, woa, wod, ada3, ffn_g, wr_t, br_col, seq):
    t, d = x2.shape
    tm = min(512, seq)
    per_b = seq // tm
    hw = oa.shape[1]
    row = lambda w: pl.BlockSpec((tm, w), lambda i: (i, 0))
    mod = lambda c: pl.BlockSpec((1, 1, d), lambda i, c=c: (i // per_b, 0, c))
    full = lambda a: pl.BlockSpec(a.shape, lambda i: (0,) * a.ndim)
    return pl.pallas_call(
        _mixout_body,
        grid=(t // tm,),
        in_specs=[
            row(d), row(hw), row(hw),
            pl.BlockSpec((tm, d), lambda i: (i, 0)),
            pl.BlockSpec((tm, d), lambda i: (i, 1)),
            full(woa), full(wod),
            mod(2),
            full(ffn_g), mod(3), mod(4),
            full(wr_t), full(br_col),
        ],
        out_specs=[
            row(d), row(d),
            pl.BlockSpec((TOP_K, tm), lambda i: (0, i)),
            pl.BlockSpec((TOP_K, tm), lambda i: (0, i)),
            pl.BlockSpec((tm, LANES), lambda i: (i, 0)),
            pl.BlockSpec((N_EXPERTS, LANES), lambda i: (0, 0)),
        ],
        out_shape=[
            jax.ShapeDtypeStruct((t, d), F32),
            jax.ShapeDtypeStruct((t, d), F32),
            jax.ShapeDtypeStruct((TOP_K, t), I32),
            jax.ShapeDtypeStruct((TOP_K, t), I32),
            jax.ShapeDtypeStruct((t, LANES), F32),
            jax.ShapeDtypeStruct((N_EXPERTS, LANES), F32),
        ],
        scratch_shapes=[pltpu.VMEM((N_EXPERTS, LANES), F32)],
        compiler_params=pltpu.CompilerParams(
            dimension_semantics=("arbitrary",), vmem_limit_bytes=VMEM_LIMIT),
        name="mixout",
    )(x2, oa, od, proj, proj, woa, wod, ada3, ffn_g, ada3, ada3, wr_t, br_col)


def _ffn_body(be_ref, nv_ref, nact_ref, asg_hbm, h2_hbm, w1_ref, b1_ref, w2_ref, b2_ref, y_hbm,
              asg_smem, xbuf, ybuf, w1c, w2c, asem, gsem, ssem, *, n_tok):
    j = pl.program_id(0)
    nact = nact_ref[0]
    bm = FFN_BLOCK
    ff = w2_ref.shape[1]
    pair = 2 * LANES
    ngroup = (2 * ff) // pair

    @pl.when((j == 0) | (be_ref[j] != be_ref[jnp.maximum(j - 1, 0)]))
    def _():
        r_p = lax.broadcasted_iota(I32, (pair, pair), 0)
        c_p = lax.broadcasted_iota(I32, (pair, pair), 1)
        src = jnp.where(c_p < LANES, 2 * c_p, 2 * (c_p - LANES) + 1)
        perm = jnp.where(r_p == src, 1.0, 0.0).astype(BF16)
        for g in range(ngroup):
            cols = slice(g * pair, (g + 1) * pair)
            w1c[:, cols] = _dot(w1_ref[0, :, cols].astype(BF16), perm).astype(BF16)
        w2c[...] = w2_ref[0].astype(BF16)

    def asg_copy(blk, slot):
        return pltpu.make_async_copy(asg_hbm.at[blk], asg_smem.at[slot], asem.at[slot])

    def token_of(a):
        if n_tok & (n_tok - 1) == 0:
            return a & (n_tok - 1)
        return lax.rem(jnp.maximum(a, 0), n_tok)

    def start_gather(blk):
        slot3, slot2 = lax.rem(blk, 3), lax.rem(blk, 2)
        ids, dst, sem = asg_smem.at[slot3], xbuf.at[slot2], gsem.at[slot2]
        for r in range(bm):
            tok = token_of(ids[0, r])
            pltpu.make_async_copy(h2_hbm.at[pl.ds(tok, 1), :], dst.at[pl.ds(r, 1), :], sem).start()

    def wait_gather(slot2):
        pltpu.make_async_copy(h2_hbm.at[pl.ds(0, bm), :], xbuf.at[slot2], gsem.at[slot2]).wait()

    def scatter_row(slot3, slot2, r):
        a = asg_smem[slot3, 0, r]
        return pltpu.make_async_copy(ybuf.at[slot2, pl.ds(r, 1), :], y_hbm.at[pl.ds(a, 1), :], ssem.at[slot2])

    def start_scatter(blk):
        slot3, slot2, nv = lax.rem(blk, 3), lax.rem(blk, 2), nv_ref[blk]
        ids, src, sem = asg_smem.at[slot3], ybuf.at[slot2], ssem.at[slot2]
        for g in range(bm // 8):
            @pl.when((g + 1) * 8 <= nv)
            def _():
                for r in range(g * 8, (g + 1) * 8):
                    pltpu.make_async_copy(src.at[pl.ds(r, 1), :], y_hbm.at[pl.ds(ids[0, r], 1), :], sem).start()

        def tail(r, c):
            scatter_row(slot3, slot2, r).start()
            return c

        lax.fori_loop((nv >> 3) << 3, nv, tail, 0)

    def wait_scatter(blk):
        slot2, nv = lax.rem(blk, 2), nv_ref[blk]
        nv8 = pl.multiple_of((nv >> 3) << 3, 8)

        @pl.when(nv8 > 0)
        def _():
            pltpu.make_async_copy(ybuf.at[slot2, pl.ds(0, nv8), :], y_hbm.at[pl.ds(0, nv8), :], ssem.at[slot2]).wait()

        def one(r, c):
            pltpu.make_async_copy(ybuf.at[slot2, pl.ds(0, 1), :], y_hbm.at[pl.ds(0, 1), :], ssem.at[slot2]).wait()
            return c

        lax.fori_loop(0, nv - nv8, one, 0)

    @pl.when(j == 0)
    def _():
        asg_copy(0, 0).start()
        asg_copy(0, 0).wait()
        start_gather(0)

        @pl.when(nact > 1)
        def _():
            asg_copy(1, 1).start()

    @pl.when(j < nact)
    def _():
        slot3, slot2 = lax.rem(j, 3), lax.rem(j, 2)

        @pl.when(j + 1 < nact)
        def _():
            nslot = lax.rem(j + 1, 3)
            asg_copy(j + 1, nslot).wait()
            start_gather(j + 1)

        @pl.when(j + 2 < nact)
        def _():
            asg_copy(j + 2, lax.rem(j + 2, 3)).start()

        wait_gather(slot2)
        xb = xbuf[slot2].astype(BF16)
        u = _dot(xb, w1c[...]) + b1_ref[0]
        acts = []
        for g in range(ngroup):
            glu = jnp.minimum(u[:, g * pair:g * pair + LANES], SWIGLU_LIMIT)
            lin = jnp.clip(u[:, g * pair + LANES:(g + 1) * pair], -SWIGLU_LIMIT, SWIGLU_LIMIT)
            acts.append((glu * _sigmoid(SWIGLU_ALPHA * glu) * (lin + 1.0)).astype(BF16))
        act = jnp.concatenate(acts, axis=1)
        yb = _dot(act, w2c[...]) + b2_ref[0]

        @pl.when(j >= 2)
        def _():
            wait_scatter(j - 2)

        ybuf[slot2] = yb
        start_scatter(j)

        @pl.when(j == nact - 1)
        def _():
            wait_scatter(j)

            @pl.when(j >= 1)
            def _():
                wait_scatter(j - 1)


def _ffn(block_expert, nvalid, nact, asg, h2, w1p, b1p, w2b, b2, n_asg):
    n_blocks = asg.shape[0]
    bm = FFN_BLOCK
    t, d = h2.shape
    ff2 = w1p.shape[2]
    ff = w2b.shape[1]
    grid_spec = pltpu.PrefetchScalarGridSpec(
        num_scalar_prefetch=3,
        grid=(n_blocks,),
        in_specs=[
            pl.BlockSpec(memory_space=pl.ANY),
            pl.BlockSpec(memory_space=pl.ANY),
            pl.BlockSpec((1, d, ff2), lambda j, be, nv, na: (be[j], 0, 0)),
            pl.BlockSpec((1, 1, ff2), lambda j, be, nv, na: (be[j], 0, 0)),
            pl.BlockSpec((1, ff, d), lambda j, be, nv, na: (be[j], 0, 0)),
            pl.BlockSpec((1, 1, d), lambda j, be, nv, na: (be[j], 0, 0)),
        ],
        out_specs=pl.BlockSpec(memory_space=pl.ANY),
        scratch_shapes=[
            pltpu.SMEM((3, 1, bm), I32),
            pltpu.VMEM((2, bm, d), F32),
            pltpu.VMEM((2, bm, d), F32),
            pltpu.VMEM((d, ff2), BF16),
            pltpu.VMEM((ff, d), BF16),
            pltpu.SemaphoreType.DMA((3,)),
            pltpu.SemaphoreType.DMA((2,)),
            pltpu.SemaphoreType.DMA((2,)),
        ],
    )
    return pl.pallas_call(
        functools.partial(_ffn_body, n_tok=t),
        grid_spec=grid_spec,
        out_shape=jax.ShapeDtypeStruct((n_asg, d), F32),
        compiler_params=pltpu.CompilerParams(
            dimension_semantics=("arbitrary",), vmem_limit_bytes=VMEM_LIMIT),
        name="ffn",
    )(block_expert, nvalid, nact, asg, h2, w1p, b1p, w2b, b2)


def _combine_body(x1_ref, y0_ref, y1_ref, y2_ref, y3_ref, gcol_ref, g2_ref, o_ref):
    gc = gcol_ref[...]
    m = (gc[:, 0:1] * y0_ref[...] + gc[:, 1:2] * y1_ref[...]
         + gc[:, 2:3] * y2_ref[...] + gc[:, 3:4] * y3_ref[...])
    o_ref[...] = (x1_ref[...] + g2_ref[0] * m).astype(o_ref.dtype)


def _combine(x1, yall, gcol, ada3, seq, out_dtype):
    t, d = x1.shape
    tm = min(512, seq)
    per_b = seq // tm
    nt = t // tm
    yk = lambda k: pl.BlockSpec((tm, d), lambda i, k=k: (k * nt + i, 0))
    return pl.pallas_call(
        _combine_body,
        grid=(nt,),
        in_specs=[
            pl.BlockSpec((tm, d), lambda i: (i, 0)),
            yk(0), yk(1), yk(2), yk(3),
            pl.BlockSpec((tm, LANES), lambda i: (i, 0)),
            pl.BlockSpec((1, 1, d), lambda i: (i // per_b, 0, 5)),
        ],
        out_specs=pl.BlockSpec((tm, d), lambda i: (i, 0)),
        out_shape=jax.ShapeDtypeStruct((t, d), out_dtype),
        compiler_params=pltpu.CompilerParams(
            dimension_semantics=("arbitrary",), vmem_limit_bytes=VMEM_LIMIT),
        name="combine",
    )(x1, yall, yall, yall, yall, gcol, ada3)


def _route_tables(idx, rank, counts, n_tok):
    bm = FFN_BLOCK
    n_asg = TOP_K * n_tok
    n_blocks = -(-(n_asg + N_EXPERTS * (bm - 1)) // bm)
    padded = (counts + bm - 1) // bm * bm
    pad_ends = jnp.cumsum(padded)
    pad_starts = pad_ends - padded
    e_ids = jnp.arange(N_EXPERTS, dtype=I32)
    start_of = jnp.sum(jnp.where(idx[None] == e_ids[:, None, None], pad_starts[:, None, None], 0), axis=0)
    dest = (start_of + rank).reshape(-1)
    asg = jnp.full((n_blocks * bm,), -1, I32).at[dest].set(jnp.arange(n_asg, dtype=I32))
    nact = (pad_ends[-1] // bm).astype(I32)
    blk_start = jnp.arange(n_blocks, dtype=I32) * bm
    active = blk_start < pad_ends[-1]
    last = jnp.sum(jnp.where(pad_ends <= pad_ends[-1] - 1, 1, 0)).astype(I32)
    be = jnp.sum(jnp.where(pad_ends[None, :] <= blk_start[:, None], 1, 0), axis=1).astype(I32)
    be = jnp.where(active, be, last)
    mine = be[:, None] == e_ids[None, :]
    cnt_b = jnp.sum(jnp.where(mine, counts[None, :], 0), axis=1)
    start_b = jnp.sum(jnp.where(mine, pad_starts[None, :], 0), axis=1)
    nvalid = jnp.where(active, jnp.clip(cnt_b - (blk_start - start_b), 0, bm), 0).astype(I32)
    return be, nvalid, nact.reshape(1), asg.reshape(n_blocks, 1, bm), n_asg


def kernel(x, c, w_ada, b_ada, mix_norm_g, ffn_norm_g, w_in, hg_lower_bound_logits, hg_out_norm_g, da_q_norm_g, da_k_norm_g, da_lambda_q1, da_lambda_k1, da_lambda_q2, da_lambda_k2, da_subln_g, w_out, w_router, b_router, w1, b1, w2, b2):
    bsz, seq, d = x.shape
    t = bsz * seq
    depth = w_ada.shape[0]
    out_dtype = x.dtype
    hw = HG_HEADS * HG_DV
    xcur = x.reshape(t, d)
    for l in range(depth):
        ada = _ada(c, w_ada[l], b_ada[l])
        ada3 = ada.reshape(bsz, 1, N_MOD * d)
        wi = w_in[l]
        n_in = wi.shape[1]
        w_in_r = jnp.concatenate([wi[:, n_in - 2 * d:], wi[:, :n_in - 2 * d]], axis=1).astype(BF16)
        col_h = (2 * d) // LANES
        col_a = col_h + 4 * HG_HEADS
        proj = _inproj(xcur, mix_norm_g[l].reshape(1, d), ada3, w_in_r, seq)

        o_a = _hgrn(proj, hg_lower_bound_logits, hg_out_norm_g[l].reshape(1, HG_DV), bsz, seq, col_h, l)
        lambda_init = 0.8 - 0.6 * math.exp(-0.3 * l)
        qg2 = jnp.tile(da_q_norm_g[l], 2).reshape(1, 2 * DA_DH)
        kg2 = jnp.tile(da_k_norm_g[l], 2).reshape(1, 2 * DA_DH)
        lam4 = jnp.stack([da_lambda_q1[l], da_lambda_k1[l], da_lambda_q2[l], da_lambda_k2[l]])
        o_d = _attn(proj, qg2, kg2, lam4, da_subln_g[l].reshape(1, 2 * DA_DH), bsz, seq, col_a, lambda_init)

        wo = w_out[l].astype(BF16)
        x1, h2, idx, rank, gcol, cnt = _mixout(
            xcur, o_a, o_d, proj, wo[:hw], wo[hw:], ada3, ffn_norm_g[l].reshape(1, d),
            w_router[l].T, b_router[l].reshape(N_EXPERTS, 1), seq)

        counts = cnt[:, 0].astype(I32)
        be, nvalid, nact, asg, n_asg = _route_tables(idx, rank, counts, t)
        b1p = b1[l].reshape(N_EXPERTS, -1, LANES, 2).transpose(0, 1, 3, 2).reshape(N_EXPERTS, 1, -1)
        yall = _ffn(be, nvalid, nact, asg, h2, w1[l], b1p, w2[l], b2[l].reshape(N_EXPERTS, 1, d), n_asg)
        xcur = _combine(x1, yall, gcol, ada3, seq, out_dtype)
    return xcur.reshape(bsz, seq, d)
```

```python
import functools
import math

import jax
import jax.numpy as jnp
from jax import lax
from jax.experimental import pallas as pl
from jax.experimental.pallas import tpu as pltpu

F32 = jnp.float32
BF16 = jnp.bfloat16
I32 = jnp.int32

HG_HEADS = 4
HG_DK = 128
HG_DV = 128
HG_CHUNK = 32
DA_HEADS = 4
DA_DH = 64
N_EXPERTS = 32
TOP_K = 4
SWIGLU_ALPHA = 1.702
SWIGLU_LIMIT = 7.0
NORM_EPS = 1e-6
LOG2E = math.log2(math.e)
N_MOD = 6

LANES = 128
VMEM_LIMIT = 56 * 1024 * 1024

HG_ROWS = 256
ATTN_TILE = 256
FFN_BLOCK = 256


def _sigmoid(x):
    return 1.0 / (1.0 + jnp.exp(-x))


def _dot(a, b):
    return jnp.dot(a, b, preferred_element_type=F32)


def _dot_nt(a, b):
    return lax.dot_general(a, b, (((1,), (1,)), ((), ())), preferred_element_type=F32)


def _split_bf16(x):
    hi = x.astype(BF16)
    lo = (x - hi.astype(F32)).astype(BF16)
    return hi, lo


def _ada_body(c_ref, w_ref, b_ref, o_ref):
    c = c_ref[...].astype(F32)
    ca = c * _sigmoid(c)
    o_ref[...] = jnp.dot(ca, w_ref[...], preferred_element_type=F32,
                         precision=lax.Precision.HIGHEST) + b_ref[...]


def _ada(c, w_ada, b_ada):
    bsz, d = c.shape
    n = w_ada.shape[1]
    tn = d
    return pl.pallas_call(
        _ada_body,
        grid=(n // tn,),
        in_specs=[
            pl.BlockSpec((bsz, d), lambda j: (0, 0)),
            pl.BlockSpec((d, tn), lambda j: (0, j)),
            pl.BlockSpec((1, tn), lambda j: (0, j)),
        ],
        out_specs=pl.BlockSpec((bsz, tn), lambda j: (0, j)),
        out_shape=jax.ShapeDtypeStruct((bsz, n), F32),
        name="ada",
    )(c, w_ada, b_ada.reshape(1, n))


def _norm_mod(x, g, shift, scale):
    ms = jnp.mean(x * x, axis=-1, keepdims=True)
    return (x * lax.rsqrt(ms + NORM_EPS) * g) * (1.0 + scale) + shift


def _inproj_body(x_ref, g_ref, sh_ref, sc_ref, w_ref, o_ref):
    h = _norm_mod(x_ref[...], g_ref[...], sh_ref[0], sc_ref[0])
    o_ref[...] = _dot(h.astype(BF16), w_ref[...]).astype(BF16)


def _inproj(x2, g, ada3, w_bf16, seq):
    t, d = x2.shape
    n = w_bf16.shape[1]
    tm = min(512, seq)
    nj = 2
    tn = n // nj
    per_b = seq // tm
    return pl.pallas_call(
        _inproj_body,
        grid=(nj, t // tm),
        in_specs=[
            pl.BlockSpec((tm, d), lambda j, i: (i, 0)),
            pl.BlockSpec((1, d), lambda j, i: (0, 0)),
            pl.BlockSpec((1, 1, d), lambda j, i: (i // per_b, 0, 0)),
            pl.BlockSpec((1, 1, d), lambda j, i: (i // per_b, 0, 1)),
            pl.BlockSpec((d, tn), lambda j, i: (0, j)),
        ],
        out_specs=pl.BlockSpec((tm, tn), lambda j, i: (i, j)),
        out_shape=jax.ShapeDtypeStruct((t, n), BF16),
        compiler_params=pltpu.CompilerParams(
            dimension_semantics=("arbitrary", "arbitrary"), vmem_limit_bytes=VMEM_LIMIT),
        name="inproj",
    )(x2, g, ada3, ada3, w_bf16)


def _hgrn_body(q_ref, f_ref, i_ref, og_ref, lbl_ref, g_ref, o_ref, *, seq, layer):
    rows, chunk = HG_ROWS, HG_CHUNK
    nchunk = rows // chunk
    lbl = lbl_ref[...].astype(F32)
    e = jnp.exp(lbl - jnp.max(lbl, axis=0, keepdims=True))
    lb = jnp.sum(e[: layer + 1], axis=0, keepdims=True) / jnp.sum(e, axis=0, keepdims=True)
    r_i = lax.broadcasted_iota(I32, (rows, rows), 0)
    c_i = lax.broadcasted_iota(I32, (rows, rows), 1)
    tri = ((r_i // chunk) == (c_i // chunk)) & (r_i >= c_i)
    tri_b = jnp.where(tri, 1.0, 0.0).astype(BF16)
    lane_chunk = lax.broadcasted_iota(I32, (HG_DK, rows), 1) // chunk
    row_chunk = lax.broadcasted_iota(I32, (rows, HG_DK), 0) // chunk
    g = g_ref[...].astype(F32)

    def block(r, st):
        sl = pl.ds(r * rows, rows)
        qr = q_ref[sl, :].astype(F32)
        fr = f_ref[sl, :].astype(F32)
        v = i_ref[sl, :].astype(F32)
        og = og_ref[sl, :].astype(F32)
        q = qr * _sigmoid(qr)
        f = lb + (1.0 - lb) * _sigmoid(fr)
        k = 1.0 - f
        logf = jnp.log(f)
        lhi, llo = _split_bf16(logf)
        bc2 = _dot(tri_b, jnp.concatenate([lhi, llo], axis=1))
        bcum = bc2[:, :HG_DK] + bc2[:, HG_DK:]
        b3 = bcum.reshape(nchunk, chunk, HG_DK)
        bl = b3[:, chunk - 1:chunk, :]
        dec = jnp.exp(bl)
        kt_f = k * jnp.exp(-bcum)
        qt_f = q * jnp.exp(bcum)
        kd = (kt_f.reshape(nchunk, chunk, HG_DK) * dec).reshape(rows, HG_DK)
        vb = v.astype(BF16)
        a = jnp.where(tri, _dot_nt(qt_f.astype(BF16), kt_f.astype(BF16)), 0.0).astype(BF16)
        kd_t = kd.T
        kd_x = jnp.concatenate([jnp.where(lane_chunk == c, kd_t, 0.0) for c in range(nchunk)], axis=0)
        kv_all = _dot(kd_x.astype(BF16), vb)
        dec_t = jnp.concatenate([dec.reshape(nchunk, HG_DK), jnp.zeros((HG_DK - nchunk, HG_DK), F32)], axis=0).T
        starts = []
        for c in range(nchunk):
            starts.append(st.astype(BF16))
            st = st * dec_t[:, c:c + 1] + kv_all[c * HG_DK:(c + 1) * HG_DK, :]
        q_x = [jnp.where(row_chunk == c, qt_f, 0.0).astype(BF16) for c in range(nchunk)]
        o = _dot(jnp.concatenate([a] + q_x, axis=1), jnp.concatenate([vb] + starts, axis=0))
        ms = jnp.mean(o * o, axis=-1, keepdims=True)
        o = o * lax.rsqrt(ms + NORM_EPS) * g
        o_ref[sl, :] = (o * (og * _sigmoid(og))).astype(o_ref.dtype)
        return st

    st = jnp.zeros((HG_DK, HG_DV), F32)
    for r in range(seq // rows):
        st = block(r, st)


def _hgrn(proj, lb_logits, norm_g, bsz, seq, col0, layer):
    t = proj.shape[0]
    blk = lambda off: pl.BlockSpec((seq, LANES), lambda b, h, off=off: (b, col0 + off + h))
    return pl.pallas_call(
        functools.partial(_hgrn_body, seq=seq, layer=layer),
        grid=(bsz, HG_HEADS),
        in_specs=[
            blk(0), blk(HG_HEADS), blk(2 * HG_HEADS), blk(3 * HG_HEADS),
            pl.BlockSpec((lb_logits.shape[0], HG_DK), lambda b, h: (0, h)),
            pl.BlockSpec((1, HG_DV), lambda b, h: (0, 0)),
        ],
        out_specs=pl.BlockSpec((seq, HG_DV), lambda b, h: (b, h)),
        out_shape=jax.ShapeDtypeStruct((t, HG_HEADS * HG_DV), BF16),
        compiler_params=pltpu.CompilerParams(
            dimension_semantics=("arbitrary", "arbitrary"), vmem_limit_bytes=VMEM_LIMIT),
        name="hgrn",
    )(proj, proj, proj, proj, lb_logits, norm_g)


def _group_norm(x, gsum_b, gain):
    hi, lo = _split_bf16(x * x)
    ss = _dot(hi, gsum_b) + _dot(lo, gsum_b)
    return x * lax.rsqrt(ss * (1.0 / DA_DH) + NORM_EPS) * gain


def _attn_body(q_ref, k_ref, v_ref, qg_ref, kg_ref, lam_ref, sg_ref, o_ref, kn_scr, v1_scr, *, seq, lambda_init):
    tq = ATTN_TILE
    qi = pl.program_id(2)
    width = 2 * DA_DH
    r_l = lax.broadcasted_iota(I32, (width, width), 0) // DA_DH
    c_l = lax.broadcasted_iota(I32, (width, width), 1) // DA_DH
    gsum_b = jnp.where(r_l == c_l, 1.0, 0.0).astype(BF16)
    lane = lax.broadcasted_iota(I32, (1, width), 1)

    @pl.when(qi == 0)
    def _():
        kg = kg_ref[...].astype(F32)

        def kblock(r, carry):
            sl = pl.ds(pl.multiple_of(r * tq, tq), tq)
            kn_scr[sl, :] = _group_norm(k_ref[sl, :].astype(F32), gsum_b, kg).astype(BF16)
            v1_scr[sl, :] = jnp.concatenate([v_ref[sl, :], jnp.ones((tq, width), BF16)], axis=1)
            return carry

        lax.fori_loop(0, seq // tq, kblock, 0)

    qn = _group_norm(q_ref[...].astype(F32), gsum_b, qg_ref[...].astype(F32)) * (DA_DH ** -0.5 * LOG2E)
    q1 = jnp.where(lane < DA_DH, qn, 0.0).astype(BF16)
    q2 = jnp.where(lane >= DA_DH, qn, 0.0).astype(BF16)
    row = lax.broadcasted_iota(I32, (tq, tq), 0)
    col = lax.broadcasted_iota(I32, (tq, tq), 1)
    keep = row >= col

    lam_v = lam_ref[...].astype(F32)
    lam = (jnp.exp(jnp.sum(lam_v[0:1] * lam_v[1:2], axis=-1, keepdims=True))
           - jnp.exp(jnp.sum(lam_v[2:3] * lam_v[3:4], axis=-1, keepdims=True)) + lambda_init)

    def softmax_v(qc, nk):
        s = _dot_nt(qc, kn_scr[0:nk, :])
        diag = jnp.where(keep, s[:, nk - tq:], -jnp.inf)
        s = diag if nk == tq else jnp.concatenate([s[:, :nk - tq], diag], axis=1)
        m = jnp.max(s, axis=-1, keepdims=True)
        return _dot(jnp.exp2(s - m).astype(BF16), v1_scr[0:nk, :])

    def tile(nblk):
        nk = nblk * tq
        a1 = softmax_v(q1, nk)
        a2 = softmax_v(q2, nk)
        o = a1[:, :width] / a1[:, width:width + 1] - lam * (a2[:, :width] / a2[:, width:width + 1])
        ms = jnp.mean(o * o, axis=-1, keepdims=True)
        o = o * lax.rsqrt(ms + NORM_EPS) * sg_ref[...].astype(F32) * (1.0 - lambda_init)
        o_ref[...] = o.astype(o_ref.dtype)

    for i in range(seq // tq):
        pl.when(qi == i)(functools.partial(tile, i + 1))


def _attn(proj, qg2, kg2, lam4, subln_g, bsz, seq, col0, lambda_init):
    t = proj.shape[0]
    tq = ATTN_TILE
    nq = seq // tq
    width = 2 * DA_DH
    return pl.pallas_call(
        functools.partial(_attn_body, seq=seq, lambda_init=lambda_init),
        grid=(bsz, DA_HEADS, nq),
        in_specs=[
            pl.BlockSpec((tq, width), lambda b, h, i: (b * nq + i, col0 + h)),
            pl.BlockSpec((seq, width), lambda b, h, i: (b, col0 + DA_HEADS + h)),
            pl.BlockSpec((seq, width), lambda b, h, i: (b, col0 + 2 * DA_HEADS + h)),
            pl.BlockSpec((1, width), lambda b, h, i: (0, 0)),
            pl.BlockSpec((1, width), lambda b, h, i: (0, 0)),
            pl.BlockSpec((4, DA_DH), lambda b, h, i: (0, 0)),
            pl.BlockSpec((1, width), lambda b, h, i: (0, 0)),
        ],
        out_specs=pl.BlockSpec((tq, width), lambda b, h, i: (b * nq + i, h)),
        out_shape=jax.ShapeDtypeStruct((t, DA_HEADS * width), BF16),
        scratch_shapes=[pltpu.VMEM((seq, width), BF16), pltpu.VMEM((seq, 2 * width), BF16)],
        compiler_params=pltpu.CompilerParams(
            dimension_semantics=("arbitrary", "arbitrary", "arbitrary"), vmem_limit_bytes=VMEM_LIMIT),
        name="attn",
    )(proj, proj, proj, qg2, kg2, lam4, subln_g)


def _mixout_body(x_ref, oa_ref, od_ref, ga_ref, gd_ref, woa_ref, wod_ref, g1_ref, g_ref, sh_ref, sc_ref,
                 wr_ref, br_ref,
                 x1_ref, h2_ref, idx_ref, rank_ref, gcol_ref, cnt_ref, carry_scr):
    i = pl.program_id(0)
    tm = x_ref.shape[0]

    @pl.when(i == 0)
    def _():
        carry_scr[...] = jnp.zeros_like(carry_scr)

    ya = _dot(oa_ref[...], woa_ref[...])
    yd = _dot(od_ref[...], wod_ref[...])
    y = _sigmoid(ga_ref[...].astype(F32)) * ya + _sigmoid(gd_ref[...].astype(F32)) * yd
    x1 = x_ref[...] + g1_ref[0] * y
    x1_ref[...] = x1
    h2 = _norm_mod(x1, g_ref[...], sh_ref[0], sc_ref[0])
    h2_ref[...] = h2

    hh, hl = _split_bf16(h2)
    wh, wl = _split_bf16(wr_ref[...])
    logits = _dot_nt(wh, hh) + _dot_nt(wl, hh) + _dot_nt(wh, hl) + br_ref[...]

    e_iota = lax.broadcasted_iota(I32, (N_EXPERTS, tm), 0).astype(F32)
    vals = logits
    tops, sels, idxs = [], [], []
    for _ in range(TOP_K):
        m = jnp.max(vals, axis=0, keepdims=True)
        idx = jnp.min(jnp.where(vals == m, e_iota, float(N_EXPERTS)), axis=0, keepdims=True)
        sel = e_iota == idx
        vals = jnp.where(sel, -jnp.inf, vals)
        tops.append(m)
        sels.append(sel)
        idxs.append(idx)
    ex = [jnp.exp(tv - tops[0]) for tv in tops]
    den = ex[0] + ex[1] + ex[2] + ex[3]
    gates = [v / den for v in ex]

    hot = jnp.where(sels[0] | sels[1] | sels[2] | sels[3], 1.0, 0.0)
    r_t = lax.broadcasted_iota(I32, (tm, tm), 0)
    c_t = lax.broadcasted_iota(I32, (tm, tm), 1)
    upper = jnp.where(r_t < c_t, 1.0, 0.0).astype(BF16)
    excl = _dot(hot.astype(BF16), upper) + carry_scr[:, 0:1]
    carry_scr[...] = carry_scr[...] + jnp.sum(hot, axis=1, keepdims=True)
    cnt_ref[...] = carry_scr[...]

    ranks = [jnp.sum(jnp.where(s, excl, 0.0), axis=0, keepdims=True) for s in sels]
    idx_ref[...] = jnp.concatenate(idxs, axis=0).astype(I32)
    rank_ref[...] = jnp.concatenate(ranks, axis=0).astype(I32)
    gpad = jnp.concatenate(gates + [jnp.zeros((LANES - TOP_K, tm), F32)], axis=0)
    gcol_ref[...] = gpad.T


def _mixout(x2, oa, od, proj, woa, wod, ada3, ffn_g, wr_t, br_col, seq):
    t, d = x2.shape
    tm = min(512, seq)
    per_b = seq // tm
    hw = oa.shape[1]
    row = lambda w: pl.BlockSpec((tm, w), lambda i: (i, 0))
    mod = lambda c: pl.BlockSpec((1, 1, d), lambda i, c=c: (i // per_b, 0, c))
    full = lambda a: pl.BlockSpec(a.shape, lambda i: (0,) * a.ndim)
    return pl.pallas_call(
        _mixout_body,
        grid=(t // tm,),
        in_specs=[
            row(d), row(hw), row(hw),
            pl.BlockSpec((tm, d), lambda i: (i, 0)),
            pl.BlockSpec((tm, d), lambda i: (i, 1)),
            full(woa), full(wod),
            mod(2),
            full(ffn_g), mod(3), mod(4),
            full(wr_t), full(br_col),
        ],
        out_specs=[
            row(d), row(d),
            pl.BlockSpec((TOP_K, tm), lambda i: (0, i)),
            pl.BlockSpec((TOP_K, tm), lambda i: (0, i)),
            pl.BlockSpec((tm, LANES), lambda i: (i, 0)),
            pl.BlockSpec((N_EXPERTS, LANES), lambda i: (0, 0)),
        ],
        out_shape=[
            jax.ShapeDtypeStruct((t, d), F32),
            jax.ShapeDtypeStruct((t, d), F32),
            jax.ShapeDtypeStruct((TOP_K, t), I32),
            jax.ShapeDtypeStruct((TOP_K, t), I32),
            jax.ShapeDtypeStruct((t, LANES), F32),
            jax.ShapeDtypeStruct((N_EXPERTS, LANES), F32),
        ],
        scratch_shapes=[pltpu.VMEM((N_EXPERTS, LANES), F32)],
        compiler_params=pltpu.CompilerParams(
            dimension_semantics=("arbitrary",), vmem_limit_bytes=VMEM_LIMIT),
        name="mixout",
    )(x2, oa, od, proj, proj, woa, wod, ada3, ffn_g, ada3, ada3, wr_t, br_col)


def _ffn_body(be_ref, nv_ref, nact_ref, asg_hbm, h2_hbm, w1_ref, b1_ref, w2_ref, b2_ref, y_hbm,
              asg_smem, xbuf, ybuf, w1c, w2c, asem, gsem, ssem, *, n_tok):
    j = pl.program_id(0)
    nact = nact_ref[0]
    bm = FFN_BLOCK
    ff = w2_ref.shape[1]
    pair = 2 * LANES
    ngroup = (2 * ff) // pair

    @pl.when((j == 0) | (be_ref[j] != be_ref[jnp.maximum(j - 1, 0)]))
    def _():
        r_p = lax.broadcasted_iota(I32, (pair, pair), 0)
        c_p = lax.broadcasted_iota(I32, (pair, pair), 1)
        src = jnp.where(c_p < LANES, 2 * c_p, 2 * (c_p - LANES) + 1)
        perm = jnp.where(r_p == src, 1.0, 0.0).astype(BF16)
        for g in range(ngroup):
            cols = slice(g * pair, (g + 1) * pair)
            w1c[:, cols] = _dot(w1_ref[0, :, cols].astype(BF16), perm).astype(BF16)
        w2c[...] = w2_ref[0].astype(BF16)

    def asg_copy(blk, slot):
        return pltpu.make_async_copy(asg_hbm.at[blk], asg_smem.at[slot], asem.at[slot])

    def token_of(a):
        if n_tok & (n_tok - 1) == 0:
            return a & (n_tok - 1)
        return lax.rem(jnp.maximum(a, 0), n_tok)

    def start_gather(blk):
        slot3, slot2 = lax.rem(blk, 3), lax.rem(blk, 2)
        ids, dst, sem = asg_smem.at[slot3], xbuf.at[slot2], gsem.at[slot2]
        for r in range(bm):
            tok = token_of(ids[0, r])
            pltpu.make_async_copy(h2_hbm.at[pl.ds(tok, 1), :], dst.at[pl.ds(r, 1), :], sem).start()

    def wait_gather(slot2):
        pltpu.make_async_copy(h2_hbm.at[pl.ds(0, bm), :], xbuf.at[slot2], gsem.at[slot2]).wait()

    def scatter_row(slot3, slot2, r):
        a = asg_smem[slot3, 0, r]
        return pltpu.make_async_copy(ybuf.at[slot2, pl.ds(r, 1), :], y_hbm.at[pl.ds(a, 1), :], ssem.at[slot2])

    def start_scatter(blk):
        slot3, slot2, nv = lax.rem(blk, 3), lax.rem(blk, 2), nv_ref[blk]
        ids, src, sem = asg_smem.at[slot3], ybuf.at[slot2], ssem.at[slot2]
        for g in range(bm // 8):
            @pl.when((g + 1) * 8 <= nv)
            def _():
                for r in range(g * 8, (g + 1) * 8):
                    pltpu.make_async_copy(src.at[pl.ds(r, 1), :], y_hbm.at[pl.ds(ids[0, r], 1), :], sem).start()

        def tail(r, c):
            scatter_row(slot3, slot2, r).start()
            return c

        lax.fori_loop((nv >> 3) << 3, nv, tail, 0)

    def wait_scatter(blk):
        slot2, nv = lax.rem(blk, 2), nv_ref[blk]
        nv8 = pl.multiple_of((nv >> 3) << 3, 8)

        @pl.when(nv8 > 0)
        def _():
            pltpu.make_async_copy(ybuf.at[slot2, pl.ds(0, nv8), :], y_hbm.at[pl.ds(0, nv8), :], ssem.at[slot2]).wait()

        def one(r, c):
            pltpu.make_async_copy(ybuf.at[slot2, pl.ds(0, 1), :], y_hbm.at[pl.ds(0, 1), :], ssem.at[slot2]).wait()
            return c

        lax.fori_loop(0, nv - nv8, one, 0)

    @pl.when(j == 0)
    def _():
        asg_copy(0, 0).start()
        asg_copy(0, 0).wait()
        start_gather(0)

        @pl.when(nact > 1)
        def _():
            asg_copy(1, 1).start()

    @pl.when(j < nact)
    def _():
        slot3, slot2 = lax.rem(j, 3), lax.rem(j, 2)

        @pl.when(j + 1 < nact)
        def _():
            nslot = lax.rem(j + 1, 3)
            asg_copy(j + 1, nslot).wait()
            start_gather(j + 1)

        @pl.when(j + 2 < nact)
        def _():
            asg_copy(j + 2, lax.rem(j + 2, 3)).start()

        wait_gather(slot2)
        xb = xbuf[slot2].astype(BF16)
        u = _dot(xb, w1c[...]) + b1_ref[0]
        acts = []
        for g in range(ngroup):
            glu = jnp.minimum(u[:, g * pair:g * pair + LANES], SWIGLU_LIMIT)
            lin = jnp.clip(u[:, g * pair + LANES:(g + 1) * pair], -SWIGLU_LIMIT, SWIGLU_LIMIT)
            acts.append((glu * _sigmoid(SWIGLU_ALPHA * glu) * (lin + 1.0)).astype(BF16))
        act = jnp.concatenate(acts, axis=1)
        yb = _dot(act, w2c[...]) + b2_ref[0]

        @pl.when(j >= 2)
        def _():
            wait_scatter(j - 2)

        ybuf[slot2] = yb
        start_scatter(j)

        @pl.when(j == nact - 1)
        def _():
            wait_scatter(j)

            @pl.when(j >= 1)
            def _():
                wait_scatter(j - 1)


def _ffn(block_expert, nvalid, nact, asg, h2, w1p, b1p, w2b, b2, n_asg):
    n_blocks = asg.shape[0]
    bm = FFN_BLOCK
    t, d = h2.shape
    ff2 = w1p.shape[2]
    ff = w2b.shape[1]
    grid_spec = pltpu.PrefetchScalarGridSpec(
        num_scalar_prefetch=3,
        grid=(n_blocks,),
        in_specs=[
            pl.BlockSpec(memory_space=pl.ANY),
            pl.BlockSpec(memory_space=pl.ANY),
            pl.BlockSpec((1, d, ff2), lambda j, be, nv, na: (be[j], 0, 0)),
            pl.BlockSpec((1, 1, ff2), lambda j, be, nv, na: (be[j], 0, 0)),
            pl.BlockSpec((1, ff, d), lambda j, be, nv, na: (be[j], 0, 0)),
            pl.BlockSpec((1, 1, d), lambda j, be, nv, na: (be[j], 0, 0)),
        ],
        out_specs=pl.BlockSpec(memory_space=pl.ANY),
        scratch_shapes=[
            pltpu.SMEM((3, 1, bm), I32),
            pltpu.VMEM((2, bm, d), F32),
            pltpu.VMEM((2, bm, d), F32),
            pltpu.VMEM((d, ff2), BF16),
            pltpu.VMEM((ff, d), BF16),
            pltpu.SemaphoreType.DMA((3,)),
            pltpu.SemaphoreType.DMA((2,)),
            pltpu.SemaphoreType.DMA((2,)),
        ],
    )
    return pl.pallas_call(
        functools.partial(_ffn_body, n_tok=t),
        grid_spec=grid_spec,
        out_shape=jax.ShapeDtypeStruct((n_asg, d), F32),
        compiler_params=pltpu.CompilerParams(
            dimension_semantics=("arbitrary",), vmem_limit_bytes=VMEM_LIMIT),
        name="ffn",
    )(block_expert, nvalid, nact, asg, h2, w1p, b1p, w2b, b2)


def _combine_body(x1_ref, y0_ref, y1_ref, y2_ref, y3_ref, gcol_ref, g2_ref, o_ref):
    gc = gcol_ref[...]
    m = (gc[:, 0:1] * y0_ref[...] + gc[:, 1:2] * y1_ref[...]
         + gc[:, 2:3] * y2_ref[...] + gc[:, 3:4] * y3_ref[...])
    o_ref[...] = (x1_ref[...] + g2_ref[0] * m).astype(o_ref.dtype)


def _combine(x1, yall, gcol, ada3, seq, out_dtype):
    t, d = x1.shape
    tm = min(512, seq)
    per_b = seq // tm
    nt = t // tm
    yk = lambda k: pl.BlockSpec((tm, d), lambda i, k=k: (k * nt + i, 0))
    return pl.pallas_call(
        _combine_body,
        grid=(nt,),
        in_specs=[
            pl.BlockSpec((tm, d), lambda i: (i, 0)),
            yk(0), yk(1), yk(2), yk(3),
            pl.BlockSpec((tm, LANES), lambda i: (i, 0)),
            pl.BlockSpec((1, 1, d), lambda i: (i // per_b, 0, 5)),
        ],
        out_specs=pl.BlockSpec((tm, d), lambda i: (i, 0)),
        out_shape=jax.ShapeDtypeStruct((t, d), out_dtype),
        compiler_params=pltpu.CompilerParams(
            dimension_semantics=("arbitrary",), vmem_limit_bytes=VMEM_LIMIT),
        name="combine",
    )(x1, yall, yall, yall, yall, gcol, ada3)


def _route_tables(idx, rank, counts, n_tok):
    bm = FFN_BLOCK
    n_asg = TOP_K * n_tok
    n_blocks = -(-(n_asg + N_EXPERTS * (bm - 1)) // bm)
    padded = (counts + bm - 1) // bm * bm
    pad_ends = jnp.cumsum(padded)
    pad_starts = pad_ends - padded
    e_ids = jnp.arange(N_EXPERTS, dtype=I32)
    start_of = jnp.sum(jnp.where(idx[None] == e_ids[:, None, None], pad_starts[:, None, None], 0), axis=0)
    dest = (start_of + rank).reshape(-1)
    asg = jnp.full((n_blocks * bm,), -1, I32).at[dest].set(jnp.arange(n_asg, dtype=I32))
    nact = (pad_ends[-1] // bm).astype(I32)
    blk_start = jnp.arange(n_blocks, dtype=I32) * bm
    active = blk_start < pad_ends[-1]
    last = jnp.sum(jnp.where(pad_ends <= pad_ends[-1] - 1, 1, 0)).astype(I32)
    be = jnp.sum(jnp.where(pad_ends[None, :] <= blk_start[:, None], 1, 0), axis=1).astype(I32)
    be = jnp.where(active, be, last)
    mine = be[:, None] == e_ids[None, :]
    cnt_b = jnp.sum(jnp.where(mine, counts[None, :], 0), axis=1)
    start_b = jnp.sum(jnp.where(mine, pad_starts[None, :], 0), axis=1)
    nvalid = jnp.where(active, jnp.clip(cnt_b - (blk_start - start_b), 0, bm), 0).astype(I32)
    return be, nvalid, nact.reshape(1), asg.reshape(n_blocks, 1, bm), n_asg


def kernel(x, c, w_ada, b_ada, mix_norm_g, ffn_norm_g, w_in, hg_lower_bound_logits, hg_out_norm_g, da_q_norm_g, da_k_norm_g, da_lambda_q1, da_lambda_k1, da_lambda_q2, da_lambda_k2, da_subln_g, w_out, w_router, b_router, w1, b1, w2, b2):
    bsz, seq, d = x.shape
    t = bsz * seq
    depth = w_ada.shape[0]
    out_dtype = x.dtype
    hw = HG_HEADS * HG_DV
    xcur = x.reshape(t, d)
    for l in range(depth):
        ada = _ada(c, w_ada[l], b_ada[l])
        ada3 = ada.reshape(bsz, 1, N_MOD * d)
        wi = w_in[l]
        n_in = wi.shape[1]
        w_in_r = jnp.concatenate([wi[:, n_in - 2 * d:], wi[:, :n_in - 2 * d]], axis=1).astype(BF16)
        col_h = (2 * d) // LANES
        col_a = col_h + 4 * HG_HEADS
        proj = _inproj(xcur, mix_norm_g[l].reshape(1, d), ada3, w_in_r, seq)

        o_a = _hgrn(proj, hg_lower_bound_logits, hg_out_norm_g[l].reshape(1, HG_DV), bsz, seq, col_h, l)
        lambda_init = 0.8 - 0.6 * math.exp(-0.3 * l)
        qg2 = jnp.tile(da_q_norm_g[l], 2).reshape(1, 2 * DA_DH)
        kg2 = jnp.tile(da_k_norm_g[l], 2).reshape(1, 2 * DA_DH)
        lam4 = jnp.stack([da_lambda_q1[l], da_lambda_k1[l], da_lambda_q2[l], da_lambda_k2[l]])
        o_d = _attn(proj, qg2, kg2, lam4, da_subln_g[l].reshape(1, 2 * DA_DH), bsz, seq, col_a, lambda_init)

        wo = w_out[l].astype(BF16)
        x1, h2, idx, rank, gcol, cnt = _mixout(
            xcur, o_a, o_d, proj, wo[:hw], wo[hw:], ada3, ffn_norm_g[l].reshape(1, d),
            w_router[l].T, b_router[l].reshape(N_EXPERTS, 1), seq)

        counts = cnt[:, 0].astype(I32)
        be, nvalid, nact, asg, n_asg = _route_tables(idx, rank, counts, t)
        b1p = b1[l].reshape(N_EXPERTS, -1, LANES, 2).transpose(0, 1, 3, 2).reshape(N_EXPERTS, 1, -1)
        yall = _ffn(be, nvalid, nact, asg, h2, w1[l], b1p, w2[l], b2[l].reshape(N_EXPERTS, 1, d), n_asg)
        xcur = _combine(x1, yall, gcol, ada3, seq, out_dtype)
    return xcur.reshape(bsz, seq, d)
```

```python
import functools
import math

import jax
import jax.numpy as jnp
from jax import lax
from jax.experimental import pallas as pl
from jax.experimental.pallas import tpu as pltpu
from jax.experimental.pallas import tpu_sc as plsc

F32 = jnp.float32
BF16 = jnp.bfloat16
I32 = jnp.int32

HG_HEADS = 4
HG_DK = 128
HG_DV = 128
HG_CHUNK = 32
DA_HEADS = 4
DA_DH = 64
N_EXPERTS = 32
TOP_K = 4
SWIGLU_ALPHA = 1.702
SWIGLU_LIMIT = 7.0
NORM_EPS = 1e-6
LOG2E = math.log2(math.e)
N_MOD = 6

LANES = 128
VMEM_LIMIT = 56 * 1024 * 1024

HG_ROWS = 256
ATTN_TILE = 256
FFN_BLOCK = 256

SC_CORES = 2
SC_WORKERS = SC_CORES * 16
SC_CHUNK = 32


def _sigmoid(x):
    return 1.0 / (1.0 + jnp.exp(-x))


def _dot(a, b):
    return jnp.dot(a, b, preferred_element_type=F32)


def _dot_nt(a, b):
    return lax.dot_general(a, b, (((1,), (1,)), ((), ())), preferred_element_type=F32)


def _split_bf16(x):
    hi = x.astype(BF16)
    lo = (x - hi.astype(F32)).astype(BF16)
    return hi, lo


def _ada_body(c_ref, w_ref, b_ref, o_ref):
    c = c_ref[...].astype(F32)
    ca = c * _sigmoid(c)
    o_ref[...] = jnp.dot(ca, w_ref[...], preferred_element_type=F32,
                         precision=lax.Precision.HIGHEST) + b_ref[...]


def _ada(c, w_ada, b_ada):
    bsz, d = c.shape
    n = w_ada.shape[1]
    tn = d
    return pl.pallas_call(
        _ada_body,
        grid=(n // tn,),
        in_specs=[
            pl.BlockSpec((bsz, d), lambda j: (0, 0)),
            pl.BlockSpec((d, tn), lambda j: (0, j)),
            pl.BlockSpec((1, tn), lambda j: (0, j)),
        ],
        out_specs=pl.BlockSpec((bsz, tn), lambda j: (0, j)),
        out_shape=jax.ShapeDtypeStruct((bsz, n), F32),
        name="ada",
    )(c, w_ada, b_ada.reshape(1, n))


def _norm_mod(x, g, shift, scale):
    ms = jnp.mean(x * x, axis=-1, keepdims=True)
    return (x * lax.rsqrt(ms + NORM_EPS) * g) * (1.0 + scale) + shift


def _inproj_body(x_ref, g_ref, sh_ref, sc_ref, w_ref, o_ref):
    h = _norm_mod(x_ref[...], g_ref[...], sh_ref[0], sc_ref[0])
    o_ref[...] = _dot(h.astype(BF16), w_ref[...]).astype(BF16)


def _inproj(x2, g, ada3, w_bf16, seq):
    t, d = x2.shape
    n = w_bf16.shape[1]
    tm = min(512, seq)
    nj = 2
    tn = n // nj
    per_b = seq // tm
    return pl.pallas_call(
        _inproj_body,
        grid=(nj, t // tm),
        in_specs=[
            pl.BlockSpec((tm, d), lambda j, i: (i, 0)),
            pl.BlockSpec((1, d), lambda j, i: (0, 0)),
            pl.BlockSpec((1, 1, d), lambda j, i: (i // per_b, 0, 0)),
            pl.BlockSpec((1, 1, d), lambda j, i: (i // per_b, 0, 1)),
            pl.BlockSpec((d, tn), lambda j, i: (0, j)),
        ],
        out_specs=pl.BlockSpec((tm, tn), lambda j, i: (i, j)),
        out_shape=jax.ShapeDtypeStruct((t, n), BF16),
        compiler_params=pltpu.CompilerParams(
            dimension_semantics=("arbitrary", "arbitrary"), vmem_limit_bytes=VMEM_LIMIT),
        name="inproj",
    )(x2, g, ada3, ada3, w_bf16)


def _hgrn_body(q_ref, f_ref, i_ref, og_ref, lbl_ref, g_ref, o_ref, *, seq, layer):
    rows, chunk = HG_ROWS, HG_CHUNK
    nchunk = rows // chunk
    lbl = lbl_ref[...].astype(F32)
    e = jnp.exp(lbl - jnp.max(lbl, axis=0, keepdims=True))
    lb = jnp.sum(e[: layer + 1], axis=0, keepdims=True) / jnp.sum(e, axis=0, keepdims=True)
    r_i = lax.broadcasted_iota(I32, (rows, rows), 0)
    c_i = lax.broadcasted_iota(I32, (rows, rows), 1)
    tri = ((r_i // chunk) == (c_i // chunk)) & (r_i >= c_i)
    tri_b = jnp.where(tri, 1.0, 0.0).astype(BF16)
    lane_chunk = lax.broadcasted_iota(I32, (HG_DK, rows), 1) // chunk
    row_chunk = lax.broadcasted_iota(I32, (rows, HG_DK), 0) // chunk
    g = g_ref[...].astype(F32)

    def block(r, st):
        sl = pl.ds(r * rows, rows)
        qr = q_ref[sl, :].astype(F32)
        fr = f_ref[sl, :].astype(F32)
        v = i_ref[sl, :].astype(F32)
        og = og_ref[sl, :].astype(F32)
        q = qr * _sigmoid(qr)
        f = lb + (1.0 - lb) * _sigmoid(fr)
        k = 1.0 - f
        logf = jnp.log(f)
        lhi, llo = _split_bf16(logf)
        bc2 = _dot(tri_b, jnp.concatenate([lhi, llo], axis=1))
        bcum = bc2[:, :HG_DK] + bc2[:, HG_DK:]
        b3 = bcum.reshape(nchunk, chunk, HG_DK)
        bl = b3[:, chunk - 1:chunk, :]
        dec = jnp.exp(bl)
        kt_f = k * jnp.exp(-bcum)
        qt_f = q * jnp.exp(bcum)
        kd = (kt_f.reshape(nchunk, chunk, HG_DK) * dec).reshape(rows, HG_DK)
        vb = v.astype(BF16)
        a = jnp.where(tri, _dot_nt(qt_f.astype(BF16), kt_f.astype(BF16)), 0.0).astype(BF16)
        kd_t = kd.T
        kd_x = jnp.concatenate([jnp.where(lane_chunk == c, kd_t, 0.0) for c in range(nchunk)], axis=0)
        kv_all = _dot(kd_x.astype(BF16), vb)
        dec_t = jnp.concatenate([dec.reshape(nchunk, HG_DK), jnp.zeros((HG_DK - nchunk, HG_DK), F32)], axis=0).T
        starts = []
        for c in range(nchunk):
            starts.append(st.astype(BF16))
            st = st * dec_t[:, c:c + 1] + kv_all[c * HG_DK:(c + 1) * HG_DK, :]
        q_x = [jnp.where(row_chunk == c, qt_f, 0.0).astype(BF16) for c in range(nchunk)]
        o = _dot(jnp.concatenate([a] + q_x, axis=1), jnp.concatenate([vb] + starts, axis=0))
        ms = jnp.mean(o * o, axis=-1, keepdims=True)
        o = o * lax.rsqrt(ms + NORM_EPS) * g
        o_ref[sl, :] = (o * (og * _sigmoid(og))).astype(o_ref.dtype)
        return st

    st = jnp.zeros((HG_DK, HG_DV), F32)
    for r in range(seq // rows):
        st = block(r, st)


def _hgrn(proj, lb_logits, norm_g, bsz, seq, col0, layer):
    t = proj.shape[0]
    blk = lambda off: pl.BlockSpec((seq, LANES), lambda b, h, off=off: (b, col0 + off + h))
    return pl.pallas_call(
        functools.partial(_hgrn_body, seq=seq, layer=layer),
        grid=(bsz, HG_HEADS),
        in_specs=[
            blk(0), blk(HG_HEADS), blk(2 * HG_HEADS), blk(3 * HG_HEADS),
            pl.BlockSpec((lb_logits.shape[0], HG_DK), lambda b, h: (0, h)),
            pl.BlockSpec((1, HG_DV), lambda b, h: (0, 0)),
        ],
        out_specs=pl.BlockSpec((seq, HG_DV), lambda b, h: (b, h)),
        out_shape=jax.ShapeDtypeStruct((t, HG_HEADS * HG_DV), BF16),
        compiler_params=pltpu.CompilerParams(
            dimension_semantics=("arbitrary", "arbitrary"), vmem_limit_bytes=VMEM_LIMIT),
        name="hgrn",
    )(proj, proj, proj, proj, lb_logits, norm_g)


def _group_norm(x, gsum_b, gain):
    hi, lo = _split_bf16(x * x)
    ss = _dot(hi, gsum_b) + _dot(lo, gsum_b)
    return x * lax.rsqrt(ss * (1.0 / DA_DH) + NORM_EPS) * gain


def _attn_body(q_ref, k_ref, v_ref, qg_ref, kg_ref, lam_ref, sg_ref, o_ref, kn_scr, v1_scr, *, seq, lambda_init):
    tq = ATTN_TILE
    qi = pl.program_id(2)
    width = 2 * DA_DH
    r_l = lax.broadcasted_iota(I32, (width, width), 0) // DA_DH
    c_l = lax.broadcasted_iota(I32, (width, width), 1) // DA_DH
    gsum_b = jnp.where(r_l == c_l, 1.0, 0.0).astype(BF16)
    lane = lax.broadcasted_iota(I32, (1, width), 1)

    @pl.when(qi == 0)
    def _():
        kg = kg_ref[...].astype(F32)

        def kblock(r, carry):
            sl = pl.ds(pl.multiple_of(r * tq, tq), tq)
            kn_scr[sl, :] = _group_norm(k_ref[sl, :].astype(F32), gsum_b, kg).astype(BF16)
            v1_scr[sl, :] = jnp.concatenate([v_ref[sl, :], jnp.ones((tq, width), BF16)], axis=1)
            return carry

        lax.fori_loop(0, seq // tq, kblock, 0)

    qn = _group_norm(q_ref[...].astype(F32), gsum_b, qg_ref[...].astype(F32)) * (DA_DH ** -0.5 * LOG2E)
    q1 = jnp.where(lane < DA_DH, qn, 0.0).astype(BF16)
    q2 = jnp.where(lane >= DA_DH, qn, 0.0).astype(BF16)
    row = lax.broadcasted_iota(I32, (tq, tq), 0)
    col = lax.broadcasted_iota(I32, (tq, tq), 1)
    keep = row >= col

    lam_v = lam_ref[...].astype(F32)
    lam = (jnp.exp(jnp.sum(lam_v[0:1] * lam_v[1:2], axis=-1, keepdims=True))
           - jnp.exp(jnp.sum(lam_v[2:3] * lam_v[3:4], axis=-1, keepdims=True)) + lambda_init)

    def softmax_v(qc, nk):
        s = _dot_nt(qc, kn_scr[0:nk, :])
        diag = jnp.where(keep, s[:, nk - tq:], -jnp.inf)
        s = diag if nk == tq else jnp.concatenate([s[:, :nk - tq], diag], axis=1)
        m = jnp.max(s, axis=-1, keepdims=True)
        return _dot(jnp.exp2(s - m).astype(BF16), v1_scr[0:nk, :])

    def tile(nblk):
        nk = nblk * tq
        a1 = softmax_v(q1, nk)
        a2 = softmax_v(q2, nk)
        o = a1[:, :width] / a1[:, width:width + 1] - lam * (a2[:, :width] / a2[:, width:width + 1])
        ms = jnp.mean(o * o, axis=-1, keepdims=True)
        o = o * lax.rsqrt(ms + NORM_EPS) * sg_ref[...].astype(F32) * (1.0 - lambda_init)
        o_ref[...] = o.astype(o_ref.dtype)

    for i in range(seq // tq):
        pl.when(qi == i)(functools.partial(tile, i + 1))


def _attn(proj, qg2, kg2, lam4, subln_g, bsz, seq, col0, lambda_init):
    t = proj.shape[0]
    tq = ATTN_TILE
    nq = seq // tq
    width = 2 * DA_DH
    return pl.pallas_call(
        functools.partial(_attn_body, seq=seq, lambda_init=lambda_init),
        grid=(bsz, DA_HEADS, nq),
        in_specs=[
            pl.BlockSpec((tq, width), lambda b, h, i: (b * nq + i, col0 + h)),
            pl.BlockSpec((seq, width), lambda b, h, i: (b, col0 + DA_HEADS + h)),
            pl.BlockSpec((seq, width), lambda b, h, i: (b, col0 + 2 * DA_HEADS + h)),
            pl.BlockSpec((1, width), lambda b, h, i: (0, 0)),
            pl.BlockSpec((1, width), lambda b, h, i: (0, 0)),
            pl.BlockSpec((4, DA_DH), lambda b, h, i: (0, 0)),
            pl.BlockSpec((1, width), lambda b, h, i: (0, 0)),
        ],
        out_specs=pl.BlockSpec((tq, width), lambda b, h, i: (b * nq + i, h)),
        out_shape=jax.ShapeDtypeStruct((t, DA_HEADS * width), BF16),
        scratch_shapes=[pltpu.VMEM((seq, width), BF16), pltpu.VMEM((seq, 2 * width), BF16)],
        compiler_params=pltpu.CompilerParams(
            dimension_semantics=("arbitrary", "arbitrary", "arbitrary"), vmem_limit_bytes=VMEM_LIMIT),
        name="attn",
    )(proj, proj, proj, qg2, kg2, lam4, subln_g)


def _mixout_body(x_ref, oa_ref, od_ref, ga_ref, gd_ref, woa_ref, wod_ref, g1_ref, g_ref, sh_ref, sc_ref,
                 wr_ref, br_ref,
                 x1_ref, h2_ref, idx_ref, rank_ref, gcol_ref, cnt_ref, carry_scr):
    i = pl.program_id(0)
    tm = x_ref.shape[0]

    @pl.when(i == 0)
    def _():
        carry_scr[...] = jnp.zeros_like(carry_scr)

    ya = _dot(oa_ref[...], woa_ref[...])
    yd = _dot(od_ref[...], wod_ref[...])
    y = _sigmoid(ga_ref[...].astype(F32)) * ya + _sigmoid(gd_ref[...].astype(F32)) * yd
    x1 = x_ref[...] + g1_ref[0] * y
    x1_ref[...] = x1
    h2 = _norm_mod(x1, g_ref[...], sh_ref[0], sc_ref[0])
    h2_ref[...] = h2

    hh, hl = _split_bf16(h2)
    wh, wl = _split_bf16(wr_ref[...])
    logits = _dot_nt(wh, hh) + _dot_nt(wl, hh) + _dot_nt(wh, hl) + br_ref[...]

    e_iota = lax.broadcasted_iota(I32, (N_EXPERTS, tm), 0).astype(F32)
    vals = logits
    tops, sels, idxs = [], [], []
    for _ in range(TOP_K):
        m = jnp.max(vals, axis=0, keepdims=True)
        idx = jnp.min(jnp.where(vals == m, e_iota, float(N_EXPERTS)), axis=0, keepdims=True)
        sel = e_iota == idx
        vals = jnp.where(sel, -jnp.inf, vals)
        tops.append(m)
        sels.append(sel)
        idxs.append(idx)
    ex = [jnp.exp(tv - tops[0]) for tv in tops]
    den = ex[0] + ex[1] + ex[2] + ex[3]
    gates = [v / den for v in ex]

    hot = jnp.where(sels[0] | sels[1] | sels[2] | sels[3], 1.0, 0.0)
    r_t = lax.broadcasted_iota(I32, (tm, tm), 0)
    c_t = lax.broadcasted_iota(I32, (tm, tm), 1)
    upper = jnp.where(r_t < c_t, 1.0, 0.0).astype(BF16)
    excl = _dot(hot.astype(BF16), upper) + carry_scr[:, 0:1]
    carry_scr[...] = carry_scr[...] + jnp.sum(hot, axis=1, keepdims=True)
    cnt_ref[...] = carry_scr[...]

    ranks = [jnp.sum(jnp.where(s, excl, 0.0), axis=0, keepdims=True) for s in sels]
    idx_ref[...] = jnp.concatenate(idxs, axis=0).astype(I32)
    rank_ref[...] = jnp.concatenate(ranks, axis=0).astype(I32)
    gpad = jnp.concatenate(gates + [jnp.zeros((LANES - TOP_K, tm), F32)], axis=0)
    gcol_ref[...] = gpad.T


def _mixout(x2, oa, od, proj, woa, wod, ada3, ffn_g, wr_t, br_col, seq):
    t, d = x2.shape
    tm = min(512, seq)
    per_b = seq // tm
    hw = oa.shape[1]
    row = lambda w: pl.BlockSpec((tm, w), lambda i: (i, 0))
    mod = lambda c: pl.BlockSpec((1, 1, d), lambda i, c=c: (i // per_b, 0, c))
    full = lambda a: pl.BlockSpec(a.shape, lambda i: (0,) * a.ndim)
    return pl.pallas_call(
        _mixout_body,
        grid=(t // tm,),
        in_specs=[
            row(d), row(hw), row(hw),
            pl.BlockSpec((tm, d), lambda i: (i, 0)),
            pl.BlockSpec((tm, d), lambda i: (i, 1)),
            full(woa), full(wod),
            mod(2),
            full(ffn_g), mod(3), mod(4),
            full(wr_t), full(br_col),
        ],
        out_specs=[
            row(d), row(d),
            pl.BlockSpec((TOP_K, tm), lambda i: (0, i)),
            pl.BlockSpec((TOP_K, tm), lambda i: (0, i)),
            pl.BlockSpec((tm, LANES), lambda i: (i, 0)),
            pl.BlockSpec((N_EXPERTS, LANES), lambda i: (0, 0)),
        ],
        out_shape=[
            jax.ShapeDtypeStruct((t, d), F32),
            jax.ShapeDtypeStruct((t, d), F32),
            jax.ShapeDtypeStruct((TOP_K, t), I32),
            jax.ShapeDtypeStruct((TOP_K, t), I32),
            jax.ShapeDtypeStruct((t, LANES), F32),
            jax.ShapeDtypeStruct((N_EXPERTS, LANES), F32),
        ],
        scratch_shapes=[pltpu.VMEM((N_EXPERTS, LANES), F32)],
        compiler_params=pltpu.CompilerParams(
            dimension_semantics=("arbitrary",), vmem_limit_bytes=VMEM_LIMIT),
        name="mixout",
    )(x2, oa, od, proj, proj, woa, wod, ada3, ffn_g, ada3, ada3, wr_t, br_col)


def _sc_mesh():
    return plsc.VectorSubcoreMesh(core_axis_name="c", subcore_axis_name="s")


def _sc_worker_base(rows_per_worker):
    wid = lax.axis_index("s") * SC_CORES + lax.axis_index("c")
    return wid * rows_per_worker


def _sc_dispatch(h2, dest, n_rows):
    t, d = h2.shape
    tpw = t // SC_WORKERS
    assert t % (SC_WORKERS * SC_CHUNK) == 0

    def body(h2_hbm, dest_hbm, xb_hbm, idx_v, rows_v):
        base = _sc_worker_base(tpw)

        @pl.loop(0, tpw // SC_CHUNK)
        def _(i):
            t0 = pl.multiple_of(base + i * SC_CHUNK, SC_CHUNK)
            pltpu.sync_copy(h2_hbm.at[pl.ds(t0, SC_CHUNK)], rows_v)
            for k in range(TOP_K):
                pltpu.sync_copy(dest_hbm.at[pl.ds(k * t + t0, SC_CHUNK)], idx_v)
                pltpu.sync_copy(rows_v, xb_hbm.at[idx_v])

    return pl.kernel(
        body, out_type=jax.ShapeDtypeStruct((n_rows, d), h2.dtype), mesh=_sc_mesh(),
        scratch_types=[pltpu.VMEM((SC_CHUNK,), I32), pltpu.VMEM((SC_CHUNK, d), h2.dtype)],
        name="dispatch",
    )(h2, dest)


def _sc_undispatch(y, dest):
    n_asg = dest.shape[0]
    d = y.shape[1]
    rpw = n_asg // SC_WORKERS
    assert n_asg % (SC_WORKERS * SC_CHUNK) == 0

    def body(y_hbm, dest_hbm, yt_hbm, idx_v, rows_v):
        base = _sc_worker_base(rpw)

        @pl.loop(0, rpw // SC_CHUNK)
        def _(i):
            r0 = pl.multiple_of(base + i * SC_CHUNK, SC_CHUNK)
            pltpu.sync_copy(dest_hbm.at[pl.ds(r0, SC_CHUNK)], idx_v)
            pltpu.sync_copy(y_hbm.at[idx_v], rows_v)
            pltpu.sync_copy(rows_v, yt_hbm.at[pl.ds(r0, SC_CHUNK)])

    return pl.kernel(
        body, out_type=jax.ShapeDtypeStruct((n_asg, d), y.dtype), mesh=_sc_mesh(),
        scratch_types=[pltpu.VMEM((SC_CHUNK,), I32), pltpu.VMEM((SC_CHUNK, d), y.dtype)],
        name="undispatch",
    )(y, dest)


def _ffn_body(be_ref, nact_ref, x_ref, w1_ref, b1_ref, w2_ref, b2_ref, y_ref, w1c, w2c):
    j = pl.program_id(0)
    ff = w2_ref.shape[1]
    pair = 2 * LANES
    ngroup = (2 * ff) // pair

    @pl.when((j == 0) | (be_ref[j] != be_ref[jnp.maximum(j - 1, 0)]))
    def _():
        r_p = lax.broadcasted_iota(I32, (pair, pair), 0)
        c_p = lax.broadcasted_iota(I32, (pair, pair), 1)
        src = jnp.where(c_p < LANES, 2 * c_p, 2 * (c_p - LANES) + 1)
        perm = jnp.where(r_p == src, 1.0, 0.0).astype(BF16)
        for g in range(ngroup):
            cols = slice(g * pair, (g + 1) * pair)
            w1c[:, cols] = _dot(w1_ref[0, :, cols].astype(BF16), perm).astype(BF16)
        w2c[...] = w2_ref[0].astype(BF16)

    @pl.when(j < nact_ref[0])
    def _():
        u = _dot(x_ref[...].astype(BF16), w1c[...]) + b1_ref[0]
        acts = []
        for g in range(ngroup):
            glu = jnp.minimum(u[:, g * pair:g * pair + LANES], SWIGLU_LIMIT)
            lin = jnp.clip(u[:, g * pair + LANES:(g + 1) * pair], -SWIGLU_LIMIT, SWIGLU_LIMIT)
            acts.append((glu * _sigmoid(SWIGLU_ALPHA * glu) * (lin + 1.0)).astype(BF16))
        act = jnp.concatenate(acts, axis=1)
        y_ref[...] = (_dot(act, w2c[...]) + b2_ref[0]).astype(y_ref.dtype)


def _ffn(block_expert, nact, xb, w1p, b1p, w2b, b2):
    bm = FFN_BLOCK
    n_rows, d = xb.shape
    n_blocks = n_rows // bm
    ff2 = w1p.shape[2]
    ff = w2b.shape[1]
    row_blk = lambda j, be, na: (jnp.minimum(j, na[0] - 1), 0)
    grid_spec = pltpu.PrefetchScalarGridSpec(
        num_scalar_prefetch=2,
        grid=(n_blocks,),
        in_specs=[
            pl.BlockSpec((bm, d), row_blk),
            pl.BlockSpec((1, d, ff2), lambda j, be, na: (be[j], 0, 0)),
            pl.BlockSpec((1, 1, ff2), lambda j, be, na: (be[j], 0, 0)),
            pl.BlockSpec((1, ff, d), lambda j, be, na: (be[j], 0, 0)),
            pl.BlockSpec((1, 1, d), lambda j, be, na: (be[j], 0, 0)),
        ],
        out_specs=pl.BlockSpec((bm, d), row_blk),
        scratch_shapes=[pltpu.VMEM((d, ff2), BF16), pltpu.VMEM((ff, d), BF16)],
    )
    return pl.pallas_call(
        _ffn_body,
        grid_spec=grid_spec,
        out_shape=jax.ShapeDtypeStruct((n_rows, d), F32),
        compiler_params=pltpu.CompilerParams(
            dimension_semantics=("arbitrary",), vmem_limit_bytes=VMEM_LIMIT),
        name="ffn",
    )(block_expert, nact, xb, w1p, b1p, w2b, b2)


def _combine_body(x1_ref, y0_ref, y1_ref, y2_ref, y3_ref, gcol_ref, g2_ref, o_ref):
    gc = gcol_ref[...]
    m = (gc[:, 0:1] * y0_ref[...] + gc[:, 1:2] * y1_ref[...]
         + gc[:, 2:3] * y2_ref[...] + gc[:, 3:4] * y3_ref[...])
    o_ref[...] = (x1_ref[...] + g2_ref[0] * m).astype(o_ref.dtype)


def _combine(x1, yall, gcol, ada3, seq, out_dtype):
    t, d = x1.shape
    tm = min(512, seq)
    per_b = seq // tm
    nt = t // tm
    yk = lambda k: pl.BlockSpec((tm, d), lambda i, k=k: (k * nt + i, 0))
    return pl.pallas_call(
        _combine_body,
        grid=(nt,),
        in_specs=[
            pl.BlockSpec((tm, d), lambda i: (i, 0)),
            yk(0), yk(1), yk(2), yk(3),
            pl.BlockSpec((tm, LANES), lambda i: (i, 0)),
            pl.BlockSpec((1, 1, d), lambda i: (i // per_b, 0, 5)),
        ],
        out_specs=pl.BlockSpec((tm, d), lambda i: (i, 0)),
        out_shape=jax.ShapeDtypeStruct((t, d), out_dtype),
        compiler_params=pltpu.CompilerParams(
            dimension_semantics=("arbitrary",), vmem_limit_bytes=VMEM_LIMIT),
        name="combine",
    )(x1, yall, yall, yall, yall, gcol, ada3)


def _route_tables(idx, rank, counts, n_tok):
    bm = FFN_BLOCK
    n_asg = TOP_K * n_tok
    n_blocks = -(-(n_asg + N_EXPERTS * (bm - 1)) // bm)
    padded = (counts + bm - 1) // bm * bm
    pad_ends = jnp.cumsum(padded)
    pad_starts = pad_ends - padded
    e_ids = jnp.arange(N_EXPERTS, dtype=I32)
    start_of = jnp.sum(jnp.where(idx[None] == e_ids[:, None, None], pad_starts[:, None, None], 0), axis=0)
    dest = (start_of + rank).reshape(-1)
    nact = (pad_ends[-1] // bm).astype(I32)
    blk_start = jnp.arange(n_blocks, dtype=I32) * bm
    last = jnp.sum(jnp.where(pad_ends <= pad_ends[-1] - 1, 1, 0)).astype(I32)
    be = jnp.sum(jnp.where(pad_ends[None, :] <= blk_start[:, None], 1, 0), axis=1).astype(I32)
    be = jnp.where(blk_start < pad_ends[-1], be, last)
    return be, nact.reshape(1), dest, n_blocks * bm


def kernel(x, c, w_ada, b_ada, mix_norm_g, ffn_norm_g, w_in, hg_lower_bound_logits, hg_out_norm_g, da_q_norm_g, da_k_norm_g, da_lambda_q1, da_lambda_k1, da_lambda_q2, da_lambda_k2, da_subln_g, w_out, w_router, b_router, w1, b1, w2, b2):
    bsz, seq, d = x.shape
    t = bsz * seq
    depth = w_ada.shape[0]
    out_dtype = x.dtype
    hw = HG_HEADS * HG_DV
    xcur = x.reshape(t, d)
    for l in range(depth):
        ada = _ada(c, w_ada[l], b_ada[l])
        ada3 = ada.reshape(bsz, 1, N_MOD * d)
        wi = w_in[l]
        n_in = wi.shape[1]
        w_in_r = jnp.concatenate([wi[:, n_in - 2 * d:], wi[:, :n_in - 2 * d]], axis=1).astype(BF16)
        col_h = (2 * d) // LANES
        col_a = col_h + 4 * HG_HEADS
        proj = _inproj(xcur, mix_norm_g[l].reshape(1, d), ada3, w_in_r, seq)

        o_a = _hgrn(proj, hg_lower_bound_logits, hg_out_norm_g[l].reshape(1, HG_DV), bsz, seq, col_h, l)
        lambda_init = 0.8 - 0.6 * math.exp(-0.3 * l)
        qg2 = jnp.tile(da_q_norm_g[l], 2).reshape(1, 2 * DA_DH)
        kg2 = jnp.tile(da_k_norm_g[l], 2).reshape(1, 2 * DA_DH)
        lam4 = jnp.stack([da_lambda_q1[l], da_lambda_k1[l], da_lambda_q2[l], da_lambda_k2[l]])
        o_d = _attn(proj, qg2, kg2, lam4, da_subln_g[l].reshape(1, 2 * DA_DH), bsz, seq, col_a, lambda_init)

        wo = w_out[l].astype(BF16)
        x1, h2, idx, rank, gcol, cnt = _mixout(
            xcur, o_a, o_d, proj, wo[:hw], wo[hw:], ada3, ffn_norm_g[l].reshape(1, d),
            w_router[l].T, b_router[l].reshape(N_EXPERTS, 1), seq)

        counts = cnt[:, 0].astype(I32)
        be, nact, dest, n_rows = _route_tables(idx, rank, counts, t)
        b1p = b1[l].reshape(N_EXPERTS, -1, LANES, 2).transpose(0, 1, 3, 2).reshape(N_EXPERTS, 1, -1)
        xb = _sc_dispatch(h2, dest, n_rows)
        yb = _ffn(be, nact, xb, w1[l], b1p, w2[l], b2[l].reshape(N_EXPERTS, 1, d))
        yall = _sc_undispatch(yb, dest)
        xcur = _combine(x1, yall, gcol, ada3, seq, out_dtype)
    return xcur.reshape(bsz, seq, d)
```

```python
import functools
import math

import jax
import jax.numpy as jnp
from jax import lax
from jax.experimental import pallas as pl
from jax.experimental.pallas import tpu as pltpu
from jax.experimental.pallas import tpu_sc as plsc

F32 = jnp.float32
BF16 = jnp.bfloat16
I32 = jnp.int32

HG_HEADS = 4
HG_DK = 128
HG_DV = 128
HG_CHUNK = 32
DA_HEADS = 4
DA_DH = 64
N_EXPERTS = 32
TOP_K = 4
SWIGLU_ALPHA = 1.702
SWIGLU_LIMIT = 7.0
NORM_EPS = 1e-6
LOG2E = math.log2(math.e)
N_MOD = 6

LANES = 128
VMEM_LIMIT = 56 * 1024 * 1024

HG_ROWS = 256
ATTN_TILE = 256
FFN_BLOCK = 512

SC_CORES = 2
SC_WORKERS = SC_CORES * 16
SC_CHUNK = 32


def _sigmoid(x):
    return 1.0 / (1.0 + jnp.exp(-x))


def _dot(a, b):
    return jnp.dot(a, b, preferred_element_type=F32)


def _dot_nt(a, b):
    return lax.dot_general(a, b, (((1,), (1,)), ((), ())), preferred_element_type=F32)


def _split_bf16(x):
    hi = x.astype(BF16)
    lo = (x - hi.astype(F32)).astype(BF16)
    return hi, lo


def _pack_halves(x):
    half = x.shape[1] // 2
    lo = lax.bitcast_convert_type(x[:, :half].astype(BF16).astype(F32), I32)
    hi = lax.bitcast_convert_type(x[:, half:].astype(BF16).astype(F32), I32)
    return lax.shift_right_logical(lo, 16) | hi


def _unpack_halves(w):
    lo = lax.bitcast_convert_type(lax.shift_left(w, 16), F32)
    hi = lax.bitcast_convert_type(w & jnp.int32(-65536), F32)
    return lo, hi


def _ada_body(c_ref, w_ref, b_ref, o_ref):
    c = c_ref[...].astype(F32)
    ca = c * _sigmoid(c)
    o_ref[...] = jnp.dot(ca, w_ref[...], preferred_element_type=F32,
                         precision=lax.Precision.HIGHEST) + b_ref[...]


def _ada(c, w_ada, b_ada):
    bsz, d = c.shape
    n = w_ada.shape[1]
    tn = d
    return pl.pallas_call(
        _ada_body,
        grid=(n // tn,),
        in_specs=[
            pl.BlockSpec((bsz, d), lambda j: (0, 0)),
            pl.BlockSpec((d, tn), lambda j: (0, j)),
            pl.BlockSpec((1, tn), lambda j: (0, j)),
        ],
        out_specs=pl.BlockSpec((bsz, tn), lambda j: (0, j)),
        out_shape=jax.ShapeDtypeStruct((bsz, n), F32),
        name="ada",
    )(c, w_ada, b_ada.reshape(1, n))


def _norm_mod(x, g, shift, scale):
    ms = jnp.mean(x * x, axis=-1, keepdims=True)
    return (x * lax.rsqrt(ms + NORM_EPS) * g) * (1.0 + scale) + shift


def _inproj_body(x_ref, g_ref, sh_ref, sc_ref, w_ref, o_ref):
    h = _norm_mod(x_ref[...], g_ref[...], sh_ref[0], sc_ref[0])
    o_ref[...] = _dot(h.astype(BF16), w_ref[...]).astype(BF16)


def _inproj(x2, g, ada3, w_bf16, seq):
    t, d = x2.shape
    n = w_bf16.shape[1]
    tm = min(512, seq)
    nj = 2
    tn = n // nj
    per_b = seq // tm
    return pl.pallas_call(
        _inproj_body,
        grid=(nj, t // tm),
        in_specs=[
            pl.BlockSpec((tm, d), lambda j, i: (i, 0)),
            pl.BlockSpec((1, d), lambda j, i: (0, 0)),
            pl.BlockSpec((1, 1, d), lambda j, i: (i // per_b, 0, 0)),
            pl.BlockSpec((1, 1, d), lambda j, i: (i // per_b, 0, 1)),
            pl.BlockSpec((d, tn), lambda j, i: (0, j)),
        ],
        out_specs=pl.BlockSpec((tm, tn), lambda j, i: (i, j)),
        out_shape=jax.ShapeDtypeStruct((t, n), BF16),
        compiler_params=pltpu.CompilerParams(
            dimension_semantics=("arbitrary", "arbitrary"), vmem_limit_bytes=VMEM_LIMIT),
        name="inproj",
    )(x2, g, ada3, ada3, w_bf16)


def _hgrn_body(q_ref, f_ref, i_ref, og_ref, lbl_ref, g_ref, o_ref, *, seq, layer):
    rows, chunk = HG_ROWS, HG_CHUNK
    nchunk = rows // chunk
    lbl = lbl_ref[...].astype(F32)
    e = jnp.exp(lbl - jnp.max(lbl, axis=0, keepdims=True))
    lb = jnp.sum(e[: layer + 1], axis=0, keepdims=True) / jnp.sum(e, axis=0, keepdims=True)
    r_i = lax.broadcasted_iota(I32, (rows, rows), 0)
    c_i = lax.broadcasted_iota(I32, (rows, rows), 1)
    tri = ((r_i // chunk) == (c_i // chunk)) & (r_i >= c_i)
    tri_b = jnp.where(tri, 1.0, 0.0).astype(BF16)
    lane_chunk = lax.broadcasted_iota(I32, (HG_DK, rows), 1) // chunk
    row_chunk = lax.broadcasted_iota(I32, (rows, HG_DK), 0) // chunk
    g = g_ref[...].astype(F32)

    def block(r, st):
        sl = pl.ds(r * rows, rows)
        qr = q_ref[sl, :].astype(F32)
        fr = f_ref[sl, :].astype(F32)
        v = i_ref[sl, :].astype(F32)
        og = og_ref[sl, :].astype(F32)
        q = qr * _sigmoid(qr)
        f = lb + (1.0 - lb) * _sigmoid(fr)
        k = 1.0 - f
        logf = jnp.log(f)
        lhi, llo = _split_bf16(logf)
        bc2 = _dot(tri_b, jnp.concatenate([lhi, llo], axis=1))
        bcum = bc2[:, :HG_DK] + bc2[:, HG_DK:]
        b3 = bcum.reshape(nchunk, chunk, HG_DK)
        bl = b3[:, chunk - 1:chunk, :]
        dec = jnp.exp(bl)
        kt_f = k * jnp.exp(-bcum)
        qt_f = q * jnp.exp(bcum)
        kd = (kt_f.reshape(nchunk, chunk, HG_DK) * dec).reshape(rows, HG_DK)
        vb = v.astype(BF16)
        a = jnp.where(tri, _dot_nt(qt_f.astype(BF16), kt_f.astype(BF16)), 0.0).astype(BF16)
        kd_t = kd.T
        kd_x = jnp.concatenate([jnp.where(lane_chunk == c, kd_t, 0.0) for c in range(nchunk)], axis=0)
        kv_all = _dot(kd_x.astype(BF16), vb)
        dec_t = jnp.concatenate([dec.reshape(nchunk, HG_DK), jnp.zeros((HG_DK - nchunk, HG_DK), F32)], axis=0).T
        starts = []
        for c in range(nchunk):
            starts.append(st.astype(BF16))
            st = st * dec_t[:, c:c + 1] + kv_all[c * HG_DK:(c + 1) * HG_DK, :]
        q_x = [jnp.where(row_chunk == c, qt_f, 0.0).astype(BF16) for c in range(nchunk)]
        o = _dot(jnp.concatenate([a] + q_x, axis=1), jnp.concatenate([vb] + starts, axis=0))
        ms = jnp.mean(o * o, axis=-1, keepdims=True)
        o = o * lax.rsqrt(ms + NORM_EPS) * g
        o_ref[sl, :] = (o * (og * _sigmoid(og))).astype(o_ref.dtype)
        return st

    st = jnp.zeros((HG_DK, HG_DV), F32)
    for r in range(seq // rows):
        st = block(r, st)


def _hgrn(proj, lb_logits, norm_g, bsz, seq, col0, layer):
    t = proj.shape[0]
    blk = lambda off: pl.BlockSpec((seq, LANES), lambda b, h, off=off: (b, col0 + off + h))
    return pl.pallas_call(
        functools.partial(_hgrn_body, seq=seq, layer=layer),
        grid=(bsz, HG_HEADS),
        in_specs=[
            blk(0), blk(HG_HEADS), blk(2 * HG_HEADS), blk(3 * HG_HEADS),
            pl.BlockSpec((lb_logits.shape[0], HG_DK), lambda b, h: (0, h)),
            pl.BlockSpec((1, HG_DV), lambda b, h: (0, 0)),
        ],
        out_specs=pl.BlockSpec((seq, HG_DV), lambda b, h: (b, h)),
        out_shape=jax.ShapeDtypeStruct((t, HG_HEADS * HG_DV), BF16),
        compiler_params=pltpu.CompilerParams(
            dimension_semantics=("arbitrary", "arbitrary"), vmem_limit_bytes=VMEM_LIMIT),
        name="hgrn",
    )(proj, proj, proj, proj, lb_logits, norm_g)


def _group_norm(x, gsum_b, gain):
    hi, lo = _split_bf16(x * x)
    ss = _dot(hi, gsum_b) + _dot(lo, gsum_b)
    return x * lax.rsqrt(ss * (1.0 / DA_DH) + NORM_EPS) * gain


def _attn_body(q_ref, k_ref, v_ref, qg_ref, kg_ref, lam_ref, sg_ref, o_ref, kn_scr, v1_scr, *, seq, lambda_init):
    tq = ATTN_TILE
    qi = pl.program_id(2)
    width = 2 * DA_DH
    r_l = lax.broadcasted_iota(I32, (width, width), 0) // DA_DH
    c_l = lax.broadcasted_iota(I32, (width, width), 1) // DA_DH
    gsum_b = jnp.where(r_l == c_l, 1.0, 0.0).astype(BF16)
    lane = lax.broadcasted_iota(I32, (1, width), 1)

    @pl.when(qi == 0)
    def _():
        kg = kg_ref[...].astype(F32)

        def kblock(r, carry):
            sl = pl.ds(pl.multiple_of(r * tq, tq), tq)
            kn_scr[sl, :] = _group_norm(k_ref[sl, :].astype(F32), gsum_b, kg).astype(BF16)
            v1_scr[sl, :] = jnp.concatenate([v_ref[sl, :], jnp.ones((tq, width), BF16)], axis=1)
            return carry

        lax.fori_loop(0, seq // tq, kblock, 0)

    qn = _group_norm(q_ref[...].astype(F32), gsum_b, qg_ref[...].astype(F32)) * (DA_DH ** -0.5 * LOG2E)
    q1 = jnp.where(lane < DA_DH, qn, 0.0).astype(BF16)
    q2 = jnp.where(lane >= DA_DH, qn, 0.0).astype(BF16)
    row = lax.broadcasted_iota(I32, (tq, tq), 0)
    col = lax.broadcasted_iota(I32, (tq, tq), 1)
    keep = row >= col

    lam_v = lam_ref[...].astype(F32)
    lam = (jnp.exp(jnp.sum(lam_v[0:1] * lam_v[1:2], axis=-1, keepdims=True))
           - jnp.exp(jnp.sum(lam_v[2:3] * lam_v[3:4], axis=-1, keepdims=True)) + lambda_init)

    def softmax_v(qc, nk):
        s = _dot_nt(qc, kn_scr[0:nk, :])
        diag = jnp.where(keep, s[:, nk - tq:], -jnp.inf)
        s = diag if nk == tq else jnp.concatenate([s[:, :nk - tq], diag], axis=1)
        m = jnp.max(s, axis=-1, keepdims=True)
        return _dot(jnp.exp2(s - m).astype(BF16), v1_scr[0:nk, :])

    def tile(nblk):
        nk = nblk * tq
        a1 = softmax_v(q1, nk)
        a2 = softmax_v(q2, nk)
        o = a1[:, :width] / a1[:, width:width + 1] - lam * (a2[:, :width] / a2[:, width:width + 1])
        ms = jnp.mean(o * o, axis=-1, keepdims=True)
        o = o * lax.rsqrt(ms + NORM_EPS) * sg_ref[...].astype(F32) * (1.0 - lambda_init)
        o_ref[...] = o.astype(o_ref.dtype)

    for i in range(seq // tq):
        pl.when(qi == i)(functools.partial(tile, i + 1))


def _attn(proj, qg2, kg2, lam4, subln_g, bsz, seq, col0, lambda_init):
    t = proj.shape[0]
    tq = ATTN_TILE
    nq = seq // tq
    width = 2 * DA_DH
    return pl.pallas_call(
        functools.partial(_attn_body, seq=seq, lambda_init=lambda_init),
        grid=(bsz, DA_HEADS, nq),
        in_specs=[
            pl.BlockSpec((tq, width), lambda b, h, i: (b * nq + i, col0 + h)),
            pl.BlockSpec((seq, width), lambda b, h, i: (b, col0 + DA_HEADS + h)),
            pl.BlockSpec((seq, width), lambda b, h, i: (b, col0 + 2 * DA_HEADS + h)),
            pl.BlockSpec((1, width), lambda b, h, i: (0, 0)),
            pl.BlockSpec((1, width), lambda b, h, i: (0, 0)),
            pl.BlockSpec((4, DA_DH), lambda b, h, i: (0, 0)),
            pl.BlockSpec((1, width), lambda b, h, i: (0, 0)),
        ],
        out_specs=pl.BlockSpec((tq, width), lambda b, h, i: (b * nq + i, h)),
        out_shape=jax.ShapeDtypeStruct((t, DA_HEADS * width), BF16),
        scratch_shapes=[pltpu.VMEM((seq, width), BF16), pltpu.VMEM((seq, 2 * width), BF16)],
        compiler_params=pltpu.CompilerParams(
            dimension_semantics=("arbitrary", "arbitrary", "arbitrary"), vmem_limit_bytes=VMEM_LIMIT),
        name="attn",
    )(proj, proj, proj, qg2, kg2, lam4, subln_g)


def _mixout_body(x_ref, oa_ref, od_ref, ga_ref, gd_ref, woa_ref, wod_ref, g1_ref, g_ref, sh_ref, sc_ref,
                 wr_ref, br_ref,
                 x1_ref, h2_ref, idx_ref, rank_ref, gcol_ref, cnt_ref, carry_scr):
    i = pl.program_id(0)
    tm = x_ref.shape[0]

    @pl.when(i == 0)
    def _():
        carry_scr[...] = jnp.zeros_like(carry_scr)

    ya = _dot(oa_ref[...], woa_ref[...])
    yd = _dot(od_ref[...], wod_ref[...])
    y = _sigmoid(ga_ref[...].astype(F32)) * ya + _sigmoid(gd_ref[...].astype(F32)) * yd
    x1 = x_ref[...] + g1_ref[0] * y
    x1_ref[...] = x1
    h2 = _norm_mod(x1, g_ref[...], sh_ref[0], sc_ref[0])
    h2_ref[...] = _pack_halves(h2)

    hh, hl = _split_bf16(h2)
    wh, wl = _split_bf16(wr_ref[...])
    logits = _dot_nt(wh, hh) + _dot_nt(wl, hh) + _dot_nt(wh, hl) + br_ref[...]

    e_iota = lax.broadcasted_iota(I32, (N_EXPERTS, tm), 0).astype(F32)
    vals = logits
    tops, sels, idxs = [], [], []
    for _ in range(TOP_K):
        m = jnp.max(vals, axis=0, keepdims=True)
        idx = jnp.min(jnp.where(vals == m, e_iota, float(N_EXPERTS)), axis=0, keepdims=True)
        sel = e_iota == idx
        vals = jnp.where(sel, -jnp.inf, vals)
        tops.append(m)
        sels.append(sel)
        idxs.append(idx)
    ex = [jnp.exp(tv - tops[0]) for tv in tops]
    den = ex[0] + ex[1] + ex[2] + ex[3]
    gates = [v / den for v in ex]

    hot = jnp.where(sels[0] | sels[1] | sels[2] | sels[3], 1.0, 0.0)
    r_t = lax.broadcasted_iota(I32, (tm, tm), 0)
    c_t = lax.broadcasted_iota(I32, (tm, tm), 1)
    upper = jnp.where(r_t < c_t, 1.0, 0.0).astype(BF16)
    excl = _dot(hot.astype(BF16), upper) + carry_scr[:, 0:1]
    carry_scr[...] = carry_scr[...] + jnp.sum(hot, axis=1, keepdims=True)
    cnt_ref[...] = carry_scr[...]

    ranks = [jnp.sum(jnp.where(s, excl, 0.0), axis=0, keepdims=True) for s in sels]
    idx_ref[...] = jnp.concatenate(idxs, axis=0).astype(I32)
    rank_ref[...] = jnp.concatenate(ranks, axis=0).astype(I32)
    gpad = jnp.concatenate(gates + [jnp.zeros((LANES - TOP_K, tm), F32)], axis=0)
    gcol_ref[...] = gpad.T


def _mixout(x2, oa, od, proj, woa, wod, ada3, ffn_g, wr_t, br_col, seq):
    t, d = x2.shape
    tm = min(512, seq)
    per_b = seq // tm
    hw = oa.shape[1]
    row = lambda w: pl.BlockSpec((tm, w), lambda i: (i, 0))
    mod = lambda c: pl.BlockSpec((1, 1, d), lambda i, c=c: (i // per_b, 0, c))
    full = lambda a: pl.BlockSpec(a.shape, lambda i: (0,) * a.ndim)
    return pl.pallas_call(
        _mixout_body,
        grid=(t // tm,),
        in_specs=[
            row(d), row(hw), row(hw),
            pl.BlockSpec((tm, d), lambda i: (i, 0)),
            pl.BlockSpec((tm, d), lambda i: (i, 1)),
            full(woa), full(wod),
            mod(2),
            full(ffn_g), mod(3), mod(4),
            full(wr_t), full(br_col),
        ],
        out_specs=[
            row(d), row(d // 2),
            pl.BlockSpec((TOP_K, tm), lambda i: (0, i)),
            pl.BlockSpec((TOP_K, tm), lambda i: (0, i)),
            pl.BlockSpec((tm, LANES), lambda i: (i, 0)),
            pl.BlockSpec((N_EXPERTS, LANES), lambda i: (0, 0)),
        ],
        out_shape=[
            jax.ShapeDtypeStruct((t, d), F32),
            jax.ShapeDtypeStruct((t, d // 2), I32),
            jax.ShapeDtypeStruct((TOP_K, t), I32),
            jax.ShapeDtypeStruct((TOP_K, t), I32),
            jax.ShapeDtypeStruct((t, LANES), F32),
            jax.ShapeDtypeStruct((N_EXPERTS, LANES), F32),
        ],
        scratch_shapes=[pltpu.VMEM((N_EXPERTS, LANES), F32)],
        compiler_params=pltpu.CompilerParams(
            dimension_semantics=("arbitrary",), vmem_limit_bytes=VMEM_LIMIT),
        name="mixout",
    )(x2, oa, od, proj, proj, woa, wod, ada3, ffn_g, ada3, ada3, wr_t, br_col)


def _sc_mesh():
    return plsc.VectorSubcoreMesh(core_axis_name="c", subcore_axis_name="s")


def _sc_worker_base(rows_per_worker):
    wid = lax.axis_index("s") * SC_CORES + lax.axis_index("c")
    return wid * rows_per_worker


def _sc_dispatch(h2, dest, n_rows):
    t, d = h2.shape
    tpw = t // SC_WORKERS
    assert t % (SC_WORKERS * SC_CHUNK) == 0

    def body(h2_hbm, dest_hbm, xb_hbm, idx_v, rows_v):
        base = _sc_worker_base(tpw)

        @pl.loop(0, tpw // SC_CHUNK)
        def _(i):
            t0 = pl.multiple_of(base + i * SC_CHUNK, SC_CHUNK)
            pltpu.sync_copy(h2_hbm.at[pl.ds(t0, SC_CHUNK)], rows_v)
            for k in range(TOP_K):
                pltpu.sync_copy(dest_hbm.at[pl.ds(k * t + t0, SC_CHUNK)], idx_v)
                pltpu.sync_copy(rows_v, xb_hbm.at[idx_v])

    return pl.kernel(
        body, out_type=jax.ShapeDtypeStruct((n_rows, d), h2.dtype), mesh=_sc_mesh(),
        scratch_types=[pltpu.VMEM((SC_CHUNK,), I32), pltpu.VMEM((SC_CHUNK, d), h2.dtype)],
        name="dispatch",
    )(h2, dest)


def _sc_undispatch(y, dest):
    n_asg = dest.shape[0]
    d = y.shape[1]
    rpw = n_asg // SC_WORKERS
    assert n_asg % (SC_WORKERS * SC_CHUNK) == 0

    def body(y_hbm, dest_hbm, yt_hbm, idx_v, rows_v):
        base = _sc_worker_base(rpw)

        @pl.loop(0, rpw // SC_CHUNK)
        def _(i):
            r0 = pl.multiple_of(base + i * SC_CHUNK, SC_CHUNK)
            pltpu.sync_copy(dest_hbm.at[pl.ds(r0, SC_CHUNK)], idx_v)
            pltpu.sync_copy(y_hbm.at[idx_v], rows_v)
            pltpu.sync_copy(rows_v, yt_hbm.at[pl.ds(r0, SC_CHUNK)])

    return pl.kernel(
        body, out_type=jax.ShapeDtypeStruct((n_asg, d), y.dtype), mesh=_sc_mesh(),
        scratch_types=[pltpu.VMEM((SC_CHUNK,), I32), pltpu.VMEM((SC_CHUNK, d), y.dtype)],
        name="undispatch",
    )(y, dest)


def _ffn_body(be_ref, nact_ref, x_ref, w1_ref, b1_ref, w2_ref, b2_ref, y_ref, w1c, w2c):
    j = pl.program_id(0)
    ff = w2_ref.shape[1]
    pair = 2 * LANES
    ngroup = (2 * ff) // pair

    @pl.when((j == 0) | (be_ref[j] != be_ref[jnp.maximum(j - 1, 0)]))
    def _():
        r_p = lax.broadcasted_iota(I32, (pair, pair), 0)
        c_p = lax.broadcasted_iota(I32, (pair, pair), 1)
        src = jnp.where(c_p < LANES, 2 * c_p, 2 * (c_p - LANES) + 1)
        perm = jnp.where(r_p == src, 1.0, 0.0).astype(BF16)
        for g in range(ngroup):
            cols = slice(g * pair, (g + 1) * pair)
            w1c[:, cols] = _dot(w1_ref[0, :, cols].astype(BF16), perm).astype(BF16)
        w2c[...] = w2_ref[0].astype(BF16)

    @pl.when(j < nact_ref[0])
    def _():
        x_lo, x_hi = _unpack_halves(x_ref[...])
        xb = jnp.concatenate([x_lo.astype(BF16), x_hi.astype(BF16)], axis=1)
        u = _dot(xb, w1c[...]) + b1_ref[0]
        acts = []
        for g in range(ngroup):
            glu = jnp.minimum(u[:, g * pair:g * pair + LANES], SWIGLU_LIMIT)
            lin = jnp.clip(u[:, g * pair + LANES:(g + 1) * pair], -SWIGLU_LIMIT, SWIGLU_LIMIT)
            acts.append((glu * _sigmoid(SWIGLU_ALPHA * glu) * (lin + 1.0)).astype(BF16))
        act = jnp.concatenate(acts, axis=1)
        y_ref[...] = _pack_halves(_dot(act, w2c[...]) + b2_ref[0])


def _ffn(block_expert, nact, xb, w1p, b1p, w2b, b2):
    bm = FFN_BLOCK
    n_rows, dw = xb.shape
    d = 2 * dw
    n_blocks = n_rows // bm
    ff2 = w1p.shape[2]
    ff = w2b.shape[1]
    row_blk = lambda j, be, na: (jnp.minimum(j, na[0] - 1), 0)
    grid_spec = pltpu.PrefetchScalarGridSpec(
        num_scalar_prefetch=2,
        grid=(n_blocks,),
        in_specs=[
            pl.BlockSpec((bm, dw), row_blk),
            pl.BlockSpec((1, d, ff2), lambda j, be, na: (be[j], 0, 0)),
            pl.BlockSpec((1, 1, ff2), lambda j, be, na: (be[j], 0, 0)),
            pl.BlockSpec((1, ff, d), lambda j, be, na: (be[j], 0, 0)),
            pl.BlockSpec((1, 1, d), lambda j, be, na: (be[j], 0, 0)),
        ],
        out_specs=pl.BlockSpec((bm, dw), row_blk),
        scratch_shapes=[pltpu.VMEM((d, ff2), BF16), pltpu.VMEM((ff, d), BF16)],
    )
    return pl.pallas_call(
        _ffn_body,
        grid_spec=grid_spec,
        out_shape=jax.ShapeDtypeStruct((n_rows, dw), I32),
        compiler_params=pltpu.CompilerParams(
            dimension_semantics=("arbitrary",), vmem_limit_bytes=VMEM_LIMIT),
        name="ffn",
    )(block_expert, nact, xb, w1p, b1p, w2b, b2)


def _combine_body(x1_ref, y0_ref, y1_ref, y2_ref, y3_ref, gcol_ref, g2_ref, o_ref):
    gc = gcol_ref[...]
    m_lo = m_hi = None
    for k, y_ref in enumerate((y0_ref, y1_ref, y2_ref, y3_ref)):
        lo, hi = _unpack_halves(y_ref[...])
        gk = gc[:, k:k + 1]
        m_lo = gk * lo if m_lo is None else m_lo + gk * lo
        m_hi = gk * hi if m_hi is None else m_hi + gk * hi
    m = jnp.concatenate([m_lo, m_hi], axis=1)
    o_ref[...] = (x1_ref[...] + g2_ref[0] * m).astype(o_ref.dtype)


def _combine(x1, yall, gcol, ada3, seq, out_dtype):
    t, d = x1.shape
    tm = min(512, seq)
    per_b = seq // tm
    nt = t // tm
    yk = lambda k: pl.BlockSpec((tm, d // 2), lambda i, k=k: (k * nt + i, 0))
    return pl.pallas_call(
        _combine_body,
        grid=(nt,),
        in_specs=[
            pl.BlockSpec((tm, d), lambda i: (i, 0)),
            yk(0), yk(1), yk(2), yk(3),
            pl.BlockSpec((tm, LANES), lambda i: (i, 0)),
            pl.BlockSpec((1, 1, d), lambda i: (i // per_b, 0, 5)),
        ],
        out_specs=pl.BlockSpec((tm, d), lambda i: (i, 0)),
        out_shape=jax.ShapeDtypeStruct((t, d), out_dtype),
        compiler_params=pltpu.CompilerParams(
            dimension_semantics=("arbitrary",), vmem_limit_bytes=VMEM_LIMIT),
        name="combine",
    )(x1, yall, yall, yall, yall, gcol, ada3)


def _route_tables(idx, rank, counts, n_tok):
    bm = FFN_BLOCK
    n_asg = TOP_K * n_tok
    n_blocks = -(-(n_asg + N_EXPERTS * (bm - 1)) // bm)
    padded = (counts + bm - 1) // bm * bm
    pad_ends = jnp.cumsum(padded)
    pad_starts = pad_ends - padded
    e_ids = jnp.arange(N_EXPERTS, dtype=I32)
    start_of = jnp.sum(jnp.where(idx[None] == e_ids[:, None, None], pad_starts[:, None, None], 0), axis=0)
    dest = (start_of + rank).reshape(-1)
    nact = (pad_ends[-1] // bm).astype(I32)
    blk_start = jnp.arange(n_blocks, dtype=I32) * bm
    last = jnp.sum(jnp.where(pad_ends <= pad_ends[-1] - 1, 1, 0)).astype(I32)
    be = jnp.sum(jnp.where(pad_ends[None, :] <= blk_start[:, None], 1, 0), axis=1).astype(I32)
    be = jnp.where(blk_start < pad_ends[-1], be, last)
    return be, nact.reshape(1), dest, n_blocks * bm


def kernel(x, c, w_ada, b_ada, mix_norm_g, ffn_norm_g, w_in, hg_lower_bound_logits, hg_out_norm_g, da_q_norm_g, da_k_norm_g, da_lambda_q1, da_lambda_k1, da_lambda_q2, da_lambda_k2, da_subln_g, w_out, w_router, b_router, w1, b1, w2, b2):
    bsz, seq, d = x.shape
    t = bsz * seq
    depth = w_ada.shape[0]
    out_dtype = x.dtype
    hw = HG_HEADS * HG_DV
    xcur = x.reshape(t, d)
    for l in range(depth):
        ada = _ada(c, w_ada[l], b_ada[l])
        ada3 = ada.reshape(bsz, 1, N_MOD * d)
        wi = w_in[l]
        n_in = wi.shape[1]
        w_in_r = jnp.concatenate([wi[:, n_in - 2 * d:], wi[:, :n_in - 2 * d]], axis=1).astype(BF16)
        col_h = (2 * d) // LANES
        col_a = col_h + 4 * HG_HEADS
        proj = _inproj(xcur, mix_norm_g[l].reshape(1, d), ada3, w_in_r, seq)

        o_a = _hgrn(proj, hg_lower_bound_logits, hg_out_norm_g[l].reshape(1, HG_DV), bsz, seq, col_h, l)
        lambda_init = 0.8 - 0.6 * math.exp(-0.3 * l)
        qg2 = jnp.tile(da_q_norm_g[l], 2).reshape(1, 2 * DA_DH)
        kg2 = jnp.tile(da_k_norm_g[l], 2).reshape(1, 2 * DA_DH)
        lam4 = jnp.stack([da_lambda_q1[l], da_lambda_k1[l], da_lambda_q2[l], da_lambda_k2[l]])
        o_d = _attn(proj, qg2, kg2, lam4, da_subln_g[l].reshape(1, 2 * DA_DH), bsz, seq, col_a, lambda_init)

        wo = w_out[l].astype(BF16)
        x1, h2, idx, rank, gcol, cnt = _mixout(
            xcur, o_a, o_d, proj, wo[:hw], wo[hw:], ada3, ffn_norm_g[l].reshape(1, d),
            w_router[l].T, b_router[l].reshape(N_EXPERTS, 1), seq)

        counts = cnt[:, 0].astype(I32)
        be, nact, dest, n_rows = _route_tables(idx, rank, counts, t)
        b1p = b1[l].reshape(N_EXPERTS, -1, LANES, 2).transpose(0, 1, 3, 2).reshape(N_EXPERTS, 1, -1)
        xb = _sc_dispatch(h2, dest, n_rows)
        yb = _ffn(be, nact, xb, w1[l], b1p, w2[l], b2[l].reshape(N_EXPERTS, 1, d))
        yall = _sc_undispatch(yb, dest)
        xcur = _combine(x1, yall, gcol, ada3, seq, out_dtype)
    return xcur.reshape(bsz, seq, d)
```

```python
import functools
import math

import jax
import jax.numpy as jnp
from jax import lax
from jax.experimental import pallas as pl
from jax.experimental.pallas import tpu as pltpu
from jax.experimental.pallas import tpu_sc as plsc

F32 = jnp.float32
BF16 = jnp.bfloat16
I32 = jnp.int32

HG_HEADS = 4
HG_DK = 128
HG_DV = 128
HG_CHUNK = 32
DA_HEADS = 4
DA_DH = 64
N_EXPERTS = 32
TOP_K = 4
SWIGLU_ALPHA = 1.702
SWIGLU_LIMIT = 7.0
NORM_EPS = 1e-6
LOG2E = math.log2(math.e)
N_MOD = 6

LANES = 128
VMEM_LIMIT = 56 * 1024 * 1024

HG_ROWS = 256
ATTN_TILE = 256
FFN_BLOCK = 512

SC_CORES = 2
SC_WORKERS = SC_CORES * 16
SC_CHUNK = 128


def _sigmoid(x):
    return 1.0 / (1.0 + jnp.exp(-x))


def _dot(a, b):
    return jnp.dot(a, b, preferred_element_type=F32)


def _dot_nt(a, b):
    return lax.dot_general(a, b, (((1,), (1,)), ((), ())), preferred_element_type=F32)


def _split_bf16(x):
    hi = x.astype(BF16)
    lo = (x - hi.astype(F32)).astype(BF16)
    return hi, lo


def _pack_halves(x):
    half = x.shape[1] // 2
    lo = lax.bitcast_convert_type(x[:, :half].astype(BF16).astype(F32), I32)
    hi = lax.bitcast_convert_type(x[:, half:].astype(BF16).astype(F32), I32)
    return lax.shift_right_logical(lo, 16) | hi


def _unpack_halves(w):
    lo = lax.bitcast_convert_type(lax.shift_left(w, 16), F32)
    hi = lax.bitcast_convert_type(w & jnp.int32(-65536), F32)
    return lo, hi


def _ada_body(c_ref, w_ref, b_ref, o_ref):
    c = c_ref[...].astype(F32)
    ca = c * _sigmoid(c)
    o_ref[...] = jnp.dot(ca, w_ref[...], preferred_element_type=F32,
                         precision=lax.Precision.HIGHEST) + b_ref[...]


def _ada(c, w_ada, b_ada):
    bsz, d = c.shape
    n = w_ada.shape[1]
    tn = d
    return pl.pallas_call(
        _ada_body,
        grid=(n // tn,),
        in_specs=[
            pl.BlockSpec((bsz, d), lambda j: (0, 0)),
            pl.BlockSpec((d, tn), lambda j: (0, j)),
            pl.BlockSpec((1, tn), lambda j: (0, j)),
        ],
        out_specs=pl.BlockSpec((bsz, tn), lambda j: (0, j)),
        out_shape=jax.ShapeDtypeStruct((bsz, n), F32),
        name="ada",
    )(c, w_ada, b_ada.reshape(1, n))


def _norm_mod(x, g, shift, scale):
    ms = jnp.mean(x * x, axis=-1, keepdims=True)
    return (x * lax.rsqrt(ms + NORM_EPS) * g) * (1.0 + scale) + shift


def _inproj_body(x_ref, g_ref, sh_ref, sc_ref, w_ref, o_ref):
    h = _norm_mod(x_ref[...], g_ref[...], sh_ref[0], sc_ref[0])
    o_ref[...] = _dot(h.astype(BF16), w_ref[...]).astype(BF16)


def _inproj(x2, g, ada3, w_bf16, seq):
    t, d = x2.shape
    n = w_bf16.shape[1]
    tm = min(512, seq)
    nj = 2
    tn = n // nj
    per_b = seq // tm
    return pl.pallas_call(
        _inproj_body,
        grid=(nj, t // tm),
        in_specs=[
            pl.BlockSpec((tm, d), lambda j, i: (i, 0)),
            pl.BlockSpec((1, d), lambda j, i: (0, 0)),
            pl.BlockSpec((1, 1, d), lambda j, i: (i // per_b, 0, 0)),
            pl.BlockSpec((1, 1, d), lambda j, i: (i // per_b, 0, 1)),
            pl.BlockSpec((d, tn), lambda j, i: (0, j)),
        ],
        out_specs=pl.BlockSpec((tm, tn), lambda j, i: (i, j)),
        out_shape=jax.ShapeDtypeStruct((t, n), BF16),
        compiler_params=pltpu.CompilerParams(
            dimension_semantics=("arbitrary", "arbitrary"), vmem_limit_bytes=VMEM_LIMIT),
        name="inproj",
    )(x2, g, ada3, ada3, w_bf16)


def _hgrn_body(q_ref, f_ref, i_ref, og_ref, lbl_ref, g_ref, o_ref, *, seq, layer):
    rows, chunk = HG_ROWS, HG_CHUNK
    nchunk = rows // chunk
    lbl = lbl_ref[...].astype(F32)
    e = jnp.exp(lbl - jnp.max(lbl, axis=0, keepdims=True))
    lb = jnp.sum(e[: layer + 1], axis=0, keepdims=True) / jnp.sum(e, axis=0, keepdims=True)
    r_i = lax.broadcasted_iota(I32, (rows, rows), 0)
    c_i = lax.broadcasted_iota(I32, (rows, rows), 1)
    tri = ((r_i // chunk) == (c_i // chunk)) & (r_i >= c_i)
    tri_b = jnp.where(tri, 1.0, 0.0).astype(BF16)
    lane_chunk = lax.broadcasted_iota(I32, (HG_DK, rows), 1) // chunk
    row_chunk = lax.broadcasted_iota(I32, (rows, HG_DK), 0) // chunk
    g = g_ref[...].astype(F32)

    def block(r, st):
        sl = pl.ds(r * rows, rows)
        qr = q_ref[sl, :].astype(F32)
        fr = f_ref[sl, :].astype(F32)
        v = i_ref[sl, :].astype(F32)
        og = og_ref[sl, :].astype(F32)
        q = qr * _sigmoid(qr)
        f = lb + (1.0 - lb) * _sigmoid(fr)
        k = 1.0 - f
        logf = jnp.log(f)
        lhi, llo = _split_bf16(logf)
        bc2 = _dot(tri_b, jnp.concatenate([lhi, llo], axis=1))
        bcum = bc2[:, :HG_DK] + bc2[:, HG_DK:]
        b3 = bcum.reshape(nchunk, chunk, HG_DK)
        bl = b3[:, chunk - 1:chunk, :]
        dec = jnp.exp(bl)
        kt_f = k * jnp.exp(-bcum)
        qt_f = q * jnp.exp(bcum)
        kd = (kt_f.reshape(nchunk, chunk, HG_DK) * dec).reshape(rows, HG_DK)
        vb = v.astype(BF16)
        a = jnp.where(tri, _dot_nt(qt_f.astype(BF16), kt_f.astype(BF16)), 0.0).astype(BF16)
        kd_t = kd.T
        kd_x = jnp.concatenate([jnp.where(lane_chunk == c, kd_t, 0.0) for c in range(nchunk)], axis=0)
        kv_all = _dot(kd_x.astype(BF16), vb)
        dec_t = jnp.concatenate([dec.reshape(nchunk, HG_DK), jnp.zeros((HG_DK - nchunk, HG_DK), F32)], axis=0).T
        starts = []
        for c in range(nchunk):
            starts.append(st.astype(BF16))
            st = st * dec_t[:, c:c + 1] + kv_all[c * HG_DK:(c + 1) * HG_DK, :]
        q_x = [jnp.where(row_chunk == c, qt_f, 0.0).astype(BF16) for c in range(nchunk)]
        o = _dot(jnp.concatenate([a] + q_x, axis=1), jnp.concatenate([vb] + starts, axis=0))
        ms = jnp.mean(o * o, axis=-1, keepdims=True)
        o = o * lax.rsqrt(ms + NORM_EPS) * g
        o_ref[sl, :] = (o * (og * _sigmoid(og))).astype(o_ref.dtype)
        return st

    st = jnp.zeros((HG_DK, HG_DV), F32)
    for r in range(seq // rows):
        st = block(r, st)


def _hgrn(proj, lb_logits, norm_g, bsz, seq, col0, layer):
    t = proj.shape[0]
    blk = lambda off: pl.BlockSpec((seq, LANES), lambda b, h, off=off: (b, col0 + off + h))
    return pl.pallas_call(
        functools.partial(_hgrn_body, seq=seq, layer=layer),
        grid=(bsz, HG_HEADS),
        in_specs=[
            blk(0), blk(HG_HEADS), blk(2 * HG_HEADS), blk(3 * HG_HEADS),
            pl.BlockSpec((lb_logits.shape[0], HG_DK), lambda b, h: (0, h)),
            pl.BlockSpec((1, HG_DV), lambda b, h: (0, 0)),
        ],
        out_specs=pl.BlockSpec((seq, HG_DV), lambda b, h: (b, h)),
        out_shape=jax.ShapeDtypeStruct((t, HG_HEADS * HG_DV), BF16),
        compiler_params=pltpu.CompilerParams(
            dimension_semantics=("arbitrary", "arbitrary"), vmem_limit_bytes=VMEM_LIMIT),
        name="hgrn",
    )(proj, proj, proj, proj, lb_logits, norm_g)


def _group_norm(x, gsum_b, gain):
    hi, lo = _split_bf16(x * x)
    ss = _dot(hi, gsum_b) + _dot(lo, gsum_b)
    return x * lax.rsqrt(ss * (1.0 / DA_DH) + NORM_EPS) * gain


def _attn_body(q_ref, k_ref, v_ref, qg_ref, kg_ref, lam_ref, sg_ref, o_ref, kn_scr, v1_scr, *, seq, lambda_init):
    tq = ATTN_TILE
    width = 2 * DA_DH
    r_l = lax.broadcasted_iota(I32, (width, width), 0) // DA_DH
    c_l = lax.broadcasted_iota(I32, (width, width), 1) // DA_DH
    gsum_b = jnp.where(r_l == c_l, 1.0, 0.0).astype(BF16)
    lane = lax.broadcasted_iota(I32, (1, width), 1)
    kg = kg_ref[...].astype(F32)
    qg = qg_ref[...].astype(F32) * (DA_DH ** -0.5 * LOG2E)
    sg = sg_ref[...].astype(F32) * (1.0 - lambda_init)
    ones = jnp.ones((tq, width), BF16)
    row = lax.broadcasted_iota(I32, (tq, tq), 0)
    col = lax.broadcasted_iota(I32, (tq, tq), 1)
    keep = row >= col

    lam_v = lam_ref[...].astype(F32)
    lam = (jnp.exp(jnp.sum(lam_v[0:1] * lam_v[1:2], axis=-1, keepdims=True))
           - jnp.exp(jnp.sum(lam_v[2:3] * lam_v[3:4], axis=-1, keepdims=True)) + lambda_init)

    def softmax_v(qc, nk):
        s = _dot_nt(qc, kn_scr[0:nk, :])
        diag = jnp.where(keep, s[:, nk - tq:], -jnp.inf)
        s = diag if nk == tq else jnp.concatenate([s[:, :nk - tq], diag], axis=1)
        m = jnp.max(s, axis=-1, keepdims=True)
        return _dot(jnp.exp2(s - m).astype(BF16), v1_scr[0:nk, :])

    for i in range(seq // tq):
        sl = slice(i * tq, (i + 1) * tq)
        kn_scr[sl, :] = _group_norm(k_ref[sl, :].astype(F32), gsum_b, kg).astype(BF16)
        v1_scr[sl, :] = jnp.concatenate([v_ref[sl, :], ones], axis=1)
        qn = _group_norm(q_ref[sl, :].astype(F32), gsum_b, qg)
        nk = (i + 1) * tq
        a1 = softmax_v(jnp.where(lane < DA_DH, qn, 0.0).astype(BF16), nk)
        a2 = softmax_v(jnp.where(lane >= DA_DH, qn, 0.0).astype(BF16), nk)
        o = a1[:, :width] / a1[:, width:width + 1] - lam * (a2[:, :width] / a2[:, width:width + 1])
        ms = jnp.mean(o * o, axis=-1, keepdims=True)
        o_ref[sl, :] = (o * lax.rsqrt(ms + NORM_EPS) * sg).astype(o_ref.dtype)


def _attn(proj, qg2, kg2, lam4, subln_g, bsz, seq, col0, lambda_init):
    t = proj.shape[0]
    width = 2 * DA_DH
    return pl.pallas_call(
        functools.partial(_attn_body, seq=seq, lambda_init=lambda_init),
        grid=(bsz, DA_HEADS),
        in_specs=[
            pl.BlockSpec((seq, width), lambda b, h: (b, col0 + h)),
            pl.BlockSpec((seq, width), lambda b, h: (b, col0 + DA_HEADS + h)),
            pl.BlockSpec((seq, width), lambda b, h: (b, col0 + 2 * DA_HEADS + h)),
            pl.BlockSpec((1, width), lambda b, h: (0, 0)),
            pl.BlockSpec((1, width), lambda b, h: (0, 0)),
            pl.BlockSpec((4, DA_DH), lambda b, h: (0, 0)),
            pl.BlockSpec((1, width), lambda b, h: (0, 0)),
        ],
        out_specs=pl.BlockSpec((seq, width), lambda b, h: (b, h)),
        out_shape=jax.ShapeDtypeStruct((t, DA_HEADS * width), BF16),
        scratch_shapes=[pltpu.VMEM((seq, width), BF16), pltpu.VMEM((seq, 2 * width), BF16)],
        compiler_params=pltpu.CompilerParams(
            dimension_semantics=("arbitrary", "arbitrary"), vmem_limit_bytes=VMEM_LIMIT),
        name="attn",
    )(proj, proj, proj, qg2, kg2, lam4, subln_g)


def _mixout_body(x_ref, oa_ref, od_ref, ga_ref, gd_ref, woa_ref, wod_ref, g1_ref, g_ref, sh_ref, sc_ref,
                 wr_ref, br_ref,
                 x1_ref, h2_ref, idx_ref, rank_ref, gcol_ref, cnt_ref, carry_scr):
    i = pl.program_id(0)
    tm = x_ref.shape[0]

    @pl.when(i == 0)
    def _():
        carry_scr[...] = jnp.zeros_like(carry_scr)

    ya = _dot(oa_ref[...], woa_ref[...])
    yd = _dot(od_ref[...], wod_ref[...])
    y = _sigmoid(ga_ref[...].astype(F32)) * ya + _sigmoid(gd_ref[...].astype(F32)) * yd
    x1 = x_ref[...] + g1_ref[0] * y
    x1_ref[...] = x1
    h2 = _norm_mod(x1, g_ref[...], sh_ref[0], sc_ref[0])
    h2_ref[...] = _pack_halves(h2)

    hh, hl = _split_bf16(h2)
    wh, wl = _split_bf16(wr_ref[...])
    logits = _dot_nt(wh, hh) + _dot_nt(wl, hh) + _dot_nt(wh, hl) + br_ref[...]

    e_iota = lax.broadcasted_iota(I32, (N_EXPERTS, tm), 0).astype(F32)
    vals = logits
    tops, sels, idxs = [], [], []
    for _ in range(TOP_K):
        m = jnp.max(vals, axis=0, keepdims=True)
        idx = jnp.min(jnp.where(vals == m, e_iota, float(N_EXPERTS)), axis=0, keepdims=True)
        sel = e_iota == idx
        vals = jnp.where(sel, -jnp.inf, vals)
        tops.append(m)
        sels.append(sel)
        idxs.append(idx)
    ex = [jnp.exp(tv - tops[0]) for tv in tops]
    den = ex[0] + ex[1] + ex[2] + ex[3]
    gates = [v / den for v in ex]

    hot = jnp.where(sels[0] | sels[1] | sels[2] | sels[3], 1.0, 0.0)
    r_t = lax.broadcasted_iota(I32, (tm, tm), 0)
    c_t = lax.broadcasted_iota(I32, (tm, tm), 1)
    upper = jnp.where(r_t < c_t, 1.0, 0.0).astype(BF16)
    excl = _dot(hot.astype(BF16), upper) + carry_scr[:, 0:1]
    carry_scr[...] = carry_scr[...] + jnp.sum(hot, axis=1, keepdims=True)
    cnt_ref[...] = carry_scr[...]

    ranks = [jnp.sum(jnp.where(s, excl, 0.0), axis=0, keepdims=True) for s in sels]
    idx_ref[...] = jnp.concatenate(idxs, axis=0).astype(I32)
    rank_ref[...] = jnp.concatenate(ranks, axis=0).astype(I32)
    gpad = jnp.concatenate(gates + [jnp.zeros((LANES - TOP_K, tm), F32)], axis=0)
    gcol_ref[...] = gpad.T


def _mixout(x2, oa, od, proj, woa, wod, ada3, ffn_g, wr_t, br_col, seq):
    t, d = x2.shape
    tm = min(512, seq)
    per_b = seq // tm
    hw = oa.shape[1]
    row = lambda w: pl.BlockSpec((tm, w), lambda i: (i, 0))
    mod = lambda c: pl.BlockSpec((1, 1, d), lambda i, c=c: (i // per_b, 0, c))
    full = lambda a: pl.BlockSpec(a.shape, lambda i: (0,) * a.ndim)
    return pl.pallas_call(
        _mixout_body,
        grid=(t // tm,),
        in_specs=[
            row(d), row(hw), row(hw),
            pl.BlockSpec((tm, d), lambda i: (i, 0)),
            pl.BlockSpec((tm, d), lambda i: (i, 1)),
            full(woa), full(wod),
            mod(2),
            full(ffn_g), mod(3), mod(4),
            full(wr_t), full(br_col),
        ],
        out_specs=[
            row(d), row(d // 2),
            pl.BlockSpec((TOP_K, tm), lambda i: (0, i)),
            pl.BlockSpec((TOP_K, tm), lambda i: (0, i)),
            pl.BlockSpec((tm, LANES), lambda i: (i, 0)),
            pl.BlockSpec((N_EXPERTS, LANES), lambda i: (0, 0)),
        ],
        out_shape=[
            jax.ShapeDtypeStruct((t, d), F32),
            jax.ShapeDtypeStruct((t, d // 2), I32),
            jax.ShapeDtypeStruct((TOP_K, t), I32),
            jax.ShapeDtypeStruct((TOP_K, t), I32),
            jax.ShapeDtypeStruct((t, LANES), F32),
            jax.ShapeDtypeStruct((N_EXPERTS, LANES), F32),
        ],
        scratch_shapes=[pltpu.VMEM((N_EXPERTS, LANES), F32)],
        compiler_params=pltpu.CompilerParams(
            dimension_semantics=("arbitrary",), vmem_limit_bytes=VMEM_LIMIT),
        name="mixout",
    )(x2, oa, od, proj, proj, woa, wod, ada3, ffn_g, ada3, ada3, wr_t, br_col)


def _sc_mesh():
    return plsc.VectorSubcoreMesh(core_axis_name="c", subcore_axis_name="s")


def _sc_worker_base(rows_per_worker):
    wid = lax.axis_index("s") * SC_CORES + lax.axis_index("c")
    return wid * rows_per_worker


def _sc_dispatch(h2, dest, n_rows):
    t, d = h2.shape
    tpw = t // SC_WORKERS
    assert t % (SC_WORKERS * SC_CHUNK) == 0

    def body(h2_hbm, dest_hbm, xb_hbm, idx_v, rows_v):
        base = _sc_worker_base(tpw)

        @pl.loop(0, tpw // SC_CHUNK)
        def _(i):
            t0 = pl.multiple_of(base + i * SC_CHUNK, SC_CHUNK)
            pltpu.sync_copy(h2_hbm.at[pl.ds(t0, SC_CHUNK)], rows_v)
            for k in range(TOP_K):
                pltpu.sync_copy(dest_hbm.at[pl.ds(k * t + t0, SC_CHUNK)], idx_v)
                pltpu.sync_copy(rows_v, xb_hbm.at[idx_v])

    return pl.kernel(
        body, out_type=jax.ShapeDtypeStruct((n_rows, d), h2.dtype), mesh=_sc_mesh(),
        scratch_types=[pltpu.VMEM((SC_CHUNK,), I32), pltpu.VMEM((SC_CHUNK, d), h2.dtype)],
        name="dispatch",
    )(h2, dest)


def _sc_undispatch(y, dest):
    n_asg = dest.shape[0]
    d = y.shape[1]
    rpw = n_asg // SC_WORKERS
    assert n_asg % (SC_WORKERS * SC_CHUNK) == 0

    def body(y_hbm, dest_hbm, yt_hbm, idx_v, rows_v):
        base = _sc_worker_base(rpw)

        @pl.loop(0, rpw // SC_CHUNK)
        def _(i):
            r0 = pl.multiple_of(base + i * SC_CHUNK, SC_CHUNK)
            pltpu.sync_copy(dest_hbm.at[pl.ds(r0, SC_CHUNK)], idx_v)
            pltpu.sync_copy(y_hbm.at[idx_v], rows_v)
            pltpu.sync_copy(rows_v, yt_hbm.at[pl.ds(r0, SC_CHUNK)])

    return pl.kernel(
        body, out_type=jax.ShapeDtypeStruct((n_asg, d), y.dtype), mesh=_sc_mesh(),
        scratch_types=[pltpu.VMEM((SC_CHUNK,), I32), pltpu.VMEM((SC_CHUNK, d), y.dtype)],
        name="undispatch",
    )(y, dest)


def _ffn_body(be_ref, nact_ref, x_ref, w1_ref, b1_ref, w2_ref, b2_ref, y_ref, w1c, w2c):
    j = pl.program_id(0)
    ff = w2_ref.shape[1]
    pair = 2 * LANES
    ngroup = (2 * ff) // pair

    @pl.when((j == 0) | (be_ref[j] != be_ref[jnp.maximum(j - 1, 0)]))
    def _():
        r_p = lax.broadcasted_iota(I32, (pair, pair), 0)
        c_p = lax.broadcasted_iota(I32, (pair, pair), 1)
        src = jnp.where(c_p < LANES, 2 * c_p, 2 * (c_p - LANES) + 1)
        perm = jnp.where(r_p == src, 1.0, 0.0).astype(BF16)
        for g in range(ngroup):
            cols = slice(g * pair, (g + 1) * pair)
            w1c[:, cols] = _dot(w1_ref[0, :, cols].astype(BF16), perm).astype(BF16)
        w2c[...] = w2_ref[0].astype(BF16)

    @pl.when(j < nact_ref[0])
    def _():
        x_lo, x_hi = _unpack_halves(x_ref[...])
        xb = jnp.concatenate([x_lo.astype(BF16), x_hi.astype(BF16)], axis=1)
        u = _dot(xb, w1c[...]) + b1_ref[0]
        acts = []
        for g in range(ngroup):
            glu = jnp.minimum(u[:, g * pair:g * pair + LANES], SWIGLU_LIMIT)
            lin = jnp.clip(u[:, g * pair + LANES:(g + 1) * pair], -SWIGLU_LIMIT, SWIGLU_LIMIT)
            acts.append((glu * _sigmoid(SWIGLU_ALPHA * glu) * (lin + 1.0)).astype(BF16))
        act = jnp.concatenate(acts, axis=1)
        y_ref[...] = _pack_halves(_dot(act, w2c[...]) + b2_ref[0])


def _ffn(block_expert, nact, xb, w1p, b1p, w2b, b2):
    bm = FFN_BLOCK
    n_rows, dw = xb.shape
    d = 2 * dw
    n_blocks = n_rows // bm
    ff2 = w1p.shape[2]
    ff = w2b.shape[1]
    row_blk = lambda j, be, na: (jnp.minimum(j, na[0] - 1), 0)
    grid_spec = pltpu.PrefetchScalarGridSpec(
        num_scalar_prefetch=2,
        grid=(n_blocks,),
        in_specs=[
            pl.BlockSpec((bm, dw), row_blk),
            pl.BlockSpec((1, d, ff2), lambda j, be, na: (be[j], 0, 0)),
            pl.BlockSpec((1, 1, ff2), lambda j, be, na: (be[j], 0, 0)),
            pl.BlockSpec((1, ff, d), lambda j, be, na: (be[j], 0, 0)),
            pl.BlockSpec((1, 1, d), lambda j, be, na: (be[j], 0, 0)),
        ],
        out_specs=pl.BlockSpec((bm, dw), row_blk),
        scratch_shapes=[pltpu.VMEM((d, ff2), BF16), pltpu.VMEM((ff, d), BF16)],
    )
    return pl.pallas_call(
        _ffn_body,
        grid_spec=grid_spec,
        out_shape=jax.ShapeDtypeStruct((n_rows, dw), I32),
        compiler_params=pltpu.CompilerParams(
            dimension_semantics=("arbitrary",), vmem_limit_bytes=VMEM_LIMIT),
        name="ffn",
    )(block_expert, nact, xb, w1p, b1p, w2b, b2)


def _combine_body(x1_ref, y0_ref, y1_ref, y2_ref, y3_ref, gcol_ref, g2_ref, o_ref):
    gc = gcol_ref[...]
    m_lo = m_hi = None
    for k, y_ref in enumerate((y0_ref, y1_ref, y2_ref, y3_ref)):
        lo, hi = _unpack_halves(y_ref[...])
        gk = gc[:, k:k + 1]
        m_lo = gk * lo if m_lo is None else m_lo + gk * lo
        m_hi = gk * hi if m_hi is None else m_hi + gk * hi
    m = jnp.concatenate([m_lo, m_hi], axis=1)
    o_ref[...] = (x1_ref[...] + g2_ref[0] * m).astype(o_ref.dtype)


def _combine(x1, yall, gcol, ada3, seq, out_dtype):
    t, d = x1.shape
    tm = min(512, seq)
    per_b = seq // tm
    nt = t // tm
    yk = lambda k: pl.BlockSpec((tm, d // 2), lambda i, k=k: (k * nt + i, 0))
    return pl.pallas_call(
        _combine_body,
        grid=(nt,),
        in_specs=[
            pl.BlockSpec((tm, d), lambda i: (i, 0)),
            yk(0), yk(1), yk(2), yk(3),
            pl.BlockSpec((tm, LANES), lambda i: (i, 0)),
            pl.BlockSpec((1, 1, d), lambda i: (i // per_b, 0, 5)),
        ],
        out_specs=pl.BlockSpec((tm, d), lambda i: (i, 0)),
        out_shape=jax.ShapeDtypeStruct((t, d), out_dtype),
        compiler_params=pltpu.CompilerParams(
            dimension_semantics=("arbitrary",), vmem_limit_bytes=VMEM_LIMIT),
        name="combine",
    )(x1, yall, yall, yall, yall, gcol, ada3)


def _route_tables(idx, rank, counts, n_tok):
    bm = FFN_BLOCK
    n_asg = TOP_K * n_tok
    n_blocks = -(-(n_asg + N_EXPERTS * (bm - 1)) // bm)
    padded = (counts + bm - 1) // bm * bm
    pad_ends = jnp.cumsum(padded)
    pad_starts = pad_ends - padded
    e_ids = jnp.arange(N_EXPERTS, dtype=I32)
    start_of = jnp.sum(jnp.where(idx[None] == e_ids[:, None, None], pad_starts[:, None, None], 0), axis=0)
    dest = (start_of + rank).reshape(-1)
    nact = (pad_ends[-1] // bm).astype(I32)
    blk_start = jnp.arange(n_blocks, dtype=I32) * bm
    last = jnp.sum(jnp.where(pad_ends <= pad_ends[-1] - 1, 1, 0)).astype(I32)
    be = jnp.sum(jnp.where(pad_ends[None, :] <= blk_start[:, None], 1, 0), axis=1).astype(I32)
    be = jnp.where(blk_start < pad_ends[-1], be, last)
    return be, nact.reshape(1), dest, n_blocks * bm


def kernel(x, c, w_ada, b_ada, mix_norm_g, ffn_norm_g, w_in, hg_lower_bound_logits, hg_out_norm_g, da_q_norm_g, da_k_norm_g, da_lambda_q1, da_lambda_k1, da_lambda_q2, da_lambda_k2, da_subln_g, w_out, w_router, b_router, w1, b1, w2, b2):
    bsz, seq, d = x.shape
    t = bsz * seq
    depth = w_ada.shape[0]
    out_dtype = x.dtype
    hw = HG_HEADS * HG_DV
    xcur = x.reshape(t, d)
    for l in range(depth):
        ada = _ada(c, w_ada[l], b_ada[l])
        ada3 = ada.reshape(bsz, 1, N_MOD * d)
        wi = w_in[l]
        n_in = wi.shape[1]
        w_in_r = jnp.concatenate([wi[:, n_in - 2 * d:], wi[:, :n_in - 2 * d]], axis=1).astype(BF16)
        col_h = (2 * d) // LANES
        col_a = col_h + 4 * HG_HEADS
        proj = _inproj(xcur, mix_norm_g[l].reshape(1, d), ada3, w_in_r, seq)

        o_a = _hgrn(proj, hg_lower_bound_logits, hg_out_norm_g[l].reshape(1, HG_DV), bsz, seq, col_h, l)
        lambda_init = 0.8 - 0.6 * math.exp(-0.3 * l)
        qg2 = jnp.tile(da_q_norm_g[l], 2).reshape(1, 2 * DA_DH)
        kg2 = jnp.tile(da_k_norm_g[l], 2).reshape(1, 2 * DA_DH)
        lam4 = jnp.stack([da_lambda_q1[l], da_lambda_k1[l], da_lambda_q2[l], da_lambda_k2[l]])
        o_d = _attn(proj, qg2, kg2, lam4, da_subln_g[l].reshape(1, 2 * DA_DH), bsz, seq, col_a, lambda_init)

        wo = w_out[l].astype(BF16)
        x1, h2, idx, rank, gcol, cnt = _mixout(
            xcur, o_a, o_d, proj, wo[:hw], wo[hw:], ada3, ffn_norm_g[l].reshape(1, d),
            w_router[l].T, b_router[l].reshape(N_EXPERTS, 1), seq)

        counts = cnt[:, 0].astype(I32)
        be, nact, dest, n_rows = _route_tables(idx, rank, counts, t)
        b1p = b1[l].reshape(N_EXPERTS, -1, LANES, 2).transpose(0, 1, 3, 2).reshape(N_EXPERTS, 1, -1)
        xb = _sc_dispatch(h2, dest, n_rows)
        yb = _ffn(be, nact, xb, w1[l], b1p, w2[l], b2[l].reshape(N_EXPERTS, 1, d))
        yall = _sc_undispatch(yb, dest)
        xcur = _combine(x1, yall, gcol, ada3, seq, out_dtype)
    return xcur.reshape(bsz, seq, d)
```

```python
import functools
import math

import jax
import jax.numpy as jnp
from jax import lax
from jax.experimental import pallas as pl
from jax.experimental.pallas import tpu as pltpu
from jax.experimental.pallas import tpu_sc as plsc

F32 = jnp.float32
BF16 = jnp.bfloat16
I32 = jnp.int32

HG_HEADS = 4
HG_DK = 128
HG_DV = 128
HG_CHUNK = 32
DA_HEADS = 4
DA_DH = 64
N_EXPERTS = 32
TOP_K = 4
SWIGLU_ALPHA = 1.702
SWIGLU_LIMIT = 7.0
NORM_EPS = 1e-6
LOG2E = math.log2(math.e)
N_MOD = 6

LANES = 128
VMEM_LIMIT = 56 * 1024 * 1024

HG_ROWS = 256
ATTN_TILE = 256
FFN_BLOCK = 512

SC_CORES = 2
SC_WORKERS = SC_CORES * 16
SC_CHUNK = 128


def _sigmoid(x):
    return 1.0 / (1.0 + jnp.exp(-x))


def _dot(a, b):
    return jnp.dot(a, b, preferred_element_type=F32)


def _dot_nt(a, b):
    return lax.dot_general(a, b, (((1,), (1,)), ((), ())), preferred_element_type=F32)


def _split_bf16(x):
    hi = x.astype(BF16)
    lo = (x - hi.astype(F32)).astype(BF16)
    return hi, lo


def _pack_halves(x):
    half = x.shape[1] // 2
    lo = lax.bitcast_convert_type(x[:, :half].astype(BF16).astype(F32), I32)
    hi = lax.bitcast_convert_type(x[:, half:].astype(BF16).astype(F32), I32)
    return lax.shift_right_logical(lo, 16) | hi


def _unpack_halves(w):
    lo = lax.bitcast_convert_type(lax.shift_left(w, 16), F32)
    hi = lax.bitcast_convert_type(w & jnp.int32(-65536), F32)
    return lo, hi


def _ada_body(c_ref, w_ref, b_ref, o_ref):
    c = c_ref[...].astype(F32)
    ca = c * _sigmoid(c)
    o_ref[...] = jnp.dot(ca, w_ref[...], preferred_element_type=F32,
                         precision=lax.Precision.HIGHEST) + b_ref[...]


def _ada(c, w_ada, b_ada):
    bsz, d = c.shape
    n = w_ada.shape[1]
    tn = d
    return pl.pallas_call(
        _ada_body,
        grid=(n // tn,),
        in_specs=[
            pl.BlockSpec((bsz, d), lambda j: (0, 0)),
            pl.BlockSpec((d, tn), lambda j: (0, j)),
            pl.BlockSpec((1, tn), lambda j: (0, j)),
        ],
        out_specs=pl.BlockSpec((bsz, tn), lambda j: (0, j)),
        out_shape=jax.ShapeDtypeStruct((bsz, n), F32),
        name="ada",
    )(c, w_ada, b_ada.reshape(1, n))


def _norm_mod(x, g, shift, scale):
    ms = jnp.mean(x * x, axis=-1, keepdims=True)
    return (x * lax.rsqrt(ms + NORM_EPS) * g) * (1.0 + scale) + shift


def _inproj_body(x_ref, g_ref, sh_ref, sc_ref, w_ref, o_ref):
    h = _norm_mod(x_ref[...], g_ref[...], sh_ref[0], sc_ref[0])
    o_ref[...] = _dot(h.astype(BF16), w_ref[...]).astype(BF16)


def _inproj(x2, g, ada3, w_bf16, seq):
    t, d = x2.shape
    n = w_bf16.shape[1]
    tm = min(512, seq)
    nj = 2
    tn = n // nj
    per_b = seq // tm
    return pl.pallas_call(
        _inproj_body,
        grid=(nj, t // tm),
        in_specs=[
            pl.BlockSpec((tm, d), lambda j, i: (i, 0)),
            pl.BlockSpec((1, d), lambda j, i: (0, 0)),
            pl.BlockSpec((1, 1, d), lambda j, i: (i // per_b, 0, 0)),
            pl.BlockSpec((1, 1, d), lambda j, i: (i // per_b, 0, 1)),
            pl.BlockSpec((d, tn), lambda j, i: (0, j)),
        ],
        out_specs=pl.BlockSpec((tm, tn), lambda j, i: (i, j)),
        out_shape=jax.ShapeDtypeStruct((t, n), BF16),
        compiler_params=pltpu.CompilerParams(
            dimension_semantics=("arbitrary", "arbitrary"), vmem_limit_bytes=VMEM_LIMIT),
        name="inproj",
    )(x2, g, ada3, ada3, w_bf16)


def _hgrn_body(q_ref, f_ref, i_ref, og_ref, lbl_ref, g_ref, o_ref, *, seq, layer):
    rows, chunk = HG_ROWS, HG_CHUNK
    nchunk = rows // chunk
    lbl = lbl_ref[...].astype(F32)
    e = jnp.exp(lbl - jnp.max(lbl, axis=0, keepdims=True))
    lb = jnp.sum(e[: layer + 1], axis=0, keepdims=True) / jnp.sum(e, axis=0, keepdims=True)
    r_i = lax.broadcasted_iota(I32, (rows, rows), 0)
    c_i = lax.broadcasted_iota(I32, (rows, rows), 1)
    tri = ((r_i // chunk) == (c_i // chunk)) & (r_i >= c_i)
    tri_b = jnp.where(tri, 1.0, 0.0).astype(BF16)
    row_chunk = lax.broadcasted_iota(I32, (rows, HG_DK), 0) // chunk
    g = g_ref[...].astype(F32)

    def block(r, st):
        sl = pl.ds(r * rows, rows)
        qr = q_ref[sl, :].astype(F32)
        fr = f_ref[sl, :].astype(F32)
        v = i_ref[sl, :].astype(F32)
        og = og_ref[sl, :].astype(F32)
        q = qr * _sigmoid(qr)
        f = lb + (1.0 - lb) * _sigmoid(fr)
        k = 1.0 - f
        logf = jnp.log(f)
        lhi, llo = _split_bf16(logf)
        bc2 = _dot(tri_b, jnp.concatenate([lhi, llo], axis=1))
        bcum = bc2[:, :HG_DK] + bc2[:, HG_DK:]
        b3 = bcum.reshape(nchunk, chunk, HG_DK)
        bl = b3[:, chunk - 1:chunk, :]
        dec = jnp.exp(bl)
        kt_f = k * jnp.exp(-bcum)
        qt_f = q * jnp.exp(bcum)
        kd = (kt_f.reshape(nchunk, chunk, HG_DK) * dec).reshape(rows, HG_DK)
        a = jnp.where(tri, _dot_nt(qt_f.astype(BF16), kt_f.astype(BF16)), 0.0).astype(BF16)
        vt_b = v.T.astype(BF16)
        kd_x = jnp.concatenate([jnp.where(row_chunk == c, kd, 0.0) for c in range(nchunk)], axis=1)
        kv_all = _dot(vt_b, kd_x.astype(BF16))
        starts = []
        for c in range(nchunk):
            starts.append(st.astype(BF16))
            st = st * dec[c] + kv_all[:, c * HG_DK:(c + 1) * HG_DK]
        q_x = [jnp.where(row_chunk == c, qt_f, 0.0).astype(BF16) for c in range(nchunk)]
        o = _dot_nt(jnp.concatenate([a] + q_x, axis=1), jnp.concatenate([vt_b] + starts, axis=1))
        ms = jnp.mean(o * o, axis=-1, keepdims=True)
        o = o * lax.rsqrt(ms + NORM_EPS) * g
        o_ref[sl, :] = (o * (og * _sigmoid(og))).astype(o_ref.dtype)
        return st

    st = jnp.zeros((HG_DV, HG_DK), F32)
    for r in range(seq // rows):
        st = block(r, st)


def _hgrn(proj, lb_logits, norm_g, bsz, seq, col0, layer):
    t = proj.shape[0]
    blk = lambda off: pl.BlockSpec((seq, LANES), lambda b, h, off=off: (b, col0 + off + h))
    return pl.pallas_call(
        functools.partial(_hgrn_body, seq=seq, layer=layer),
        grid=(bsz, HG_HEADS),
        in_specs=[
            blk(0), blk(HG_HEADS), blk(2 * HG_HEADS), blk(3 * HG_HEADS),
            pl.BlockSpec((lb_logits.shape[0], HG_DK), lambda b, h: (0, h)),
            pl.BlockSpec((1, HG_DV), lambda b, h: (0, 0)),
        ],
        out_specs=pl.BlockSpec((seq, HG_DV), lambda b, h: (b, h)),
        out_shape=jax.ShapeDtypeStruct((t, HG_HEADS * HG_DV), BF16),
        compiler_params=pltpu.CompilerParams(
            dimension_semantics=("arbitrary", "arbitrary"), vmem_limit_bytes=VMEM_LIMIT),
        name="hgrn",
    )(proj, proj, proj, proj, lb_logits, norm_g)


def _group_norm(x, gsum_b, gain):
    hi, lo = _split_bf16(x * x)
    ss = _dot(hi, gsum_b) + _dot(lo, gsum_b)
    return x * lax.rsqrt(ss * (1.0 / DA_DH) + NORM_EPS) * gain


def _attn_body(q_ref, k_ref, v_ref, qg_ref, kg_ref, lam_ref, sg_ref, o_ref, kn_scr, v1_scr, *, seq, lambda_init):
    tq = ATTN_TILE
    width = 2 * DA_DH
    r_l = lax.broadcasted_iota(I32, (width, width), 0) // DA_DH
    c_l = lax.broadcasted_iota(I32, (width, width), 1) // DA_DH
    gsum_b = jnp.where(r_l == c_l, 1.0, 0.0).astype(BF16)
    lane = lax.broadcasted_iota(I32, (1, width), 1)
    kg = kg_ref[...].astype(F32)
    qg = qg_ref[...].astype(F32) * (DA_DH ** -0.5 * LOG2E)
    sg = sg_ref[...].astype(F32) * (1.0 - lambda_init)
    ones = jnp.ones((tq, width), BF16)
    row = lax.broadcasted_iota(I32, (tq, tq), 0)
    col = lax.broadcasted_iota(I32, (tq, tq), 1)
    keep = row >= col

    lam_v = lam_ref[...].astype(F32)
    lam = (jnp.exp(jnp.sum(lam_v[0:1] * lam_v[1:2], axis=-1, keepdims=True))
           - jnp.exp(jnp.sum(lam_v[2:3] * lam_v[3:4], axis=-1, keepdims=True)) + lambda_init)

    def softmax_v(qc, nk):
        s = _dot_nt(qc, kn_scr[0:nk, :])
        diag = jnp.where(keep, s[:, nk - tq:], -jnp.inf)
        s = diag if nk == tq else jnp.concatenate([s[:, :nk - tq], diag], axis=1)
        m = jnp.max(s, axis=-1, keepdims=True)
        return _dot(jnp.exp2(s - m).astype(BF16), v1_scr[0:nk, :])

    for i in range(seq // tq):
        sl = slice(i * tq, (i + 1) * tq)
        kn_scr[sl, :] = _group_norm(k_ref[sl, :].astype(F32), gsum_b, kg).astype(BF16)
        v1_scr[sl, :] = jnp.concatenate([v_ref[sl, :], ones], axis=1)
        qn = _group_norm(q_ref[sl, :].astype(F32), gsum_b, qg)
        nk = (i + 1) * tq
        a1 = softmax_v(jnp.where(lane < DA_DH, qn, 0.0).astype(BF16), nk)
        a2 = softmax_v(jnp.where(lane >= DA_DH, qn, 0.0).astype(BF16), nk)
        o = a1[:, :width] / a1[:, width:width + 1] - lam * (a2[:, :width] / a2[:, width:width + 1])
        ms = jnp.mean(o * o, axis=-1, keepdims=True)
        o_ref[sl, :] = (o * lax.rsqrt(ms + NORM_EPS) * sg).astype(o_ref.dtype)


def _attn(proj, qg2, kg2, lam4, subln_g, bsz, seq, col0, lambda_init):
    t = proj.shape[0]
    width = 2 * DA_DH
    return pl.pallas_call(
        functools.partial(_attn_body, seq=seq, lambda_init=lambda_init),
        grid=(bsz, DA_HEADS),
        in_specs=[
            pl.BlockSpec((seq, width), lambda b, h: (b, col0 + h)),
            pl.BlockSpec((seq, width), lambda b, h: (b, col0 + DA_HEADS + h)),
            pl.BlockSpec((seq, width), lambda b, h: (b, col0 + 2 * DA_HEADS + h)),
            pl.BlockSpec((1, width), lambda b, h: (0, 0)),
            pl.BlockSpec((1, width), lambda b, h: (0, 0)),
            pl.BlockSpec((4, DA_DH), lambda b, h: (0, 0)),
            pl.BlockSpec((1, width), lambda b, h: (0, 0)),
        ],
        out_specs=pl.BlockSpec((seq, width), lambda b, h: (b, h)),
        out_shape=jax.ShapeDtypeStruct((t, DA_HEADS * width), BF16),
        scratch_shapes=[pltpu.VMEM((seq, width), BF16), pltpu.VMEM((seq, 2 * width), BF16)],
        compiler_params=pltpu.CompilerParams(
            dimension_semantics=("arbitrary", "arbitrary"), vmem_limit_bytes=VMEM_LIMIT),
        name="attn",
    )(proj, proj, proj, qg2, kg2, lam4, subln_g)


def _mixout_body(x_ref, oa_ref, od_ref, ga_ref, gd_ref, woa_ref, wod_ref, g1_ref, g_ref, sh_ref, sc_ref,
                 wr_ref, br_ref,
                 x1_ref, h2_ref, idx_ref, rank_ref, gcol_ref, cnt_ref, carry_scr):
    i = pl.program_id(0)
    tm = x_ref.shape[0]

    @pl.when(i == 0)
    def _():
        carry_scr[...] = jnp.zeros_like(carry_scr)

    ya = _dot(oa_ref[...], woa_ref[...])
    yd = _dot(od_ref[...], wod_ref[...])
    y = _sigmoid(ga_ref[...].astype(F32)) * ya + _sigmoid(gd_ref[...].astype(F32)) * yd
    x1 = x_ref[...] + g1_ref[0] * y
    x1_ref[...] = x1
    h2 = _norm_mod(x1, g_ref[...], sh_ref[0], sc_ref[0])
    h2_ref[...] = _pack_halves(h2)

    hh, hl = _split_bf16(h2)
    wh, wl = _split_bf16(wr_ref[...])
    logits = _dot_nt(wh, hh) + _dot_nt(wl, hh) + _dot_nt(wh, hl) + br_ref[...]

    e_iota = lax.broadcasted_iota(I32, (N_EXPERTS, tm), 0).astype(F32)
    vals = logits
    tops, sels, idxs = [], [], []
    for _ in range(TOP_K):
        m = jnp.max(vals, axis=0, keepdims=True)
        idx = jnp.min(jnp.where(vals == m, e_iota, float(N_EXPERTS)), axis=0, keepdims=True)
        sel = e_iota == idx
        vals = jnp.where(sel, -jnp.inf, vals)
        tops.append(m)
        sels.append(sel)
        idxs.append(idx)
    ex = [jnp.exp(tv - tops[0]) for tv in tops]
    den = ex[0] + ex[1] + ex[2] + ex[3]
    gates = [v / den for v in ex]

    hot = jnp.where(sels[0] | sels[1] | sels[2] | sels[3], 1.0, 0.0)
    r_t = lax.broadcasted_iota(I32, (tm, tm), 0)
    c_t = lax.broadcasted_iota(I32, (tm, tm), 1)
    upper = jnp.where(r_t < c_t, 1.0, 0.0).astype(BF16)
    excl = _dot(hot.astype(BF16), upper) + carry_scr[:, 0:1]
    carry_scr[...] = carry_scr[...] + jnp.sum(hot, axis=1, keepdims=True)
    cnt_ref[...] = carry_scr[...]

    ranks = [jnp.sum(jnp.where(s, excl, 0.0), axis=0, keepdims=True) for s in sels]
    idx_ref[...] = jnp.concatenate(idxs, axis=0).astype(I32)
    rank_ref[...] = jnp.concatenate(ranks, axis=0).astype(I32)
    gpad = jnp.concatenate(gates + [jnp.zeros((LANES - TOP_K, tm), F32)], axis=0)
    gcol_ref[...] = gpad.T


def _mixout(x2, oa, od, proj, woa, wod, ada3, ffn_g, wr_t, br_col, seq):
    t, d = x2.shape
    tm = min(512, seq)
    per_b = seq // tm
    hw = oa.shape[1]
    row = lambda w: pl.BlockSpec((tm, w), lambda i: (i, 0))
    mod = lambda c: pl.BlockSpec((1, 1, d), lambda i, c=c: (i // per_b, 0, c))
    full = lambda a: pl.BlockSpec(a.shape, lambda i: (0,) * a.ndim)
    return pl.pallas_call(
        _mixout_body,
        grid=(t // tm,),
        in_specs=[
            row(d), row(hw), row(hw),
            pl.BlockSpec((tm, d), lambda i: (i, 0)),
            pl.BlockSpec((tm, d), lambda i: (i, 1)),
            full(woa), full(wod),
            mod(2),
            full(ffn_g), mod(3), mod(4),
            full(wr_t), full(br_col),
        ],
        out_specs=[
            row(d), row(d // 2),
            pl.BlockSpec((TOP_K, tm), lambda i: (0, i)),
            pl.BlockSpec((TOP_K, tm), lambda i: (0, i)),
            pl.BlockSpec((tm, LANES), lambda i: (i, 0)),
            pl.BlockSpec((N_EXPERTS, LANES), lambda i: (0, 0)),
        ],
        out_shape=[
            jax.ShapeDtypeStruct((t, d), F32),
            jax.ShapeDtypeStruct((t, d // 2), I32),
            jax.ShapeDtypeStruct((TOP_K, t), I32),
            jax.ShapeDtypeStruct((TOP_K, t), I32),
            jax.ShapeDtypeStruct((t, LANES), F32),
            jax.ShapeDtypeStruct((N_EXPERTS, LANES), F32),
        ],
        scratch_shapes=[pltpu.VMEM((N_EXPERTS, LANES), F32)],
        compiler_params=pltpu.CompilerParams(
            dimension_semantics=("arbitrary",), vmem_limit_bytes=VMEM_LIMIT),
        name="mixout",
    )(x2, oa, od, proj, proj, woa, wod, ada3, ffn_g, ada3, ada3, wr_t, br_col)


def _sc_mesh():
    return plsc.VectorSubcoreMesh(core_axis_name="c", subcore_axis_name="s")


def _sc_worker_base(rows_per_worker):
    wid = lax.axis_index("s") * SC_CORES + lax.axis_index("c")
    return wid * rows_per_worker


def _sc_dispatch(h2, dest, n_rows):
    t, d = h2.shape
    tpw = t // SC_WORKERS
    assert t % (SC_WORKERS * SC_CHUNK) == 0

    def body(h2_hbm, dest_hbm, xb_hbm, idx_v, rows_v):
        base = _sc_worker_base(tpw)

        @pl.loop(0, tpw // SC_CHUNK)
        def _(i):
            t0 = pl.multiple_of(base + i * SC_CHUNK, SC_CHUNK)
            pltpu.sync_copy(h2_hbm.at[pl.ds(t0, SC_CHUNK)], rows_v)
            for k in range(TOP_K):
                pltpu.sync_copy(dest_hbm.at[pl.ds(k * t + t0, SC_CHUNK)], idx_v)
                pltpu.sync_copy(rows_v, xb_hbm.at[idx_v])

    return pl.kernel(
        body, out_type=jax.ShapeDtypeStruct((n_rows, d), h2.dtype), mesh=_sc_mesh(),
        scratch_types=[pltpu.VMEM((SC_CHUNK,), I32), pltpu.VMEM((SC_CHUNK, d), h2.dtype)],
        name="dispatch",
    )(h2, dest)


def _sc_undispatch(y, dest):
    n_asg = dest.shape[0]
    d = y.shape[1]
    rpw = n_asg // SC_WORKERS
    assert n_asg % (SC_WORKERS * SC_CHUNK) == 0

    def body(y_hbm, dest_hbm, yt_hbm, idx_v, rows_v):
        base = _sc_worker_base(rpw)

        @pl.loop(0, rpw // SC_CHUNK)
        def _(i):
            r0 = pl.multiple_of(base + i * SC_CHUNK, SC_CHUNK)
            pltpu.sync_copy(dest_hbm.at[pl.ds(r0, SC_CHUNK)], idx_v)
            pltpu.sync_copy(y_hbm.at[idx_v], rows_v)
            pltpu.sync_copy(rows_v, yt_hbm.at[pl.ds(r0, SC_CHUNK)])

    return pl.kernel(
        body, out_type=jax.ShapeDtypeStruct((n_asg, d), y.dtype), mesh=_sc_mesh(),
        scratch_types=[pltpu.VMEM((SC_CHUNK,), I32), pltpu.VMEM((SC_CHUNK, d), y.dtype)],
        name="undispatch",
    )(y, dest)


def _ffn_body(be_ref, first_ref, nxt_ref, slot_ref, nact_ref, x_ref, w1_hbm, b1_ref, w2_hbm, b2_ref, y_ref,
              w1f, w2f, w1c, w2c, sem1, sem2):
    j = pl.program_id(0)
    ff = w2f.shape[1]
    pair = 2 * LANES
    ngroup = (2 * ff) // pair

    def weight_copies(e, slot):
        return (pltpu.make_async_copy(w1_hbm.at[e], w1f.at[slot], sem1.at[slot]),
                pltpu.make_async_copy(w2_hbm.at[e], w2f.at[slot], sem2.at[slot]))

    @pl.when(j == 0)
    def _():
        for cp in weight_copies(be_ref[0], 0):
            cp.start()

    @pl.when(first_ref[j] == 1)
    def _():
        slot = slot_ref[j]
        for cp in weight_copies(be_ref[j], slot):
            cp.wait()

        @pl.when(nxt_ref[j] >= 0)
        def _():
            for cp in weight_copies(nxt_ref[j], 1 - slot):
                cp.start()

        r_p = lax.broadcasted_iota(I32, (pair, pair), 0)
        c_p = lax.broadcasted_iota(I32, (pair, pair), 1)
        src = jnp.where(c_p < LANES, 2 * c_p, 2 * (c_p - LANES) + 1)
        perm = jnp.where(r_p == src, 1.0, 0.0).astype(BF16)
        for g in range(ngroup):
            cols = slice(g * pair, (g + 1) * pair)
            w1c[:, cols] = _dot(w1f[slot, :, cols].astype(BF16), perm).astype(BF16)
        w2c[...] = w2f[slot].astype(BF16)

    @pl.when(j < nact_ref[0])
    def _():
        x_lo, x_hi = _unpack_halves(x_ref[...])
        xb = jnp.concatenate([x_lo.astype(BF16), x_hi.astype(BF16)], axis=1)
        u = _dot(xb, w1c[...]) + b1_ref[0]
        acts = []
        for g in range(ngroup):
            glu = jnp.minimum(u[:, g * pair:g * pair + LANES], SWIGLU_LIMIT)
            lin = jnp.clip(u[:, g * pair + LANES:(g + 1) * pair], -SWIGLU_LIMIT, SWIGLU_LIMIT)
            acts.append((glu * _sigmoid(SWIGLU_ALPHA * glu) * (lin + 1.0)).astype(BF16))
        act = jnp.concatenate(acts, axis=1)
        y_ref[...] = _pack_halves(_dot(act, w2c[...]) + b2_ref[0])


def _ffn(block_expert, first, nxt, slot, nact, xb, w1, b1p, w2, b2):
    bm = FFN_BLOCK
    n_rows, dw = xb.shape
    d = 2 * dw
    n_blocks = n_rows // bm
    ff2 = w1.shape[2]
    ff = w2.shape[1]
    row_blk = lambda j, be, fi, nx, sl, na: (jnp.minimum(j, na[0] - 1), 0)
    bias_blk = lambda j, be, fi, nx, sl, na: (be[j], 0, 0)
    grid_spec = pltpu.PrefetchScalarGridSpec(
        num_scalar_prefetch=5,
        grid=(n_blocks,),
        in_specs=[
            pl.BlockSpec((bm, dw), row_blk),
            pl.BlockSpec(memory_space=pl.ANY),
            pl.BlockSpec((1, 1, ff2), bias_blk),
            pl.BlockSpec(memory_space=pl.ANY),
            pl.BlockSpec((1, 1, d), bias_blk),
        ],
        out_specs=pl.BlockSpec((bm, dw), row_blk),
        scratch_shapes=[
            pltpu.VMEM((2, d, ff2), F32), pltpu.VMEM((2, ff, d), F32),
            pltpu.VMEM((d, ff2), BF16), pltpu.VMEM((ff, d), BF16),
            pltpu.SemaphoreType.DMA((2,)), pltpu.SemaphoreType.DMA((2,)),
        ],
    )
    return pl.pallas_call(
        _ffn_body,
        grid_spec=grid_spec,
        out_shape=jax.ShapeDtypeStruct((n_rows, dw), I32),
        compiler_params=pltpu.CompilerParams(
            dimension_semantics=("arbitrary",), vmem_limit_bytes=VMEM_LIMIT),
        name="ffn",
    )(block_expert, first, nxt, slot, nact, xb, w1, b1p, w2, b2)


def _combine_body(x1_ref, y0_ref, y1_ref, y2_ref, y3_ref, gcol_ref, g2_ref, o_ref):
    gc = gcol_ref[...]
    m_lo = m_hi = None
    for k, y_ref in enumerate((y0_ref, y1_ref, y2_ref, y3_ref)):
        lo, hi = _unpack_halves(y_ref[...])
        gk = gc[:, k:k + 1]
        m_lo = gk * lo if m_lo is None else m_lo + gk * lo
        m_hi = gk * hi if m_hi is None else m_hi + gk * hi
    m = jnp.concatenate([m_lo, m_hi], axis=1)
    o_ref[...] = (x1_ref[...] + g2_ref[0] * m).astype(o_ref.dtype)


def _combine(x1, yall, gcol, ada3, seq, out_dtype):
    t, d = x1.shape
    tm = min(512, seq)
    per_b = seq // tm
    nt = t // tm
    yk = lambda k: pl.BlockSpec((tm, d // 2), lambda i, k=k: (k * nt + i, 0))
    return pl.pallas_call(
        _combine_body,
        grid=(nt,),
        in_specs=[
            pl.BlockSpec((tm, d), lambda i: (i, 0)),
            yk(0), yk(1), yk(2), yk(3),
            pl.BlockSpec((tm, LANES), lambda i: (i, 0)),
            pl.BlockSpec((1, 1, d), lambda i: (i // per_b, 0, 5)),
        ],
        out_specs=pl.BlockSpec((tm, d), lambda i: (i, 0)),
        out_shape=jax.ShapeDtypeStruct((t, d), out_dtype),
        compiler_params=pltpu.CompilerParams(
            dimension_semantics=("arbitrary",), vmem_limit_bytes=VMEM_LIMIT),
        name="combine",
    )(x1, yall, yall, yall, yall, gcol, ada3)


def _route_tables(idx, rank, counts, n_tok):
    bm = FFN_BLOCK
    n_asg = TOP_K * n_tok
    n_blocks = -(-(n_asg + N_EXPERTS * (bm - 1)) // bm)
    padded = (counts + bm - 1) // bm * bm
    pad_ends = jnp.cumsum(padded)
    pad_starts = pad_ends - padded
    e_ids = jnp.arange(N_EXPERTS, dtype=I32)
    start_of = jnp.sum(jnp.where(idx[None] == e_ids[:, None, None], pad_starts[:, None, None], 0), axis=0)
    dest = (start_of + rank).reshape(-1)
    nact = (pad_ends[-1] // bm).astype(I32)
    blk_start = jnp.arange(n_blocks, dtype=I32) * bm
    last = jnp.sum(jnp.where(pad_ends <= pad_ends[-1] - 1, 1, 0)).astype(I32)
    be = jnp.sum(jnp.where(pad_ends[None, :] <= blk_start[:, None], 1, 0), axis=1).astype(I32)
    active = blk_start < pad_ends[-1]
    be = jnp.where(active, be, last)
    blk = jnp.arange(n_blocks, dtype=I32)
    first = active & ((blk == 0) | (be != jnp.roll(be, 1)))
    slot = (jnp.cumsum(first.astype(I32)) - 1) & 1
    later_first = first[None, :] & (blk[None, :] > blk[:, None])
    nxt_pos = jnp.min(jnp.where(later_first, blk[None, :], n_blocks), axis=1)
    nxt = jnp.sum(jnp.where(blk[None, :] == nxt_pos[:, None], be[None, :], 0), axis=1)
    nxt = jnp.where(nxt_pos < n_blocks, nxt, -1).astype(I32)
    return be, first.astype(I32), nxt, slot.astype(I32), nact.reshape(1), dest, n_blocks * bm


def kernel(x, c, w_ada, b_ada, mix_norm_g, ffn_norm_g, w_in, hg_lower_bound_logits, hg_out_norm_g, da_q_norm_g, da_k_norm_g, da_lambda_q1, da_lambda_k1, da_lambda_q2, da_lambda_k2, da_subln_g, w_out, w_router, b_router, w1, b1, w2, b2):
    bsz, seq, d = x.shape
    t = bsz * seq
    depth = w_ada.shape[0]
    out_dtype = x.dtype
    hw = HG_HEADS * HG_DV
    xcur = x.reshape(t, d)
    for l in range(depth):
        ada = _ada(c, w_ada[l], b_ada[l])
        ada3 = ada.reshape(bsz, 1, N_MOD * d)
        wi = w_in[l]
        n_in = wi.shape[1]
        w_in_r = jnp.concatenate([wi[:, n_in - 2 * d:], wi[:, :n_in - 2 * d]], axis=1).astype(BF16)
        col_h = (2 * d) // LANES
        col_a = col_h + 4 * HG_HEADS
        proj = _inproj(xcur, mix_norm_g[l].reshape(1, d), ada3, w_in_r, seq)

        o_a = _hgrn(proj, hg_lower_bound_logits, hg_out_norm_g[l].reshape(1, HG_DV), bsz, seq, col_h, l)
        lambda_init = 0.8 - 0.6 * math.exp(-0.3 * l)
        qg2 = jnp.tile(da_q_norm_g[l], 2).reshape(1, 2 * DA_DH)
        kg2 = jnp.tile(da_k_norm_g[l], 2).reshape(1, 2 * DA_DH)
        lam4 = jnp.stack([da_lambda_q1[l], da_lambda_k1[l], da_lambda_q2[l], da_lambda_k2[l]])
        o_d = _attn(proj, qg2, kg2, lam4, da_subln_g[l].reshape(1, 2 * DA_DH), bsz, seq, col_a, lambda_init)

        wo = w_out[l].astype(BF16)
        x1, h2, idx, rank, gcol, cnt = _mixout(
            xcur, o_a, o_d, proj, wo[:hw], wo[hw:], ada3, ffn_norm_g[l].reshape(1, d),
            w_router[l].T, b_router[l].reshape(N_EXPERTS, 1), seq)

        counts = cnt[:, 0].astype(I32)
        be, first, nxt, slot, nact, dest, n_rows = _route_tables(idx, rank, counts, t)
        b1p = b1[l].reshape(N_EXPERTS, -1, LANES, 2).transpose(0, 1, 3, 2).reshape(N_EXPERTS, 1, -1)
        xb = _sc_dispatch(h2, dest, n_rows)
        yb = _ffn(be, first, nxt, slot, nact, xb, w1[l], b1p, w2[l], b2[l].reshape(N_EXPERTS, 1, d))
        yall = _sc_undispatch(yb, dest)
        xcur = _combine(x1, yall, gcol, ada3, seq, out_dtype)
    return xcur.reshape(bsz, seq, d)
```

```python
import functools
import math

import jax
import jax.numpy as jnp
from jax import lax
from jax.experimental import pallas as pl
from jax.experimental.pallas import tpu as pltpu
from jax.experimental.pallas import tpu_sc as plsc

F32 = jnp.float32
BF16 = jnp.bfloat16
I32 = jnp.int32

HG_HEADS = 4
HG_DK = 128
HG_DV = 128
HG_CHUNK = 32
DA_HEADS = 4
DA_DH = 64
N_EXPERTS = 32
TOP_K = 4
SWIGLU_ALPHA = 1.702
SWIGLU_LIMIT = 7.0
NORM_EPS = 1e-6
LOG2E = math.log2(math.e)
N_MOD = 6

LANES = 128
VMEM_LIMIT = 56 * 1024 * 1024

HG_ROWS = 256
ATTN_TILE = 256
FFN_BLOCK = 512

SC_CORES = 2
SC_WORKERS = SC_CORES * 16
SC_CHUNK = 128


def _sigmoid(x):
    return 1.0 / (1.0 + jnp.exp(-x))


def _dot(a, b):
    return jnp.dot(a, b, preferred_element_type=F32)


def _dot_nt(a, b):
    return lax.dot_general(a, b, (((1,), (1,)), ((), ())), preferred_element_type=F32)


def _split_bf16(x):
    hi = x.astype(BF16)
    lo = (x - hi.astype(F32)).astype(BF16)
    return hi, lo


def _pack_halves(x):
    half = x.shape[1] // 2
    lo = lax.bitcast_convert_type(x[:, :half].astype(BF16).astype(F32), I32)
    hi = lax.bitcast_convert_type(x[:, half:].astype(BF16).astype(F32), I32)
    return lax.shift_right_logical(lo, 16) | hi


def _unpack_halves(w):
    lo = lax.bitcast_convert_type(lax.shift_left(w, 16), F32)
    hi = lax.bitcast_convert_type(w & jnp.int32(-65536), F32)
    return lo, hi


def _ada_body(c_ref, w_ref, b_ref, o_ref):
    c = c_ref[...].astype(F32)
    ca = c * _sigmoid(c)
    o_ref[...] = jnp.dot(ca, w_ref[...], preferred_element_type=F32,
                         precision=lax.Precision.HIGHEST) + b_ref[...]


def _ada(c, w_ada, b_ada):
    bsz, d = c.shape
    n = w_ada.shape[1]
    tn = d
    return pl.pallas_call(
        _ada_body,
        grid=(n // tn,),
        in_specs=[
            pl.BlockSpec((bsz, d), lambda j: (0, 0)),
            pl.BlockSpec((d, tn), lambda j: (0, j)),
            pl.BlockSpec((1, tn), lambda j: (0, j)),
        ],
        out_specs=pl.BlockSpec((bsz, tn), lambda j: (0, j)),
        out_shape=jax.ShapeDtypeStruct((bsz, n), F32),
        name="ada",
    )(c, w_ada, b_ada.reshape(1, n))


def _norm_mod(x, g, shift, scale):
    ms = jnp.mean(x * x, axis=-1, keepdims=True)
    return (x * lax.rsqrt(ms + NORM_EPS) * g) * (1.0 + scale) + shift


def _inproj_body(x_ref, g_ref, sh_ref, sc_ref, w_ref, o_ref):
    h = _norm_mod(x_ref[...], g_ref[...], sh_ref[0], sc_ref[0])
    o_ref[...] = _dot(h.astype(BF16), w_ref[...]).astype(BF16)


def _inproj(x2, g, ada3, w_bf16, seq):
    t, d = x2.shape
    n = w_bf16.shape[1]
    tm = min(512, seq)
    nj = 2
    tn = n // nj
    per_b = seq // tm
    return pl.pallas_call(
        _inproj_body,
        grid=(nj, t // tm),
        in_specs=[
            pl.BlockSpec((tm, d), lambda j, i: (i, 0)),
            pl.BlockSpec((1, d), lambda j, i: (0, 0)),
            pl.BlockSpec((1, 1, d), lambda j, i: (i // per_b, 0, 0)),
            pl.BlockSpec((1, 1, d), lambda j, i: (i // per_b, 0, 1)),
            pl.BlockSpec((d, tn), lambda j, i: (0, j)),
        ],
        out_specs=pl.BlockSpec((tm, tn), lambda j, i: (i, j)),
        out_shape=jax.ShapeDtypeStruct((t, n), BF16),
        compiler_params=pltpu.CompilerParams(
            dimension_semantics=("arbitrary", "arbitrary"), vmem_limit_bytes=VMEM_LIMIT),
        name="inproj",
    )(x2, g, ada3, ada3, w_bf16)


def _hgrn_body(q_ref, f_ref, i_ref, og_ref, lbl_ref, g_ref, o_ref, *, seq, layer):
    rows, chunk = HG_ROWS, HG_CHUNK
    nchunk = rows // chunk
    lbl = lbl_ref[...].astype(F32)
    e = jnp.exp(lbl - jnp.max(lbl, axis=0, keepdims=True))
    lb = jnp.sum(e[: layer + 1], axis=0, keepdims=True) / jnp.sum(e, axis=0, keepdims=True)
    r_i = lax.broadcasted_iota(I32, (rows, rows), 0)
    c_i = lax.broadcasted_iota(I32, (rows, rows), 1)
    tri = ((r_i // chunk) == (c_i // chunk)) & (r_i >= c_i)
    tri_b = jnp.where(tri, 1.0, 0.0).astype(BF16)
    row_chunk = lax.broadcasted_iota(I32, (rows, HG_DK), 0) // chunk
    g = g_ref[...].astype(F32)

    def block(r, st):
        sl = pl.ds(r * rows, rows)
        qr = q_ref[sl, :].astype(F32)
        fr = f_ref[sl, :].astype(F32)
        v = i_ref[sl, :].astype(F32)
        og = og_ref[sl, :].astype(F32)
        q = qr * _sigmoid(qr)
        f = lb + (1.0 - lb) * _sigmoid(fr)
        k = 1.0 - f
        logf = jnp.log(f)
        lhi, llo = _split_bf16(logf)
        bc2 = _dot(tri_b, jnp.concatenate([lhi, llo], axis=1))
        bcum = bc2[:, :HG_DK] + bc2[:, HG_DK:]
        b3 = bcum.reshape(nchunk, chunk, HG_DK)
        bl = b3[:, chunk - 1:chunk, :]
        dec = jnp.exp(bl)
        kt_f = k * jnp.exp(-bcum)
        qt_f = q * jnp.exp(bcum)
        kd = (kt_f.reshape(nchunk, chunk, HG_DK) * dec).reshape(rows, HG_DK)
        a = jnp.where(tri, _dot_nt(qt_f.astype(BF16), kt_f.astype(BF16)), 0.0).astype(BF16)
        vt_b = v.T.astype(BF16)
        kd_x = jnp.concatenate([jnp.where(row_chunk == c, kd, 0.0) for c in range(nchunk)], axis=1)
        kv_all = _dot(vt_b, kd_x.astype(BF16))
        starts = []
        for c in range(nchunk):
            starts.append(st.astype(BF16))
            st = st * dec[c] + kv_all[:, c * HG_DK:(c + 1) * HG_DK]
        q_x = [jnp.where(row_chunk == c, qt_f, 0.0).astype(BF16) for c in range(nchunk)]
        o = _dot_nt(jnp.concatenate([a] + q_x, axis=1), jnp.concatenate([vt_b] + starts, axis=1))
        ms = jnp.mean(o * o, axis=-1, keepdims=True)
        o = o * lax.rsqrt(ms + NORM_EPS) * g
        o_ref[sl, :] = (o * (og * _sigmoid(og))).astype(o_ref.dtype)
        return st

    st = jnp.zeros((HG_DV, HG_DK), F32)
    for r in range(seq // rows):
        st = block(r, st)


def _hgrn(proj, lb_logits, norm_g, bsz, seq, col0, layer):
    t = proj.shape[0]
    blk = lambda off: pl.BlockSpec((seq, LANES), lambda b, h, off=off: (b, col0 + off + h))
    return pl.pallas_call(
        functools.partial(_hgrn_body, seq=seq, layer=layer),
        grid=(bsz, HG_HEADS),
        in_specs=[
            blk(0), blk(HG_HEADS), blk(2 * HG_HEADS), blk(3 * HG_HEADS),
            pl.BlockSpec((lb_logits.shape[0], HG_DK), lambda b, h: (0, h)),
            pl.BlockSpec((1, HG_DV), lambda b, h: (0, 0)),
        ],
        out_specs=pl.BlockSpec((seq, HG_DV), lambda b, h: (b, h)),
        out_shape=jax.ShapeDtypeStruct((t, HG_HEADS * HG_DV), BF16),
        compiler_params=pltpu.CompilerParams(
            dimension_semantics=("arbitrary", "arbitrary"), vmem_limit_bytes=VMEM_LIMIT),
        name="hgrn",
    )(proj, proj, proj, proj, lb_logits, norm_g)


def _group_norm(x, gsum_b, gain):
    hi, lo = _split_bf16(x * x)
    ss = _dot(hi, gsum_b) + _dot(lo, gsum_b)
    return x * lax.rsqrt(ss * (1.0 / DA_DH) + NORM_EPS) * gain


def _attn_body(q_ref, k_ref, v_ref, qg_ref, kg_ref, lam_ref, sg_ref, o_ref, kn_scr, v1_scr, *, seq, lambda_init):
    tq = ATTN_TILE
    width = 2 * DA_DH
    r_l = lax.broadcasted_iota(I32, (width, width), 0) // DA_DH
    c_l = lax.broadcasted_iota(I32, (width, width), 1) // DA_DH
    gsum_b = jnp.where(r_l == c_l, 1.0, 0.0).astype(BF16)
    lane = lax.broadcasted_iota(I32, (1, width), 1)
    kg = kg_ref[...].astype(F32)
    qg = qg_ref[...].astype(F32) * (DA_DH ** -0.5 * LOG2E)
    sg = sg_ref[...].astype(F32) * (1.0 - lambda_init)
    ones = jnp.ones((tq, width), BF16)
    row = lax.broadcasted_iota(I32, (tq, tq), 0)
    col = lax.broadcasted_iota(I32, (tq, tq), 1)
    keep = row >= col

    lam_v = lam_ref[...].astype(F32)
    lam = (jnp.exp(jnp.sum(lam_v[0:1] * lam_v[1:2], axis=-1, keepdims=True))
           - jnp.exp(jnp.sum(lam_v[2:3] * lam_v[3:4], axis=-1, keepdims=True)) + lambda_init)

    def softmax_v(qc, nk):
        s = _dot_nt(qc, kn_scr[0:nk, :])
        diag = jnp.where(keep, s[:, nk - tq:], -jnp.inf)
        s = diag if nk == tq else jnp.concatenate([s[:, :nk - tq], diag], axis=1)
        m = jnp.max(s, axis=-1, keepdims=True)
        return _dot(jnp.exp2(s - m).astype(BF16), v1_scr[0:nk, :])

    for i in range(seq // tq):
        sl = slice(i * tq, (i + 1) * tq)
        kn_scr[sl, :] = _group_norm(k_ref[sl, :].astype(F32), gsum_b, kg).astype(BF16)
        v1_scr[sl, :] = jnp.concatenate([v_ref[sl, :], ones], axis=1)
        qn = _group_norm(q_ref[sl, :].astype(F32), gsum_b, qg)
        nk = (i + 1) * tq
        a1 = softmax_v(jnp.where(lane < DA_DH, qn, 0.0).astype(BF16), nk)
        a2 = softmax_v(jnp.where(lane >= DA_DH, qn, 0.0).astype(BF16), nk)
        o = a1[:, :width] / a1[:, width:width + 1] - lam * (a2[:, :width] / a2[:, width:width + 1])
        ms = jnp.mean(o * o, axis=-1, keepdims=True)
        o_ref[sl, :] = (o * lax.rsqrt(ms + NORM_EPS) * sg).astype(o_ref.dtype)


def _attn(proj, qg2, kg2, lam4, subln_g, bsz, seq, col0, lambda_init):
    t = proj.shape[0]
    width = 2 * DA_DH
    return pl.pallas_call(
        functools.partial(_attn_body, seq=seq, lambda_init=lambda_init),
        grid=(bsz, DA_HEADS),
        in_specs=[
            pl.BlockSpec((seq, width), lambda b, h: (b, col0 + h)),
            pl.BlockSpec((seq, width), lambda b, h: (b, col0 + DA_HEADS + h)),
            pl.BlockSpec((seq, width), lambda b, h: (b, col0 + 2 * DA_HEADS + h)),
            pl.BlockSpec((1, width), lambda b, h: (0, 0)),
            pl.BlockSpec((1, width), lambda b, h: (0, 0)),
            pl.BlockSpec((4, DA_DH), lambda b, h: (0, 0)),
            pl.BlockSpec((1, width), lambda b, h: (0, 0)),
        ],
        out_specs=pl.BlockSpec((seq, width), lambda b, h: (b, h)),
        out_shape=jax.ShapeDtypeStruct((t, DA_HEADS * width), BF16),
        scratch_shapes=[pltpu.VMEM((seq, width), BF16), pltpu.VMEM((seq, 2 * width), BF16)],
        compiler_params=pltpu.CompilerParams(
            dimension_semantics=("arbitrary", "arbitrary"), vmem_limit_bytes=VMEM_LIMIT),
        name="attn",
    )(proj, proj, proj, qg2, kg2, lam4, subln_g)


def _mixout_body(x_ref, oa_ref, od_ref, ga_ref, gd_ref, woa_ref, wod_ref, g1_ref, g_ref, sh_ref, sc_ref,
                 wr_ref, br_ref,
                 x1_ref, h2_ref, idx_ref, rank_ref, gcol_ref, cnt_ref, carry_scr):
    i = pl.program_id(0)
    tm = x_ref.shape[0]

    @pl.when(i == 0)
    def _():
        carry_scr[...] = jnp.zeros_like(carry_scr)

    ya = _dot(oa_ref[...], woa_ref[...])
    yd = _dot(od_ref[...], wod_ref[...])
    y = _sigmoid(ga_ref[...].astype(F32)) * ya + _sigmoid(gd_ref[...].astype(F32)) * yd
    x1 = x_ref[...] + g1_ref[0] * y
    x1_ref[...] = x1
    h2 = _norm_mod(x1, g_ref[...], sh_ref[0], sc_ref[0])
    h2_ref[...] = _pack_halves(h2)

    hh, hl = _split_bf16(h2)
    wh, wl = _split_bf16(wr_ref[...])
    logits = _dot_nt(wh, hh) + _dot_nt(wl, hh) + _dot_nt(wh, hl) + br_ref[...]

    e_iota = lax.broadcasted_iota(I32, (N_EXPERTS, tm), 0).astype(F32)
    vals = logits
    tops, sels, idxs = [], [], []
    for _ in range(TOP_K):
        m = jnp.max(vals, axis=0, keepdims=True)
        idx = jnp.min(jnp.where(vals == m, e_iota, float(N_EXPERTS)), axis=0, keepdims=True)
        sel = e_iota == idx
        vals = jnp.where(sel, -jnp.inf, vals)
        tops.append(m)
        sels.append(sel)
        idxs.append(idx)
    ex = [jnp.exp(tv - tops[0]) for tv in tops]
    den = ex[0] + ex[1] + ex[2] + ex[3]
    gates = [v / den for v in ex]

    hot = jnp.where(sels[0] | sels[1] | sels[2] | sels[3], 1.0, 0.0)
    r_t = lax.broadcasted_iota(I32, (tm, tm), 0)
    c_t = lax.broadcasted_iota(I32, (tm, tm), 1)
    upper = jnp.where(r_t < c_t, 1.0, 0.0).astype(BF16)
    excl = _dot(hot.astype(BF16), upper) + carry_scr[:, 0:1]
    carry_scr[...] = carry_scr[...] + jnp.sum(hot, axis=1, keepdims=True)
    cnt_ref[...] = carry_scr[...]

    ranks = [jnp.sum(jnp.where(s, excl, 0.0), axis=0, keepdims=True) for s in sels]
    idx_ref[...] = jnp.concatenate(idxs, axis=0).astype(I32)
    rank_ref[...] = jnp.concatenate(ranks, axis=0).astype(I32)
    gpad = jnp.concatenate(gates + [jnp.zeros((LANES - TOP_K, tm), F32)], axis=0)
    gcol_ref[...] = gpad.T


def _mixout(x2, oa, od, proj, woa, wod, ada3, ffn_g, wr_t, br_col, seq):
    t, d = x2.shape
    tm = min(512, seq)
    per_b = seq // tm
    hw = oa.shape[1]
    row = lambda w: pl.BlockSpec((tm, w), lambda i: (i, 0))
    mod = lambda c: pl.BlockSpec((1, 1, d), lambda i, c=c: (i // per_b, 0, c))
    full = lambda a: pl.BlockSpec(a.shape, lambda i: (0,) * a.ndim)
    return pl.pallas_call(
        _mixout_body,
        grid=(t // tm,),
        in_specs=[
            row(d), row(hw), row(hw),
            pl.BlockSpec((tm, d), lambda i: (i, 0)),
            pl.BlockSpec((tm, d), lambda i: (i, 1)),
            full(woa), full(wod),
            mod(2),
            full(ffn_g), mod(3), mod(4),
            full(wr_t), full(br_col),
        ],
        out_specs=[
            row(d), row(d // 2),
            pl.BlockSpec((TOP_K, tm), lambda i: (0, i)),
            pl.BlockSpec((TOP_K, tm), lambda i: (0, i)),
            pl.BlockSpec((tm, LANES), lambda i: (i, 0)),
            pl.BlockSpec((N_EXPERTS, LANES), lambda i: (0, 0)),
        ],
        out_shape=[
            jax.ShapeDtypeStruct((t, d), F32),
            jax.ShapeDtypeStruct((t, d // 2), I32),
            jax.ShapeDtypeStruct((TOP_K, t), I32),
            jax.ShapeDtypeStruct((TOP_K, t), I32),
            jax.ShapeDtypeStruct((t, LANES), F32),
            jax.ShapeDtypeStruct((N_EXPERTS, LANES), F32),
        ],
        scratch_shapes=[pltpu.VMEM((N_EXPERTS, LANES), F32)],
        compiler_params=pltpu.CompilerParams(
            dimension_semantics=("arbitrary",), vmem_limit_bytes=VMEM_LIMIT),
        name="mixout",
    )(x2, oa, od, proj, proj, woa, wod, ada3, ffn_g, ada3, ada3, wr_t, br_col)


def _sc_mesh():
    return plsc.VectorSubcoreMesh(core_axis_name="c", subcore_axis_name="s")


def _sc_worker_base(rows_per_worker):
    wid = lax.axis_index("s") * SC_CORES + lax.axis_index("c")
    return wid * rows_per_worker


def _sc_dispatch(h2, dest, n_rows):
    t, d = h2.shape
    tpw = t // SC_WORKERS
    assert t % (SC_WORKERS * SC_CHUNK) == 0

    def body(h2_hbm, dest_hbm, xb_hbm, idx_v, rows_v):
        base = _sc_worker_base(tpw)

        @pl.loop(0, tpw // SC_CHUNK)
        def _(i):
            t0 = pl.multiple_of(base + i * SC_CHUNK, SC_CHUNK)
            pltpu.sync_copy(h2_hbm.at[pl.ds(t0, SC_CHUNK)], rows_v)
            for k in range(TOP_K):
                pltpu.sync_copy(dest_hbm.at[pl.ds(k * t + t0, SC_CHUNK)], idx_v)
                pltpu.sync_copy(rows_v, xb_hbm.at[idx_v])

    return pl.kernel(
        body, out_type=jax.ShapeDtypeStruct((n_rows, d), h2.dtype), mesh=_sc_mesh(),
        scratch_types=[pltpu.VMEM((SC_CHUNK,), I32), pltpu.VMEM((SC_CHUNK, d), h2.dtype)],
        name="dispatch",
    )(h2, dest)


def _sc_undispatch(y, dest):
    n_asg = dest.shape[0]
    d = y.shape[1]
    rpw = n_asg // SC_WORKERS
    assert n_asg % (SC_WORKERS * SC_CHUNK) == 0

    def body(y_hbm, dest_hbm, yt_hbm, idx_v, rows_v):
        base = _sc_worker_base(rpw)

        @pl.loop(0, rpw // SC_CHUNK)
        def _(i):
            r0 = pl.multiple_of(base + i * SC_CHUNK, SC_CHUNK)
            pltpu.sync_copy(dest_hbm.at[pl.ds(r0, SC_CHUNK)], idx_v)
            pltpu.sync_copy(y_hbm.at[idx_v], rows_v)
            pltpu.sync_copy(rows_v, yt_hbm.at[pl.ds(r0, SC_CHUNK)])

    return pl.kernel(
        body, out_type=jax.ShapeDtypeStruct((n_asg, d), y.dtype), mesh=_sc_mesh(),
        scratch_types=[pltpu.VMEM((SC_CHUNK,), I32), pltpu.VMEM((SC_CHUNK, d), y.dtype)],
        name="undispatch",
    )(y, dest)


def _ffn_body(be_ref, first_ref, nxt_ref, slot_ref, nact_ref, x_ref, w1_hbm, b1_ref, w2_hbm, b2_ref, y_ref,
              w1f, w2f, w1c, w2c, sem1, sem2):
    j = pl.program_id(0)
    ff = w2f.shape[1]
    pair = 2 * LANES
    ngroup = (2 * ff) // pair

    def weight_copies(e, slot):
        return (pltpu.make_async_copy(w1_hbm.at[e], w1f.at[slot], sem1.at[slot]),
                pltpu.make_async_copy(w2_hbm.at[e], w2f.at[slot], sem2.at[slot]))

    @pl.when(j == 0)
    def _():
        for cp in weight_copies(be_ref[0], 0):
            cp.start()

    @pl.when(first_ref[j] == 1)
    def _():
        slot = slot_ref[j]
        for cp in weight_copies(be_ref[j], slot):
            cp.wait()

        @pl.when(nxt_ref[j] >= 0)
        def _():
            for cp in weight_copies(nxt_ref[j], 1 - slot):
                cp.start()

        r_p = lax.broadcasted_iota(I32, (pair, pair), 0)
        c_p = lax.broadcasted_iota(I32, (pair, pair), 1)
        src = jnp.where(c_p < LANES, 2 * c_p, 2 * (c_p - LANES) + 1)
        perm = jnp.where(r_p == src, 1.0, 0.0).astype(BF16)
        for g in range(ngroup):
            cols = slice(g * pair, (g + 1) * pair)
            w1c[:, cols] = _dot(w1f[slot, :, cols].astype(BF16), perm).astype(BF16)
        w2c[...] = w2f[slot].astype(BF16)

    @pl.when(j < nact_ref[0])
    def _():
        x_lo, x_hi = _unpack_halves(x_ref[...])
        xb = jnp.concatenate([x_lo.astype(BF16), x_hi.astype(BF16)], axis=1)
        u = _dot(xb, w1c[...]) + b1_ref[0]
        acts = []
        for g in range(ngroup):
            glu = jnp.minimum(u[:, g * pair:g * pair + LANES], SWIGLU_LIMIT)
            lin = jnp.clip(u[:, g * pair + LANES:(g + 1) * pair], -SWIGLU_LIMIT, SWIGLU_LIMIT)
            acts.append((glu * _sigmoid(SWIGLU_ALPHA * glu) * (lin + 1.0)).astype(BF16))
        act = jnp.concatenate(acts, axis=1)
        y_ref[...] = _pack_halves(_dot(act, w2c[...]) + b2_ref[0])


def _ffn(block_expert, first, nxt, slot, nact, xb, w1, b1p, w2, b2):
    bm = FFN_BLOCK
    n_rows, dw = xb.shape
    d = 2 * dw
    n_blocks = n_rows // bm
    ff2 = w1.shape[2]
    ff = w2.shape[1]
    row_blk = lambda j, be, fi, nx, sl, na: (jnp.minimum(j, na[0] - 1), 0)
    bias_blk = lambda j, be, fi, nx, sl, na: (be[j], 0, 0)
    grid_spec = pltpu.PrefetchScalarGridSpec(
        num_scalar_prefetch=5,
        grid=(n_blocks,),
        in_specs=[
            pl.BlockSpec((bm, dw), row_blk),
            pl.BlockSpec(memory_space=pl.ANY),
            pl.BlockSpec((1, 1, ff2), bias_blk),
            pl.BlockSpec(memory_space=pl.ANY),
            pl.BlockSpec((1, 1, d), bias_blk),
        ],
        out_specs=pl.BlockSpec((bm, dw), row_blk),
        scratch_shapes=[
            pltpu.VMEM((2, d, ff2), F32), pltpu.VMEM((2, ff, d), F32),
            pltpu.VMEM((d, ff2), BF16), pltpu.VMEM((ff, d), BF16),
            pltpu.SemaphoreType.DMA((2,)), pltpu.SemaphoreType.DMA((2,)),
        ],
    )
    return pl.pallas_call(
        _ffn_body,
        grid_spec=grid_spec,
        out_shape=jax.ShapeDtypeStruct((n_rows, dw), I32),
        compiler_params=pltpu.CompilerParams(
            dimension_semantics=("arbitrary",), vmem_limit_bytes=VMEM_LIMIT),
        name="ffn",
    )(block_expert, first, nxt, slot, nact, xb, w1, b1p, w2, b2)


def _combine_body(x1_ref, y0_ref, y1_ref, y2_ref, y3_ref, gcol_ref, g2_ref, o_ref):
    gc = gcol_ref[...]
    m_lo = m_hi = None
    for k, y_ref in enumerate((y0_ref, y1_ref, y2_ref, y3_ref)):
        lo, hi = _unpack_halves(y_ref[...])
        gk = gc[:, k:k + 1]
        m_lo = gk * lo if m_lo is None else m_lo + gk * lo
        m_hi = gk * hi if m_hi is None else m_hi + gk * hi
    m = jnp.concatenate([m_lo, m_hi], axis=1)
    o_ref[...] = (x1_ref[...] + g2_ref[0] * m).astype(o_ref.dtype)


def _combine(xres, yall, gcol, ada3, seq, tok0):
    t, d = xres.shape
    tm = min(512, seq)
    per_b = seq // tm
    nt = yall.shape[0] // (TOP_K * tm)
    blk0 = tok0 // tm
    yk = lambda k: pl.BlockSpec((tm, d // 2), lambda i, k=k: (k * nt + i, 0))
    return pl.pallas_call(
        _combine_body,
        grid=(nt,),
        in_specs=[
            pl.BlockSpec((tm, d), lambda i: (blk0 + i, 0)),
            yk(0), yk(1), yk(2), yk(3),
            pl.BlockSpec((tm, LANES), lambda i: (blk0 + i, 0)),
            pl.BlockSpec((1, 1, d), lambda i: ((blk0 + i) // per_b, 0, 5)),
        ],
        out_specs=pl.BlockSpec((tm, d), lambda i: (blk0 + i, 0)),
        out_shape=jax.ShapeDtypeStruct((t, d), xres.dtype),
        input_output_aliases={0: 0},
        compiler_params=pltpu.CompilerParams(
            dimension_semantics=("arbitrary",), vmem_limit_bytes=VMEM_LIMIT),
        name="combine",
    )(xres, yall, yall, yall, yall, gcol, ada3)


def _route_tables(idx, rank, counts, n_tok):
    bm = FFN_BLOCK
    n_asg = TOP_K * n_tok
    n_blocks = -(-(n_asg + N_EXPERTS * (bm - 1)) // bm)
    padded = (counts + bm - 1) // bm * bm
    pad_ends = jnp.cumsum(padded)
    pad_starts = pad_ends - padded
    e_ids = jnp.arange(N_EXPERTS, dtype=I32)
    start_of = jnp.sum(jnp.where(idx[None] == e_ids[:, None, None], pad_starts[:, None, None], 0), axis=0)
    dest = (start_of + rank).reshape(-1)
    nact = (pad_ends[-1] // bm).astype(I32)
    blk_start = jnp.arange(n_blocks, dtype=I32) * bm
    last = jnp.sum(jnp.where(pad_ends <= pad_ends[-1] - 1, 1, 0)).astype(I32)
    be = jnp.sum(jnp.where(pad_ends[None, :] <= blk_start[:, None], 1, 0), axis=1).astype(I32)
    active = blk_start < pad_ends[-1]
    be = jnp.where(active, be, last)
    blk = jnp.arange(n_blocks, dtype=I32)
    first = active & ((blk == 0) | (be != jnp.roll(be, 1)))
    slot = (jnp.cumsum(first.astype(I32)) - 1) & 1
    later_first = first[None, :] & (blk[None, :] > blk[:, None])
    nxt_pos = jnp.min(jnp.where(later_first, blk[None, :], n_blocks), axis=1)
    nxt = jnp.sum(jnp.where(blk[None, :] == nxt_pos[:, None], be[None, :], 0), axis=1)
    nxt = jnp.where(nxt_pos < n_blocks, nxt, -1).astype(I32)
    return be, first.astype(I32), nxt, slot.astype(I32), nact.reshape(1), dest, n_blocks * bm


def kernel(x, c, w_ada, b_ada, mix_norm_g, ffn_norm_g, w_in, hg_lower_bound_logits, hg_out_norm_g, da_q_norm_g, da_k_norm_g, da_lambda_q1, da_lambda_k1, da_lambda_q2, da_lambda_k2, da_subln_g, w_out, w_router, b_router, w1, b1, w2, b2):
    bsz, seq, d = x.shape
    t = bsz * seq
    depth = w_ada.shape[0]
    out_dtype = x.dtype
    hw = HG_HEADS * HG_DV
    xcur = x.reshape(t, d)
    for l in range(depth):
        ada = _ada(c, w_ada[l], b_ada[l])
        ada3 = ada.reshape(bsz, 1, N_MOD * d)
        wi = w_in[l]
        n_in = wi.shape[1]
        w_in_r = jnp.concatenate([wi[:, n_in - 2 * d:], wi[:, :n_in - 2 * d]], axis=1).astype(BF16)
        col_h = (2 * d) // LANES
        col_a = col_h + 4 * HG_HEADS
        proj = _inproj(xcur, mix_norm_g[l].reshape(1, d), ada3, w_in_r, seq)

        o_a = _hgrn(proj, hg_lower_bound_logits, hg_out_norm_g[l].reshape(1, HG_DV), bsz, seq, col_h, l)
        lambda_init = 0.8 - 0.6 * math.exp(-0.3 * l)
        qg2 = jnp.tile(da_q_norm_g[l], 2).reshape(1, 2 * DA_DH)
        kg2 = jnp.tile(da_k_norm_g[l], 2).reshape(1, 2 * DA_DH)
        lam4 = jnp.stack([da_lambda_q1[l], da_lambda_k1[l], da_lambda_q2[l], da_lambda_k2[l]])
        o_d = _attn(proj, qg2, kg2, lam4, da_subln_g[l].reshape(1, 2 * DA_DH), bsz, seq, col_a, lambda_init)

        wo = w_out[l].astype(BF16)
        x1, h2, idx, rank, gcol, cnt = _mixout(
            xcur, o_a, o_d, proj, wo[:hw], wo[hw:], ada3, ffn_norm_g[l].reshape(1, d),
            w_router[l].T, b_router[l].reshape(N_EXPERTS, 1), seq)

        counts = cnt[:, 0].astype(I32)
        be, first, nxt, slot, nact, dest, n_rows = _route_tables(idx, rank, counts, t)
        b1p = b1[l].reshape(N_EXPERTS, -1, LANES, 2).transpose(0, 1, 3, 2).reshape(N_EXPERTS, 1, -1)
        xb = _sc_dispatch(h2, dest, n_rows)
        yb = _ffn(be, first, nxt, slot, nact, xb, w1[l], b1p, w2[l], b2[l].reshape(N_EXPERTS, 1, d))
        nchunk = max(n for n in (4, 2, 1) if t % (n * max(seq, SC_WORKERS * SC_CHUNK // TOP_K)) == 0)
        tc = t // nchunk
        dest_kt = dest.reshape(TOP_K, t)
        xcur = x1
        for ci in range(nchunk):
            yall = _sc_undispatch(yb, dest_kt[:, ci * tc:(ci + 1) * tc].reshape(-1))
            xcur = _combine(xcur, yall, gcol, ada3, seq, ci * tc)
        xcur = xcur.astype(out_dtype)
    return xcur.reshape(bsz, seq, d)
```

```python
import functools
import math

import jax
import jax.numpy as jnp
from jax import lax
from jax.experimental import pallas as pl
from jax.experimental.pallas import tpu as pltpu
from jax.experimental.pallas import tpu_sc as plsc

F32 = jnp.float32
BF16 = jnp.bfloat16
I32 = jnp.int32

HG_HEADS = 4
HG_DK = 128
HG_DV = 128
HG_CHUNK = 32
DA_HEADS = 4
DA_DH = 64
N_EXPERTS = 32
TOP_K = 4
SWIGLU_ALPHA = 1.702
SWIGLU_LIMIT = 7.0
NORM_EPS = 1e-6
LOG2E = math.log2(math.e)
N_MOD = 6

LANES = 128
VMEM_LIMIT = 56 * 1024 * 1024

HG_ROWS = 256
ATTN_TILE = 256
FFN_BLOCK = 512

SC_CORES = 2
SC_WORKERS = SC_CORES * 16
SC_CHUNK = 128


def _sigmoid(x):
    return 1.0 / (1.0 + jnp.exp(-x))


def _dot(a, b):
    return jnp.dot(a, b, preferred_element_type=F32)


def _dot_nt(a, b):
    return lax.dot_general(a, b, (((1,), (1,)), ((), ())), preferred_element_type=F32)


def _split_bf16(x):
    hi = x.astype(BF16)
    lo = (x - hi.astype(F32)).astype(BF16)
    return hi, lo


def _pack_halves(x):
    half = x.shape[1] // 2
    lo = lax.bitcast_convert_type(x[:, :half].astype(BF16).astype(F32), I32)
    hi = lax.bitcast_convert_type(x[:, half:].astype(BF16).astype(F32), I32)
    return lax.shift_right_logical(lo, 16) | hi


def _unpack_halves(w):
    lo = lax.bitcast_convert_type(lax.shift_left(w, 16), F32)
    hi = lax.bitcast_convert_type(w & jnp.int32(-65536), F32)
    return lo, hi


def _ada_body(c_ref, w_ref, b_ref, o_ref):
    c = c_ref[...].astype(F32)
    ca = c * _sigmoid(c)
    o_ref[...] = jnp.dot(ca, w_ref[...], preferred_element_type=F32,
                         precision=lax.Precision.HIGHEST) + b_ref[...]


def _ada(c, w_ada, b_ada):
    bsz, d = c.shape
    n = w_ada.shape[1]
    tn = d
    return pl.pallas_call(
        _ada_body,
        grid=(n // tn,),
        in_specs=[
            pl.BlockSpec((bsz, d), lambda j: (0, 0)),
            pl.BlockSpec((d, tn), lambda j: (0, j)),
            pl.BlockSpec((1, tn), lambda j: (0, j)),
        ],
        out_specs=pl.BlockSpec((bsz, tn), lambda j: (0, j)),
        out_shape=jax.ShapeDtypeStruct((bsz, n), F32),
        name="ada",
    )(c, w_ada, b_ada.reshape(1, n))


def _norm_mod(x, g, shift, scale):
    ms = jnp.mean(x * x, axis=-1, keepdims=True)
    return (x * lax.rsqrt(ms + NORM_EPS) * g) * (1.0 + scale) + shift


def _inproj_body(x_ref, g_ref, sh_ref, sc_ref, w_ref, o_ref):
    h = _norm_mod(x_ref[...], g_ref[...], sh_ref[0], sc_ref[0])
    o_ref[...] = _dot(h.astype(BF16), w_ref[...]).astype(BF16)


def _inproj(x2, g, ada3, w_bf16, seq):
    t, d = x2.shape
    n = w_bf16.shape[1]
    tm = min(1024, seq)
    nj = 2
    tn = n // nj
    per_b = seq // tm
    return pl.pallas_call(
        _inproj_body,
        grid=(nj, t // tm),
        in_specs=[
            pl.BlockSpec((tm, d), lambda j, i: (i, 0)),
            pl.BlockSpec((1, d), lambda j, i: (0, 0)),
            pl.BlockSpec((1, 1, d), lambda j, i: (i // per_b, 0, 0)),
            pl.BlockSpec((1, 1, d), lambda j, i: (i // per_b, 0, 1)),
            pl.BlockSpec((d, tn), lambda j, i: (0, j)),
        ],
        out_specs=pl.BlockSpec((tm, tn), lambda j, i: (i, j)),
        out_shape=jax.ShapeDtypeStruct((t, n), BF16),
        compiler_params=pltpu.CompilerParams(
            dimension_semantics=("arbitrary", "arbitrary"), vmem_limit_bytes=VMEM_LIMIT),
        name="inproj",
    )(x2, g, ada3, ada3, w_bf16)


def _hgrn_body(q_ref, f_ref, i_ref, og_ref, lbl_ref, g_ref, o_ref, *, seq, layer):
    rows, chunk = HG_ROWS, HG_CHUNK
    nchunk = rows // chunk
    lbl = lbl_ref[...].astype(F32)
    e = jnp.exp(lbl - jnp.max(lbl, axis=0, keepdims=True))
    lb = jnp.sum(e[: layer + 1], axis=0, keepdims=True) / jnp.sum(e, axis=0, keepdims=True)
    r_i = lax.broadcasted_iota(I32, (rows, rows), 0)
    c_i = lax.broadcasted_iota(I32, (rows, rows), 1)
    tri = ((r_i // chunk) == (c_i // chunk)) & (r_i >= c_i)
    tri_b = jnp.where(tri, 1.0, 0.0).astype(BF16)
    row_chunk = lax.broadcasted_iota(I32, (rows, HG_DK), 0) // chunk
    g = g_ref[...].astype(F32)

    def block(r, st):
        sl = pl.ds(r * rows, rows)
        qr = q_ref[sl, :].astype(F32)
        fr = f_ref[sl, :].astype(F32)
        v = i_ref[sl, :].astype(F32)
        og = og_ref[sl, :].astype(F32)
        q = qr * _sigmoid(qr)
        f = lb + (1.0 - lb) * _sigmoid(fr)
        k = 1.0 - f
        logf = jnp.log(f)
        lhi, llo = _split_bf16(logf)
        bc2 = _dot(tri_b, jnp.concatenate([lhi, llo], axis=1))
        bcum = bc2[:, :HG_DK] + bc2[:, HG_DK:]
        b3 = bcum.reshape(nchunk, chunk, HG_DK)
        bl = b3[:, chunk - 1:chunk, :]
        dec = jnp.exp(bl)
        kt_f = k * jnp.exp(-bcum)
        qt_f = q * jnp.exp(bcum)
        kd = (kt_f.reshape(nchunk, chunk, HG_DK) * dec).reshape(rows, HG_DK)
        a = jnp.where(tri, _dot_nt(qt_f.astype(BF16), kt_f.astype(BF16)), 0.0).astype(BF16)
        vt_b = v.T.astype(BF16)
        kd_x = jnp.concatenate([jnp.where(row_chunk == c, kd, 0.0) for c in range(nchunk)], axis=1)
        kv_all = _dot(vt_b, kd_x.astype(BF16))
        starts = []
        for c in range(nchunk):
            starts.append(st.astype(BF16))
            st = st * dec[c] + kv_all[:, c * HG_DK:(c + 1) * HG_DK]
        q_x = [jnp.where(row_chunk == c, qt_f, 0.0).astype(BF16) for c in range(nchunk)]
        o = _dot_nt(jnp.concatenate([a] + q_x, axis=1), jnp.concatenate([vt_b] + starts, axis=1))
        ms = jnp.mean(o * o, axis=-1, keepdims=True)
        o = o * lax.rsqrt(ms + NORM_EPS) * g
        o_ref[sl, :] = (o * (og * _sigmoid(og))).astype(o_ref.dtype)
        return st

    st = jnp.zeros((HG_DV, HG_DK), F32)
    for r in range(seq // rows):
        st = block(r, st)


def _hgrn(proj, lb_logits, norm_g, bsz, seq, col0, layer):
    t = proj.shape[0]
    blk = lambda off: pl.BlockSpec((seq, LANES), lambda b, h, off=off: (b, col0 + off + h))
    return pl.pallas_call(
        functools.partial(_hgrn_body, seq=seq, layer=layer),
        grid=(bsz, HG_HEADS),
        in_specs=[
            blk(0), blk(HG_HEADS), blk(2 * HG_HEADS), blk(3 * HG_HEADS),
            pl.BlockSpec((lb_logits.shape[0], HG_DK), lambda b, h: (0, h)),
            pl.BlockSpec((1, HG_DV), lambda b, h: (0, 0)),
        ],
        out_specs=pl.BlockSpec((seq, HG_DV), lambda b, h: (b, h)),
        out_shape=jax.ShapeDtypeStruct((t, HG_HEADS * HG_DV), BF16),
        compiler_params=pltpu.CompilerParams(
            dimension_semantics=("arbitrary", "arbitrary"), vmem_limit_bytes=VMEM_LIMIT),
        name="hgrn",
    )(proj, proj, proj, proj, lb_logits, norm_g)


def _group_norm(x, gsum_b, gain):
    hi, lo = _split_bf16(x * x)
    ss = _dot(hi, gsum_b) + _dot(lo, gsum_b)
    return x * lax.rsqrt(ss * (1.0 / DA_DH) + NORM_EPS) * gain


def _attn_body(q_ref, k_ref, v_ref, qg_ref, kg_ref, lam_ref, sg_ref, o_ref, kn_scr, v1_scr, *, seq, lambda_init):
    tq = ATTN_TILE
    width = 2 * DA_DH
    r_l = lax.broadcasted_iota(I32, (width, width), 0) // DA_DH
    c_l = lax.broadcasted_iota(I32, (width, width), 1) // DA_DH
    gsum_b = jnp.where(r_l == c_l, 1.0, 0.0).astype(BF16)
    lane = lax.broadcasted_iota(I32, (1, width), 1)
    kg = kg_ref[...].astype(F32)
    qg = qg_ref[...].astype(F32) * (DA_DH ** -0.5 * LOG2E)
    sg = sg_ref[...].astype(F32) * (1.0 - lambda_init)
    ones = jnp.ones((tq, width), BF16)
    row = lax.broadcasted_iota(I32, (tq, tq), 0)
    col = lax.broadcasted_iota(I32, (tq, tq), 1)
    keep = row >= col

    lam_v = lam_ref[...].astype(F32)
    lam = (jnp.exp(jnp.sum(lam_v[0:1] * lam_v[1:2], axis=-1, keepdims=True))
           - jnp.exp(jnp.sum(lam_v[2:3] * lam_v[3:4], axis=-1, keepdims=True)) + lambda_init)

    def softmax_v(qc, nk):
        s = _dot_nt(qc, kn_scr[0:nk, :])
        diag = jnp.where(keep, s[:, nk - tq:], -jnp.inf)
        s = diag if nk == tq else jnp.concatenate([s[:, :nk - tq], diag], axis=1)
        m = jnp.max(s, axis=-1, keepdims=True)
        return _dot(jnp.exp2(s - m).astype(BF16), v1_scr[0:nk, :])

    for i in range(seq // tq):
        sl = slice(i * tq, (i + 1) * tq)
        kn_scr[sl, :] = _group_norm(k_ref[sl, :].astype(F32), gsum_b, kg).astype(BF16)
        v1_scr[sl, :] = jnp.concatenate([v_ref[sl, :], ones], axis=1)
        qn = _group_norm(q_ref[sl, :].astype(F32), gsum_b, qg)
        nk = (i + 1) * tq
        a1 = softmax_v(jnp.where(lane < DA_DH, qn, 0.0).astype(BF16), nk)
        a2 = softmax_v(jnp.where(lane >= DA_DH, qn, 0.0).astype(BF16), nk)
        o = a1[:, :width] / a1[:, width:width + 1] - lam * (a2[:, :width] / a2[:, width:width + 1])
        ms = jnp.mean(o * o, axis=-1, keepdims=True)
        o_ref[sl, :] = (o * lax.rsqrt(ms + NORM_EPS) * sg).astype(o_ref.dtype)


def _attn(proj, qg2, kg2, lam4, subln_g, bsz, seq, col0, lambda_init):
    t = proj.shape[0]
    width = 2 * DA_DH
    return pl.pallas_call(
        functools.partial(_attn_body, seq=seq, lambda_init=lambda_init),
        grid=(bsz, DA_HEADS),
        in_specs=[
            pl.BlockSpec((seq, width), lambda b, h: (b, col0 + h)),
            pl.BlockSpec((seq, width), lambda b, h: (b, col0 + DA_HEADS + h)),
            pl.BlockSpec((seq, width), lambda b, h: (b, col0 + 2 * DA_HEADS + h)),
            pl.BlockSpec((1, width), lambda b, h: (0, 0)),
            pl.BlockSpec((1, width), lambda b, h: (0, 0)),
            pl.BlockSpec((4, DA_DH), lambda b, h: (0, 0)),
            pl.BlockSpec((1, width), lambda b, h: (0, 0)),
        ],
        out_specs=pl.BlockSpec((seq, width), lambda b, h: (b, h)),
        out_shape=jax.ShapeDtypeStruct((t, DA_HEADS * width), BF16),
        scratch_shapes=[pltpu.VMEM((seq, width), BF16), pltpu.VMEM((seq, 2 * width), BF16)],
        compiler_params=pltpu.CompilerParams(
            dimension_semantics=("arbitrary", "arbitrary"), vmem_limit_bytes=VMEM_LIMIT),
        name="attn",
    )(proj, proj, proj, qg2, kg2, lam4, subln_g)


def _mixout_body(x_ref, oa_ref, od_ref, ga_ref, gd_ref, woa_ref, wod_ref, g1_ref, g_ref, sh_ref, sc_ref,
                 wr_ref, br_ref,
                 x1_ref, h2_ref, idx_ref, rank_ref, gcol_ref, cnt_ref, carry_scr):
    i = pl.program_id(0)
    tm = x_ref.shape[0]

    @pl.when(i == 0)
    def _():
        carry_scr[...] = jnp.zeros_like(carry_scr)

    ya = _dot(oa_ref[...], woa_ref[...])
    yd = _dot(od_ref[...], wod_ref[...])
    y = _sigmoid(ga_ref[...].astype(F32)) * ya + _sigmoid(gd_ref[...].astype(F32)) * yd
    x1 = x_ref[...] + g1_ref[0] * y
    x1_ref[...] = x1
    h2 = _norm_mod(x1, g_ref[...], sh_ref[0], sc_ref[0])
    h2_ref[...] = _pack_halves(h2)

    hh, hl = _split_bf16(h2)
    wh, wl = _split_bf16(wr_ref[...])
    logits = _dot_nt(wh, hh) + _dot_nt(wl, hh) + _dot_nt(wh, hl) + br_ref[...]

    e_iota = lax.broadcasted_iota(I32, (N_EXPERTS, tm), 0).astype(F32)
    vals = logits
    tops, sels, idxs = [], [], []
    for _ in range(TOP_K):
        m = jnp.max(vals, axis=0, keepdims=True)
        idx = jnp.min(jnp.where(vals == m, e_iota, float(N_EXPERTS)), axis=0, keepdims=True)
        sel = e_iota == idx
        vals = jnp.where(sel, -jnp.inf, vals)
        tops.append(m)
        sels.append(sel)
        idxs.append(idx)
    ex = [jnp.exp(tv - tops[0]) for tv in tops]
    den = ex[0] + ex[1] + ex[2] + ex[3]
    gates = [v / den for v in ex]

    hot = jnp.where(sels[0] | sels[1] | sels[2] | sels[3], 1.0, 0.0)
    r_t = lax.broadcasted_iota(I32, (tm, tm), 0)
    c_t = lax.broadcasted_iota(I32, (tm, tm), 1)
    upper = jnp.where(r_t < c_t, 1.0, 0.0).astype(BF16)
    excl = _dot(hot.astype(BF16), upper) + carry_scr[:, 0:1]
    carry_scr[...] = carry_scr[...] + jnp.sum(hot, axis=1, keepdims=True)
    cnt_ref[...] = carry_scr[...]

    ranks = [jnp.sum(jnp.where(s, excl, 0.0), axis=0, keepdims=True) for s in sels]
    idx_ref[...] = jnp.concatenate(idxs, axis=0).astype(I32)
    rank_ref[...] = jnp.concatenate(ranks, axis=0).astype(I32)
    gpad = jnp.concatenate(gates + [jnp.zeros((LANES - TOP_K, tm), F32)], axis=0)
    gcol_ref[...] = gpad.T


def _mixout(x2, oa, od, proj, woa, wod, ada3, ffn_g, wr_t, br_col, seq):
    t, d = x2.shape
    tm = min(512, seq)
    per_b = seq // tm
    hw = oa.shape[1]
    row = lambda w: pl.BlockSpec((tm, w), lambda i: (i, 0))
    mod = lambda c: pl.BlockSpec((1, 1, d), lambda i, c=c: (i // per_b, 0, c))
    full = lambda a: pl.BlockSpec(a.shape, lambda i: (0,) * a.ndim)
    return pl.pallas_call(
        _mixout_body,
        grid=(t // tm,),
        in_specs=[
            row(d), row(hw), row(hw),
            pl.BlockSpec((tm, d), lambda i: (i, 0)),
            pl.BlockSpec((tm, d), lambda i: (i, 1)),
            full(woa), full(wod),
            mod(2),
            full(ffn_g), mod(3), mod(4),
            full(wr_t), full(br_col),
        ],
        out_specs=[
            row(d), row(d // 2),
            pl.BlockSpec((TOP_K, tm), lambda i: (0, i)),
            pl.BlockSpec((TOP_K, tm), lambda i: (0, i)),
            pl.BlockSpec((tm, LANES), lambda i: (i, 0)),
            pl.BlockSpec((N_EXPERTS, LANES), lambda i: (0, 0)),
        ],
        out_shape=[
            jax.ShapeDtypeStruct((t, d), F32),
            jax.ShapeDtypeStruct((t, d // 2), I32),
            jax.ShapeDtypeStruct((TOP_K, t), I32),
            jax.ShapeDtypeStruct((TOP_K, t), I32),
            jax.ShapeDtypeStruct((t, LANES), F32),
            jax.ShapeDtypeStruct((N_EXPERTS, LANES), F32),
        ],
        scratch_shapes=[pltpu.VMEM((N_EXPERTS, LANES), F32)],
        compiler_params=pltpu.CompilerParams(
            dimension_semantics=("arbitrary",), vmem_limit_bytes=VMEM_LIMIT),
        name="mixout",
    )(x2, oa, od, proj, proj, woa, wod, ada3, ffn_g, ada3, ada3, wr_t, br_col)


def _sc_mesh():
    return plsc.VectorSubcoreMesh(core_axis_name="c", subcore_axis_name="s")


def _sc_worker_base(rows_per_worker):
    wid = lax.axis_index("s") * SC_CORES + lax.axis_index("c")
    return wid * rows_per_worker


def _sc_dispatch(h2, dest, n_rows):
    t, d = h2.shape
    tpw = t // SC_WORKERS
    assert t % (SC_WORKERS * SC_CHUNK) == 0

    def body(h2_hbm, dest_hbm, xb_hbm, idx_v, rows_v):
        base = _sc_worker_base(tpw)

        @pl.loop(0, tpw // SC_CHUNK)
        def _(i):
            t0 = pl.multiple_of(base + i * SC_CHUNK, SC_CHUNK)
            pltpu.sync_copy(h2_hbm.at[pl.ds(t0, SC_CHUNK)], rows_v)
            for k in range(TOP_K):
                pltpu.sync_copy(dest_hbm.at[pl.ds(k * t + t0, SC_CHUNK)], idx_v)
                pltpu.sync_copy(rows_v, xb_hbm.at[idx_v])

    return pl.kernel(
        body, out_type=jax.ShapeDtypeStruct((n_rows, d), h2.dtype), mesh=_sc_mesh(),
        scratch_types=[pltpu.VMEM((SC_CHUNK,), I32), pltpu.VMEM((SC_CHUNK, d), h2.dtype)],
        name="dispatch",
    )(h2, dest)


def _sc_undispatch(y, dest):
    n_asg = dest.shape[0]
    d = y.shape[1]
    rpw = n_asg // SC_WORKERS
    assert n_asg % (SC_WORKERS * SC_CHUNK) == 0

    def body(y_hbm, dest_hbm, yt_hbm, idx_v, rows_v):
        base = _sc_worker_base(rpw)

        @pl.loop(0, rpw // SC_CHUNK)
        def _(i):
            r0 = pl.multiple_of(base + i * SC_CHUNK, SC_CHUNK)
            pltpu.sync_copy(dest_hbm.at[pl.ds(r0, SC_CHUNK)], idx_v)
            pltpu.sync_copy(y_hbm.at[idx_v], rows_v)
            pltpu.sync_copy(rows_v, yt_hbm.at[pl.ds(r0, SC_CHUNK)])

    return pl.kernel(
        body, out_type=jax.ShapeDtypeStruct((n_asg, d), y.dtype), mesh=_sc_mesh(),
        scratch_types=[pltpu.VMEM((SC_CHUNK,), I32), pltpu.VMEM((SC_CHUNK, d), y.dtype)],
        name="undispatch",
    )(y, dest)


def _ffn_body(be_ref, first_ref, nxt_ref, slot_ref, nact_ref, x_ref, w1_hbm, b1_ref, w2_hbm, b2_ref, y_ref,
              w1f, w2f, w1c, w2c, sem1, sem2):
    j = pl.program_id(0)
    ff = w2f.shape[1]
    pair = 2 * LANES
    ngroup = (2 * ff) // pair

    def weight_copies(e, slot):
        return (pltpu.make_async_copy(w1_hbm.at[e], w1f.at[slot], sem1.at[slot]),
                pltpu.make_async_copy(w2_hbm.at[e], w2f.at[slot], sem2.at[slot]))

    @pl.when(j == 0)
    def _():
        for cp in weight_copies(be_ref[0], 0):
            cp.start()

    @pl.when(first_ref[j] == 1)
    def _():
        slot = slot_ref[j]
        for cp in weight_copies(be_ref[j], slot):
            cp.wait()

        @pl.when(nxt_ref[j] >= 0)
        def _():
            for cp in weight_copies(nxt_ref[j], 1 - slot):
                cp.start(priority=1)

        r_p = lax.broadcasted_iota(I32, (pair, pair), 0)
        c_p = lax.broadcasted_iota(I32, (pair, pair), 1)
        src = jnp.where(c_p < LANES, 2 * c_p, 2 * (c_p - LANES) + 1)
        perm = jnp.where(r_p == src, 1.0, 0.0).astype(BF16)
        for g in range(ngroup):
            cols = slice(g * pair, (g + 1) * pair)
            w1c[:, cols] = _dot(w1f[slot, :, cols].astype(BF16), perm).astype(BF16)
        w2c[...] = w2f[slot].astype(BF16)

    @pl.when(j < nact_ref[0])
    def _():
        x_lo, x_hi = _unpack_halves(x_ref[...])
        xb = jnp.concatenate([x_lo.astype(BF16), x_hi.astype(BF16)], axis=1)
        u = _dot(xb, w1c[...]) + b1_ref[0]
        acts = []
        for g in range(ngroup):
            glu = jnp.minimum(u[:, g * pair:g * pair + LANES], SWIGLU_LIMIT)
            lin = jnp.clip(u[:, g * pair + LANES:(g + 1) * pair], -SWIGLU_LIMIT, SWIGLU_LIMIT)
            acts.append((glu * _sigmoid(SWIGLU_ALPHA * glu) * (lin + 1.0)).astype(BF16))
        act = jnp.concatenate(acts, axis=1)
        y_ref[...] = _pack_halves(_dot(act, w2c[...]) + b2_ref[0])


def _ffn(block_expert, first, nxt, slot, nact, xb, w1, b1p, w2, b2):
    bm = FFN_BLOCK
    n_rows, dw = xb.shape
    d = 2 * dw
    n_blocks = n_rows // bm
    ff2 = w1.shape[2]
    ff = w2.shape[1]
    row_blk = lambda j, be, fi, nx, sl, na: (jnp.minimum(j, na[0] - 1), 0)
    bias_blk = lambda j, be, fi, nx, sl, na: (be[j], 0, 0)
    grid_spec = pltpu.PrefetchScalarGridSpec(
        num_scalar_prefetch=5,
        grid=(n_blocks,),
        in_specs=[
            pl.BlockSpec((bm, dw), row_blk),
            pl.BlockSpec(memory_space=pl.ANY),
            pl.BlockSpec((1, 1, ff2), bias_blk),
            pl.BlockSpec(memory_space=pl.ANY),
            pl.BlockSpec((1, 1, d), bias_blk),
        ],
        out_specs=pl.BlockSpec((bm, dw), row_blk),
        scratch_shapes=[
            pltpu.VMEM((2, d, ff2), F32), pltpu.VMEM((2, ff, d), F32),
            pltpu.VMEM((d, ff2), BF16), pltpu.VMEM((ff, d), BF16),
            pltpu.SemaphoreType.DMA((2,)), pltpu.SemaphoreType.DMA((2,)),
        ],
    )
    return pl.pallas_call(
        _ffn_body,
        grid_spec=grid_spec,
        out_shape=jax.ShapeDtypeStruct((n_rows, dw), I32),
        compiler_params=pltpu.CompilerParams(
            dimension_semantics=("arbitrary",), vmem_limit_bytes=VMEM_LIMIT),
        name="ffn",
    )(block_expert, first, nxt, slot, nact, xb, w1, b1p, w2, b2)


def _combine_body(x1_ref, y0_ref, y1_ref, y2_ref, y3_ref, gcol_ref, g2_ref, o_ref):
    gc = gcol_ref[...]
    m_lo = m_hi = None
    for k, y_ref in enumerate((y0_ref, y1_ref, y2_ref, y3_ref)):
        lo, hi = _unpack_halves(y_ref[...])
        gk = gc[:, k:k + 1]
        m_lo = gk * lo if m_lo is None else m_lo + gk * lo
        m_hi = gk * hi if m_hi is None else m_hi + gk * hi
    m = jnp.concatenate([m_lo, m_hi], axis=1)
    o_ref[...] = (x1_ref[...] + g2_ref[0] * m).astype(o_ref.dtype)


def _combine(x1, yall, gcol, ada3, seq, out_dtype):
    t, d = x1.shape
    tm = min(1024, seq)
    per_b = seq // tm
    nt = t // tm
    yk = lambda k: pl.BlockSpec((tm, d // 2), lambda i, k=k: (k * nt + i, 0))
    return pl.pallas_call(
        _combine_body,
        grid=(nt,),
        in_specs=[
            pl.BlockSpec((tm, d), lambda i: (i, 0)),
            yk(0), yk(1), yk(2), yk(3),
            pl.BlockSpec((tm, LANES), lambda i: (i, 0)),
            pl.BlockSpec((1, 1, d), lambda i: (i // per_b, 0, 5)),
        ],
        out_specs=pl.BlockSpec((tm, d), lambda i: (i, 0)),
        out_shape=jax.ShapeDtypeStruct((t, d), out_dtype),
        compiler_params=pltpu.CompilerParams(
            dimension_semantics=("arbitrary",), vmem_limit_bytes=VMEM_LIMIT),
        name="combine",
    )(x1, yall, yall, yall, yall, gcol, ada3)


def _route_tables(idx, rank, counts, n_tok):
    bm = FFN_BLOCK
    n_asg = TOP_K * n_tok
    n_blocks = -(-(n_asg + N_EXPERTS * (bm - 1)) // bm)
    padded = (counts + bm - 1) // bm * bm
    pad_ends = jnp.cumsum(padded)
    pad_starts = pad_ends - padded
    e_ids = jnp.arange(N_EXPERTS, dtype=I32)
    start_of = jnp.sum(jnp.where(idx[None] == e_ids[:, None, None], pad_starts[:, None, None], 0), axis=0)
    dest = (start_of + rank).reshape(-1)
    nact = (pad_ends[-1] // bm).astype(I32)
    blk_start = jnp.arange(n_blocks, dtype=I32) * bm
    last = jnp.sum(jnp.where(pad_ends <= pad_ends[-1] - 1, 1, 0)).astype(I32)
    be = jnp.sum(jnp.where(pad_ends[None, :] <= blk_start[:, None], 1, 0), axis=1).astype(I32)
    active = blk_start < pad_ends[-1]
    be = jnp.where(active, be, last)
    blk = jnp.arange(n_blocks, dtype=I32)
    first = active & ((blk == 0) | (be != jnp.roll(be, 1)))
    slot = (jnp.cumsum(first.astype(I32)) - 1) & 1
    later_first = first[None, :] & (blk[None, :] > blk[:, None])
    nxt_pos = jnp.min(jnp.where(later_first, blk[None, :], n_blocks), axis=1)
    nxt = jnp.sum(jnp.where(blk[None, :] == nxt_pos[:, None], be[None, :], 0), axis=1)
    nxt = jnp.where(nxt_pos < n_blocks, nxt, -1).astype(I32)
    return be, first.astype(I32), nxt, slot.astype(I32), nact.reshape(1), dest, n_blocks * bm


def kernel(x, c, w_ada, b_ada, mix_norm_g, ffn_norm_g, w_in, hg_lower_bound_logits, hg_out_norm_g, da_q_norm_g, da_k_norm_g, da_lambda_q1, da_lambda_k1, da_lambda_q2, da_lambda_k2, da_subln_g, w_out, w_router, b_router, w1, b1, w2, b2):
    bsz, seq, d = x.shape
    t = bsz * seq
    depth = w_ada.shape[0]
    out_dtype = x.dtype
    hw = HG_HEADS * HG_DV
    xcur = x.reshape(t, d)
    for l in range(depth):
        ada = _ada(c, w_ada[l], b_ada[l])
        ada3 = ada.reshape(bsz, 1, N_MOD * d)
        wi = w_in[l]
        n_in = wi.shape[1]
        w_in_r = jnp.concatenate([wi[:, n_in - 2 * d:], wi[:, :n_in - 2 * d]], axis=1).astype(BF16)
        col_h = (2 * d) // LANES
        col_a = col_h + 4 * HG_HEADS
        proj = _inproj(xcur, mix_norm_g[l].reshape(1, d), ada3, w_in_r, seq)

        o_a = _hgrn(proj, hg_lower_bound_logits, hg_out_norm_g[l].reshape(1, HG_DV), bsz, seq, col_h, l)
        lambda_init = 0.8 - 0.6 * math.exp(-0.3 * l)
        qg2 = jnp.tile(da_q_norm_g[l], 2).reshape(1, 2 * DA_DH)
        kg2 = jnp.tile(da_k_norm_g[l], 2).reshape(1, 2 * DA_DH)
        lam4 = jnp.stack([da_lambda_q1[l], da_lambda_k1[l], da_lambda_q2[l], da_lambda_k2[l]])
        o_d = _attn(proj, qg2, kg2, lam4, da_subln_g[l].reshape(1, 2 * DA_DH), bsz, seq, col_a, lambda_init)

        wo = w_out[l].astype(BF16)
        x1, h2, idx, rank, gcol, cnt = _mixout(
            xcur, o_a, o_d, proj, wo[:hw], wo[hw:], ada3, ffn_norm_g[l].reshape(1, d),
            w_router[l].T, b_router[l].reshape(N_EXPERTS, 1), seq)

        counts = cnt[:, 0].astype(I32)
        be, first, nxt, slot, nact, dest, n_rows = _route_tables(idx, rank, counts, t)
        b1p = b1[l].reshape(N_EXPERTS, -1, LANES, 2).transpose(0, 1, 3, 2).reshape(N_EXPERTS, 1, -1)
        xb = _sc_dispatch(h2, dest, n_rows)
        yb = _ffn(be, first, nxt, slot, nact, xb, w1[l], b1p, w2[l], b2[l].reshape(N_EXPERTS, 1, d))
        yall = _sc_undispatch(yb, dest)
        xcur = _combine(x1, yall, gcol, ada3, seq, out_dtype)
    return xcur.reshape(bsz, seq, d)
```

```python
import functools
import math

import jax
import jax.numpy as jnp
from jax import lax
from jax.experimental import pallas as pl
from jax.experimental.pallas import tpu as pltpu
from jax.experimental.pallas import tpu_sc as plsc

F32 = jnp.float32
BF16 = jnp.bfloat16
I32 = jnp.int32

HG_HEADS = 4
HG_DK = 128
HG_DV = 128
HG_CHUNK = 32
DA_HEADS = 4
DA_DH = 64
N_EXPERTS = 32
TOP_K = 4
SWIGLU_ALPHA = 1.702
SWIGLU_LIMIT = 7.0
NORM_EPS = 1e-6
LOG2E = math.log2(math.e)
N_MOD = 6

LANES = 128
VMEM_LIMIT = 56 * 1024 * 1024

HG_ROWS = 256
ATTN_TILE = 256
FFN_BLOCK = 512

SC_CORES = 2
SC_WORKERS = SC_CORES * 16
SC_CHUNK = 128


def _sigmoid(x):
    return 1.0 / (1.0 + jnp.exp(-x))


def _dot(a, b):
    return jnp.dot(a, b, preferred_element_type=F32)


def _dot_nt(a, b):
    return lax.dot_general(a, b, (((1,), (1,)), ((), ())), preferred_element_type=F32)


def _split_bf16(x):
    hi = x.astype(BF16)
    lo = (x - hi.astype(F32)).astype(BF16)
    return hi, lo


def _pack_halves(x):
    half = x.shape[1] // 2
    lo = lax.bitcast_convert_type(x[:, :half].astype(BF16).astype(F32), I32)
    hi = lax.bitcast_convert_type(x[:, half:].astype(BF16).astype(F32), I32)
    return lax.shift_right_logical(lo, 16) | hi


def _unpack_halves(w):
    lo = lax.bitcast_convert_type(lax.shift_left(w, 16), F32)
    hi = lax.bitcast_convert_type(w & jnp.int32(-65536), F32)
    return lo, hi


def _ada_body(c_ref, w_ref, b_ref, o_ref):
    c = c_ref[...].astype(F32)
    ca = c * _sigmoid(c)
    o_ref[...] = jnp.dot(ca, w_ref[...], preferred_element_type=F32,
                         precision=lax.Precision.HIGHEST) + b_ref[...]


def _ada(c, w_ada, b_ada):
    bsz, d = c.shape
    n = w_ada.shape[1]
    tn = d
    return pl.pallas_call(
        _ada_body,
        grid=(n // tn,),
        in_specs=[
            pl.BlockSpec((bsz, d), lambda j: (0, 0)),
            pl.BlockSpec((d, tn), lambda j: (0, j)),
            pl.BlockSpec((1, tn), lambda j: (0, j)),
        ],
        out_specs=pl.BlockSpec((bsz, tn), lambda j: (0, j)),
        out_shape=jax.ShapeDtypeStruct((bsz, n), F32),
        name="ada",
    )(c, w_ada, b_ada.reshape(1, n))


def _norm_mod(x, g, shift, scale):
    ms = jnp.mean(x * x, axis=-1, keepdims=True)
    return (x * lax.rsqrt(ms + NORM_EPS) * g) * (1.0 + scale) + shift


def _inproj_body(x_ref, g_ref, sh_ref, sc_ref, w_ref, o_ref):
    h = _norm_mod(x_ref[...], g_ref[...], sh_ref[0], sc_ref[0])
    o_ref[...] = _dot(h.astype(BF16), w_ref[...]).astype(BF16)


def _inproj(x2, g, ada3, w_bf16, seq):
    t, d = x2.shape
    n = w_bf16.shape[1]
    tm = min(1024, seq)
    nj = 2
    tn = n // nj
    per_b = seq // tm
    return pl.pallas_call(
        _inproj_body,
        grid=(nj, t // tm),
        in_specs=[
            pl.BlockSpec((tm, d), lambda j, i: (i, 0)),
            pl.BlockSpec((1, d), lambda j, i: (0, 0)),
            pl.BlockSpec((1, 1, d), lambda j, i: (i // per_b, 0, 0)),
            pl.BlockSpec((1, 1, d), lambda j, i: (i // per_b, 0, 1)),
            pl.BlockSpec((d, tn), lambda j, i: (0, j)),
        ],
        out_specs=pl.BlockSpec((tm, tn), lambda j, i: (i, j)),
        out_shape=jax.ShapeDtypeStruct((t, n), BF16),
        compiler_params=pltpu.CompilerParams(
            dimension_semantics=("arbitrary", "arbitrary"), vmem_limit_bytes=VMEM_LIMIT),
        name="inproj",
    )(x2, g, ada3, ada3, w_bf16)


def _hgrn_body(q_ref, f_ref, i_ref, og_ref, lbl_ref, g_ref, o_ref, *, seq, layer):
    rows, chunk = HG_ROWS, HG_CHUNK
    nchunk = rows // chunk
    lbl = lbl_ref[...].astype(F32)
    e = jnp.exp(lbl - jnp.max(lbl, axis=0, keepdims=True))
    lb = jnp.sum(e[: layer + 1], axis=0, keepdims=True) / jnp.sum(e, axis=0, keepdims=True)
    r_i = lax.broadcasted_iota(I32, (rows, rows), 0)
    c_i = lax.broadcasted_iota(I32, (rows, rows), 1)
    tri = ((r_i // chunk) == (c_i // chunk)) & (r_i >= c_i)
    tri_b = jnp.where(tri, 1.0, 0.0).astype(BF16)
    row_chunk = lax.broadcasted_iota(I32, (rows, HG_DK), 0) // chunk
    g = g_ref[...].astype(F32)

    def block(r, st):
        sl = pl.ds(r * rows, rows)
        qr = q_ref[sl, :].astype(F32)
        fr = f_ref[sl, :].astype(F32)
        v = i_ref[sl, :].astype(F32)
        og = og_ref[sl, :].astype(F32)
        q = qr * _sigmoid(qr)
        f = lb + (1.0 - lb) * _sigmoid(fr)
        k = 1.0 - f
        logf = jnp.log(f)
        lhi, llo = _split_bf16(logf)
        bc2 = _dot(tri_b, jnp.concatenate([lhi, llo], axis=1))
        bcum = bc2[:, :HG_DK] + bc2[:, HG_DK:]
        b3 = bcum.reshape(nchunk, chunk, HG_DK)
        bl = b3[:, chunk - 1:chunk, :]
        dec = jnp.exp(bl)
        kt_f = k * jnp.exp(-bcum)
        qt_f = q * jnp.exp(bcum)
        kd = (kt_f.reshape(nchunk, chunk, HG_DK) * dec).reshape(rows, HG_DK)
        a = jnp.where(tri, _dot_nt(qt_f.astype(BF16), kt_f.astype(BF16)), 0.0).astype(BF16)
        vt_b = v.T.astype(BF16)
        kd_x = jnp.concatenate([jnp.where(row_chunk == c, kd, 0.0) for c in range(nchunk)], axis=1)
        kv_all = _dot(vt_b, kd_x.astype(BF16))
        starts = []
        for c in range(nchunk):
            starts.append(st.astype(BF16))
            st = st * dec[c] + kv_all[:, c * HG_DK:(c + 1) * HG_DK]
        q_x = [jnp.where(row_chunk == c, qt_f, 0.0).astype(BF16) for c in range(nchunk)]
        o = _dot_nt(jnp.concatenate([a] + q_x, axis=1), jnp.concatenate([vt_b] + starts, axis=1))
        ms = jnp.mean(o * o, axis=-1, keepdims=True)
        o = o * lax.rsqrt(ms + NORM_EPS) * g
        o_ref[sl, :] = (o * (og * _sigmoid(og))).astype(o_ref.dtype)
        return st

    st = jnp.zeros((HG_DV, HG_DK), F32)
    for r in range(seq // rows):
        st = block(r, st)


def _hgrn(proj, lb_logits, norm_g, bsz, seq, col0, layer):
    t = proj.shape[0]
    blk = lambda off: pl.BlockSpec((seq, LANES), lambda b, h, off=off: (b, col0 + off + h))
    return pl.pallas_call(
        functools.partial(_hgrn_body, seq=seq, layer=layer),
        grid=(bsz, HG_HEADS),
        in_specs=[
            blk(0), blk(HG_HEADS), blk(2 * HG_HEADS), blk(3 * HG_HEADS),
            pl.BlockSpec((lb_logits.shape[0], HG_DK), lambda b, h: (0, h)),
            pl.BlockSpec((1, HG_DV), lambda b, h: (0, 0)),
        ],
        out_specs=pl.BlockSpec((seq, HG_DV), lambda b, h: (b, h)),
        out_shape=jax.ShapeDtypeStruct((t, HG_HEADS * HG_DV), BF16),
        compiler_params=pltpu.CompilerParams(
            dimension_semantics=("arbitrary", "arbitrary"), vmem_limit_bytes=VMEM_LIMIT),
        name="hgrn",
    )(proj, proj, proj, proj, lb_logits, norm_g)


def _group_norm(x, gsum_b, gain):
    ss = _dot((x * x).astype(BF16), gsum_b)
    return x * lax.rsqrt(ss * (1.0 / DA_DH) + NORM_EPS) * gain


def _attn_body(q_ref, k_ref, v_ref, qg_ref, kg_ref, lam_ref, sg_ref, o_ref, kn_scr, v1_scr, *, seq, lambda_init):
    tq = ATTN_TILE
    width = 2 * DA_DH
    r_l = lax.broadcasted_iota(I32, (width, width), 0) // DA_DH
    c_l = lax.broadcasted_iota(I32, (width, width), 1) // DA_DH
    gsum_b = jnp.where(r_l == c_l, 1.0, 0.0).astype(BF16)
    lane = lax.broadcasted_iota(I32, (1, width), 1)
    kg = kg_ref[...].astype(F32)
    qg = qg_ref[...].astype(F32) * (DA_DH ** -0.5 * LOG2E)
    sg = sg_ref[...].astype(F32) * (1.0 - lambda_init)
    ones = jnp.ones((tq, width), BF16)
    row = lax.broadcasted_iota(I32, (tq, tq), 0)
    col = lax.broadcasted_iota(I32, (tq, tq), 1)
    keep = row >= col

    lam_v = lam_ref[...].astype(F32)
    lam = (jnp.exp(jnp.sum(lam_v[0:1] * lam_v[1:2], axis=-1, keepdims=True))
           - jnp.exp(jnp.sum(lam_v[2:3] * lam_v[3:4], axis=-1, keepdims=True)) + lambda_init)

    def softmax_v(qc, nk):
        s = _dot_nt(qc, kn_scr[0:nk, :])
        diag = jnp.where(keep, s[:, nk - tq:], -jnp.inf)
        s = diag if nk == tq else jnp.concatenate([s[:, :nk - tq], diag], axis=1)
        m = jnp.max(s, axis=-1, keepdims=True)
        return _dot(jnp.exp2(s - m).astype(BF16), v1_scr[0:nk, :])

    for i in range(seq // tq):
        sl = slice(i * tq, (i + 1) * tq)
        kn_scr[sl, :] = _group_norm(k_ref[sl, :].astype(F32), gsum_b, kg).astype(BF16)
        v1_scr[sl, :] = jnp.concatenate([v_ref[sl, :], ones], axis=1)
        qn = _group_norm(q_ref[sl, :].astype(F32), gsum_b, qg)
        nk = (i + 1) * tq
        a1 = softmax_v(jnp.where(lane < DA_DH, qn, 0.0).astype(BF16), nk)
        a2 = softmax_v(jnp.where(lane >= DA_DH, qn, 0.0).astype(BF16), nk)
        o = a1[:, :width] / a1[:, width:width + 1] - lam * (a2[:, :width] / a2[:, width:width + 1])
        ms = jnp.mean(o * o, axis=-1, keepdims=True)
        o_ref[sl, :] = (o * lax.rsqrt(ms + NORM_EPS) * sg).astype(o_ref.dtype)


def _attn(proj, qg2, kg2, lam4, subln_g, bsz, seq, col0, lambda_init):
    t = proj.shape[0]
    width = 2 * DA_DH
    return pl.pallas_call(
        functools.partial(_attn_body, seq=seq, lambda_init=lambda_init),
        grid=(bsz, DA_HEADS),
        in_specs=[
            pl.BlockSpec((seq, width), lambda b, h: (b, col0 + h)),
            pl.BlockSpec((seq, width), lambda b, h: (b, col0 + DA_HEADS + h)),
            pl.BlockSpec((seq, width), lambda b, h: (b, col0 + 2 * DA_HEADS + h)),
            pl.BlockSpec((1, width), lambda b, h: (0, 0)),
            pl.BlockSpec((1, width), lambda b, h: (0, 0)),
            pl.BlockSpec((4, DA_DH), lambda b, h: (0, 0)),
            pl.BlockSpec((1, width), lambda b, h: (0, 0)),
        ],
        out_specs=pl.BlockSpec((seq, width), lambda b, h: (b, h)),
        out_shape=jax.ShapeDtypeStruct((t, DA_HEADS * width), BF16),
        scratch_shapes=[pltpu.VMEM((seq, width), BF16), pltpu.VMEM((seq, 2 * width), BF16)],
        compiler_params=pltpu.CompilerParams(
            dimension_semantics=("arbitrary", "arbitrary"), vmem_limit_bytes=VMEM_LIMIT),
        name="attn",
    )(proj, proj, proj, qg2, kg2, lam4, subln_g)


def _mixout_body(x_ref, oa_ref, od_ref, ga0_ref, ga1_ref, gd0_ref, gd1_ref, wo_ref, g1_ref, g_ref, sh_ref, sc_ref,
                 wr_ref, br_ref,
                 x1_ref, h2_ref, idx_ref, rank_ref, gcol_ref, cnt_ref, carry_scr, wo_scr):
    i = pl.program_id(0)
    tm = x_ref.shape[0]
    hw = oa_ref.shape[1]

    @pl.when(i == 0)
    def _():
        carry_scr[...] = jnp.zeros_like(carry_scr)
        wo_scr[...] = wo_ref[...].astype(BF16)

    ya = _dot(oa_ref[...], wo_scr[0:hw, :])
    yd = _dot(od_ref[...], wo_scr[hw:, :])
    ga = jnp.concatenate([ga0_ref[...], ga1_ref[...]], axis=1).astype(F32)
    gd = jnp.concatenate([gd0_ref[...], gd1_ref[...]], axis=1).astype(F32)
    y = _sigmoid(ga) * ya + _sigmoid(gd) * yd
    x1 = x_ref[...] + g1_ref[0] * y
    x1_ref[...] = x1
    h2 = _norm_mod(x1, g_ref[...], sh_ref[0], sc_ref[0])
    h2_ref[...] = _pack_halves(h2)

    hh, hl = _split_bf16(h2)
    wh, wl = _split_bf16(wr_ref[...])
    logits = _dot_nt(wh, hh) + _dot_nt(wl, hh) + _dot_nt(wh, hl) + br_ref[...]

    e_iota = lax.broadcasted_iota(I32, (N_EXPERTS, tm), 0).astype(F32)
    vals = logits
    tops, sels, idxs = [], [], []
    for _ in range(TOP_K):
        m = jnp.max(vals, axis=0, keepdims=True)
        idx = jnp.min(jnp.where(vals == m, e_iota, float(N_EXPERTS)), axis=0, keepdims=True)
        sel = e_iota == idx
        vals = jnp.where(sel, -jnp.inf, vals)
        tops.append(m)
        sels.append(sel)
        idxs.append(idx)
    ex = [jnp.exp(tv - tops[0]) for tv in tops]
    den = ex[0] + ex[1] + ex[2] + ex[3]
    gates = [v / den for v in ex]

    hot = jnp.where(sels[0] | sels[1] | sels[2] | sels[3], 1.0, 0.0)
    r_t = lax.broadcasted_iota(I32, (tm, tm), 0)
    c_t = lax.broadcasted_iota(I32, (tm, tm), 1)
    upper = jnp.where(r_t < c_t, 1.0, 0.0).astype(BF16)
    excl = _dot(hot.astype(BF16), upper) + carry_scr[:, 0:1]
    carry_scr[...] = carry_scr[...] + jnp.sum(hot, axis=1, keepdims=True)
    cnt_ref[...] = carry_scr[...]

    ranks = [jnp.sum(jnp.where(s, excl, 0.0), axis=0, keepdims=True) for s in sels]
    idx_ref[...] = jnp.concatenate(idxs, axis=0).astype(I32)
    rank_ref[...] = jnp.concatenate(ranks, axis=0).astype(I32)
    gpad = jnp.concatenate(gates + [jnp.zeros((LANES - TOP_K, tm), F32)], axis=0)
    gcol_ref[...] = gpad.T


def _mixout(x2, oa, od, proj, col_g, wo, ada3, ffn_g, wr_t, br_col, seq):
    t, d = x2.shape
    tm = min(512, seq)
    per_b = seq // tm
    hw = oa.shape[1]
    row = lambda w: pl.BlockSpec((tm, w), lambda i: (i, 0))
    gate = lambda c: pl.BlockSpec((tm, d // 2), lambda i, c=c: (i, col_g + c))
    mod = lambda c: pl.BlockSpec((1, 1, d), lambda i, c=c: (i // per_b, 0, c))
    full = lambda a: pl.BlockSpec(a.shape, lambda i: (0,) * a.ndim)
    return pl.pallas_call(
        _mixout_body,
        grid=(t // tm,),
        in_specs=[
            row(d), row(hw), row(hw),
            gate(0), gate(1), gate(2), gate(3),
            full(wo),
            mod(2),
            full(ffn_g), mod(3), mod(4),
            full(wr_t), full(br_col),
        ],
        out_specs=[
            row(d), row(d // 2),
            pl.BlockSpec((TOP_K, tm), lambda i: (0, i)),
            pl.BlockSpec((TOP_K, tm), lambda i: (0, i)),
            pl.BlockSpec((tm, LANES), lambda i: (i, 0)),
            pl.BlockSpec((N_EXPERTS, LANES), lambda i: (0, 0)),
        ],
        out_shape=[
            jax.ShapeDtypeStruct((t, d), F32),
            jax.ShapeDtypeStruct((t, d // 2), I32),
            jax.ShapeDtypeStruct((TOP_K, t), I32),
            jax.ShapeDtypeStruct((TOP_K, t), I32),
            jax.ShapeDtypeStruct((t, LANES), F32),
            jax.ShapeDtypeStruct((N_EXPERTS, LANES), F32),
        ],
        scratch_shapes=[pltpu.VMEM((N_EXPERTS, LANES), F32), pltpu.VMEM(wo.shape, BF16)],
        compiler_params=pltpu.CompilerParams(
            dimension_semantics=("arbitrary",), vmem_limit_bytes=VMEM_LIMIT),
        name="mixout",
    )(x2, oa, od, proj, proj, proj, proj, wo, ada3, ffn_g, ada3, ada3, wr_t, br_col)


def _sc_mesh():
    return plsc.VectorSubcoreMesh(core_axis_name="c", subcore_axis_name="s")


def _sc_worker_base(rows_per_worker):
    wid = lax.axis_index("s") * SC_CORES + lax.axis_index("c")
    return wid * rows_per_worker


def _sc_dispatch(h2, dest, n_rows):
    t, d = h2.shape
    tpw = t // SC_WORKERS
    assert t % (SC_WORKERS * SC_CHUNK) == 0

    def body(h2_hbm, dest_hbm, xb_hbm, idx_v, rows_v):
        base = _sc_worker_base(tpw)

        @pl.loop(0, tpw // SC_CHUNK)
        def _(i):
            t0 = pl.multiple_of(base + i * SC_CHUNK, SC_CHUNK)
            pltpu.sync_copy(h2_hbm.at[pl.ds(t0, SC_CHUNK)], rows_v)
            for k in range(TOP_K):
                pltpu.sync_copy(dest_hbm.at[pl.ds(k * t + t0, SC_CHUNK)], idx_v)
                pltpu.sync_copy(rows_v, xb_hbm.at[idx_v])

    return pl.kernel(
        body, out_type=jax.ShapeDtypeStruct((n_rows, d), h2.dtype), mesh=_sc_mesh(),
        scratch_types=[pltpu.VMEM((SC_CHUNK,), I32), pltpu.VMEM((SC_CHUNK, d), h2.dtype)],
        name="dispatch",
    )(h2, dest)


def _sc_undispatch(y, dest):
    n_asg = dest.shape[0]
    d = y.shape[1]
    rpw = n_asg // SC_WORKERS
    assert n_asg % (SC_WORKERS * SC_CHUNK) == 0

    def body(y_hbm, dest_hbm, yt_hbm, idx_v, rows_v):
        base = _sc_worker_base(rpw)

        @pl.loop(0, rpw // SC_CHUNK)
        def _(i):
            r0 = pl.multiple_of(base + i * SC_CHUNK, SC_CHUNK)
            pltpu.sync_copy(dest_hbm.at[pl.ds(r0, SC_CHUNK)], idx_v)
            pltpu.sync_copy(y_hbm.at[idx_v], rows_v)
            pltpu.sync_copy(rows_v, yt_hbm.at[pl.ds(r0, SC_CHUNK)])

    return pl.kernel(
        body, out_type=jax.ShapeDtypeStruct((n_asg, d), y.dtype), mesh=_sc_mesh(),
        scratch_types=[pltpu.VMEM((SC_CHUNK,), I32), pltpu.VMEM((SC_CHUNK, d), y.dtype)],
        name="undispatch",
    )(y, dest)


def _ffn_body(be_ref, first_ref, nxt_ref, slot_ref, nact_ref, x_ref, w1_hbm, b1_ref, w2_hbm, b2_ref, y_ref,
              w1f, w2f, w1c, w2c, sem1, sem2):
    j = pl.program_id(0)
    ff = w2f.shape[1]
    pair = 2 * LANES
    ngroup = (2 * ff) // pair

    def weight_copies(e, slot):
        return (pltpu.make_async_copy(w1_hbm.at[e], w1f.at[slot], sem1.at[slot]),
                pltpu.make_async_copy(w2_hbm.at[e], w2f.at[slot], sem2.at[slot]))

    @pl.when(j == 0)
    def _():
        for cp in weight_copies(be_ref[0], 0):
            cp.start()

    @pl.when(first_ref[j] == 1)
    def _():
        slot = slot_ref[j]
        for cp in weight_copies(be_ref[j], slot):
            cp.wait()

        @pl.when(nxt_ref[j] >= 0)
        def _():
            for cp in weight_copies(nxt_ref[j], 1 - slot):
                cp.start(priority=1)

        r_p = lax.broadcasted_iota(I32, (pair, pair), 0)
        c_p = lax.broadcasted_iota(I32, (pair, pair), 1)
        src = jnp.where(c_p < LANES, 2 * c_p, 2 * (c_p - LANES) + 1)
        perm = jnp.where(r_p == src, 1.0, 0.0).astype(BF16)
        for g in range(ngroup):
            cols = slice(g * pair, (g + 1) * pair)
            w1c[:, cols] = _dot(w1f[slot, :, cols].astype(BF16), perm).astype(BF16)
        w2c[...] = w2f[slot].astype(BF16)

    @pl.when(j < nact_ref[0])
    def _():
        x_lo, x_hi = _unpack_halves(x_ref[...])
        xb = jnp.concatenate([x_lo.astype(BF16), x_hi.astype(BF16)], axis=1)
        u = _dot(xb, w1c[...]) + b1_ref[0]
        acts = []
        for g in range(ngroup):
            glu = jnp.minimum(u[:, g * pair:g * pair + LANES], SWIGLU_LIMIT)
            lin = jnp.clip(u[:, g * pair + LANES:(g + 1) * pair], -SWIGLU_LIMIT, SWIGLU_LIMIT)
            acts.append((glu * _sigmoid(SWIGLU_ALPHA * glu) * (lin + 1.0)).astype(BF16))
        act = jnp.concatenate(acts, axis=1)
        y_ref[...] = _pack_halves(_dot(act, w2c[...]) + b2_ref[0])


def _ffn(block_expert, first, nxt, slot, nact, xb, w1, b1p, w2, b2):
    bm = FFN_BLOCK
    n_rows, dw = xb.shape
    d = 2 * dw
    n_blocks = n_rows // bm
    ff2 = w1.shape[2]
    ff = w2.shape[1]
    row_blk = lambda j, be, fi, nx, sl, na: (jnp.minimum(j, na[0] - 1), 0)
    bias_blk = lambda j, be, fi, nx, sl, na: (be[j], 0, 0)
    grid_spec = pltpu.PrefetchScalarGridSpec(
        num_scalar_prefetch=5,
        grid=(n_blocks,),
        in_specs=[
            pl.BlockSpec((bm, dw), row_blk),
            pl.BlockSpec(memory_space=pl.ANY),
            pl.BlockSpec((1, 1, ff2), bias_blk),
            pl.BlockSpec(memory_space=pl.ANY),
            pl.BlockSpec((1, 1, d), bias_blk),
        ],
        out_specs=pl.BlockSpec((bm, dw), row_blk),
        scratch_shapes=[
            pltpu.VMEM((2, d, ff2), F32), pltpu.VMEM((2, ff, d), F32),
            pltpu.VMEM((d, ff2), BF16), pltpu.VMEM((ff, d), BF16),
            pltpu.SemaphoreType.DMA((2,)), pltpu.SemaphoreType.DMA((2,)),
        ],
    )
    return pl.pallas_call(
        _ffn_body,
        grid_spec=grid_spec,
        out_shape=jax.ShapeDtypeStruct((n_rows, dw), I32),
        compiler_params=pltpu.CompilerParams(
            dimension_semantics=("arbitrary",), vmem_limit_bytes=VMEM_LIMIT),
        name="ffn",
    )(block_expert, first, nxt, slot, nact, xb, w1, b1p, w2, b2)


def _combine_body(x1_ref, y0_ref, y1_ref, y2_ref, y3_ref, gcol_ref, g2_ref, o_ref):
    gc = gcol_ref[...]
    m_lo = m_hi = None
    for k, y_ref in enumerate((y0_ref, y1_ref, y2_ref, y3_ref)):
        lo, hi = _unpack_halves(y_ref[...])
        gk = gc[:, k:k + 1]
        m_lo = gk * lo if m_lo is None else m_lo + gk * lo
        m_hi = gk * hi if m_hi is None else m_hi + gk * hi
    m = jnp.concatenate([m_lo, m_hi], axis=1)
    o_ref[...] = (x1_ref[...] + g2_ref[0] * m).astype(o_ref.dtype)


def _combine(x1, yall, gcol, ada3, seq, out_dtype):
    t, d = x1.shape
    tm = min(1024, seq)
    per_b = seq // tm
    nt = t // tm
    yk = lambda k: pl.BlockSpec((tm, d // 2), lambda i, k=k: (k * nt + i, 0))
    return pl.pallas_call(
        _combine_body,
        grid=(nt,),
        in_specs=[
            pl.BlockSpec((tm, d), lambda i: (i, 0)),
            yk(0), yk(1), yk(2), yk(3),
            pl.BlockSpec((tm, LANES), lambda i: (i, 0)),
            pl.BlockSpec((1, 1, d), lambda i: (i // per_b, 0, 5)),
        ],
        out_specs=pl.BlockSpec((tm, d), lambda i: (i, 0)),
        out_shape=jax.ShapeDtypeStruct((t, d), out_dtype),
        compiler_params=pltpu.CompilerParams(
            dimension_semantics=("arbitrary",), vmem_limit_bytes=VMEM_LIMIT),
        name="combine",
    )(x1, yall, yall, yall, yall, gcol, ada3)


def _route_tables(idx, rank, counts, n_tok):
    bm = FFN_BLOCK
    n_asg = TOP_K * n_tok
    n_blocks = -(-(n_asg + N_EXPERTS * (bm - 1)) // bm)
    padded = (counts + bm - 1) // bm * bm
    pad_ends = jnp.cumsum(padded)
    pad_starts = pad_ends - padded
    e_ids = jnp.arange(N_EXPERTS, dtype=I32)
    start_of = jnp.sum(jnp.where(idx[None] == e_ids[:, None, None], pad_starts[:, None, None], 0), axis=0)
    dest = (start_of + rank).reshape(-1)
    nact = (pad_ends[-1] // bm).astype(I32)
    blk_start = jnp.arange(n_blocks, dtype=I32) * bm
    last = jnp.sum(jnp.where(pad_ends <= pad_ends[-1] - 1, 1, 0)).astype(I32)
    be = jnp.sum(jnp.where(pad_ends[None, :] <= blk_start[:, None], 1, 0), axis=1).astype(I32)
    active = blk_start < pad_ends[-1]
    be = jnp.where(active, be, last)
    blk = jnp.arange(n_blocks, dtype=I32)
    first = active & ((blk == 0) | (be != jnp.roll(be, 1)))
    slot = (jnp.cumsum(first.astype(I32)) - 1) & 1
    later_first = first[None, :] & (blk[None, :] > blk[:, None])
    nxt_pos = jnp.min(jnp.where(later_first, blk[None, :], n_blocks), axis=1)
    nxt = jnp.sum(jnp.where(blk[None, :] == nxt_pos[:, None], be[None, :], 0), axis=1)
    nxt = jnp.where(nxt_pos < n_blocks, nxt, -1).astype(I32)
    return be, first.astype(I32), nxt, slot.astype(I32), nact.reshape(1), dest, n_blocks * bm


def kernel(x, c, w_ada, b_ada, mix_norm_g, ffn_norm_g, w_in, hg_lower_bound_logits, hg_out_norm_g, da_q_norm_g, da_k_norm_g, da_lambda_q1, da_lambda_k1, da_lambda_q2, da_lambda_k2, da_subln_g, w_out, w_router, b_router, w1, b1, w2, b2):
    bsz, seq, d = x.shape
    t = bsz * seq
    depth = w_ada.shape[0]
    out_dtype = x.dtype
    hw = HG_HEADS * HG_DV
    xcur = x.reshape(t, d)
    for l in range(depth):
        ada = _ada(c, w_ada[l], b_ada[l])
        ada3 = ada.reshape(bsz, 1, N_MOD * d)
        col_h = 0
        col_a = col_h + 4 * HG_HEADS
        col_g = (4 * hw + 3 * DA_HEADS * 2 * DA_DH) // (d // 2)
        proj = _inproj(xcur, mix_norm_g[l].reshape(1, d), ada3, w_in[l].astype(BF16), seq)

        o_a = _hgrn(proj, hg_lower_bound_logits, hg_out_norm_g[l].reshape(1, HG_DV), bsz, seq, col_h, l)
        lambda_init = 0.8 - 0.6 * math.exp(-0.3 * l)
        qg2 = jnp.tile(da_q_norm_g[l], 2).reshape(1, 2 * DA_DH)
        kg2 = jnp.tile(da_k_norm_g[l], 2).reshape(1, 2 * DA_DH)
        lam4 = jnp.stack([da_lambda_q1[l], da_lambda_k1[l], da_lambda_q2[l], da_lambda_k2[l]])
        o_d = _attn(proj, qg2, kg2, lam4, da_subln_g[l].reshape(1, 2 * DA_DH), bsz, seq, col_a, lambda_init)

        x1, h2, idx, rank, gcol, cnt = _mixout(
            xcur, o_a, o_d, proj, col_g, w_out[l], ada3, ffn_norm_g[l].reshape(1, d),
            w_router[l].T, b_router[l].reshape(N_EXPERTS, 1), seq)

        counts = cnt[:, 0].astype(I32)
        be, first, nxt, slot, nact, dest, n_rows = _route_tables(idx, rank, counts, t)
        b1p = b1[l].reshape(N_EXPERTS, -1, LANES, 2).transpose(0, 1, 3, 2).reshape(N_EXPERTS, 1, -1)
        xb = _sc_dispatch(h2, dest, n_rows)
        yb = _ffn(be, first, nxt, slot, nact, xb, w1[l], b1p, w2[l], b2[l].reshape(N_EXPERTS, 1, d))
        yall = _sc_undispatch(yb, dest)
        xcur = _combine(x1, yall, gcol, ada3, seq, out_dtype)
    return xcur.reshape(bsz, seq, d)
```

```python
import functools
import math

import jax
import jax.numpy as jnp
from jax import lax
from jax.experimental import pallas as pl
from jax.experimental.pallas import tpu as pltpu
from jax.experimental.pallas import tpu_sc as plsc

F32 = jnp.float32
BF16 = jnp.bfloat16
I32 = jnp.int32

HG_HEADS = 4
HG_DK = 128
HG_DV = 128
HG_CHUNK = 32
DA_HEADS = 4
DA_DH = 64
N_EXPERTS = 32
TOP_K = 4
SWIGLU_ALPHA = 1.702
SWIGLU_LIMIT = 7.0
NORM_EPS = 1e-6
LOG2E = math.log2(math.e)
N_MOD = 6

LANES = 128
VMEM_LIMIT = 56 * 1024 * 1024

HG_ROWS = 256
ATTN_TILE = 256
FFN_BLOCK = 512

SC_CORES = 2
SC_WORKERS = SC_CORES * 16
SC_CHUNK = 128


def _sigmoid(x):
    return 1.0 / (1.0 + jnp.exp(-x))


def _dot(a, b):
    return jnp.dot(a, b, preferred_element_type=F32)


def _dot_nt(a, b):
    return lax.dot_general(a, b, (((1,), (1,)), ((), ())), preferred_element_type=F32)


def _split_bf16(x):
    hi = x.astype(BF16)
    lo = (x - hi.astype(F32)).astype(BF16)
    return hi, lo


def _pack_halves(x):
    half = x.shape[1] // 2
    lo = lax.bitcast_convert_type(x[:, :half].astype(BF16).astype(F32), I32)
    hi = lax.bitcast_convert_type(x[:, half:].astype(BF16).astype(F32), I32)
    return lax.shift_right_logical(lo, 16) | hi


def _unpack_halves(w):
    lo = lax.bitcast_convert_type(lax.shift_left(w, 16), F32)
    hi = lax.bitcast_convert_type(w & jnp.int32(-65536), F32)
    return lo, hi


def _ada_body(c_ref, w_ref, b_ref, o_ref):
    c = c_ref[...].astype(F32)
    ch, cl = _split_bf16(c * _sigmoid(c))
    wh, wl = _split_bf16(w_ref[...])
    o_ref[...] = _dot(ch, wh) + _dot(cl, wh) + _dot(ch, wl) + b_ref[...]


def _ada(c, w_ada, b_ada):
    bsz, d = c.shape
    n = w_ada.shape[1]
    tn = d
    return pl.pallas_call(
        _ada_body,
        grid=(n // tn,),
        in_specs=[
            pl.BlockSpec((bsz, d), lambda j: (0, 0)),
            pl.BlockSpec((d, tn), lambda j: (0, j)),
            pl.BlockSpec((1, tn), lambda j: (0, j)),
        ],
        out_specs=pl.BlockSpec((bsz, tn), lambda j: (0, j)),
        out_shape=jax.ShapeDtypeStruct((bsz, n), F32),
        name="ada",
    )(c, w_ada, b_ada.reshape(1, n))


def _norm_mod(x, g, shift, scale):
    ms = jnp.mean(x * x, axis=-1, keepdims=True)
    return (x * lax.rsqrt(ms + NORM_EPS) * g) * (1.0 + scale) + shift


def _inproj_body(x_ref, g_ref, sh_ref, sc_ref, w_ref, o_ref):
    h = _norm_mod(x_ref[...], g_ref[...], sh_ref[0], sc_ref[0])
    o_ref[...] = _dot(h.astype(BF16), w_ref[...]).astype(BF16)


def _inproj(x2, g, ada3, w_bf16, seq):
    t, d = x2.shape
    n = w_bf16.shape[1]
    tm = min(1024, seq)
    nj = 2
    tn = n // nj
    per_b = seq // tm
    return pl.pallas_call(
        _inproj_body,
        grid=(nj, t // tm),
        in_specs=[
            pl.BlockSpec((tm, d), lambda j, i: (i, 0)),
            pl.BlockSpec((1, d), lambda j, i: (0, 0)),
            pl.BlockSpec((1, 1, d), lambda j, i: (i // per_b, 0, 0)),
            pl.BlockSpec((1, 1, d), lambda j, i: (i // per_b, 0, 1)),
            pl.BlockSpec((d, tn), lambda j, i: (0, j)),
        ],
        out_specs=pl.BlockSpec((tm, tn), lambda j, i: (i, j)),
        out_shape=jax.ShapeDtypeStruct((t, n), BF16),
        compiler_params=pltpu.CompilerParams(
            dimension_semantics=("arbitrary", "arbitrary"), vmem_limit_bytes=VMEM_LIMIT),
        name="inproj",
    )(x2, g, ada3, ada3, w_bf16)


def _hgrn_body(q_ref, f_ref, i_ref, og_ref, lbl_ref, g_ref, o_ref, *, seq, layer):
    rows, chunk = HG_ROWS, HG_CHUNK
    nchunk = rows // chunk
    lbl = lbl_ref[...].astype(F32)
    e = jnp.exp(lbl - jnp.max(lbl, axis=0, keepdims=True))
    lb = jnp.sum(e[: layer + 1], axis=0, keepdims=True) / jnp.sum(e, axis=0, keepdims=True)
    r_i = lax.broadcasted_iota(I32, (rows, rows), 0)
    c_i = lax.broadcasted_iota(I32, (rows, rows), 1)
    tri = ((r_i // chunk) == (c_i // chunk)) & (r_i >= c_i)
    tri_b = jnp.where(tri, 1.0, 0.0).astype(BF16)
    row_chunk = lax.broadcasted_iota(I32, (rows, HG_DK), 0) // chunk
    g = g_ref[...].astype(F32)

    def block(r, st):
        sl = pl.ds(r * rows, rows)
        qr = q_ref[sl, :].astype(F32)
        fr = f_ref[sl, :].astype(F32)
        v = i_ref[sl, :].astype(F32)
        og = og_ref[sl, :].astype(F32)
        q = qr * _sigmoid(qr)
        f = lb + (1.0 - lb) * _sigmoid(fr)
        k = 1.0 - f
        logf = jnp.log(f)
        lhi, llo = _split_bf16(logf)
        bc2 = _dot(tri_b, jnp.concatenate([lhi, llo], axis=1))
        bcum = bc2[:, :HG_DK] + bc2[:, HG_DK:]
        b3 = bcum.reshape(nchunk, chunk, HG_DK)
        bl = b3[:, chunk - 1:chunk, :]
        dec = jnp.exp(bl)
        kt_f = k * jnp.exp(-bcum)
        qt_f = q * jnp.exp(bcum)
        kd = (kt_f.reshape(nchunk, chunk, HG_DK) * dec).reshape(rows, HG_DK)
        a = jnp.where(tri, _dot_nt(qt_f.astype(BF16), kt_f.astype(BF16)), 0.0).astype(BF16)
        vt_b = v.T.astype(BF16)
        kd_x = jnp.concatenate([jnp.where(row_chunk == c, kd, 0.0) for c in range(nchunk)], axis=1)
        kv_all = _dot(vt_b, kd_x.astype(BF16))
        starts = []
        for c in range(nchunk):
            starts.append(st.astype(BF16))
            st = st * dec[c] + kv_all[:, c * HG_DK:(c + 1) * HG_DK]
        q_x = [jnp.where(row_chunk == c, qt_f, 0.0).astype(BF16) for c in range(nchunk)]
        o = _dot_nt(jnp.concatenate([a] + q_x, axis=1), jnp.concatenate([vt_b] + starts, axis=1))
        ms = jnp.mean(o * o, axis=-1, keepdims=True)
        o = o * lax.rsqrt(ms + NORM_EPS) * g
        o_ref[sl, :] = (o * (og * _sigmoid(og))).astype(o_ref.dtype)
        return st

    st = jnp.zeros((HG_DV, HG_DK), F32)
    for r in range(seq // rows):
        st = block(r, st)


def _hgrn(proj, lb_logits, norm_g, bsz, seq, col0, layer):
    t = proj.shape[0]
    blk = lambda off: pl.BlockSpec((seq, LANES), lambda b, h, off=off: (b, col0 + off + h))
    return pl.pallas_call(
        functools.partial(_hgrn_body, seq=seq, layer=layer),
        grid=(bsz, HG_HEADS),
        in_specs=[
            blk(0), blk(HG_HEADS), blk(2 * HG_HEADS), blk(3 * HG_HEADS),
            pl.BlockSpec((lb_logits.shape[0], HG_DK), lambda b, h: (0, h)),
            pl.BlockSpec((1, HG_DV), lambda b, h: (0, 0)),
        ],
        out_specs=pl.BlockSpec((seq, HG_DV), lambda b, h: (b, h)),
        out_shape=jax.ShapeDtypeStruct((t, HG_HEADS * HG_DV), BF16),
        compiler_params=pltpu.CompilerParams(
            dimension_semantics=("arbitrary", "arbitrary"), vmem_limit_bytes=VMEM_LIMIT),
        name="hgrn",
    )(proj, proj, proj, proj, lb_logits, norm_g)


def _group_norm(x, gsum_b, gain):
    ss = _dot((x * x).astype(BF16), gsum_b)
    return x * lax.rsqrt(ss * (1.0 / DA_DH) + NORM_EPS) * gain


def _attn_body(q_ref, k_ref, v_ref, qg_ref, kg_ref, lam_ref, sg_ref, o_ref, kn_scr, v1_scr, *, seq, lambda_init):
    tq = ATTN_TILE
    width = 2 * DA_DH
    r_l = lax.broadcasted_iota(I32, (width, width), 0) // DA_DH
    c_l = lax.broadcasted_iota(I32, (width, width), 1) // DA_DH
    gsum_b = jnp.where(r_l == c_l, 1.0, 0.0).astype(BF16)
    lane = lax.broadcasted_iota(I32, (1, width), 1)
    kg = kg_ref[...].astype(F32)
    qg = qg_ref[...].astype(F32) * (DA_DH ** -0.5 * LOG2E)
    sg = sg_ref[...].astype(F32) * (1.0 - lambda_init)
    ones = jnp.ones((tq, width), BF16)
    row = lax.broadcasted_iota(I32, (tq, tq), 0)
    col = lax.broadcasted_iota(I32, (tq, tq), 1)
    keep = row >= col

    lam_v = lam_ref[...].astype(F32)
    lam = (jnp.exp(jnp.sum(lam_v[0:1] * lam_v[1:2], axis=-1, keepdims=True))
           - jnp.exp(jnp.sum(lam_v[2:3] * lam_v[3:4], axis=-1, keepdims=True)) + lambda_init)

    def softmax_v(qc, nk):
        s = _dot_nt(qc, kn_scr[0:nk, :])
        diag = jnp.where(keep, s[:, nk - tq:], -jnp.inf)
        s = diag if nk == tq else jnp.concatenate([s[:, :nk - tq], diag], axis=1)
        m = jnp.max(s, axis=-1, keepdims=True)
        return _dot(jnp.exp2(s - m).astype(BF16), v1_scr[0:nk, :])

    for i in range(seq // tq):
        sl = slice(i * tq, (i + 1) * tq)
        kn_scr[sl, :] = _group_norm(k_ref[sl, :].astype(F32), gsum_b, kg).astype(BF16)
        v1_scr[sl, :] = jnp.concatenate([v_ref[sl, :], ones], axis=1)
        qn = _group_norm(q_ref[sl, :].astype(F32), gsum_b, qg)
        nk = (i + 1) * tq
        a1 = softmax_v(jnp.where(lane < DA_DH, qn, 0.0).astype(BF16), nk)
        a2 = softmax_v(jnp.where(lane >= DA_DH, qn, 0.0).astype(BF16), nk)
        o = a1[:, :width] / a1[:, width:width + 1] - lam * (a2[:, :width] / a2[:, width:width + 1])
        ms = jnp.mean(o * o, axis=-1, keepdims=True)
        o_ref[sl, :] = (o * lax.rsqrt(ms + NORM_EPS) * sg).astype(o_ref.dtype)


def _attn(proj, qg2, kg2, lam4, subln_g, bsz, seq, col0, lambda_init):
    t = proj.shape[0]
    width = 2 * DA_DH
    return pl.pallas_call(
        functools.partial(_attn_body, seq=seq, lambda_init=lambda_init),
        grid=(bsz, DA_HEADS),
        in_specs=[
            pl.BlockSpec((seq, width), lambda b, h: (b, col0 + h)),
            pl.BlockSpec((seq, width), lambda b, h: (b, col0 + DA_HEADS + h)),
            pl.BlockSpec((seq, width), lambda b, h: (b, col0 + 2 * DA_HEADS + h)),
            pl.BlockSpec((1, width), lambda b, h: (0, 0)),
            pl.BlockSpec((1, width), lambda b, h: (0, 0)),
            pl.BlockSpec((4, DA_DH), lambda b, h: (0, 0)),
            pl.BlockSpec((1, width), lambda b, h: (0, 0)),
        ],
        out_specs=pl.BlockSpec((seq, width), lambda b, h: (b, h)),
        out_shape=jax.ShapeDtypeStruct((t, DA_HEADS * width), BF16),
        scratch_shapes=[pltpu.VMEM((seq, width), BF16), pltpu.VMEM((seq, 2 * width), BF16)],
        compiler_params=pltpu.CompilerParams(
            dimension_semantics=("arbitrary", "arbitrary"), vmem_limit_bytes=VMEM_LIMIT),
        name="attn",
    )(proj, proj, proj, qg2, kg2, lam4, subln_g)


def _mixout_body(x_ref, oa_ref, od_ref, ga0_ref, ga1_ref, gd0_ref, gd1_ref, wo_ref, g1_ref, g_ref, sh_ref, sc_ref,
                 wr_ref, br_ref,
                 x1_ref, h2_ref, idx_ref, rank_ref, gcol_ref, cnt_ref, carry_scr, wo_scr):
    i = pl.program_id(0)
    tm = x_ref.shape[0]
    hw = oa_ref.shape[1]

    @pl.when(i == 0)
    def _():
        carry_scr[...] = jnp.zeros_like(carry_scr)
        wo_scr[...] = wo_ref[...].astype(BF16)

    ya = _dot(oa_ref[...], wo_scr[0:hw, :])
    yd = _dot(od_ref[...], wo_scr[hw:, :])
    ga = jnp.concatenate([ga0_ref[...], ga1_ref[...]], axis=1).astype(F32)
    gd = jnp.concatenate([gd0_ref[...], gd1_ref[...]], axis=1).astype(F32)
    y = _sigmoid(ga) * ya + _sigmoid(gd) * yd
    x1 = x_ref[...] + g1_ref[0] * y
    x1_ref[...] = x1
    h2 = _norm_mod(x1, g_ref[...], sh_ref[0], sc_ref[0])
    h2_ref[...] = _pack_halves(h2)

    hh, hl = _split_bf16(h2)
    wh, wl = _split_bf16(wr_ref[...])
    logits = _dot_nt(wh, hh) + _dot_nt(wl, hh) + _dot_nt(wh, hl) + br_ref[...]

    e_iota = lax.broadcasted_iota(I32, (N_EXPERTS, tm), 0).astype(F32)
    vals = logits
    tops, sels, idxs = [], [], []
    for _ in range(TOP_K):
        m = jnp.max(vals, axis=0, keepdims=True)
        idx = jnp.min(jnp.where(vals == m, e_iota, float(N_EXPERTS)), axis=0, keepdims=True)
        sel = e_iota == idx
        vals = jnp.where(sel, -jnp.inf, vals)
        tops.append(m)
        sels.append(sel)
        idxs.append(idx)
    ex = [jnp.exp(tv - tops[0]) for tv in tops]
    den = ex[0] + ex[1] + ex[2] + ex[3]
    gates = [v / den for v in ex]

    hot = jnp.where(sels[0] | sels[1] | sels[2] | sels[3], 1.0, 0.0)
    r_t = lax.broadcasted_iota(I32, (tm, tm), 0)
    c_t = lax.broadcasted_iota(I32, (tm, tm), 1)
    upper = jnp.where(r_t < c_t, 1.0, 0.0).astype(BF16)
    excl = _dot(hot.astype(BF16), upper) + carry_scr[:, 0:1]
    carry_scr[...] = carry_scr[...] + jnp.sum(hot, axis=1, keepdims=True)
    cnt_ref[...] = carry_scr[...]

    ranks = [jnp.sum(jnp.where(s, excl, 0.0), axis=0, keepdims=True) for s in sels]
    idx_ref[...] = jnp.concatenate(idxs, axis=0).astype(I32)
    rank_ref[...] = jnp.concatenate(ranks, axis=0).astype(I32)
    gpad = jnp.concatenate(gates + [jnp.zeros((LANES - TOP_K, tm), F32)], axis=0)
    gcol_ref[...] = gpad.T


def _mixout(x2, oa, od, proj, col_g, wo, ada3, ffn_g, wr_t, br_col, seq):
    t, d = x2.shape
    tm = min(512, seq)
    per_b = seq // tm
    hw = oa.shape[1]
    row = lambda w: pl.BlockSpec((tm, w), lambda i: (i, 0))
    gate = lambda c: pl.BlockSpec((tm, d // 2), lambda i, c=c: (i, col_g + c))
    mod = lambda c: pl.BlockSpec((1, 1, d), lambda i, c=c: (i // per_b, 0, c))
    full = lambda a: pl.BlockSpec(a.shape, lambda i: (0,) * a.ndim)
    return pl.pallas_call(
        _mixout_body,
        grid=(t // tm,),
        in_specs=[
            row(d), row(hw), row(hw),
            gate(0), gate(1), gate(2), gate(3),
            full(wo),
            mod(2),
            full(ffn_g), mod(3), mod(4),
            full(wr_t), full(br_col),
        ],
        out_specs=[
            row(d), row(d // 2),
            pl.BlockSpec((TOP_K, tm), lambda i: (0, i)),
            pl.BlockSpec((TOP_K, tm), lambda i: (0, i)),
            pl.BlockSpec((tm, LANES), lambda i: (i, 0)),
            pl.BlockSpec((N_EXPERTS, LANES), lambda i: (0, 0)),
        ],
        out_shape=[
            jax.ShapeDtypeStruct((t, d), F32),
            jax.ShapeDtypeStruct((t, d // 2), I32),
            jax.ShapeDtypeStruct((TOP_K, t), I32),
            jax.ShapeDtypeStruct((TOP_K, t), I32),
            jax.ShapeDtypeStruct((t, LANES), F32),
            jax.ShapeDtypeStruct((N_EXPERTS, LANES), F32),
        ],
        scratch_shapes=[pltpu.VMEM((N_EXPERTS, LANES), F32), pltpu.VMEM(wo.shape, BF16)],
        compiler_params=pltpu.CompilerParams(
            dimension_semantics=("arbitrary",), vmem_limit_bytes=VMEM_LIMIT),
        name="mixout",
    )(x2, oa, od, proj, proj, proj, proj, wo, ada3, ffn_g, ada3, ada3, wr_t, br_col)


def _sc_mesh():
    return plsc.VectorSubcoreMesh(core_axis_name="c", subcore_axis_name="s")


def _sc_worker_base(rows_per_worker):
    wid = lax.axis_index("s") * SC_CORES + lax.axis_index("c")
    return wid * rows_per_worker


def _sc_dispatch(h2, dest, n_rows):
    t, d = h2.shape
    tpw = t // SC_WORKERS
    assert t % (SC_WORKERS * SC_CHUNK) == 0

    def body(h2_hbm, dest_hbm, xb_hbm, idx_v, rows_v):
        base = _sc_worker_base(tpw)

        @pl.loop(0, tpw // SC_CHUNK)
        def _(i):
            t0 = pl.multiple_of(base + i * SC_CHUNK, SC_CHUNK)
            pltpu.sync_copy(h2_hbm.at[pl.ds(t0, SC_CHUNK)], rows_v)
            for k in range(TOP_K):
                pltpu.sync_copy(dest_hbm.at[pl.ds(k * t + t0, SC_CHUNK)], idx_v)
                pltpu.sync_copy(rows_v, xb_hbm.at[idx_v])

    return pl.kernel(
        body, out_type=jax.ShapeDtypeStruct((n_rows, d), h2.dtype), mesh=_sc_mesh(),
        scratch_types=[pltpu.VMEM((SC_CHUNK,), I32), pltpu.VMEM((SC_CHUNK, d), h2.dtype)],
        name="dispatch",
    )(h2, dest)


def _sc_undispatch(y, dest):
    n_asg = dest.shape[0]
    d = y.shape[1]
    rpw = n_asg // SC_WORKERS
    assert n_asg % (SC_WORKERS * SC_CHUNK) == 0

    def body(y_hbm, dest_hbm, yt_hbm, idx_v, rows_v):
        base = _sc_worker_base(rpw)

        @pl.loop(0, rpw // SC_CHUNK)
        def _(i):
            r0 = pl.multiple_of(base + i * SC_CHUNK, SC_CHUNK)
            pltpu.sync_copy(dest_hbm.at[pl.ds(r0, SC_CHUNK)], idx_v)
            pltpu.sync_copy(y_hbm.at[idx_v], rows_v)
            pltpu.sync_copy(rows_v, yt_hbm.at[pl.ds(r0, SC_CHUNK)])

    return pl.kernel(
        body, out_type=jax.ShapeDtypeStruct((n_asg, d), y.dtype), mesh=_sc_mesh(),
        scratch_types=[pltpu.VMEM((SC_CHUNK,), I32), pltpu.VMEM((SC_CHUNK, d), y.dtype)],
        name="undispatch",
    )(y, dest)


def _ffn_body(be_ref, first_ref, nxt_ref, slot_ref, nv_ref, nact_ref, x_ref, w1_hbm, b1_ref, w2_hbm, b2_ref, y_ref,
              w1f, w2f, w1c, w2c, sem1, sem2):
    j = pl.program_id(0)
    ff = w2f.shape[1]
    pair = 2 * LANES
    ngroup = (2 * ff) // pair

    def weight_copies(e, slot):
        return (pltpu.make_async_copy(w1_hbm.at[e], w1f.at[slot], sem1.at[slot]),
                pltpu.make_async_copy(w2_hbm.at[e], w2f.at[slot], sem2.at[slot]))

    @pl.when(j == 0)
    def _():
        for cp in weight_copies(be_ref[0], 0):
            cp.start()

    @pl.when(first_ref[j] == 1)
    def _():
        slot = slot_ref[j]
        for cp in weight_copies(be_ref[j], slot):
            cp.wait()

        @pl.when(nxt_ref[j] >= 0)
        def _():
            for cp in weight_copies(nxt_ref[j], 1 - slot):
                cp.start(priority=1)

        r_p = lax.broadcasted_iota(I32, (pair, pair), 0)
        c_p = lax.broadcasted_iota(I32, (pair, pair), 1)
        src = jnp.where(c_p < LANES, 2 * c_p, 2 * (c_p - LANES) + 1)
        perm = jnp.where(r_p == src, 1.0, 0.0).astype(BF16)
        for g in range(ngroup):
            cols = slice(g * pair, (g + 1) * pair)
            w1c[:, cols] = _dot(w1f[slot, :, cols].astype(BF16), perm).astype(BF16)
        w2c[...] = w2f[slot].astype(BF16)

    def expert_rows(nrows):
        x_lo, x_hi = _unpack_halves(x_ref[0:nrows, :])
        xb = jnp.concatenate([x_lo.astype(BF16), x_hi.astype(BF16)], axis=1)
        u = _dot(xb, w1c[...]) + b1_ref[0]
        acts = []
        for g in range(ngroup):
            glu = jnp.minimum(u[:, g * pair:g * pair + LANES], SWIGLU_LIMIT)
            lin = jnp.clip(u[:, g * pair + LANES:(g + 1) * pair], -SWIGLU_LIMIT, SWIGLU_LIMIT)
            acts.append((glu * _sigmoid(SWIGLU_ALPHA * glu) * (lin + 1.0)).astype(BF16))
        act = jnp.concatenate(acts, axis=1)
        y_ref[0:nrows, :] = _pack_halves(_dot(act, w2c[...]) + b2_ref[0])

    active = j < nact_ref[0]
    half = x_ref.shape[0] // 2
    pl.when(active & (nv_ref[j] > half))(functools.partial(expert_rows, x_ref.shape[0]))
    pl.when(active & (nv_ref[j] <= half))(functools.partial(expert_rows, half))


def _ffn(block_expert, first, nxt, slot, nvalid, nact, xb, w1, b1p, w2, b2):
    bm = FFN_BLOCK
    n_rows, dw = xb.shape
    d = 2 * dw
    n_blocks = n_rows // bm
    ff2 = w1.shape[2]
    ff = w2.shape[1]
    row_blk = lambda j, be, fi, nx, sl, nv, na: (jnp.minimum(j, na[0] - 1), 0)
    bias_blk = lambda j, be, fi, nx, sl, nv, na: (be[j], 0, 0)
    grid_spec = pltpu.PrefetchScalarGridSpec(
        num_scalar_prefetch=6,
        grid=(n_blocks,),
        in_specs=[
            pl.BlockSpec((bm, dw), row_blk),
            pl.BlockSpec(memory_space=pl.ANY),
            pl.BlockSpec((1, 1, ff2), bias_blk),
            pl.BlockSpec(memory_space=pl.ANY),
            pl.BlockSpec((1, 1, d), bias_blk),
        ],
        out_specs=pl.BlockSpec((bm, dw), row_blk),
        scratch_shapes=[
            pltpu.VMEM((2, d, ff2), F32), pltpu.VMEM((2, ff, d), F32),
            pltpu.VMEM((d, ff2), BF16), pltpu.VMEM((ff, d), BF16),
            pltpu.SemaphoreType.DMA((2,)), pltpu.SemaphoreType.DMA((2,)),
        ],
    )
    return pl.pallas_call(
        _ffn_body,
        grid_spec=grid_spec,
        out_shape=jax.ShapeDtypeStruct((n_rows, dw), I32),
        compiler_params=pltpu.CompilerParams(
            dimension_semantics=("arbitrary",), vmem_limit_bytes=VMEM_LIMIT),
        name="ffn",
    )(block_expert, first, nxt, slot, nvalid, nact, xb, w1, b1p, w2, b2)


def _combine_body(x1_ref, y0_ref, y1_ref, y2_ref, y3_ref, gcol_ref, g2_ref, o_ref):
    gc = gcol_ref[...]
    m_lo = m_hi = None
    for k, y_ref in enumerate((y0_ref, y1_ref, y2_ref, y3_ref)):
        lo, hi = _unpack_halves(y_ref[...])
        gk = gc[:, k:k + 1]
        m_lo = gk * lo if m_lo is None else m_lo + gk * lo
        m_hi = gk * hi if m_hi is None else m_hi + gk * hi
    m = jnp.concatenate([m_lo, m_hi], axis=1)
    o_ref[...] = (x1_ref[...] + g2_ref[0] * m).astype(o_ref.dtype)


def _combine(x1, yall, gcol, ada3, seq, out_dtype):
    t, d = x1.shape
    tm = min(1024, seq)
    per_b = seq // tm
    nt = t // tm
    yk = lambda k: pl.BlockSpec((tm, d // 2), lambda i, k=k: (k * nt + i, 0))
    return pl.pallas_call(
        _combine_body,
        grid=(nt,),
        in_specs=[
            pl.BlockSpec((tm, d), lambda i: (i, 0)),
            yk(0), yk(1), yk(2), yk(3),
            pl.BlockSpec((tm, LANES), lambda i: (i, 0)),
            pl.BlockSpec((1, 1, d), lambda i: (i // per_b, 0, 5)),
        ],
        out_specs=pl.BlockSpec((tm, d), lambda i: (i, 0)),
        out_shape=jax.ShapeDtypeStruct((t, d), out_dtype),
        compiler_params=pltpu.CompilerParams(
            dimension_semantics=("arbitrary",), vmem_limit_bytes=VMEM_LIMIT),
        name="combine",
    )(x1, yall, yall, yall, yall, gcol, ada3)


def _route_tables(idx, rank, counts, n_tok):
    bm = FFN_BLOCK
    n_asg = TOP_K * n_tok
    n_blocks = -(-(n_asg + N_EXPERTS * (bm - 1)) // bm)
    padded = (counts + bm - 1) // bm * bm
    pad_ends = jnp.cumsum(padded)
    pad_starts = pad_ends - padded
    e_ids = jnp.arange(N_EXPERTS, dtype=I32)
    start_of = jnp.sum(jnp.where(idx[None] == e_ids[:, None, None], pad_starts[:, None, None], 0), axis=0)
    dest = (start_of + rank).reshape(-1)
    nact = (pad_ends[-1] // bm).astype(I32)
    blk_start = jnp.arange(n_blocks, dtype=I32) * bm
    last = jnp.sum(jnp.where(pad_ends <= pad_ends[-1] - 1, 1, 0)).astype(I32)
    be = jnp.sum(jnp.where(pad_ends[None, :] <= blk_start[:, None], 1, 0), axis=1).astype(I32)
    active = blk_start < pad_ends[-1]
    be = jnp.where(active, be, last)
    blk = jnp.arange(n_blocks, dtype=I32)
    first = active & ((blk == 0) | (be != jnp.roll(be, 1)))
    slot = (jnp.cumsum(first.astype(I32)) - 1) & 1
    later_first = first[None, :] & (blk[None, :] > blk[:, None])
    nxt_pos = jnp.min(jnp.where(later_first, blk[None, :], n_blocks), axis=1)
    nxt = jnp.sum(jnp.where(blk[None, :] == nxt_pos[:, None], be[None, :], 0), axis=1)
    nxt = jnp.where(nxt_pos < n_blocks, nxt, -1).astype(I32)
    mine = be[:, None] == e_ids[None, :]
    cnt_b = jnp.sum(jnp.where(mine, counts[None, :], 0), axis=1)
    start_b = jnp.sum(jnp.where(mine, pad_starts[None, :], 0), axis=1)
    nvalid = jnp.where(active, jnp.clip(cnt_b - (blk_start - start_b), 0, bm), 0).astype(I32)
    return be, first.astype(I32), nxt, slot.astype(I32), nvalid, nact.reshape(1), dest, n_blocks * bm


def kernel(x, c, w_ada, b_ada, mix_norm_g, ffn_norm_g, w_in, hg_lower_bound_logits, hg_out_norm_g, da_q_norm_g, da_k_norm_g, da_lambda_q1, da_lambda_k1, da_lambda_q2, da_lambda_k2, da_subln_g, w_out, w_router, b_router, w1, b1, w2, b2):
    bsz, seq, d = x.shape
    t = bsz * seq
    depth = w_ada.shape[0]
    out_dtype = x.dtype
    hw = HG_HEADS * HG_DV
    xcur = x.reshape(t, d)
    for l in range(depth):
        ada = _ada(c, w_ada[l], b_ada[l])
        ada3 = ada.reshape(bsz, 1, N_MOD * d)
        col_h = 0
        col_a = col_h + 4 * HG_HEADS
        col_g = (4 * hw + 3 * DA_HEADS * 2 * DA_DH) // (d // 2)
        proj = _inproj(xcur, mix_norm_g[l].reshape(1, d), ada3, w_in[l].astype(BF16), seq)

        o_a = _hgrn(proj, hg_lower_bound_logits, hg_out_norm_g[l].reshape(1, HG_DV), bsz, seq, col_h, l)
        lambda_init = 0.8 - 0.6 * math.exp(-0.3 * l)
        qg2 = jnp.tile(da_q_norm_g[l], 2).reshape(1, 2 * DA_DH)
        kg2 = jnp.tile(da_k_norm_g[l], 2).reshape(1, 2 * DA_DH)
        lam4 = jnp.stack([da_lambda_q1[l], da_lambda_k1[l], da_lambda_q2[l], da_lambda_k2[l]])
        o_d = _attn(proj, qg2, kg2, lam4, da_subln_g[l].reshape(1, 2 * DA_DH), bsz, seq, col_a, lambda_init)

        x1, h2, idx, rank, gcol, cnt = _mixout(
            xcur, o_a, o_d, proj, col_g, w_out[l], ada3, ffn_norm_g[l].reshape(1, d),
            w_router[l].T, b_router[l].reshape(N_EXPERTS, 1), seq)

        counts = cnt[:, 0].astype(I32)
        be, first, nxt, slot, nvalid, nact, dest, n_rows = _route_tables(idx, rank, counts, t)
        b1p = b1[l].reshape(N_EXPERTS, -1, LANES, 2).transpose(0, 1, 3, 2).reshape(N_EXPERTS, 1, -1)
        xb = _sc_dispatch(h2, dest, n_rows)
        yb = _ffn(be, first, nxt, slot, nvalid, nact, xb, w1[l], b1p, w2[l], b2[l].reshape(N_EXPERTS, 1, d))
        yall = _sc_undispatch(yb, dest)
        xcur = _combine(x1, yall, gcol, ada3, seq, out_dtype)
    return xcur.reshape(bsz, seq, d)
```

```python
import functools
import math

import jax
import jax.numpy as jnp
from jax import lax
from jax.experimental import pallas as pl
from jax.experimental.pallas import tpu as pltpu
from jax.experimental.pallas import tpu_sc as plsc

F32 = jnp.float32
BF16 = jnp.bfloat16
I32 = jnp.int32

HG_HEADS = 4
HG_DK = 128
HG_DV = 128
HG_CHUNK = 32
DA_HEADS = 4
DA_DH = 64
N_EXPERTS = 32
TOP_K = 4
SWIGLU_ALPHA = 1.702
SWIGLU_LIMIT = 7.0
NORM_EPS = 1e-6
LOG2E = math.log2(math.e)
N_MOD = 6

LANES = 128
VMEM_LIMIT = 56 * 1024 * 1024

HG_ROWS = 256
ATTN_TILE = 256
FFN_BLOCK = 512

SC_CORES = 2
SC_WORKERS = SC_CORES * 16
SC_CHUNK = 128


def _sigmoid(x):
    return 1.0 / (1.0 + jnp.exp(-x))


def _dot(a, b):
    return jnp.dot(a, b, preferred_element_type=F32)


def _dot_nt(a, b):
    return lax.dot_general(a, b, (((1,), (1,)), ((), ())), preferred_element_type=F32)


def _split_bf16(x):
    hi = x.astype(BF16)
    lo = (x - hi.astype(F32)).astype(BF16)
    return hi, lo


def _pack_halves(x):
    half = x.shape[1] // 2
    lo = lax.bitcast_convert_type(x[:, :half].astype(BF16).astype(F32), I32)
    hi = lax.bitcast_convert_type(x[:, half:].astype(BF16).astype(F32), I32)
    return lax.shift_right_logical(lo, 16) | hi


def _unpack_halves(w):
    lo = lax.bitcast_convert_type(lax.shift_left(w, 16), F32)
    hi = lax.bitcast_convert_type(w & jnp.int32(-65536), F32)
    return lo, hi


def _ada_body(c_ref, w_ref, b_ref, o_ref):
    c = c_ref[...].astype(F32)
    ch, cl = _split_bf16(c * _sigmoid(c))
    wh, wl = _split_bf16(w_ref[...])
    o_ref[...] = _dot(ch, wh) + _dot(cl, wh) + _dot(ch, wl) + b_ref[...]


def _ada(c, w_ada, b_ada):
    bsz, d = c.shape
    n = w_ada.shape[1]
    tn = d
    return pl.pallas_call(
        _ada_body,
        grid=(n // tn,),
        in_specs=[
            pl.BlockSpec((bsz, d), lambda j: (0, 0)),
            pl.BlockSpec((d, tn), lambda j: (0, j)),
            pl.BlockSpec((1, tn), lambda j: (0, j)),
        ],
        out_specs=pl.BlockSpec((bsz, tn), lambda j: (0, j)),
        out_shape=jax.ShapeDtypeStruct((bsz, n), F32),
        name="ada",
    )(c, w_ada, b_ada.reshape(1, n))


def _norm_mod(x, g, shift, scale):
    ms = jnp.mean(x * x, axis=-1, keepdims=True)
    return (x * lax.rsqrt(ms + NORM_EPS) * g) * (1.0 + scale) + shift


def _inproj_body(x_ref, g_ref, sh_ref, sc_ref, w_ref, o_ref):
    h = _norm_mod(x_ref[...], g_ref[...], sh_ref[0], sc_ref[0])
    o_ref[...] = _dot(h.astype(BF16), w_ref[...]).astype(BF16)


def _inproj(x2, g, ada3, w_bf16, seq):
    t, d = x2.shape
    n = w_bf16.shape[1]
    tm = min(1024, seq)
    nj = 2
    tn = n // nj
    per_b = seq // tm
    return pl.pallas_call(
        _inproj_body,
        grid=(nj, t // tm),
        in_specs=[
            pl.BlockSpec((tm, d), lambda j, i: (i, 0)),
            pl.BlockSpec((1, d), lambda j, i: (0, 0)),
            pl.BlockSpec((1, 1, d), lambda j, i: (i // per_b, 0, 0)),
            pl.BlockSpec((1, 1, d), lambda j, i: (i // per_b, 0, 1)),
            pl.BlockSpec((d, tn), lambda j, i: (0, j)),
        ],
        out_specs=pl.BlockSpec((tm, tn), lambda j, i: (i, j)),
        out_shape=jax.ShapeDtypeStruct((t, n), BF16),
        compiler_params=pltpu.CompilerParams(
            dimension_semantics=("arbitrary", "arbitrary"), vmem_limit_bytes=VMEM_LIMIT),
        name="inproj",
    )(x2, g, ada3, ada3, w_bf16)


def _hgrn_body(q_ref, f_ref, i_ref, og_ref, lbl_ref, g_ref, o_ref, *, seq, layer):
    rows, chunk = HG_ROWS, HG_CHUNK
    nchunk = rows // chunk
    lbl = lbl_ref[...].astype(F32)
    e = jnp.exp(lbl - jnp.max(lbl, axis=0, keepdims=True))
    lb = jnp.sum(e[: layer + 1], axis=0, keepdims=True) / jnp.sum(e, axis=0, keepdims=True)
    r_i = lax.broadcasted_iota(I32, (rows, rows), 0)
    c_i = lax.broadcasted_iota(I32, (rows, rows), 1)
    tri = ((r_i // chunk) == (c_i // chunk)) & (r_i >= c_i)
    tri_b = jnp.where(tri, 1.0, 0.0).astype(BF16)
    row_chunk = lax.broadcasted_iota(I32, (rows, HG_DK), 0) // chunk
    g = g_ref[...].astype(F32)

    def block(r, st):
        sl = pl.ds(r * rows, rows)
        qr = q_ref[sl, :].astype(F32)
        fr = f_ref[sl, :].astype(F32)
        v = i_ref[sl, :].astype(F32)
        og = og_ref[sl, :].astype(F32)
        q = qr * _sigmoid(qr)
        f = lb + (1.0 - lb) * _sigmoid(fr)
        k = 1.0 - f
        logf = jnp.log(f)
        lhi, llo = _split_bf16(logf)
        bc2 = _dot(tri_b, jnp.concatenate([lhi, llo], axis=1))
        bcum = bc2[:, :HG_DK] + bc2[:, HG_DK:]
        b3 = bcum.reshape(nchunk, chunk, HG_DK)
        bl = b3[:, chunk - 1:chunk, :]
        dec = jnp.exp(bl)
        kt_f = k * jnp.exp(-bcum)
        qt_f = q * jnp.exp(bcum)
        kd = (kt_f.reshape(nchunk, chunk, HG_DK) * dec).reshape(rows, HG_DK)
        a = jnp.where(tri, _dot_nt(qt_f.astype(BF16), kt_f.astype(BF16)), 0.0).astype(BF16)
        vt_b = v.T.astype(BF16)
        kd_x = jnp.concatenate([jnp.where(row_chunk == c, kd, 0.0) for c in range(nchunk)], axis=1)
        kv_all = _dot(vt_b, kd_x.astype(BF16))
        starts = []
        for c in range(nchunk):
            starts.append(st.astype(BF16))
            st = st * dec[c] + kv_all[:, c * HG_DK:(c + 1) * HG_DK]
        q_x = [jnp.where(row_chunk == c, qt_f, 0.0).astype(BF16) for c in range(nchunk)]
        o = _dot_nt(jnp.concatenate([a] + q_x, axis=1), jnp.concatenate([vt_b] + starts, axis=1))
        ms = jnp.mean(o * o, axis=-1, keepdims=True)
        o = o * lax.rsqrt(ms + NORM_EPS) * g
        o_ref[sl, :] = (o * (og * _sigmoid(og))).astype(o_ref.dtype)
        return st

    st = jnp.zeros((HG_DV, HG_DK), F32)
    for r in range(seq // rows):
        st = block(r, st)


def _hgrn(proj, lb_logits, norm_g, bsz, seq, col0, layer):
    t = proj.shape[0]
    blk = lambda off: pl.BlockSpec((seq, LANES), lambda b, h, off=off: (b, col0 + off + h))
    return pl.pallas_call(
        functools.partial(_hgrn_body, seq=seq, layer=layer),
        grid=(bsz, HG_HEADS),
        in_specs=[
            blk(0), blk(HG_HEADS), blk(2 * HG_HEADS), blk(3 * HG_HEADS),
            pl.BlockSpec((lb_logits.shape[0], HG_DK), lambda b, h: (0, h)),
            pl.BlockSpec((1, HG_DV), lambda b, h: (0, 0)),
        ],
        out_specs=pl.BlockSpec((seq, HG_DV), lambda b, h: (b, h)),
        out_shape=jax.ShapeDtypeStruct((t, HG_HEADS * HG_DV), BF16),
        compiler_params=pltpu.CompilerParams(
            dimension_semantics=("arbitrary", "arbitrary"), vmem_limit_bytes=VMEM_LIMIT),
        name="hgrn",
    )(proj, proj, proj, proj, lb_logits, norm_g)


def _group_norm(x, gsum_b, gain):
    ss = _dot((x * x).astype(BF16), gsum_b)
    return x * lax.rsqrt(ss * (1.0 / DA_DH) + NORM_EPS) * gain


def _attn_body(q_ref, k_ref, v_ref, qg_ref, kg_ref, lam_ref, sg_ref, o_ref, kn_scr, v1_scr, *, seq, lambda_init):
    tq = ATTN_TILE
    width = 2 * DA_DH
    r_l = lax.broadcasted_iota(I32, (width, width), 0) // DA_DH
    c_l = lax.broadcasted_iota(I32, (width, width), 1) // DA_DH
    gsum_b = jnp.where(r_l == c_l, 1.0, 0.0).astype(BF16)
    lane = lax.broadcasted_iota(I32, (1, width), 1)
    kg = kg_ref[...].astype(F32)
    qg = qg_ref[...].astype(F32) * (DA_DH ** -0.5 * LOG2E)
    sg = sg_ref[...].astype(F32) * (1.0 - lambda_init)
    ones = jnp.ones((tq, width), BF16)
    row = lax.broadcasted_iota(I32, (tq, tq), 0)
    col = lax.broadcasted_iota(I32, (tq, tq), 1)
    keep = row >= col

    lam_v = lam_ref[...].astype(F32)
    lam = (jnp.exp(jnp.sum(lam_v[0:1] * lam_v[1:2], axis=-1, keepdims=True))
           - jnp.exp(jnp.sum(lam_v[2:3] * lam_v[3:4], axis=-1, keepdims=True)) + lambda_init)

    def softmax_v(qc, nk):
        s = _dot_nt(qc, kn_scr[0:nk, :])
        diag = jnp.where(keep, s[:, nk - tq:], -jnp.inf)
        s = diag if nk == tq else jnp.concatenate([s[:, :nk - tq], diag], axis=1)
        m = jnp.max(s, axis=-1, keepdims=True)
        return _dot(jnp.exp2(s - m).astype(BF16), v1_scr[0:nk, :])

    for i in range(seq // tq):
        sl = slice(i * tq, (i + 1) * tq)
        kn_scr[sl, :] = _group_norm(k_ref[sl, :].astype(F32), gsum_b, kg).astype(BF16)
        v1_scr[sl, :] = jnp.concatenate([v_ref[sl, :], ones], axis=1)
        qn = _group_norm(q_ref[sl, :].astype(F32), gsum_b, qg)
        nk = (i + 1) * tq
        a1 = softmax_v(jnp.where(lane < DA_DH, qn, 0.0).astype(BF16), nk)
        a2 = softmax_v(jnp.where(lane >= DA_DH, qn, 0.0).astype(BF16), nk)
        o = a1[:, :width] / a1[:, width:width + 1] - lam * (a2[:, :width] / a2[:, width:width + 1])
        ms = jnp.mean(o * o, axis=-1, keepdims=True)
        o_ref[sl, :] = (o * lax.rsqrt(ms + NORM_EPS) * sg).astype(o_ref.dtype)


def _attn(proj, qg2, kg2, lam4, subln_g, bsz, seq, col0, lambda_init):
    t = proj.shape[0]
    width = 2 * DA_DH
    return pl.pallas_call(
        functools.partial(_attn_body, seq=seq, lambda_init=lambda_init),
        grid=(bsz, DA_HEADS),
        in_specs=[
            pl.BlockSpec((seq, width), lambda b, h: (b, col0 + h)),
            pl.BlockSpec((seq, width), lambda b, h: (b, col0 + DA_HEADS + h)),
            pl.BlockSpec((seq, width), lambda b, h: (b, col0 + 2 * DA_HEADS + h)),
            pl.BlockSpec((1, width), lambda b, h: (0, 0)),
            pl.BlockSpec((1, width), lambda b, h: (0, 0)),
            pl.BlockSpec((4, DA_DH), lambda b, h: (0, 0)),
            pl.BlockSpec((1, width), lambda b, h: (0, 0)),
        ],
        out_specs=pl.BlockSpec((seq, width), lambda b, h: (b, h)),
        out_shape=jax.ShapeDtypeStruct((t, DA_HEADS * width), BF16),
        scratch_shapes=[pltpu.VMEM((seq, width), BF16), pltpu.VMEM((seq, 2 * width), BF16)],
        compiler_params=pltpu.CompilerParams(
            dimension_semantics=("arbitrary", "arbitrary"), vmem_limit_bytes=VMEM_LIMIT),
        name="attn",
    )(proj, proj, proj, qg2, kg2, lam4, subln_g)


def _mixout_body(x_ref, oa_ref, od_ref, ga0_ref, ga1_ref, gd0_ref, gd1_ref, wo_ref, g1_ref, g_ref, sh_ref, sc_ref,
                 wr_ref, br_ref,
                 x1_ref, h2_ref, idx_ref, rank_ref, gcol_ref, cnt_ref, carry_scr, wo_scr):
    i = pl.program_id(0)
    tm = x_ref.shape[0]
    hw = oa_ref.shape[1]

    @pl.when(i == 0)
    def _():
        carry_scr[...] = jnp.zeros_like(carry_scr)
        wo_scr[...] = wo_ref[...].astype(BF16)

    ya = _dot(oa_ref[...], wo_scr[0:hw, :])
    yd = _dot(od_ref[...], wo_scr[hw:, :])
    ga = jnp.concatenate([ga0_ref[...], ga1_ref[...]], axis=1).astype(F32)
    gd = jnp.concatenate([gd0_ref[...], gd1_ref[...]], axis=1).astype(F32)
    y = _sigmoid(ga) * ya + _sigmoid(gd) * yd
    x1 = x_ref[...] + g1_ref[0] * y
    x1_ref[...] = x1
    h2 = _norm_mod(x1, g_ref[...], sh_ref[0], sc_ref[0])
    h2_ref[...] = _pack_halves(h2)

    hh, hl = _split_bf16(h2)
    wh, wl = _split_bf16(wr_ref[...])
    logits = _dot_nt(wh, hh) + _dot_nt(wl, hh) + _dot_nt(wh, hl) + br_ref[...]

    e_iota = lax.broadcasted_iota(I32, (N_EXPERTS, tm), 0).astype(F32)
    vals = logits
    tops, sels, idxs = [], [], []
    for _ in range(TOP_K):
        m = jnp.max(vals, axis=0, keepdims=True)
        idx = jnp.min(jnp.where(vals == m, e_iota, float(N_EXPERTS)), axis=0, keepdims=True)
        sel = e_iota == idx
        vals = jnp.where(sel, -jnp.inf, vals)
        tops.append(m)
        sels.append(sel)
        idxs.append(idx)
    ex = [jnp.exp(tv - tops[0]) for tv in tops]
    den = ex[0] + ex[1] + ex[2] + ex[3]
    gates = [v / den for v in ex]

    hot = jnp.where(sels[0] | sels[1] | sels[2] | sels[3], 1.0, 0.0)
    r_t = lax.broadcasted_iota(I32, (tm, tm), 0)
    c_t = lax.broadcasted_iota(I32, (tm, tm), 1)
    upper = jnp.where(r_t < c_t, 1.0, 0.0).astype(BF16)
    excl = _dot(hot.astype(BF16), upper) + carry_scr[:, 0:1]
    carry_scr[...] = carry_scr[...] + jnp.sum(hot, axis=1, keepdims=True)
    cnt_ref[...] = carry_scr[...]

    ranks = [jnp.sum(jnp.where(s, excl, 0.0), axis=0, keepdims=True) for s in sels]
    idx_ref[...] = jnp.concatenate(idxs, axis=0).astype(I32)
    rank_ref[...] = jnp.concatenate(ranks, axis=0).astype(I32)
    gpad = jnp.concatenate(gates + [jnp.zeros((LANES - TOP_K, tm), F32)], axis=0)
    gcol_ref[...] = gpad.T


def _mixout(x2, oa, od, proj, col_g, wo, ada3, ffn_g, wr_t, br_col, seq):
    t, d = x2.shape
    tm = min(512, seq)
    per_b = seq // tm
    hw = oa.shape[1]
    row = lambda w: pl.BlockSpec((tm, w), lambda i: (i, 0))
    gate = lambda c: pl.BlockSpec((tm, d // 2), lambda i, c=c: (i, col_g + c))
    mod = lambda c: pl.BlockSpec((1, 1, d), lambda i, c=c: (i // per_b, 0, c))
    full = lambda a: pl.BlockSpec(a.shape, lambda i: (0,) * a.ndim)
    return pl.pallas_call(
        _mixout_body,
        grid=(t // tm,),
        in_specs=[
            row(d), row(hw), row(hw),
            gate(0), gate(1), gate(2), gate(3),
            full(wo),
            mod(2),
            full(ffn_g), mod(3), mod(4),
            full(wr_t), full(br_col),
        ],
        out_specs=[
            row(d), row(d // 2),
            pl.BlockSpec((TOP_K, tm), lambda i: (0, i)),
            pl.BlockSpec((TOP_K, tm), lambda i: (0, i)),
            pl.BlockSpec((tm, LANES), lambda i: (i, 0)),
            pl.BlockSpec((N_EXPERTS, LANES), lambda i: (0, 0)),
        ],
        out_shape=[
            jax.ShapeDtypeStruct((t, d), F32),
            jax.ShapeDtypeStruct((t, d // 2), I32),
            jax.ShapeDtypeStruct((TOP_K, t), I32),
            jax.ShapeDtypeStruct((TOP_K, t), I32),
            jax.ShapeDtypeStruct((t, LANES), F32),
            jax.ShapeDtypeStruct((N_EXPERTS, LANES), F32),
        ],
        scratch_shapes=[pltpu.VMEM((N_EXPERTS, LANES), F32), pltpu.VMEM(wo.shape, BF16)],
        compiler_params=pltpu.CompilerParams(
            dimension_semantics=("arbitrary",), vmem_limit_bytes=VMEM_LIMIT),
        name="mixout",
    )(x2, oa, od, proj, proj, proj, proj, wo, ada3, ffn_g, ada3, ada3, wr_t, br_col)


def _sc_mesh():
    return plsc.VectorSubcoreMesh(core_axis_name="c", subcore_axis_name="s")


def _sc_worker_base(rows_per_worker):
    wid = lax.axis_index("s") * SC_CORES + lax.axis_index("c")
    return wid * rows_per_worker


def _sc_dispatch(h2, dest, n_rows):
    t, d = h2.shape
    tpw = t // SC_WORKERS
    assert t % (SC_WORKERS * SC_CHUNK) == 0

    def body(h2_hbm, dest_hbm, xb_hbm, idx_v, rows_v):
        base = _sc_worker_base(tpw)

        @pl.loop(0, tpw // SC_CHUNK)
        def _(i):
            t0 = pl.multiple_of(base + i * SC_CHUNK, SC_CHUNK)
            pltpu.sync_copy(h2_hbm.at[pl.ds(t0, SC_CHUNK)], rows_v)
            for k in range(TOP_K):
                pltpu.sync_copy(dest_hbm.at[pl.ds(k * t + t0, SC_CHUNK)], idx_v)
                pltpu.sync_copy(rows_v, xb_hbm.at[idx_v])

    return pl.kernel(
        body, out_type=jax.ShapeDtypeStruct((n_rows, d), h2.dtype), mesh=_sc_mesh(),
        scratch_types=[pltpu.VMEM((SC_CHUNK,), I32), pltpu.VMEM((SC_CHUNK, d), h2.dtype)],
        name="dispatch",
    )(h2, dest)


def _sc_undispatch(y, dest):
    n_asg = dest.shape[0]
    d = y.shape[1]
    rpw = n_asg // SC_WORKERS
    assert n_asg % (SC_WORKERS * SC_CHUNK) == 0

    def body(y_hbm, dest_hbm, yt_hbm, idx_v, rows_v):
        base = _sc_worker_base(rpw)

        @pl.loop(0, rpw // SC_CHUNK)
        def _(i):
            r0 = pl.multiple_of(base + i * SC_CHUNK, SC_CHUNK)
            pltpu.sync_copy(dest_hbm.at[pl.ds(r0, SC_CHUNK)], idx_v)
            pltpu.sync_copy(y_hbm.at[idx_v], rows_v)
            pltpu.sync_copy(rows_v, yt_hbm.at[pl.ds(r0, SC_CHUNK)])

    return pl.kernel(
        body, out_type=jax.ShapeDtypeStruct((n_asg, d), y.dtype), mesh=_sc_mesh(),
        scratch_types=[pltpu.VMEM((SC_CHUNK,), I32), pltpu.VMEM((SC_CHUNK, d), y.dtype)],
        name="undispatch",
    )(y, dest)


def _ffn_body(be_ref, first_ref, nxt_ref, slot_ref, nv_ref, nact_ref, x_ref, w1_hbm, b1_ref, w2_hbm, b2_ref, y_ref,
              w1f, w2f, w1c, w2c, sem1, sem2):
    j = pl.program_id(0)
    ff = w2f.shape[1]
    pair = 2 * LANES
    ngroup = (2 * ff) // pair

    def weight_copies(e, slot):
        return (pltpu.make_async_copy(w1_hbm.at[e], w1f.at[slot], sem1.at[slot]),
                pltpu.make_async_copy(w2_hbm.at[e], w2f.at[slot], sem2.at[slot]))

    @pl.when(j == 0)
    def _():
        for cp in weight_copies(be_ref[0], 0):
            cp.start()

    @pl.when(first_ref[j] == 1)
    def _():
        slot = slot_ref[j]
        for cp in weight_copies(be_ref[j], slot):
            cp.wait()

        @pl.when(nxt_ref[j] >= 0)
        def _():
            for cp in weight_copies(nxt_ref[j], 1 - slot):
                cp.start(priority=1)

        r_p = lax.broadcasted_iota(I32, (pair, pair), 0)
        c_p = lax.broadcasted_iota(I32, (pair, pair), 1)
        src = jnp.where(c_p < LANES, 2 * c_p, 2 * (c_p - LANES) + 1)
        perm = jnp.where(r_p == src, 1.0, 0.0).astype(BF16)
        for g in range(ngroup):
            cols = slice(g * pair, (g + 1) * pair)
            w1c[:, cols] = _dot(w1f[slot, :, cols].astype(BF16), perm).astype(BF16)
        w2c[...] = w2f[slot].astype(BF16)

    def expert_rows(nrows):
        x_lo, x_hi = _unpack_halves(x_ref[0:nrows, :])
        xb = jnp.concatenate([x_lo.astype(BF16), x_hi.astype(BF16)], axis=1)
        u = _dot(xb, w1c[...]) + b1_ref[0]
        acts = []
        for g in range(ngroup):
            glu = jnp.minimum(u[:, g * pair:g * pair + LANES], SWIGLU_LIMIT)
            lin = jnp.clip(u[:, g * pair + LANES:(g + 1) * pair], -SWIGLU_LIMIT, SWIGLU_LIMIT)
            acts.append((glu * _sigmoid(SWIGLU_ALPHA * glu) * (lin + 1.0)).astype(BF16))
        act = jnp.concatenate(acts, axis=1)
        y_ref[0:nrows, :] = _pack_halves(_dot(act, w2c[...]) + b2_ref[0])

    active = j < nact_ref[0]
    quarter = x_ref.shape[0] // 4
    for q in range(1, 5):
        fits = (nv_ref[j] > (q - 1) * quarter) & (nv_ref[j] <= q * quarter)
        pl.when(active & fits)(functools.partial(expert_rows, q * quarter))


def _ffn(block_expert, first, nxt, slot, nvalid, nact, xb, w1, b1p, w2, b2):
    bm = FFN_BLOCK
    n_rows, dw = xb.shape
    d = 2 * dw
    n_blocks = n_rows // bm
    ff2 = w1.shape[2]
    ff = w2.shape[1]
    row_blk = lambda j, be, fi, nx, sl, nv, na: (jnp.minimum(j, na[0] - 1), 0)
    bias_blk = lambda j, be, fi, nx, sl, nv, na: (be[j], 0, 0)
    grid_spec = pltpu.PrefetchScalarGridSpec(
        num_scalar_prefetch=6,
        grid=(n_blocks,),
        in_specs=[
            pl.BlockSpec((bm, dw), row_blk),
            pl.BlockSpec(memory_space=pl.ANY),
            pl.BlockSpec((1, 1, ff2), bias_blk),
            pl.BlockSpec(memory_space=pl.ANY),
            pl.BlockSpec((1, 1, d), bias_blk),
        ],
        out_specs=pl.BlockSpec((bm, dw), row_blk),
        scratch_shapes=[
            pltpu.VMEM((2, d, ff2), F32), pltpu.VMEM((2, ff, d), F32),
            pltpu.VMEM((d, ff2), BF16), pltpu.VMEM((ff, d), BF16),
            pltpu.SemaphoreType.DMA((2,)), pltpu.SemaphoreType.DMA((2,)),
        ],
    )
    return pl.pallas_call(
        _ffn_body,
        grid_spec=grid_spec,
        out_shape=jax.ShapeDtypeStruct((n_rows, dw), I32),
        compiler_params=pltpu.CompilerParams(
            dimension_semantics=("arbitrary",), vmem_limit_bytes=VMEM_LIMIT),
        name="ffn",
    )(block_expert, first, nxt, slot, nvalid, nact, xb, w1, b1p, w2, b2)


def _combine_body(x1_ref, y0_ref, y1_ref, y2_ref, y3_ref, gcol_ref, g2_ref, o_ref):
    gc = gcol_ref[...]
    m_lo = m_hi = None
    for k, y_ref in enumerate((y0_ref, y1_ref, y2_ref, y3_ref)):
        lo, hi = _unpack_halves(y_ref[...])
        gk = gc[:, k:k + 1]
        m_lo = gk * lo if m_lo is None else m_lo + gk * lo
        m_hi = gk * hi if m_hi is None else m_hi + gk * hi
    m = jnp.concatenate([m_lo, m_hi], axis=1)
    o_ref[...] = (x1_ref[...] + g2_ref[0] * m).astype(o_ref.dtype)


def _combine(x1, yall, gcol, ada3, seq, out_dtype):
    t, d = x1.shape
    tm = min(1024, seq)
    per_b = seq // tm
    nt = t // tm
    yk = lambda k: pl.BlockSpec((tm, d // 2), lambda i, k=k: (k * nt + i, 0))
    return pl.pallas_call(
        _combine_body,
        grid=(nt,),
        in_specs=[
            pl.BlockSpec((tm, d), lambda i: (i, 0)),
            yk(0), yk(1), yk(2), yk(3),
            pl.BlockSpec((tm, LANES), lambda i: (i, 0)),
            pl.BlockSpec((1, 1, d), lambda i: (i // per_b, 0, 5)),
        ],
        out_specs=pl.BlockSpec((tm, d), lambda i: (i, 0)),
        out_shape=jax.ShapeDtypeStruct((t, d), out_dtype),
        compiler_params=pltpu.CompilerParams(
            dimension_semantics=("arbitrary",), vmem_limit_bytes=VMEM_LIMIT),
        name="combine",
    )(x1, yall, yall, yall, yall, gcol, ada3)


def _route_tables(idx, rank, counts, n_tok):
    bm = FFN_BLOCK
    n_asg = TOP_K * n_tok
    n_blocks = -(-(n_asg + N_EXPERTS * (bm - 1)) // bm)
    padded = (counts + bm - 1) // bm * bm
    pad_ends = jnp.cumsum(padded)
    pad_starts = pad_ends - padded
    e_ids = jnp.arange(N_EXPERTS, dtype=I32)
    start_of = jnp.sum(jnp.where(idx[None] == e_ids[:, None, None], pad_starts[:, None, None], 0), axis=0)
    dest = (start_of + rank).reshape(-1)
    nact = (pad_ends[-1] // bm).astype(I32)
    blk_start = jnp.arange(n_blocks, dtype=I32) * bm
    last = jnp.sum(jnp.where(pad_ends <= pad_ends[-1] - 1, 1, 0)).astype(I32)
    be = jnp.sum(jnp.where(pad_ends[None, :] <= blk_start[:, None], 1, 0), axis=1).astype(I32)
    active = blk_start < pad_ends[-1]
    be = jnp.where(active, be, last)
    blk = jnp.arange(n_blocks, dtype=I32)
    first = active & ((blk == 0) | (be != jnp.roll(be, 1)))
    slot = (jnp.cumsum(first.astype(I32)) - 1) & 1
    later_first = first[None, :] & (blk[None, :] > blk[:, None])
    nxt_pos = jnp.min(jnp.where(later_first, blk[None, :], n_blocks), axis=1)
    nxt = jnp.sum(jnp.where(blk[None, :] == nxt_pos[:, None], be[None, :], 0), axis=1)
    nxt = jnp.where(nxt_pos < n_blocks, nxt, -1).astype(I32)
    mine = be[:, None] == e_ids[None, :]
    cnt_b = jnp.sum(jnp.where(mine, counts[None, :], 0), axis=1)
    start_b = jnp.sum(jnp.where(mine, pad_starts[None, :], 0), axis=1)
    nvalid = jnp.where(active, jnp.clip(cnt_b - (blk_start - start_b), 0, bm), 0).astype(I32)
    return be, first.astype(I32), nxt, slot.astype(I32), nvalid, nact.reshape(1), dest, n_blocks * bm


def kernel(x, c, w_ada, b_ada, mix_norm_g, ffn_norm_g, w_in, hg_lower_bound_logits, hg_out_norm_g, da_q_norm_g, da_k_norm_g, da_lambda_q1, da_lambda_k1, da_lambda_q2, da_lambda_k2, da_subln_g, w_out, w_router, b_router, w1, b1, w2, b2):
    bsz, seq, d = x.shape
    t = bsz * seq
    depth = w_ada.shape[0]
    out_dtype = x.dtype
    hw = HG_HEADS * HG_DV
    xcur = x.reshape(t, d)
    for l in range(depth):
        ada = _ada(c, w_ada[l], b_ada[l])
        ada3 = ada.reshape(bsz, 1, N_MOD * d)
        col_h = 0
        col_a = col_h + 4 * HG_HEADS
        col_g = (4 * hw + 3 * DA_HEADS * 2 * DA_DH) // (d // 2)
        proj = _inproj(xcur, mix_norm_g[l].reshape(1, d), ada3, w_in[l].astype(BF16), seq)

        o_a = _hgrn(proj, hg_lower_bound_logits, hg_out_norm_g[l].reshape(1, HG_DV), bsz, seq, col_h, l)
        lambda_init = 0.8 - 0.6 * math.exp(-0.3 * l)
        qg2 = jnp.tile(da_q_norm_g[l], 2).reshape(1, 2 * DA_DH)
        kg2 = jnp.tile(da_k_norm_g[l], 2).reshape(1, 2 * DA_DH)
        lam4 = jnp.stack([da_lambda_q1[l], da_lambda_k1[l], da_lambda_q2[l], da_lambda_k2[l]])
        o_d = _attn(proj, qg2, kg2, lam4, da_subln_g[l].reshape(1, 2 * DA_DH), bsz, seq, col_a, lambda_init)

        x1, h2, idx, rank, gcol, cnt = _mixout(
            xcur, o_a, o_d, proj, col_g, w_out[l], ada3, ffn_norm_g[l].reshape(1, d),
            w_router[l].T, b_router[l].reshape(N_EXPERTS, 1), seq)

        counts = cnt[:, 0].astype(I32)
        be, first, nxt, slot, nvalid, nact, dest, n_rows = _route_tables(idx, rank, counts, t)
        b1p = b1[l].reshape(N_EXPERTS, -1, LANES, 2).transpose(0, 1, 3, 2).reshape(N_EXPERTS, 1, -1)
        xb = _sc_dispatch(h2, dest, n_rows)
        yb = _ffn(be, first, nxt, slot, nvalid, nact, xb, w1[l], b1p, w2[l], b2[l].reshape(N_EXPERTS, 1, d))
        yall = _sc_undispatch(yb, dest)
        xcur = _combine(x1, yall, gcol, ada3, seq, out_dtype)
    return xcur.reshape(bsz, seq, d)
```

```python
import functools
import math

import jax
import jax.numpy as jnp
from jax import lax
from jax.experimental import pallas as pl
from jax.experimental.pallas import tpu as pltpu
from jax.experimental.pallas import tpu_sc as plsc

F32 = jnp.float32
BF16 = jnp.bfloat16
I32 = jnp.int32

HG_HEADS = 4
HG_DK = 128
HG_DV = 128
HG_CHUNK = 32
DA_HEADS = 4
DA_DH = 64
N_EXPERTS = 32
TOP_K = 4
SWIGLU_ALPHA = 1.702
SWIGLU_LIMIT = 7.0
NORM_EPS = 1e-6
LOG2E = math.log2(math.e)
N_MOD = 6

LANES = 128
VMEM_LIMIT = 56 * 1024 * 1024

HG_ROWS = 256
ATTN_TILE = 256
FFN_BLOCK = 512

SC_CORES = 2
SC_WORKERS = SC_CORES * 16
SC_CHUNK = 128


def _sigmoid(x):
    return 1.0 / (1.0 + jnp.exp(-x))


def _dot(a, b):
    return jnp.dot(a, b, preferred_element_type=F32)


def _dot_nt(a, b):
    return lax.dot_general(a, b, (((1,), (1,)), ((), ())), preferred_element_type=F32)


def _split_bf16(x):
    hi = x.astype(BF16)
    lo = (x - hi.astype(F32)).astype(BF16)
    return hi, lo


def _pack_halves(x):
    half = x.shape[1] // 2
    lo = lax.bitcast_convert_type(x[:, :half].astype(BF16).astype(F32), I32)
    hi = lax.bitcast_convert_type(x[:, half:].astype(BF16).astype(F32), I32)
    return lax.shift_right_logical(lo, 16) | hi


def _unpack_halves(w):
    lo = lax.bitcast_convert_type(lax.shift_left(w, 16), F32)
    hi = lax.bitcast_convert_type(w & jnp.int32(-65536), F32)
    return lo, hi


def _ada_body(c_ref, w_ref, b_ref, o_ref):
    c = c_ref[...].astype(F32)
    ch, cl = _split_bf16(c * _sigmoid(c))
    wh, wl = _split_bf16(w_ref[...])
    o_ref[...] = _dot(ch, wh) + _dot(cl, wh) + _dot(ch, wl) + b_ref[...]


def _ada(c, w_ada, b_ada):
    bsz, d = c.shape
    n = w_ada.shape[1]
    tn = d
    return pl.pallas_call(
        _ada_body,
        grid=(n // tn,),
        in_specs=[
            pl.BlockSpec((bsz, d), lambda j: (0, 0)),
            pl.BlockSpec((d, tn), lambda j: (0, j)),
            pl.BlockSpec((1, tn), lambda j: (0, j)),
        ],
        out_specs=pl.BlockSpec((bsz, tn), lambda j: (0, j)),
        out_shape=jax.ShapeDtypeStruct((bsz, n), F32),
        name="ada",
    )(c, w_ada, b_ada.reshape(1, n))


def _norm_mod(x, g, shift, scale):
    ms = jnp.mean(x * x, axis=-1, keepdims=True)
    return (x * lax.rsqrt(ms + NORM_EPS) * g) * (1.0 + scale) + shift


def _inproj_body(x_ref, g_ref, sh_ref, sc_ref, w_ref, o_ref):
    h = _norm_mod(x_ref[...], g_ref[...], sh_ref[0], sc_ref[0])
    o_ref[...] = _dot(h.astype(BF16), w_ref[...]).astype(BF16)


def _inproj(x2, g, ada3, w_bf16, seq):
    t, d = x2.shape
    n = w_bf16.shape[1]
    tm = min(512, seq)
    nj = 1
    tn = n // nj
    per_b = seq // tm
    return pl.pallas_call(
        _inproj_body,
        grid=(nj, t // tm),
        in_specs=[
            pl.BlockSpec((tm, d), lambda j, i: (i, 0)),
            pl.BlockSpec((1, d), lambda j, i: (0, 0)),
            pl.BlockSpec((1, 1, d), lambda j, i: (i // per_b, 0, 0)),
            pl.BlockSpec((1, 1, d), lambda j, i: (i // per_b, 0, 1)),
            pl.BlockSpec((d, tn), lambda j, i: (0, j)),
        ],
        out_specs=pl.BlockSpec((tm, tn), lambda j, i: (i, j)),
        out_shape=jax.ShapeDtypeStruct((t, n), BF16),
        compiler_params=pltpu.CompilerParams(
            dimension_semantics=("arbitrary", "arbitrary"), vmem_limit_bytes=VMEM_LIMIT),
        name="inproj",
    )(x2, g, ada3, ada3, w_bf16)


def _hgrn_body(q_ref, f_ref, i_ref, og_ref, lbl_ref, g_ref, o_ref, *, seq, layer):
    rows, chunk = HG_ROWS, HG_CHUNK
    nchunk = rows // chunk
    lbl = lbl_ref[...].astype(F32)
    e = jnp.exp(lbl - jnp.max(lbl, axis=0, keepdims=True))
    lb = jnp.sum(e[: layer + 1], axis=0, keepdims=True) / jnp.sum(e, axis=0, keepdims=True)
    r_i = lax.broadcasted_iota(I32, (rows, rows), 0)
    c_i = lax.broadcasted_iota(I32, (rows, rows), 1)
    tri = ((r_i // chunk) == (c_i // chunk)) & (r_i >= c_i)
    tri_b = jnp.where(tri, 1.0, 0.0).astype(BF16)
    row_chunk = lax.broadcasted_iota(I32, (rows, HG_DK), 0) // chunk
    g = g_ref[...].astype(F32)

    def block(r, st):
        sl = pl.ds(r * rows, rows)
        qr = q_ref[sl, :].astype(F32)
        fr = f_ref[sl, :].astype(F32)
        v = i_ref[sl, :].astype(F32)
        og = og_ref[sl, :].astype(F32)
        q = qr * _sigmoid(qr)
        f = lb + (1.0 - lb) * _sigmoid(fr)
        k = 1.0 - f
        logf = jnp.log(f)
        lhi, llo = _split_bf16(logf)
        bc2 = _dot(tri_b, jnp.concatenate([lhi, llo], axis=1))
        bcum = bc2[:, :HG_DK] + bc2[:, HG_DK:]
        b3 = bcum.reshape(nchunk, chunk, HG_DK)
        bl = b3[:, chunk - 1:chunk, :]
        dec = jnp.exp(bl)
        kt_f = k * jnp.exp(-bcum)
        qt_f = q * jnp.exp(bcum)
        kd = (kt_f.reshape(nchunk, chunk, HG_DK) * dec).reshape(rows, HG_DK)
        a = jnp.where(tri, _dot_nt(qt_f.astype(BF16), kt_f.astype(BF16)), 0.0).astype(BF16)
        vt_b = v.T.astype(BF16)
        kd_x = jnp.concatenate([jnp.where(row_chunk == c, kd, 0.0) for c in range(nchunk)], axis=1)
        kv_all = _dot(vt_b, kd_x.astype(BF16))
        starts = []
        for c in range(nchunk):
            starts.append(st.astype(BF16))
            st = st * dec[c] + kv_all[:, c * HG_DK:(c + 1) * HG_DK]
        q_x = [jnp.where(row_chunk == c, qt_f, 0.0).astype(BF16) for c in range(nchunk)]
        o = _dot_nt(jnp.concatenate([a] + q_x, axis=1), jnp.concatenate([vt_b] + starts, axis=1))
        ms = jnp.mean(o * o, axis=-1, keepdims=True)
        o = o * lax.rsqrt(ms + NORM_EPS) * g
        o_ref[sl, :] = (o * (og * _sigmoid(og))).astype(o_ref.dtype)
        return st

    st = jnp.zeros((HG_DV, HG_DK), F32)
    for r in range(seq // rows):
        st = block(r, st)


def _hgrn(proj, lb_logits, norm_g, bsz, seq, col0, layer):
    t = proj.shape[0]
    blk = lambda off: pl.BlockSpec((seq, LANES), lambda b, h, off=off: (b, col0 + off + h))
    return pl.pallas_call(
        functools.partial(_hgrn_body, seq=seq, layer=layer),
        grid=(bsz, HG_HEADS),
        in_specs=[
            blk(0), blk(HG_HEADS), blk(2 * HG_HEADS), blk(3 * HG_HEADS),
            pl.BlockSpec((lb_logits.shape[0], HG_DK), lambda b, h: (0, h)),
            pl.BlockSpec((1, HG_DV), lambda b, h: (0, 0)),
        ],
        out_specs=pl.BlockSpec((seq, HG_DV), lambda b, h: (b, h)),
        out_shape=jax.ShapeDtypeStruct((t, HG_HEADS * HG_DV), BF16),
        compiler_params=pltpu.CompilerParams(
            dimension_semantics=("arbitrary", "arbitrary"), vmem_limit_bytes=VMEM_LIMIT),
        name="hgrn",
    )(proj, proj, proj, proj, lb_logits, norm_g)


def _group_norm(x, gsum_b, gain):
    ss = _dot((x * x).astype(BF16), gsum_b)
    return x * lax.rsqrt(ss * (1.0 / DA_DH) + NORM_EPS) * gain


def _attn_body(q_ref, k_ref, v_ref, qg_ref, kg_ref, lam_ref, sg_ref, o_ref, kn_scr, v1_scr, *, seq, lambda_init):
    tq = ATTN_TILE
    width = 2 * DA_DH
    r_l = lax.broadcasted_iota(I32, (width, width), 0) // DA_DH
    c_l = lax.broadcasted_iota(I32, (width, width), 1) // DA_DH
    gsum_b = jnp.where(r_l == c_l, 1.0, 0.0).astype(BF16)
    lane = lax.broadcasted_iota(I32, (1, width), 1)
    kg = kg_ref[...].astype(F32)
    qg = qg_ref[...].astype(F32) * (DA_DH ** -0.5 * LOG2E)
    sg = sg_ref[...].astype(F32) * (1.0 - lambda_init)
    ones = jnp.ones((tq, width), BF16)
    row = lax.broadcasted_iota(I32, (tq, tq), 0)
    col = lax.broadcasted_iota(I32, (tq, tq), 1)
    keep = row >= col

    lam_v = lam_ref[...].astype(F32)
    lam = (jnp.exp(jnp.sum(lam_v[0:1] * lam_v[1:2], axis=-1, keepdims=True))
           - jnp.exp(jnp.sum(lam_v[2:3] * lam_v[3:4], axis=-1, keepdims=True)) + lambda_init)

    def softmax_v(qc, nk):
        s = _dot_nt(qc, kn_scr[0:nk, :])
        diag = jnp.where(keep, s[:, nk - tq:], -jnp.inf)
        s = diag if nk == tq else jnp.concatenate([s[:, :nk - tq], diag], axis=1)
        m = jnp.max(s, axis=-1, keepdims=True)
        return _dot(jnp.exp2(s - m).astype(BF16), v1_scr[0:nk, :])

    for i in range(seq // tq):
        sl = slice(i * tq, (i + 1) * tq)
        kn_scr[sl, :] = _group_norm(k_ref[sl, :].astype(F32), gsum_b, kg).astype(BF16)
        v1_scr[sl, :] = jnp.concatenate([v_ref[sl, :], ones], axis=1)
        qn = _group_norm(q_ref[sl, :].astype(F32), gsum_b, qg)
        nk = (i + 1) * tq
        a1 = softmax_v(jnp.where(lane < DA_DH, qn, 0.0).astype(BF16), nk)
        a2 = softmax_v(jnp.where(lane >= DA_DH, qn, 0.0).astype(BF16), nk)
        o = a1[:, :width] / a1[:, width:width + 1] - lam * (a2[:, :width] / a2[:, width:width + 1])
        ms = jnp.mean(o * o, axis=-1, keepdims=True)
        o_ref[sl, :] = (o * lax.rsqrt(ms + NORM_EPS) * sg).astype(o_ref.dtype)


def _attn(proj, qg2, kg2, lam4, subln_g, bsz, seq, col0, lambda_init):
    t = proj.shape[0]
    width = 2 * DA_DH
    return pl.pallas_call(
        functools.partial(_attn_body, seq=seq, lambda_init=lambda_init),
        grid=(bsz, DA_HEADS),
        in_specs=[
            pl.BlockSpec((seq, width), lambda b, h: (b, col0 + h)),
            pl.BlockSpec((seq, width), lambda b, h: (b, col0 + DA_HEADS + h)),
            pl.BlockSpec((seq, width), lambda b, h: (b, col0 + 2 * DA_HEADS + h)),
            pl.BlockSpec((1, width), lambda b, h: (0, 0)),
            pl.BlockSpec((1, width), lambda b, h: (0, 0)),
            pl.BlockSpec((4, DA_DH), lambda b, h: (0, 0)),
            pl.BlockSpec((1, width), lambda b, h: (0, 0)),
        ],
        out_specs=pl.BlockSpec((seq, width), lambda b, h: (b, h)),
        out_shape=jax.ShapeDtypeStruct((t, DA_HEADS * width), BF16),
        scratch_shapes=[pltpu.VMEM((seq, width), BF16), pltpu.VMEM((seq, 2 * width), BF16)],
        compiler_params=pltpu.CompilerParams(
            dimension_semantics=("arbitrary", "arbitrary"), vmem_limit_bytes=VMEM_LIMIT),
        name="attn",
    )(proj, proj, proj, qg2, kg2, lam4, subln_g)


def _mixout_body(x_ref, oa_ref, od_ref, ga0_ref, ga1_ref, gd0_ref, gd1_ref, wo_ref, g1_ref, g_ref, sh_ref, sc_ref,
                 wr_ref, br_ref,
                 x1_ref, h2_ref, idx_ref, rank_ref, gcol_ref, cnt_ref, carry_scr, wo_scr):
    i = pl.program_id(0)
    tm = x_ref.shape[0]
    hw = oa_ref.shape[1]

    @pl.when(i == 0)
    def _():
        carry_scr[...] = jnp.zeros_like(carry_scr)
        wo_scr[...] = wo_ref[...].astype(BF16)

    ya = _dot(oa_ref[...], wo_scr[0:hw, :])
    yd = _dot(od_ref[...], wo_scr[hw:, :])
    ga = jnp.concatenate([ga0_ref[...], ga1_ref[...]], axis=1).astype(F32)
    gd = jnp.concatenate([gd0_ref[...], gd1_ref[...]], axis=1).astype(F32)
    y = _sigmoid(ga) * ya + _sigmoid(gd) * yd
    x1 = x_ref[...] + g1_ref[0] * y
    x1_ref[...] = x1
    h2 = _norm_mod(x1, g_ref[...], sh_ref[0], sc_ref[0])
    h2_ref[...] = _pack_halves(h2)

    hh, hl = _split_bf16(h2)
    wh, wl = _split_bf16(wr_ref[...])
    logits = _dot_nt(wh, hh) + _dot_nt(wl, hh) + _dot_nt(wh, hl) + br_ref[...]

    e_iota = lax.broadcasted_iota(I32, (N_EXPERTS, tm), 0).astype(F32)
    vals = logits
    tops, sels, idxs = [], [], []
    for _ in range(TOP_K):
        m = jnp.max(vals, axis=0, keepdims=True)
        idx = jnp.min(jnp.where(vals == m, e_iota, float(N_EXPERTS)), axis=0, keepdims=True)
        sel = e_iota == idx
        vals = jnp.where(sel, -jnp.inf, vals)
        tops.append(m)
        sels.append(sel)
        idxs.append(idx)
    ex = [jnp.exp(tv - tops[0]) for tv in tops]
    den = ex[0] + ex[1] + ex[2] + ex[3]
    gates = [v / den for v in ex]

    hot = jnp.where(sels[0] | sels[1] | sels[2] | sels[3], 1.0, 0.0)
    r_t = lax.broadcasted_iota(I32, (tm, tm), 0)
    c_t = lax.broadcasted_iota(I32, (tm, tm), 1)
    upper = jnp.where(r_t < c_t, 1.0, 0.0).astype(BF16)
    excl = _dot(hot.astype(BF16), upper) + carry_scr[:, 0:1]
    carry_scr[...] = carry_scr[...] + jnp.sum(hot, axis=1, keepdims=True)
    cnt_ref[...] = carry_scr[...]

    ranks = [jnp.sum(jnp.where(s, excl, 0.0), axis=0, keepdims=True) for s in sels]
    idx_ref[...] = jnp.concatenate(idxs, axis=0).astype(I32)
    rank_ref[...] = jnp.concatenate(ranks, axis=0).astype(I32)
    gpad = jnp.concatenate(gates + [jnp.zeros((LANES - TOP_K, tm), F32)], axis=0)
    gcol_ref[...] = gpad.T


def _mixout(x2, oa, od, proj, col_g, wo, ada3, ffn_g, wr_t, br_col, seq):
    t, d = x2.shape
    tm = min(1024, seq)
    per_b = seq // tm
    hw = oa.shape[1]
    row = lambda w: pl.BlockSpec((tm, w), lambda i: (i, 0))
    gate = lambda c: pl.BlockSpec((tm, d // 2), lambda i, c=c: (i, col_g + c))
    mod = lambda c: pl.BlockSpec((1, 1, d), lambda i, c=c: (i // per_b, 0, c))
    full = lambda a: pl.BlockSpec(a.shape, lambda i: (0,) * a.ndim)
    return pl.pallas_call(
        _mixout_body,
        grid=(t // tm,),
        in_specs=[
            row(d), row(hw), row(hw),
            gate(0), gate(1), gate(2), gate(3),
            full(wo),
            mod(2),
            full(ffn_g), mod(3), mod(4),
            full(wr_t), full(br_col),
        ],
        out_specs=[
            row(d), row(d // 2),
            pl.BlockSpec((TOP_K, tm), lambda i: (0, i)),
            pl.BlockSpec((TOP_K, tm), lambda i: (0, i)),
            pl.BlockSpec((tm, LANES), lambda i: (i, 0)),
            pl.BlockSpec((N_EXPERTS, LANES), lambda i: (0, 0)),
        ],
        out_shape=[
            jax.ShapeDtypeStruct((t, d), F32),
            jax.ShapeDtypeStruct((t, d // 2), I32),
            jax.ShapeDtypeStruct((TOP_K, t), I32),
            jax.ShapeDtypeStruct((TOP_K, t), I32),
            jax.ShapeDtypeStruct((t, LANES), F32),
            jax.ShapeDtypeStruct((N_EXPERTS, LANES), F32),
        ],
        scratch_shapes=[pltpu.VMEM((N_EXPERTS, LANES), F32), pltpu.VMEM(wo.shape, BF16)],
        compiler_params=pltpu.CompilerParams(
            dimension_semantics=("arbitrary",), vmem_limit_bytes=VMEM_LIMIT),
        name="mixout",
    )(x2, oa, od, proj, proj, proj, proj, wo, ada3, ffn_g, ada3, ada3, wr_t, br_col)


def _sc_mesh():
    return plsc.VectorSubcoreMesh(core_axis_name="c", subcore_axis_name="s")


def _sc_worker_base(rows_per_worker):
    wid = lax.axis_index("s") * SC_CORES + lax.axis_index("c")
    return wid * rows_per_worker


def _sc_dispatch(h2, dest, n_rows):
    t, d = h2.shape
    tpw = t // SC_WORKERS
    assert t % (SC_WORKERS * SC_CHUNK) == 0

    def body(h2_hbm, dest_hbm, xb_hbm, idx_v, rows_v):
        base = _sc_worker_base(tpw)

        @pl.loop(0, tpw // SC_CHUNK)
        def _(i):
            t0 = pl.multiple_of(base + i * SC_CHUNK, SC_CHUNK)
            pltpu.sync_copy(h2_hbm.at[pl.ds(t0, SC_CHUNK)], rows_v)
            for k in range(TOP_K):
                pltpu.sync_copy(dest_hbm.at[pl.ds(k * t + t0, SC_CHUNK)], idx_v)
                pltpu.sync_copy(rows_v, xb_hbm.at[idx_v])

    return pl.kernel(
        body, out_type=jax.ShapeDtypeStruct((n_rows, d), h2.dtype), mesh=_sc_mesh(),
        scratch_types=[pltpu.VMEM((SC_CHUNK,), I32), pltpu.VMEM((SC_CHUNK, d), h2.dtype)],
        name="dispatch",
    )(h2, dest)


def _sc_undispatch(y, dest):
    n_asg = dest.shape[0]
    d = y.shape[1]
    rpw = n_asg // SC_WORKERS
    assert n_asg % (SC_WORKERS * SC_CHUNK) == 0

    def body(y_hbm, dest_hbm, yt_hbm, idx_v, rows_v):
        base = _sc_worker_base(rpw)

        @pl.loop(0, rpw // SC_CHUNK)
        def _(i):
            r0 = pl.multiple_of(base + i * SC_CHUNK, SC_CHUNK)
            pltpu.sync_copy(dest_hbm.at[pl.ds(r0, SC_CHUNK)], idx_v)
            pltpu.sync_copy(y_hbm.at[idx_v], rows_v)
            pltpu.sync_copy(rows_v, yt_hbm.at[pl.ds(r0, SC_CHUNK)])

    return pl.kernel(
        body, out_type=jax.ShapeDtypeStruct((n_asg, d), y.dtype), mesh=_sc_mesh(),
        scratch_types=[pltpu.VMEM((SC_CHUNK,), I32), pltpu.VMEM((SC_CHUNK, d), y.dtype)],
        name="undispatch",
    )(y, dest)


def _ffn_body(be_ref, first_ref, nxt_ref, slot_ref, nv_ref, nact_ref, x_ref, w1_hbm, b1_ref, w2_hbm, b2_ref, y_ref,
              w1f, w2f, w1c, w2c, sem1, sem2):
    j = pl.program_id(0)
    ff = w2f.shape[1]
    pair = 2 * LANES
    ngroup = (2 * ff) // pair

    def weight_copies(e, slot):
        return (pltpu.make_async_copy(w1_hbm.at[e], w1f.at[slot], sem1.at[slot]),
                pltpu.make_async_copy(w2_hbm.at[e], w2f.at[slot], sem2.at[slot]))

    @pl.when(j == 0)
    def _():
        for cp in weight_copies(be_ref[0], 0):
            cp.start()

    @pl.when(first_ref[j] == 1)
    def _():
        slot = slot_ref[j]
        for cp in weight_copies(be_ref[j], slot):
            cp.wait()

        @pl.when(nxt_ref[j] >= 0)
        def _():
            for cp in weight_copies(nxt_ref[j], 1 - slot):
                cp.start(priority=1)

        r_p = lax.broadcasted_iota(I32, (pair, pair), 0)
        c_p = lax.broadcasted_iota(I32, (pair, pair), 1)
        src = jnp.where(c_p < LANES, 2 * c_p, 2 * (c_p - LANES) + 1)
        perm = jnp.where(r_p == src, 1.0, 0.0).astype(BF16)
        for g in range(ngroup):
            cols = slice(g * pair, (g + 1) * pair)
            w1c[:, cols] = _dot(w1f[slot, :, cols].astype(BF16), perm).astype(BF16)
        w2c[...] = w2f[slot].astype(BF16)

    def expert_rows(nrows):
        x_lo, x_hi = _unpack_halves(x_ref[0:nrows, :])
        xb = jnp.concatenate([x_lo.astype(BF16), x_hi.astype(BF16)], axis=1)
        u = _dot(xb, w1c[...]) + b1_ref[0]
        acts = []
        for g in range(ngroup):
            glu = jnp.minimum(u[:, g * pair:g * pair + LANES], SWIGLU_LIMIT)
            lin = jnp.clip(u[:, g * pair + LANES:(g + 1) * pair], -SWIGLU_LIMIT, SWIGLU_LIMIT)
            acts.append((glu * _sigmoid(SWIGLU_ALPHA * glu) * (lin + 1.0)).astype(BF16))
        act = jnp.concatenate(acts, axis=1)
        y_ref[0:nrows, :] = _pack_halves(_dot(act, w2c[...]) + b2_ref[0])

    active = j < nact_ref[0]
    half = x_ref.shape[0] // 2
    pl.when(active & (nv_ref[j] > half))(functools.partial(expert_rows, x_ref.shape[0]))
    pl.when(active & (nv_ref[j] <= half))(functools.partial(expert_rows, half))


def _ffn(block_expert, first, nxt, slot, nvalid, nact, xb, w1, b1p, w2, b2):
    bm = FFN_BLOCK
    n_rows, dw = xb.shape
    d = 2 * dw
    n_blocks = n_rows // bm
    ff2 = w1.shape[2]
    ff = w2.shape[1]
    row_blk = lambda j, be, fi, nx, sl, nv, na: (jnp.minimum(j, na[0] - 1), 0)
    bias_blk = lambda j, be, fi, nx, sl, nv, na: (be[j], 0, 0)
    grid_spec = pltpu.PrefetchScalarGridSpec(
        num_scalar_prefetch=6,
        grid=(n_blocks,),
        in_specs=[
            pl.BlockSpec((bm, dw), row_blk),
            pl.BlockSpec(memory_space=pl.ANY),
            pl.BlockSpec((1, 1, ff2), bias_blk),
            pl.BlockSpec(memory_space=pl.ANY),
            pl.BlockSpec((1, 1, d), bias_blk),
        ],
        out_specs=pl.BlockSpec((bm, dw), row_blk),
        scratch_shapes=[
            pltpu.VMEM((2, d, ff2), F32), pltpu.VMEM((2, ff, d), F32),
            pltpu.VMEM((d, ff2), BF16), pltpu.VMEM((ff, d), BF16),
            pltpu.SemaphoreType.DMA((2,)), pltpu.SemaphoreType.DMA((2,)),
        ],
    )
    return pl.pallas_call(
        _ffn_body,
        grid_spec=grid_spec,
        out_shape=jax.ShapeDtypeStruct((n_rows, dw), I32),
        compiler_params=pltpu.CompilerParams(
            dimension_semantics=("arbitrary",), vmem_limit_bytes=VMEM_LIMIT),
        name="ffn",
    )(block_expert, first, nxt, slot, nvalid, nact, xb, w1, b1p, w2, b2)


def _combine_body(x1_ref, y0_ref, y1_ref, y2_ref, y3_ref, gcol_ref, g2_ref, o_ref):
    gc = gcol_ref[...]
    m_lo = m_hi = None
    for k, y_ref in enumerate((y0_ref, y1_ref, y2_ref, y3_ref)):
        lo, hi = _unpack_halves(y_ref[...])
        gk = gc[:, k:k + 1]
        m_lo = gk * lo if m_lo is None else m_lo + gk * lo
        m_hi = gk * hi if m_hi is None else m_hi + gk * hi
    m = jnp.concatenate([m_lo, m_hi], axis=1)
    o_ref[...] = (x1_ref[...] + g2_ref[0] * m).astype(o_ref.dtype)


def _combine(x1, yall, gcol, ada3, seq, out_dtype):
    t, d = x1.shape
    tm = min(1024, seq)
    per_b = seq // tm
    nt = t // tm
    yk = lambda k: pl.BlockSpec((tm, d // 2), lambda i, k=k: (k * nt + i, 0))
    return pl.pallas_call(
        _combine_body,
        grid=(nt,),
        in_specs=[
            pl.BlockSpec((tm, d), lambda i: (i, 0)),
            yk(0), yk(1), yk(2), yk(3),
            pl.BlockSpec((tm, LANES), lambda i: (i, 0)),
            pl.BlockSpec((1, 1, d), lambda i: (i // per_b, 0, 5)),
        ],
        out_specs=pl.BlockSpec((tm, d), lambda i: (i, 0)),
        out_shape=jax.ShapeDtypeStruct((t, d), out_dtype),
        compiler_params=pltpu.CompilerParams(
            dimension_semantics=("arbitrary",), vmem_limit_bytes=VMEM_LIMIT),
        name="combine",
    )(x1, yall, yall, yall, yall, gcol, ada3)


def _route_tables(idx, rank, counts, n_tok):
    bm = FFN_BLOCK
    n_asg = TOP_K * n_tok
    n_blocks = -(-(n_asg + N_EXPERTS * (bm - 1)) // bm)
    padded = (counts + bm - 1) // bm * bm
    pad_ends = jnp.cumsum(padded)
    pad_starts = pad_ends - padded
    e_ids = jnp.arange(N_EXPERTS, dtype=I32)
    start_of = jnp.sum(jnp.where(idx[None] == e_ids[:, None, None], pad_starts[:, None, None], 0), axis=0)
    dest = (start_of + rank).reshape(-1)
    nact = (pad_ends[-1] // bm).astype(I32)
    blk_start = jnp.arange(n_blocks, dtype=I32) * bm
    last = jnp.sum(jnp.where(pad_ends <= pad_ends[-1] - 1, 1, 0)).astype(I32)
    be = jnp.sum(jnp.where(pad_ends[None, :] <= blk_start[:, None], 1, 0), axis=1).astype(I32)
    active = blk_start < pad_ends[-1]
    be = jnp.where(active, be, last)
    blk = jnp.arange(n_blocks, dtype=I32)
    first = active & ((blk == 0) | (be != jnp.roll(be, 1)))
    slot = (jnp.cumsum(first.astype(I32)) - 1) & 1
    later_first = first[None, :] & (blk[None, :] > blk[:, None])
    nxt_pos = jnp.min(jnp.where(later_first, blk[None, :], n_blocks), axis=1)
    nxt = jnp.sum(jnp.where(blk[None, :] == nxt_pos[:, None], be[None, :], 0), axis=1)
    nxt = jnp.where(nxt_pos < n_blocks, nxt, -1).astype(I32)
    mine = be[:, None] == e_ids[None, :]
    cnt_b = jnp.sum(jnp.where(mine, counts[None, :], 0), axis=1)
    start_b = jnp.sum(jnp.where(mine, pad_starts[None, :], 0), axis=1)
    nvalid = jnp.where(active, jnp.clip(cnt_b - (blk_start - start_b), 0, bm), 0).astype(I32)
    return be, first.astype(I32), nxt, slot.astype(I32), nvalid, nact.reshape(1), dest, n_blocks * bm


def kernel(x, c, w_ada, b_ada, mix_norm_g, ffn_norm_g, w_in, hg_lower_bound_logits, hg_out_norm_g, da_q_norm_g, da_k_norm_g, da_lambda_q1, da_lambda_k1, da_lambda_q2, da_lambda_k2, da_subln_g, w_out, w_router, b_router, w1, b1, w2, b2):
    bsz, seq, d = x.shape
    t = bsz * seq
    depth = w_ada.shape[0]
    out_dtype = x.dtype
    hw = HG_HEADS * HG_DV
    xcur = x.reshape(t, d)
    for l in range(depth):
        ada = _ada(c, w_ada[l], b_ada[l])
        ada3 = ada.reshape(bsz, 1, N_MOD * d)
        col_h = 0
        col_a = col_h + 4 * HG_HEADS
        col_g = (4 * hw + 3 * DA_HEADS * 2 * DA_DH) // (d // 2)
        proj = _inproj(xcur, mix_norm_g[l].reshape(1, d), ada3, w_in[l].astype(BF16), seq)

        o_a = _hgrn(proj, hg_lower_bound_logits, hg_out_norm_g[l].reshape(1, HG_DV), bsz, seq, col_h, l)
        lambda_init = 0.8 - 0.6 * math.exp(-0.3 * l)
        qg2 = jnp.tile(da_q_norm_g[l], 2).reshape(1, 2 * DA_DH)
        kg2 = jnp.tile(da_k_norm_g[l], 2).reshape(1, 2 * DA_DH)
        lam4 = jnp.stack([da_lambda_q1[l], da_lambda_k1[l], da_lambda_q2[l], da_lambda_k2[l]])
        o_d = _attn(proj, qg2, kg2, lam4, da_subln_g[l].reshape(1, 2 * DA_DH), bsz, seq, col_a, lambda_init)

        x1, h2, idx, rank, gcol, cnt = _mixout(
            xcur, o_a, o_d, proj, col_g, w_out[l], ada3, ffn_norm_g[l].reshape(1, d),
            w_router[l].T, b_router[l].reshape(N_EXPERTS, 1), seq)

        counts = cnt[:, 0].astype(I32)
        be, first, nxt, slot, nvalid, nact, dest, n_rows = _route_tables(idx, rank, counts, t)
        b1p = b1[l].reshape(N_EXPERTS, -1, LANES, 2).transpose(0, 1, 3, 2).reshape(N_EXPERTS, 1, -1)
        xb = _sc_dispatch(h2, dest, n_rows)
        yb = _ffn(be, first, nxt, slot, nvalid, nact, xb, w1[l], b1p, w2[l], b2[l].reshape(N_EXPERTS, 1, d))
        yall = _sc_undispatch(yb, dest)
        xcur = _combine(x1, yall, gcol, ada3, seq, out_dtype)
    return xcur.reshape(bsz, seq, d)
```

```python
import functools
import math

import jax
import jax.numpy as jnp
from jax import lax
from jax.experimental import pallas as pl
from jax.experimental.pallas import tpu as pltpu
from jax.experimental.pallas import tpu_sc as plsc

F32 = jnp.float32
BF16 = jnp.bfloat16
I32 = jnp.int32

HG_HEADS = 4
HG_DK = 128
HG_DV = 128
HG_CHUNK = 32
DA_HEADS = 4
DA_DH = 64
N_EXPERTS = 32
TOP_K = 4
SWIGLU_ALPHA = 1.702
SWIGLU_LIMIT = 7.0
NORM_EPS = 1e-6
LOG2E = math.log2(math.e)
N_MOD = 6

LANES = 128
VMEM_LIMIT = 56 * 1024 * 1024

HG_ROWS = 256
ATTN_TILE = 256
FFN_BLOCK = 512

SC_CORES = 2
SC_WORKERS = SC_CORES * 16
SC_CHUNK = 128


def _sigmoid(x):
    return 1.0 / (1.0 + jnp.exp(-x))


def _dot(a, b):
    return jnp.dot(a, b, preferred_element_type=F32)


def _dot_nt(a, b):
    return lax.dot_general(a, b, (((1,), (1,)), ((), ())), preferred_element_type=F32)


def _split_bf16(x):
    hi = x.astype(BF16)
    lo = (x - hi.astype(F32)).astype(BF16)
    return hi, lo


def _pack_halves(x):
    half = x.shape[1] // 2
    lo = lax.bitcast_convert_type(x[:, :half].astype(BF16).astype(F32), I32)
    hi = lax.bitcast_convert_type(x[:, half:].astype(BF16).astype(F32), I32)
    return lax.shift_right_logical(lo, 16) | hi


def _unpack_halves(w):
    lo = lax.bitcast_convert_type(lax.shift_left(w, 16), F32)
    hi = lax.bitcast_convert_type(w & jnp.int32(-65536), F32)
    return lo, hi


def _ada_body(c_ref, w_ref, b_ref, o_ref):
    c = c_ref[...].astype(F32)
    ch, cl = _split_bf16(c * _sigmoid(c))
    wh, wl = _split_bf16(w_ref[...])
    o_ref[...] = _dot(ch, wh) + _dot(cl, wh) + _dot(ch, wl) + b_ref[...]


def _ada(c, w_ada, b_ada):
    bsz, d = c.shape
    n = w_ada.shape[1]
    tn = 2 * d
    return pl.pallas_call(
        _ada_body,
        grid=(n // tn,),
        in_specs=[
            pl.BlockSpec((bsz, d), lambda j: (0, 0)),
            pl.BlockSpec((d, tn), lambda j: (0, j)),
            pl.BlockSpec((1, tn), lambda j: (0, j)),
        ],
        out_specs=pl.BlockSpec((bsz, tn), lambda j: (0, j)),
        out_shape=jax.ShapeDtypeStruct((bsz, n), F32),
        name="ada",
    )(c, w_ada, b_ada.reshape(1, n))


def _norm_mod(x, g, shift, scale):
    ms = jnp.mean(x * x, axis=-1, keepdims=True)
    return (x * lax.rsqrt(ms + NORM_EPS) * g) * (1.0 + scale) + shift


def _inproj_body(x_ref, g_ref, sh_ref, sc_ref, w_ref, o_ref):
    h = _norm_mod(x_ref[...], g_ref[...], sh_ref[0], sc_ref[0])
    o_ref[...] = _dot(h.astype(BF16), w_ref[...]).astype(BF16)


def _inproj(x2, g, ada3, w_bf16, seq):
    t, d = x2.shape
    n = w_bf16.shape[1]
    tm = min(512, seq)
    nj = 1
    tn = n // nj
    per_b = seq // tm
    return pl.pallas_call(
        _inproj_body,
        grid=(nj, t // tm),
        in_specs=[
            pl.BlockSpec((tm, d), lambda j, i: (i, 0)),
            pl.BlockSpec((1, d), lambda j, i: (0, 0)),
            pl.BlockSpec((1, 1, d), lambda j, i: (i // per_b, 0, 0)),
            pl.BlockSpec((1, 1, d), lambda j, i: (i // per_b, 0, 1)),
            pl.BlockSpec((d, tn), lambda j, i: (0, j)),
        ],
        out_specs=pl.BlockSpec((tm, tn), lambda j, i: (i, j)),
        out_shape=jax.ShapeDtypeStruct((t, n), BF16),
        compiler_params=pltpu.CompilerParams(
            dimension_semantics=("arbitrary", "arbitrary"), vmem_limit_bytes=VMEM_LIMIT),
        name="inproj",
    )(x2, g, ada3, ada3, w_bf16)


def _hgrn_body(q_ref, f_ref, i_ref, og_ref, lbl_ref, g_ref, o_ref, *, seq, layer):
    rows, chunk = HG_ROWS, HG_CHUNK
    nchunk = rows // chunk
    lbl = lbl_ref[...].astype(F32)
    e = jnp.exp(lbl - jnp.max(lbl, axis=0, keepdims=True))
    lb = jnp.sum(e[: layer + 1], axis=0, keepdims=True) / jnp.sum(e, axis=0, keepdims=True)
    r_i = lax.broadcasted_iota(I32, (rows, rows), 0)
    c_i = lax.broadcasted_iota(I32, (rows, rows), 1)
    tri = ((r_i // chunk) == (c_i // chunk)) & (r_i >= c_i)
    tri_b = jnp.where(tri, 1.0, 0.0).astype(BF16)
    row_chunk = lax.broadcasted_iota(I32, (rows, HG_DK), 0) // chunk
    g = g_ref[...].astype(F32)

    def block(r, st):
        sl = pl.ds(r * rows, rows)
        qr = q_ref[sl, :].astype(F32)
        fr = f_ref[sl, :].astype(F32)
        v = i_ref[sl, :].astype(F32)
        og = og_ref[sl, :].astype(F32)
        q = qr * _sigmoid(qr)
        f = lb + (1.0 - lb) * _sigmoid(fr)
        k = 1.0 - f
        logf = jnp.log(f)
        lhi, llo = _split_bf16(logf)
        bc2 = _dot(tri_b, jnp.concatenate([lhi, llo], axis=1))
        bcum = bc2[:, :HG_DK] + bc2[:, HG_DK:]
        b3 = bcum.reshape(nchunk, chunk, HG_DK)
        bl = b3[:, chunk - 1:chunk, :]
        dec = jnp.exp(bl)
        kt_f = k * jnp.exp(-bcum)
        qt_f = q * jnp.exp(bcum)
        kd = (kt_f.reshape(nchunk, chunk, HG_DK) * dec).reshape(rows, HG_DK)
        a = jnp.where(tri, _dot_nt(qt_f.astype(BF16), kt_f.astype(BF16)), 0.0).astype(BF16)
        vt_b = v.T.astype(BF16)
        kd_x = jnp.concatenate([jnp.where(row_chunk == c, kd, 0.0) for c in range(nchunk)], axis=1)
        kv_all = _dot(vt_b, kd_x.astype(BF16))
        starts = []
        for c in range(nchunk):
            starts.append(st.astype(BF16))
            st = st * dec[c] + kv_all[:, c * HG_DK:(c + 1) * HG_DK]
        q_x = [jnp.where(row_chunk == c, qt_f, 0.0).astype(BF16) for c in range(nchunk)]
        o = _dot_nt(jnp.concatenate([a] + q_x, axis=1), jnp.concatenate([vt_b] + starts, axis=1))
        ms = jnp.mean(o * o, axis=-1, keepdims=True)
        o = o * lax.rsqrt(ms + NORM_EPS) * g
        o_ref[sl, :] = (o * (og * _sigmoid(og))).astype(o_ref.dtype)
        return st

    st = jnp.zeros((HG_DV, HG_DK), F32)
    for r in range(seq // rows):
        st = block(r, st)


def _hgrn(proj, lb_logits, norm_g, bsz, seq, col0, layer):
    t = proj.shape[0]
    blk = lambda off: pl.BlockSpec((seq, LANES), lambda b, h, off=off: (b, col0 + off + h))
    return pl.pallas_call(
        functools.partial(_hgrn_body, seq=seq, layer=layer),
        grid=(bsz, HG_HEADS),
        in_specs=[
            blk(0), blk(HG_HEADS), blk(2 * HG_HEADS), blk(3 * HG_HEADS),
            pl.BlockSpec((lb_logits.shape[0], HG_DK), lambda b, h: (0, h)),
            pl.BlockSpec((1, HG_DV), lambda b, h: (0, 0)),
        ],
        out_specs=pl.BlockSpec((seq, HG_DV), lambda b, h: (b, h)),
        out_shape=jax.ShapeDtypeStruct((t, HG_HEADS * HG_DV), BF16),
        compiler_params=pltpu.CompilerParams(
            dimension_semantics=("arbitrary", "arbitrary"), vmem_limit_bytes=VMEM_LIMIT),
        name="hgrn",
    )(proj, proj, proj, proj, lb_logits, norm_g)


def _group_norm(x, gsum_b, gain):
    ss = _dot((x * x).astype(BF16), gsum_b)
    return x * lax.rsqrt(ss * (1.0 / DA_DH) + NORM_EPS) * gain


def _attn_body(q_ref, k_ref, v_ref, qg_ref, kg_ref, lam_ref, sg_ref, o_ref, kn_scr, v1_scr, *, seq, lambda_init):
    tq = ATTN_TILE
    width = 2 * DA_DH
    r_l = lax.broadcasted_iota(I32, (width, width), 0) // DA_DH
    c_l = lax.broadcasted_iota(I32, (width, width), 1) // DA_DH
    gsum_b = jnp.where(r_l == c_l, 1.0, 0.0).astype(BF16)
    lane = lax.broadcasted_iota(I32, (1, width), 1)
    kg = kg_ref[...].astype(F32)
    qg = qg_ref[...].astype(F32) * (DA_DH ** -0.5 * LOG2E)
    sg = sg_ref[...].astype(F32) * (1.0 - lambda_init)
    ones = jnp.ones((tq, width), BF16)
    row = lax.broadcasted_iota(I32, (tq, tq), 0)
    col = lax.broadcasted_iota(I32, (tq, tq), 1)
    keep = row >= col

    lam_v = lam_ref[...].astype(F32)
    lam = (jnp.exp(jnp.sum(lam_v[0:1] * lam_v[1:2], axis=-1, keepdims=True))
           - jnp.exp(jnp.sum(lam_v[2:3] * lam_v[3:4], axis=-1, keepdims=True)) + lambda_init)

    def softmax_v(qc, nk):
        s = _dot_nt(qc, kn_scr[0:nk, :])
        diag = jnp.where(keep, s[:, nk - tq:], -jnp.inf)
        s = diag if nk == tq else jnp.concatenate([s[:, :nk - tq], diag], axis=1)
        m = jnp.max(s, axis=-1, keepdims=True)
        return _dot(jnp.exp2(s - m).astype(BF16), v1_scr[0:nk, :])

    for i in range(seq // tq):
        sl = slice(i * tq, (i + 1) * tq)
        kn_scr[sl, :] = _group_norm(k_ref[sl, :].astype(F32), gsum_b, kg).astype(BF16)
        v1_scr[sl, :] = jnp.concatenate([v_ref[sl, :], ones], axis=1)
        qn = _group_norm(q_ref[sl, :].astype(F32), gsum_b, qg)
        nk = (i + 1) * tq
        a1 = softmax_v(jnp.where(lane < DA_DH, qn, 0.0).astype(BF16), nk)
        a2 = softmax_v(jnp.where(lane >= DA_DH, qn, 0.0).astype(BF16), nk)
        o = a1[:, :width] / a1[:, width:width + 1] - lam * (a2[:, :width] / a2[:, width:width + 1])
        ms = jnp.mean(o * o, axis=-1, keepdims=True)
        o_ref[sl, :] = (o * lax.rsqrt(ms + NORM_EPS) * sg).astype(o_ref.dtype)


def _attn(proj, qg2, kg2, lam4, subln_g, bsz, seq, col0, lambda_init):
    t = proj.shape[0]
    width = 2 * DA_DH
    return pl.pallas_call(
        functools.partial(_attn_body, seq=seq, lambda_init=lambda_init),
        grid=(bsz, DA_HEADS),
        in_specs=[
            pl.BlockSpec((seq, width), lambda b, h: (b, col0 + h)),
            pl.BlockSpec((seq, width), lambda b, h: (b, col0 + DA_HEADS + h)),
            pl.BlockSpec((seq, width), lambda b, h: (b, col0 + 2 * DA_HEADS + h)),
            pl.BlockSpec((1, width), lambda b, h: (0, 0)),
            pl.BlockSpec((1, width), lambda b, h: (0, 0)),
            pl.BlockSpec((4, DA_DH), lambda b, h: (0, 0)),
            pl.BlockSpec((1, width), lambda b, h: (0, 0)),
        ],
        out_specs=pl.BlockSpec((seq, width), lambda b, h: (b, h)),
        out_shape=jax.ShapeDtypeStruct((t, DA_HEADS * width), BF16),
        scratch_shapes=[pltpu.VMEM((seq, width), BF16), pltpu.VMEM((seq, 2 * width), BF16)],
        compiler_params=pltpu.CompilerParams(
            dimension_semantics=("arbitrary", "arbitrary"), vmem_limit_bytes=VMEM_LIMIT),
        name="attn",
    )(proj, proj, proj, qg2, kg2, lam4, subln_g)


def _mixout_body(x_ref, oa_ref, od_ref, ga0_ref, ga1_ref, gd0_ref, gd1_ref, wo_ref, g1_ref, g_ref, sh_ref, sc_ref,
                 wr_ref, br_ref,
                 x1_ref, h2_ref, idx_ref, rank_ref, gcol_ref, cnt_ref, carry_scr, wo_scr):
    i = pl.program_id(0)
    tm = x_ref.shape[0]
    hw = oa_ref.shape[1]

    @pl.when(i == 0)
    def _():
        carry_scr[...] = jnp.zeros_like(carry_scr)
        wo_scr[...] = wo_ref[...].astype(BF16)

    ya = _dot(oa_ref[...], wo_scr[0:hw, :])
    yd = _dot(od_ref[...], wo_scr[hw:, :])
    ga = jnp.concatenate([ga0_ref[...], ga1_ref[...]], axis=1).astype(F32)
    gd = jnp.concatenate([gd0_ref[...], gd1_ref[...]], axis=1).astype(F32)
    y = _sigmoid(ga) * ya + _sigmoid(gd) * yd
    x1 = x_ref[...] + g1_ref[0] * y
    x1_ref[...] = x1
    h2 = _norm_mod(x1, g_ref[...], sh_ref[0], sc_ref[0])
    h2_ref[...] = _pack_halves(h2)

    hh, hl = _split_bf16(h2)
    wh, wl = _split_bf16(wr_ref[...])
    logits = _dot_nt(wh, hh) + _dot_nt(wl, hh) + _dot_nt(wh, hl) + br_ref[...]

    e_iota = lax.broadcasted_iota(I32, (N_EXPERTS, tm), 0).astype(F32)
    vals = logits
    tops, sels, idxs = [], [], []
    for _ in range(TOP_K):
        m = jnp.max(vals, axis=0, keepdims=True)
        idx = jnp.min(jnp.where(vals == m, e_iota, float(N_EXPERTS)), axis=0, keepdims=True)
        sel = e_iota == idx
        vals = jnp.where(sel, -jnp.inf, vals)
        tops.append(m)
        sels.append(sel)
        idxs.append(idx)
    ex = [jnp.exp(tv - tops[0]) for tv in tops]
    den = ex[0] + ex[1] + ex[2] + ex[3]
    gates = [v / den for v in ex]

    hot = jnp.where(sels[0] | sels[1] | sels[2] | sels[3], 1.0, 0.0)
    r_t = lax.broadcasted_iota(I32, (tm, tm), 0)
    c_t = lax.broadcasted_iota(I32, (tm, tm), 1)
    upper = jnp.where(r_t < c_t, 1.0, 0.0).astype(BF16)
    excl = _dot(hot.astype(BF16), upper) + carry_scr[:, 0:1]
    carry_scr[...] = carry_scr[...] + jnp.sum(hot, axis=1, keepdims=True)
    cnt_ref[...] = carry_scr[...]

    ranks = [jnp.sum(jnp.where(s, excl, 0.0), axis=0, keepdims=True) for s in sels]
    idx_ref[...] = jnp.concatenate(idxs, axis=0).astype(I32)
    rank_ref[...] = jnp.concatenate(ranks, axis=0).astype(I32)
    gpad = jnp.concatenate(gates + [jnp.zeros((LANES - TOP_K, tm), F32)], axis=0)
    gcol_ref[...] = gpad.T


def _mixout(x2, oa, od, proj, col_g, wo, ada3, ffn_g, wr_t, br_col, seq):
    t, d = x2.shape
    tm = min(1024, seq)
    per_b = seq // tm
    hw = oa.shape[1]
    row = lambda w: pl.BlockSpec((tm, w), lambda i: (i, 0))
    gate = lambda c: pl.BlockSpec((tm, d // 2), lambda i, c=c: (i, col_g + c))
    mod = lambda c: pl.BlockSpec((1, 1, d), lambda i, c=c: (i // per_b, 0, c))
    full = lambda a: pl.BlockSpec(a.shape, lambda i: (0,) * a.ndim)
    return pl.pallas_call(
        _mixout_body,
        grid=(t // tm,),
        in_specs=[
            row(d), row(hw), row(hw),
            gate(0), gate(1), gate(2), gate(3),
            full(wo),
            mod(2),
            full(ffn_g), mod(3), mod(4),
            full(wr_t), full(br_col),
        ],
        out_specs=[
            row(d), row(d // 2),
            pl.BlockSpec((TOP_K, tm), lambda i: (0, i)),
            pl.BlockSpec((TOP_K, tm), lambda i: (0, i)),
            pl.BlockSpec((tm, LANES), lambda i: (i, 0)),
            pl.BlockSpec((N_EXPERTS, LANES), lambda i: (0, 0)),
        ],
        out_shape=[
            jax.ShapeDtypeStruct((t, d), F32),
            jax.ShapeDtypeStruct((t, d // 2), I32),
            jax.ShapeDtypeStruct((TOP_K, t), I32),
            jax.ShapeDtypeStruct((TOP_K, t), I32),
            jax.ShapeDtypeStruct((t, LANES), F32),
            jax.ShapeDtypeStruct((N_EXPERTS, LANES), F32),
        ],
        scratch_shapes=[pltpu.VMEM((N_EXPERTS, LANES), F32), pltpu.VMEM(wo.shape, BF16)],
        compiler_params=pltpu.CompilerParams(
            dimension_semantics=("arbitrary",), vmem_limit_bytes=VMEM_LIMIT),
        name="mixout",
    )(x2, oa, od, proj, proj, proj, proj, wo, ada3, ffn_g, ada3, ada3, wr_t, br_col)


def _sc_mesh():
    return plsc.VectorSubcoreMesh(core_axis_name="c", subcore_axis_name="s")


def _sc_worker_id():
    return lax.axis_index("s") * SC_CORES + lax.axis_index("c")


def _sc_dispatch(h2, dest, n_rows):
    t, d = h2.shape
    assert t % (SC_WORKERS * SC_CHUNK) == 0
    cpw = t // (SC_WORKERS * SC_CHUNK)
    dest_w = dest.reshape(TOP_K, SC_WORKERS, cpw, SC_CHUNK).transpose(1, 0, 2, 3).reshape(-1, SC_CHUNK)
    ipw = TOP_K * cpw

    def body(h2_hbm, dest_hbm, xb_hbm, idx_v, rows_v):
        wid = _sc_worker_id()
        pltpu.sync_copy(dest_hbm.at[pl.ds(pl.multiple_of(wid * ipw, ipw), ipw)], idx_v)

        @pl.loop(0, cpw)
        def _(c):
            t0 = pl.multiple_of((wid * cpw + c) * SC_CHUNK, SC_CHUNK)
            pltpu.sync_copy(h2_hbm.at[pl.ds(t0, SC_CHUNK)], rows_v)
            for k in range(TOP_K):
                pltpu.sync_copy(rows_v, xb_hbm.at[idx_v.at[k * cpw + c]])

    return pl.kernel(
        body, out_type=jax.ShapeDtypeStruct((n_rows, d), h2.dtype), mesh=_sc_mesh(),
        scratch_types=[pltpu.VMEM((ipw, SC_CHUNK), I32), pltpu.VMEM((SC_CHUNK, d), h2.dtype)],
        name="dispatch",
    )(h2, dest_w)


def _sc_undispatch(y, dest):
    n_asg = dest.shape[0]
    d = y.shape[1]
    assert n_asg % (SC_WORKERS * SC_CHUNK) == 0
    cpw = n_asg // (SC_WORKERS * SC_CHUNK)

    def body(y_hbm, dest_hbm, yt_hbm, idx_v, rows_v):
        wid = _sc_worker_id()
        pltpu.sync_copy(dest_hbm.at[pl.ds(pl.multiple_of(wid * cpw, cpw), cpw)], idx_v)

        @pl.loop(0, cpw)
        def _(c):
            r0 = pl.multiple_of((wid * cpw + c) * SC_CHUNK, SC_CHUNK)
            pltpu.sync_copy(y_hbm.at[idx_v.at[c]], rows_v)
            pltpu.sync_copy(rows_v, yt_hbm.at[pl.ds(r0, SC_CHUNK)])

    return pl.kernel(
        body, out_type=jax.ShapeDtypeStruct((n_asg, d), y.dtype), mesh=_sc_mesh(),
        scratch_types=[pltpu.VMEM((cpw, SC_CHUNK), I32), pltpu.VMEM((SC_CHUNK, d), y.dtype)],
        name="undispatch",
    )(y, dest.reshape(-1, SC_CHUNK))


def _ffn_body(be_ref, first_ref, nxt_ref, slot_ref, nv_ref, nact_ref, x_ref, w1_hbm, b1_ref, w2_hbm, b2_ref, y_ref,
              w1f, w2f, w1c, w2c, sem1, sem2):
    j = pl.program_id(0)
    ff = w2f.shape[1]
    pair = 2 * LANES
    ngroup = (2 * ff) // pair

    def weight_copies(e, slot):
        return (pltpu.make_async_copy(w1_hbm.at[e], w1f.at[slot], sem1.at[slot]),
                pltpu.make_async_copy(w2_hbm.at[e], w2f.at[slot], sem2.at[slot]))

    @pl.when(j == 0)
    def _():
        for cp in weight_copies(be_ref[0], 0):
            cp.start()

    @pl.when(first_ref[j] == 1)
    def _():
        slot = slot_ref[j]
        for cp in weight_copies(be_ref[j], slot):
            cp.wait()

        @pl.when(nxt_ref[j] >= 0)
        def _():
            for cp in weight_copies(nxt_ref[j], 1 - slot):
                cp.start(priority=1)

        r_p = lax.broadcasted_iota(I32, (pair, pair), 0)
        c_p = lax.broadcasted_iota(I32, (pair, pair), 1)
        src = jnp.where(c_p < LANES, 2 * c_p, 2 * (c_p - LANES) + 1)
        perm = jnp.where(r_p == src, 1.0, 0.0).astype(BF16)
        for g in range(ngroup):
            cols = slice(g * pair, (g + 1) * pair)
            w1c[:, cols] = _dot(w1f[slot, :, cols].astype(BF16), perm).astype(BF16)
        w2c[...] = w2f[slot].astype(BF16)

    def expert_rows(nrows):
        x_lo, x_hi = _unpack_halves(x_ref[0:nrows, :])
        xb = jnp.concatenate([x_lo.astype(BF16), x_hi.astype(BF16)], axis=1)
        u = _dot(xb, w1c[...]) + b1_ref[0]
        acts = []
        for g in range(ngroup):
            glu = jnp.minimum(u[:, g * pair:g * pair + LANES], SWIGLU_LIMIT)
            lin = jnp.clip(u[:, g * pair + LANES:(g + 1) * pair], -SWIGLU_LIMIT, SWIGLU_LIMIT)
            acts.append((glu * _sigmoid(SWIGLU_ALPHA * glu) * (lin + 1.0)).astype(BF16))
        act = jnp.concatenate(acts, axis=1)
        y_ref[0:nrows, :] = _pack_halves(_dot(act, w2c[...]) + b2_ref[0])

    active = j < nact_ref[0]
    half = x_ref.shape[0] // 2
    pl.when(active & (nv_ref[j] > half))(functools.partial(expert_rows, x_ref.shape[0]))
    pl.when(active & (nv_ref[j] <= half))(functools.partial(expert_rows, half))


def _ffn(block_expert, first, nxt, slot, nvalid, nact, xb, w1, b1p, w2, b2):
    bm = FFN_BLOCK
    n_rows, dw = xb.shape
    d = 2 * dw
    n_blocks = n_rows // bm
    ff2 = w1.shape[2]
    ff = w2.shape[1]
    row_blk = lambda j, be, fi, nx, sl, nv, na: (jnp.minimum(j, na[0] - 1), 0)
    bias_blk = lambda j, be, fi, nx, sl, nv, na: (be[j], 0, 0)
    grid_spec = pltpu.PrefetchScalarGridSpec(
        num_scalar_prefetch=6,
        grid=(n_blocks,),
        in_specs=[
            pl.BlockSpec((bm, dw), row_blk),
            pl.BlockSpec(memory_space=pl.ANY),
            pl.BlockSpec((1, 1, ff2), bias_blk),
            pl.BlockSpec(memory_space=pl.ANY),
            pl.BlockSpec((1, 1, d), bias_blk),
        ],
        out_specs=pl.BlockSpec((bm, dw), row_blk),
        scratch_shapes=[
            pltpu.VMEM((2, d, ff2), F32), pltpu.VMEM((2, ff, d), F32),
            pltpu.VMEM((d, ff2), BF16), pltpu.VMEM((ff, d), BF16),
            pltpu.SemaphoreType.DMA((2,)), pltpu.SemaphoreType.DMA((2,)),
        ],
    )
    return pl.pallas_call(
        _ffn_body,
        grid_spec=grid_spec,
        out_shape=jax.ShapeDtypeStruct((n_rows, dw), I32),
        compiler_params=pltpu.CompilerParams(
            dimension_semantics=("arbitrary",), vmem_limit_bytes=VMEM_LIMIT),
        name="ffn",
    )(block_expert, first, nxt, slot, nvalid, nact, xb, w1, b1p, w2, b2)


def _combine_body(x1_ref, y0_ref, y1_ref, y2_ref, y3_ref, gcol_ref, g2_ref, o_ref):
    gc = gcol_ref[...]
    m_lo = m_hi = None
    for k, y_ref in enumerate((y0_ref, y1_ref, y2_ref, y3_ref)):
        lo, hi = _unpack_halves(y_ref[...])
        gk = gc[:, k:k + 1]
        m_lo = gk * lo if m_lo is None else m_lo + gk * lo
        m_hi = gk * hi if m_hi is None else m_hi + gk * hi
    m = jnp.concatenate([m_lo, m_hi], axis=1)
    o_ref[...] = (x1_ref[...] + g2_ref[0] * m).astype(o_ref.dtype)


def _combine(x1, yall, gcol, ada3, seq, out_dtype):
    t, d = x1.shape
    tm = min(1024, seq)
    per_b = seq // tm
    nt = t // tm
    yk = lambda k: pl.BlockSpec((tm, d // 2), lambda i, k=k: (k * nt + i, 0))
    return pl.pallas_call(
        _combine_body,
        grid=(nt,),
        in_specs=[
            pl.BlockSpec((tm, d), lambda i: (i, 0)),
            yk(0), yk(1), yk(2), yk(3),
            pl.BlockSpec((tm, LANES), lambda i: (i, 0)),
            pl.BlockSpec((1, 1, d), lambda i: (i // per_b, 0, 5)),
        ],
        out_specs=pl.BlockSpec((tm, d), lambda i: (i, 0)),
        out_shape=jax.ShapeDtypeStruct((t, d), out_dtype),
        compiler_params=pltpu.CompilerParams(
            dimension_semantics=("arbitrary",), vmem_limit_bytes=VMEM_LIMIT),
        name="combine",
    )(x1, yall, yall, yall, yall, gcol, ada3)


def _route_tables(idx, rank, counts, n_tok):
    bm = FFN_BLOCK
    n_asg = TOP_K * n_tok
    n_blocks = -(-(n_asg + N_EXPERTS * (bm - 1)) // bm)
    padded = (counts + bm - 1) // bm * bm
    pad_ends = jnp.cumsum(padded)
    pad_starts = pad_ends - padded
    e_ids = jnp.arange(N_EXPERTS, dtype=I32)
    start_of = jnp.sum(jnp.where(idx[None] == e_ids[:, None, None], pad_starts[:, None, None], 0), axis=0)
    dest = (start_of + rank).reshape(-1)
    nact = (pad_ends[-1] // bm).astype(I32)
    blk_start = jnp.arange(n_blocks, dtype=I32) * bm
    last = jnp.sum(jnp.where(pad_ends <= pad_ends[-1] - 1, 1, 0)).astype(I32)
    be = jnp.sum(jnp.where(pad_ends[None, :] <= blk_start[:, None], 1, 0), axis=1).astype(I32)
    active = blk_start < pad_ends[-1]
    be = jnp.where(active, be, last)
    blk = jnp.arange(n_blocks, dtype=I32)
    first = active & ((blk == 0) | (be != jnp.roll(be, 1)))
    slot = (jnp.cumsum(first.astype(I32)) - 1) & 1
    later_first = first[None, :] & (blk[None, :] > blk[:, None])
    nxt_pos = jnp.min(jnp.where(later_first, blk[None, :], n_blocks), axis=1)
    nxt = jnp.sum(jnp.where(blk[None, :] == nxt_pos[:, None], be[None, :], 0), axis=1)
    nxt = jnp.where(nxt_pos < n_blocks, nxt, -1).astype(I32)
    mine = be[:, None] == e_ids[None, :]
    cnt_b = jnp.sum(jnp.where(mine, counts[None, :], 0), axis=1)
    start_b = jnp.sum(jnp.where(mine, pad_starts[None, :], 0), axis=1)
    nvalid = jnp.where(active, jnp.clip(cnt_b - (blk_start - start_b), 0, bm), 0).astype(I32)
    return be, first.astype(I32), nxt, slot.astype(I32), nvalid, nact.reshape(1), dest, n_blocks * bm


def kernel(x, c, w_ada, b_ada, mix_norm_g, ffn_norm_g, w_in, hg_lower_bound_logits, hg_out_norm_g, da_q_norm_g, da_k_norm_g, da_lambda_q1, da_lambda_k1, da_lambda_q2, da_lambda_k2, da_subln_g, w_out, w_router, b_router, w1, b1, w2, b2):
    bsz, seq, d = x.shape
    t = bsz * seq
    depth = w_ada.shape[0]
    out_dtype = x.dtype
    hw = HG_HEADS * HG_DV
    xcur = x.reshape(t, d)
    for l in range(depth):
        ada = _ada(c, w_ada[l], b_ada[l])
        ada3 = ada.reshape(bsz, 1, N_MOD * d)
        col_h = 0
        col_a = col_h + 4 * HG_HEADS
        col_g = (4 * hw + 3 * DA_HEADS * 2 * DA_DH) // (d // 2)
        proj = _inproj(xcur, mix_norm_g[l].reshape(1, d), ada3, w_in[l].astype(BF16), seq)

        o_a = _hgrn(proj, hg_lower_bound_logits, hg_out_norm_g[l].reshape(1, HG_DV), bsz, seq, col_h, l)
        lambda_init = 0.8 - 0.6 * math.exp(-0.3 * l)
        qg2 = jnp.tile(da_q_norm_g[l], 2).reshape(1, 2 * DA_DH)
        kg2 = jnp.tile(da_k_norm_g[l], 2).reshape(1, 2 * DA_DH)
        lam4 = jnp.stack([da_lambda_q1[l], da_lambda_k1[l], da_lambda_q2[l], da_lambda_k2[l]])
        o_d = _attn(proj, qg2, kg2, lam4, da_subln_g[l].reshape(1, 2 * DA_DH), bsz, seq, col_a, lambda_init)

        x1, h2, idx, rank, gcol, cnt = _mixout(
            xcur, o_a, o_d, proj, col_g, w_out[l], ada3, ffn_norm_g[l].reshape(1, d),
            w_router[l].T, b_router[l].reshape(N_EXPERTS, 1), seq)

        counts = cnt[:, 0].astype(I32)
        be, first, nxt, slot, nvalid, nact, dest, n_rows = _route_tables(idx, rank, counts, t)
        b1p = b1[l].reshape(N_EXPERTS, -1, LANES, 2).transpose(0, 1, 3, 2).reshape(N_EXPERTS, 1, -1)
        xb = _sc_dispatch(h2, dest, n_rows)
        yb = _ffn(be, first, nxt, slot, nvalid, nact, xb, w1[l], b1p, w2[l], b2[l].reshape(N_EXPERTS, 1, d))
        yall = _sc_undispatch(yb, dest)
        xcur = _combine(x1, yall, gcol, ada3, seq, out_dtype)
    return xcur.reshape(bsz, seq, d)
```

```python
import functools
import math

import jax
import jax.numpy as jnp
from jax import lax
from jax.experimental import pallas as pl
from jax.experimental.pallas import tpu as pltpu
from jax.experimental.pallas import tpu_sc as plsc

F32 = jnp.float32
BF16 = jnp.bfloat16
I32 = jnp.int32

HG_HEADS = 4
HG_DK = 128
HG_DV = 128
HG_CHUNK = 32
DA_HEADS = 4
DA_DH = 64
N_EXPERTS = 32
TOP_K = 4
SWIGLU_ALPHA = 1.702
SWIGLU_LIMIT = 7.0
NORM_EPS = 1e-6
LOG2E = math.log2(math.e)
N_MOD = 6

LANES = 128
VMEM_LIMIT = 56 * 1024 * 1024

HG_ROWS = 256
ATTN_TILE = 256
FFN_BLOCK = 512

SC_CORES = 2
SC_WORKERS = SC_CORES * 16
SC_CHUNK = 128


def _sigmoid(x):
    return 1.0 / (1.0 + jnp.exp(-x))


def _dot(a, b):
    return jnp.dot(a, b, preferred_element_type=F32)


def _dot_nt(a, b):
    return lax.dot_general(a, b, (((1,), (1,)), ((), ())), preferred_element_type=F32)


def _split_bf16(x):
    hi = x.astype(BF16)
    lo = (x - hi.astype(F32)).astype(BF16)
    return hi, lo


def _pack_halves(x):
    half = x.shape[1] // 2
    lo = lax.bitcast_convert_type(x[:, :half].astype(BF16).astype(F32), I32)
    hi = lax.bitcast_convert_type(x[:, half:].astype(BF16).astype(F32), I32)
    return lax.shift_right_logical(lo, 16) | hi


def _unpack_halves(w):
    lo = lax.bitcast_convert_type(lax.shift_left(w, 16), F32)
    hi = lax.bitcast_convert_type(w & jnp.int32(-65536), F32)
    return lo, hi


def _ada_body(c_ref, w_ref, b_ref, o_ref):
    c = c_ref[...].astype(F32)
    ch, cl = _split_bf16(c * _sigmoid(c))
    wh, wl = _split_bf16(w_ref[...])
    o_ref[...] = _dot(ch, wh) + _dot(cl, wh) + _dot(ch, wl) + b_ref[...]


def _ada(c, w_ada, b_ada):
    bsz, d = c.shape
    n = w_ada.shape[1]
    tn = 2 * d
    return pl.pallas_call(
        _ada_body,
        grid=(n // tn,),
        in_specs=[
            pl.BlockSpec((bsz, d), lambda j: (0, 0)),
            pl.BlockSpec((d, tn), lambda j: (0, j)),
            pl.BlockSpec((1, tn), lambda j: (0, j)),
        ],
        out_specs=pl.BlockSpec((bsz, tn), lambda j: (0, j)),
        out_shape=jax.ShapeDtypeStruct((bsz, n), F32),
        name="ada",
    )(c, w_ada, b_ada.reshape(1, n))


def _norm_mod(x, g, shift, scale):
    ms = jnp.mean(x * x, axis=-1, keepdims=True)
    return (x * lax.rsqrt(ms + NORM_EPS) * g) * (1.0 + scale) + shift


def _inproj_body(x_ref, g_ref, sh_ref, sc_ref, w_ref, o_ref):
    h = _norm_mod(x_ref[...], g_ref[...], sh_ref[0], sc_ref[0])
    o_ref[...] = _dot(h.astype(BF16), w_ref[...]).astype(BF16)


def _inproj(x2, g, ada3, w_bf16, seq):
    t, d = x2.shape
    n = w_bf16.shape[1]
    tm = min(512, seq)
    nj = 1
    tn = n // nj
    per_b = seq // tm
    return pl.pallas_call(
        _inproj_body,
        grid=(nj, t // tm),
        in_specs=[
            pl.BlockSpec((tm, d), lambda j, i: (i, 0)),
            pl.BlockSpec((1, d), lambda j, i: (0, 0)),
            pl.BlockSpec((1, 1, d), lambda j, i: (i // per_b, 0, 0)),
            pl.BlockSpec((1, 1, d), lambda j, i: (i // per_b, 0, 1)),
            pl.BlockSpec((d, tn), lambda j, i: (0, j)),
        ],
        out_specs=pl.BlockSpec((tm, tn), lambda j, i: (i, j)),
        out_shape=jax.ShapeDtypeStruct((t, n), BF16),
        compiler_params=pltpu.CompilerParams(
            dimension_semantics=("arbitrary", "arbitrary"), vmem_limit_bytes=VMEM_LIMIT),
        name="inproj",
    )(x2, g, ada3, ada3, w_bf16)


def _hgrn_body(q_ref, f_ref, i_ref, og_ref, lbl_ref, g_ref, o_ref, *, seq, layer):
    rows, chunk = HG_ROWS, HG_CHUNK
    nchunk = rows // chunk
    lbl = lbl_ref[...].astype(F32)
    e = jnp.exp(lbl - jnp.max(lbl, axis=0, keepdims=True))
    lb = jnp.sum(e[: layer + 1], axis=0, keepdims=True) / jnp.sum(e, axis=0, keepdims=True)
    r_i = lax.broadcasted_iota(I32, (rows, rows), 0)
    c_i = lax.broadcasted_iota(I32, (rows, rows), 1)
    tri = ((r_i // chunk) == (c_i // chunk)) & (r_i >= c_i)
    tri_b = jnp.where(tri, 1.0, 0.0).astype(BF16)
    row_chunk = lax.broadcasted_iota(I32, (rows, HG_DK), 0) // chunk
    g = g_ref[...].astype(F32)

    def block(r, st):
        sl = pl.ds(r * rows, rows)
        qr = q_ref[sl, :].astype(F32)
        fr = f_ref[sl, :].astype(F32)
        v = i_ref[sl, :].astype(F32)
        og = og_ref[sl, :].astype(F32)
        q = qr * _sigmoid(qr)
        f = lb + (1.0 - lb) * _sigmoid(fr)
        k = 1.0 - f
        logf = jnp.log(f)
        lhi, llo = _split_bf16(logf)
        bc2 = _dot(tri_b, jnp.concatenate([lhi, llo], axis=1))
        bcum = bc2[:, :HG_DK] + bc2[:, HG_DK:]
        b3 = bcum.reshape(nchunk, chunk, HG_DK)
        bl = b3[:, chunk - 1:chunk, :]
        dec = jnp.exp(bl)
        kt_f = k * jnp.exp(-bcum)
        qt_f = q * jnp.exp(bcum)
        kd = (kt_f.reshape(nchunk, chunk, HG_DK) * dec).reshape(rows, HG_DK)
        a = jnp.where(tri, _dot_nt(qt_f.astype(BF16), kt_f.astype(BF16)), 0.0).astype(BF16)
        vt_b = v.T.astype(BF16)
        kd_x = jnp.concatenate([jnp.where(row_chunk == c, kd, 0.0) for c in range(nchunk)], axis=1)
        kv_all = _dot(vt_b, kd_x.astype(BF16))
        starts = []
        for c in range(nchunk):
            starts.append(st.astype(BF16))
            st = st * dec[c] + kv_all[:, c * HG_DK:(c + 1) * HG_DK]
        q_x = [jnp.where(row_chunk == c, qt_f, 0.0).astype(BF16) for c in range(nchunk)]
        o = _dot_nt(jnp.concatenate([a] + q_x, axis=1), jnp.concatenate([vt_b] + starts, axis=1))
        ms = jnp.mean(o * o, axis=-1, keepdims=True)
        o = o * lax.rsqrt(ms + NORM_EPS) * g
        o_ref[sl, :] = (o * (og * _sigmoid(og))).astype(o_ref.dtype)
        return st

    st = jnp.zeros((HG_DV, HG_DK), F32)
    for r in range(seq // rows):
        st = block(r, st)


def _hgrn(proj, lb_logits, norm_g, bsz, seq, col0, layer):
    t = proj.shape[0]
    blk = lambda off: pl.BlockSpec((seq, LANES), lambda b, h, off=off: (b, col0 + off + h))
    return pl.pallas_call(
        functools.partial(_hgrn_body, seq=seq, layer=layer),
        grid=(bsz, HG_HEADS),
        in_specs=[
            blk(0), blk(HG_HEADS), blk(2 * HG_HEADS), blk(3 * HG_HEADS),
            pl.BlockSpec((lb_logits.shape[0], HG_DK), lambda b, h: (0, h)),
            pl.BlockSpec((1, HG_DV), lambda b, h: (0, 0)),
        ],
        out_specs=pl.BlockSpec((seq, HG_DV), lambda b, h: (b, h)),
        out_shape=jax.ShapeDtypeStruct((t, HG_HEADS * HG_DV), BF16),
        compiler_params=pltpu.CompilerParams(
            dimension_semantics=("arbitrary", "arbitrary"), vmem_limit_bytes=VMEM_LIMIT),
        name="hgrn",
    )(proj, proj, proj, proj, lb_logits, norm_g)


def _group_norm(x, gsum_b, gain):
    ss = _dot((x * x).astype(BF16), gsum_b)
    return x * lax.rsqrt(ss * (1.0 / DA_DH) + NORM_EPS) * gain


def _attn_body(q_ref, k_ref, v_ref, qg_ref, kg_ref, lam_ref, sg_ref, o_ref, kn_scr, v1_scr, *, seq, lambda_init):
    tq = ATTN_TILE
    width = 2 * DA_DH
    r_l = lax.broadcasted_iota(I32, (width, width), 0) // DA_DH
    c_l = lax.broadcasted_iota(I32, (width, width), 1) // DA_DH
    gsum_b = jnp.where(r_l == c_l, 1.0, 0.0).astype(BF16)
    lane = lax.broadcasted_iota(I32, (1, width), 1)
    kg = kg_ref[...].astype(F32)
    qg = qg_ref[...].astype(F32) * (DA_DH ** -0.5 * LOG2E)
    sg = sg_ref[...].astype(F32) * (1.0 - lambda_init)
    ones = jnp.ones((tq, width), BF16)
    row = lax.broadcasted_iota(I32, (tq, tq), 0)
    col = lax.broadcasted_iota(I32, (tq, tq), 1)
    keep = row >= col

    lam_v = lam_ref[...].astype(F32)
    lam = (jnp.exp(jnp.sum(lam_v[0:1] * lam_v[1:2], axis=-1, keepdims=True))
           - jnp.exp(jnp.sum(lam_v[2:3] * lam_v[3:4], axis=-1, keepdims=True)) + lambda_init)

    def softmax_v(qc, nk):
        s = _dot_nt(qc, kn_scr[0:nk, :])
        diag = jnp.where(keep, s[:, nk - tq:], -jnp.inf)
        s = diag if nk == tq else jnp.concatenate([s[:, :nk - tq], diag], axis=1)
        m = jnp.max(s, axis=-1, keepdims=True)
        return _dot(jnp.exp2(s - m).astype(BF16), v1_scr[0:nk, :])

    for i in range(seq // tq):
        sl = slice(i * tq, (i + 1) * tq)
        kn_scr[sl, :] = _group_norm(k_ref[sl, :].astype(F32), gsum_b, kg).astype(BF16)
        v1_scr[sl, :] = jnp.concatenate([v_ref[sl, :], ones], axis=1)
        qn = _group_norm(q_ref[sl, :].astype(F32), gsum_b, qg)
        nk = (i + 1) * tq
        a1 = softmax_v(jnp.where(lane < DA_DH, qn, 0.0).astype(BF16), nk)
        a2 = softmax_v(jnp.where(lane >= DA_DH, qn, 0.0).astype(BF16), nk)
        o = a1[:, :width] / a1[:, width:width + 1] - lam * (a2[:, :width] / a2[:, width:width + 1])
        ms = jnp.mean(o * o, axis=-1, keepdims=True)
        o_ref[sl, :] = (o * lax.rsqrt(ms + NORM_EPS) * sg).astype(o_ref.dtype)


def _attn(proj, qg2, kg2, lam4, subln_g, bsz, seq, col0, lambda_init):
    t = proj.shape[0]
    width = 2 * DA_DH
    return pl.pallas_call(
        functools.partial(_attn_body, seq=seq, lambda_init=lambda_init),
        grid=(bsz, DA_HEADS),
        in_specs=[
            pl.BlockSpec((seq, width), lambda b, h: (b, col0 + h)),
            pl.BlockSpec((seq, width), lambda b, h: (b, col0 + DA_HEADS + h)),
            pl.BlockSpec((seq, width), lambda b, h: (b, col0 + 2 * DA_HEADS + h)),
            pl.BlockSpec((1, width), lambda b, h: (0, 0)),
            pl.BlockSpec((1, width), lambda b, h: (0, 0)),
            pl.BlockSpec((4, DA_DH), lambda b, h: (0, 0)),
            pl.BlockSpec((1, width), lambda b, h: (0, 0)),
        ],
        out_specs=pl.BlockSpec((seq, width), lambda b, h: (b, h)),
        out_shape=jax.ShapeDtypeStruct((t, DA_HEADS * width), BF16),
        scratch_shapes=[pltpu.VMEM((seq, width), BF16), pltpu.VMEM((seq, 2 * width), BF16)],
        compiler_params=pltpu.CompilerParams(
            dimension_semantics=("arbitrary", "arbitrary"), vmem_limit_bytes=VMEM_LIMIT),
        name="attn",
    )(proj, proj, proj, qg2, kg2, lam4, subln_g)


def _mixout_body(x_ref, oa_ref, od_ref, ga0_ref, ga1_ref, gd0_ref, gd1_ref, wo_ref, g1_ref, g_ref, sh_ref, sc_ref,
                 wr_ref, br_ref,
                 x1_ref, h2_ref, idx_ref, rank_ref, gcol_ref, cnt_ref, carry_scr, wo_scr):
    i = pl.program_id(0)
    tm = x_ref.shape[0]
    hw = oa_ref.shape[1]

    @pl.when(i == 0)
    def _():
        carry_scr[...] = jnp.zeros_like(carry_scr)
        wo_scr[...] = wo_ref[...].astype(BF16)

    ya = _dot(oa_ref[...], wo_scr[0:hw, :])
    yd = _dot(od_ref[...], wo_scr[hw:, :])
    ga = jnp.concatenate([ga0_ref[...], ga1_ref[...]], axis=1).astype(F32)
    gd = jnp.concatenate([gd0_ref[...], gd1_ref[...]], axis=1).astype(F32)
    y = _sigmoid(ga) * ya + _sigmoid(gd) * yd
    x1 = x_ref[...] + g1_ref[0] * y
    x1_ref[...] = x1
    h2 = _norm_mod(x1, g_ref[...], sh_ref[0], sc_ref[0])
    h2_ref[...] = _pack_halves(h2)

    hh, hl = _split_bf16(h2)
    wh, wl = _split_bf16(wr_ref[...])
    logits = _dot_nt(wh, hh) + _dot_nt(wl, hh) + _dot_nt(wh, hl) + br_ref[...]

    e_iota = lax.broadcasted_iota(I32, (N_EXPERTS, tm), 0).astype(F32)
    vals = logits
    tops, sels, idxs = [], [], []
    for _ in range(TOP_K):
        m = jnp.max(vals, axis=0, keepdims=True)
        idx = jnp.min(jnp.where(vals == m, e_iota, float(N_EXPERTS)), axis=0, keepdims=True)
        sel = e_iota == idx
        vals = jnp.where(sel, -jnp.inf, vals)
        tops.append(m)
        sels.append(sel)
        idxs.append(idx)
    ex = [jnp.exp(tv - tops[0]) for tv in tops]
    den = ex[0] + ex[1] + ex[2] + ex[3]
    gates = [v / den for v in ex]

    hot = jnp.where(sels[0] | sels[1] | sels[2] | sels[3], 1.0, 0.0)
    r_t = lax.broadcasted_iota(I32, (tm, tm), 0)
    c_t = lax.broadcasted_iota(I32, (tm, tm), 1)
    upper = jnp.where(r_t < c_t, 1.0, 0.0).astype(BF16)
    excl = _dot(hot.astype(BF16), upper) + carry_scr[:, 0:1]
    carry_scr[...] = carry_scr[...] + jnp.sum(hot, axis=1, keepdims=True)
    cnt_ref[...] = carry_scr[...]

    ranks = [jnp.sum(jnp.where(s, excl, 0.0), axis=0, keepdims=True) for s in sels]
    idx_ref[...] = jnp.concatenate(idxs, axis=0).astype(I32)
    rank_ref[...] = jnp.concatenate(ranks, axis=0).astype(I32)
    gpad = jnp.concatenate(gates + [jnp.zeros((LANES - TOP_K, tm), F32)], axis=0)
    gcol_ref[...] = gpad.T


def _mixout(x2, oa, od, proj, col_g, wo, ada3, ffn_g, wr_t, br_col, seq):
    t, d = x2.shape
    tm = min(1024, seq)
    per_b = seq // tm
    hw = oa.shape[1]
    row = lambda w: pl.BlockSpec((tm, w), lambda i: (i, 0))
    gate = lambda c: pl.BlockSpec((tm, d // 2), lambda i, c=c: (i, col_g + c))
    mod = lambda c: pl.BlockSpec((1, 1, d), lambda i, c=c: (i // per_b, 0, c))
    full = lambda a: pl.BlockSpec(a.shape, lambda i: (0,) * a.ndim)
    return pl.pallas_call(
        _mixout_body,
        grid=(t // tm,),
        in_specs=[
            row(d), row(hw), row(hw),
            gate(0), gate(1), gate(2), gate(3),
            full(wo),
            mod(2),
            full(ffn_g), mod(3), mod(4),
            full(wr_t), full(br_col),
        ],
        out_specs=[
            row(d), row(d // 2),
            pl.BlockSpec((TOP_K, tm), lambda i: (0, i)),
            pl.BlockSpec((TOP_K, tm), lambda i: (0, i)),
            pl.BlockSpec((tm, LANES), lambda i: (i, 0)),
            pl.BlockSpec((N_EXPERTS, LANES), lambda i: (0, 0)),
        ],
        out_shape=[
            jax.ShapeDtypeStruct((t, d), F32),
            jax.ShapeDtypeStruct((t, d // 2), I32),
            jax.ShapeDtypeStruct((TOP_K, t), I32),
            jax.ShapeDtypeStruct((TOP_K, t), I32),
            jax.ShapeDtypeStruct((t, LANES), F32),
            jax.ShapeDtypeStruct((N_EXPERTS, LANES), F32),
        ],
        scratch_shapes=[pltpu.VMEM((N_EXPERTS, LANES), F32), pltpu.VMEM(wo.shape, BF16)],
        compiler_params=pltpu.CompilerParams(
            dimension_semantics=("arbitrary",), vmem_limit_bytes=VMEM_LIMIT),
        name="mixout",
    )(x2, oa, od, proj, proj, proj, proj, wo, ada3, ffn_g, ada3, ada3, wr_t, br_col)


def _sc_mesh():
    return plsc.VectorSubcoreMesh(core_axis_name="c", subcore_axis_name="s")


def _sc_worker_id():
    return lax.axis_index("s") * SC_CORES + lax.axis_index("c")


def _sc_dispatch(h2, dest, n_rows):
    t, d = h2.shape
    assert t % (SC_WORKERS * SC_CHUNK) == 0
    cpw = t // (SC_WORKERS * SC_CHUNK)
    dest_w = dest.reshape(TOP_K, SC_WORKERS, cpw, SC_CHUNK).transpose(1, 0, 2, 3).reshape(-1, SC_CHUNK)
    ipw = TOP_K * cpw

    def body(h2_hbm, dest_hbm, xb_hbm, idx_v, rows_v, sem):
        wid = _sc_worker_id()
        pltpu.sync_copy(dest_hbm.at[pl.ds(pl.multiple_of(wid * ipw, ipw), ipw)], idx_v)

        @pl.loop(0, cpw)
        def _(c):
            t0 = pl.multiple_of((wid * cpw + c) * SC_CHUNK, SC_CHUNK)
            pltpu.sync_copy(h2_hbm.at[pl.ds(t0, SC_CHUNK)], rows_v)
            copies = [pltpu.make_async_copy(rows_v, xb_hbm.at[idx_v.at[k * cpw + c]], sem) for k in range(TOP_K)]
            for cp in copies:
                cp.start()
            for cp in copies:
                cp.wait()

    return pl.kernel(
        body, out_type=jax.ShapeDtypeStruct((n_rows, d), h2.dtype), mesh=_sc_mesh(),
        scratch_types=[pltpu.VMEM((ipw, SC_CHUNK), I32), pltpu.VMEM((SC_CHUNK, d), h2.dtype),
                       pltpu.SemaphoreType.DMA],
        name="dispatch",
    )(h2, dest_w)


def _sc_undispatch(y, dest):
    n_asg = dest.shape[0]
    d = y.shape[1]
    rows = SC_CHUNK // 2
    assert n_asg % (2 * SC_WORKERS * rows) == 0
    cpw = n_asg // (SC_WORKERS * rows)

    def body(y_hbm, dest_hbm, yt_hbm, idx_v, rows_v, gsem, wsem):
        wid = _sc_worker_id()
        pltpu.sync_copy(dest_hbm.at[pl.ds(pl.multiple_of(wid * cpw, cpw), cpw)], idx_v)

        def gather(c, b):
            return pltpu.make_async_copy(y_hbm.at[idx_v.at[c]], rows_v.at[b], gsem.at[b])

        def write(c, b):
            r0 = pl.multiple_of((wid * cpw + c) * rows, rows)
            return pltpu.make_async_copy(rows_v.at[b], yt_hbm.at[pl.ds(r0, rows)], wsem.at[b])

        gather(0, 0).start()

        @pl.loop(0, cpw, step=2)
        def _(c0):
            for b in range(2):
                c = c0 + b

                @pl.when(c + 1 < cpw)
                def _():
                    @pl.when(c >= 1)
                    def _():
                        write(c - 1, 1 - b).wait()
                    gather(c + 1, 1 - b).start()

                gather(c, b).wait()
                write(c, b).start()

        write(cpw - 2, 0).wait()
        write(cpw - 1, 1).wait()

    return pl.kernel(
        body, out_type=jax.ShapeDtypeStruct((n_asg, d), y.dtype), mesh=_sc_mesh(),
        scratch_types=[pltpu.VMEM((cpw, rows), I32), pltpu.VMEM((2, rows, d), y.dtype),
                       pltpu.SemaphoreType.DMA((2,)), pltpu.SemaphoreType.DMA((2,))],
        name="undispatch",
    )(y, dest.reshape(-1, rows))


def _ffn_body(be_ref, first_ref, nxt_ref, slot_ref, nv_ref, nact_ref, x_ref, w1_hbm, b1_ref, w2_hbm, b2_ref, y_ref,
              w1f, w2f, w1c, w2c, sem1, sem2):
    j = pl.program_id(0)
    ff = w2f.shape[1]
    pair = 2 * LANES
    ngroup = (2 * ff) // pair

    def weight_copies(e, slot):
        return (pltpu.make_async_copy(w1_hbm.at[e], w1f.at[slot], sem1.at[slot]),
                pltpu.make_async_copy(w2_hbm.at[e], w2f.at[slot], sem2.at[slot]))

    @pl.when(j == 0)
    def _():
        for cp in weight_copies(be_ref[0], 0):
            cp.start()

    @pl.when(first_ref[j] == 1)
    def _():
        slot = slot_ref[j]
        for cp in weight_copies(be_ref[j], slot):
            cp.wait()

        @pl.when(nxt_ref[j] >= 0)
        def _():
            for cp in weight_copies(nxt_ref[j], 1 - slot):
                cp.start(priority=1)

        r_p = lax.broadcasted_iota(I32, (pair, pair), 0)
        c_p = lax.broadcasted_iota(I32, (pair, pair), 1)
        src = jnp.where(c_p < LANES, 2 * c_p, 2 * (c_p - LANES) + 1)
        perm = jnp.where(r_p == src, 1.0, 0.0).astype(BF16)
        for g in range(ngroup):
            cols = slice(g * pair, (g + 1) * pair)
            w1c[:, cols] = _dot(w1f[slot, :, cols].astype(BF16), perm).astype(BF16)
        w2c[...] = w2f[slot].astype(BF16)

    def expert_rows(nrows):
        x_lo, x_hi = _unpack_halves(x_ref[0:nrows, :])
        xb = jnp.concatenate([x_lo.astype(BF16), x_hi.astype(BF16)], axis=1)
        u = _dot(xb, w1c[...]) + b1_ref[0]
        acts = []
        for g in range(ngroup):
            glu = jnp.minimum(u[:, g * pair:g * pair + LANES], SWIGLU_LIMIT)
            lin = jnp.clip(u[:, g * pair + LANES:(g + 1) * pair], -SWIGLU_LIMIT, SWIGLU_LIMIT)
            acts.append((glu * _sigmoid(SWIGLU_ALPHA * glu) * (lin + 1.0)).astype(BF16))
        act = jnp.concatenate(acts, axis=1)
        y_ref[0:nrows, :] = _pack_halves(_dot(act, w2c[...]) + b2_ref[0])

    active = j < nact_ref[0]
    half = x_ref.shape[0] // 2
    pl.when(active & (nv_ref[j] > half))(functools.partial(expert_rows, x_ref.shape[0]))
    pl.when(active & (nv_ref[j] <= half))(functools.partial(expert_rows, half))


def _ffn(block_expert, first, nxt, slot, nvalid, nact, xb, w1, b1p, w2, b2):
    bm = FFN_BLOCK
    n_rows, dw = xb.shape
    d = 2 * dw
    n_blocks = n_rows // bm
    ff2 = w1.shape[2]
    ff = w2.shape[1]
    row_blk = lambda j, be, fi, nx, sl, nv, na: (jnp.minimum(j, na[0] - 1), 0)
    bias_blk = lambda j, be, fi, nx, sl, nv, na: (be[j], 0, 0)
    grid_spec = pltpu.PrefetchScalarGridSpec(
        num_scalar_prefetch=6,
        grid=(n_blocks,),
        in_specs=[
            pl.BlockSpec((bm, dw), row_blk),
            pl.BlockSpec(memory_space=pl.ANY),
            pl.BlockSpec((1, 1, ff2), bias_blk),
            pl.BlockSpec(memory_space=pl.ANY),
            pl.BlockSpec((1, 1, d), bias_blk),
        ],
        out_specs=pl.BlockSpec((bm, dw), row_blk),
        scratch_shapes=[
            pltpu.VMEM((2, d, ff2), F32), pltpu.VMEM((2, ff, d), F32),
            pltpu.VMEM((d, ff2), BF16), pltpu.VMEM((ff, d), BF16),
            pltpu.SemaphoreType.DMA((2,)), pltpu.SemaphoreType.DMA((2,)),
        ],
    )
    return pl.pallas_call(
        _ffn_body,
        grid_spec=grid_spec,
        out_shape=jax.ShapeDtypeStruct((n_rows, dw), I32),
        compiler_params=pltpu.CompilerParams(
            dimension_semantics=("arbitrary",), vmem_limit_bytes=VMEM_LIMIT),
        name="ffn",
    )(block_expert, first, nxt, slot, nvalid, nact, xb, w1, b1p, w2, b2)


def _combine_body(x1_ref, y0_ref, y1_ref, y2_ref, y3_ref, gcol_ref, g2_ref, o_ref):
    gc = gcol_ref[...]
    m_lo = m_hi = None
    for k, y_ref in enumerate((y0_ref, y1_ref, y2_ref, y3_ref)):
        lo, hi = _unpack_halves(y_ref[...])
        gk = gc[:, k:k + 1]
        m_lo = gk * lo if m_lo is None else m_lo + gk * lo
        m_hi = gk * hi if m_hi is None else m_hi + gk * hi
    m = jnp.concatenate([m_lo, m_hi], axis=1)
    o_ref[...] = (x1_ref[...] + g2_ref[0] * m).astype(o_ref.dtype)


def _combine(x1, yall, gcol, ada3, seq, out_dtype):
    t, d = x1.shape
    tm = min(1024, seq)
    per_b = seq // tm
    nt = t // tm
    yk = lambda k: pl.BlockSpec((tm, d // 2), lambda i, k=k: (k * nt + i, 0))
    return pl.pallas_call(
        _combine_body,
        grid=(nt,),
        in_specs=[
            pl.BlockSpec((tm, d), lambda i: (i, 0)),
            yk(0), yk(1), yk(2), yk(3),
            pl.BlockSpec((tm, LANES), lambda i: (i, 0)),
            pl.BlockSpec((1, 1, d), lambda i: (i // per_b, 0, 5)),
        ],
        out_specs=pl.BlockSpec((tm, d), lambda i: (i, 0)),
        out_shape=jax.ShapeDtypeStruct((t, d), out_dtype),
        compiler_params=pltpu.CompilerParams(
            dimension_semantics=("arbitrary",), vmem_limit_bytes=VMEM_LIMIT),
        name="combine",
    )(x1, yall, yall, yall, yall, gcol, ada3)


def _route_tables(idx, rank, counts, n_tok):
    bm = FFN_BLOCK
    n_asg = TOP_K * n_tok
    n_blocks = -(-(n_asg + N_EXPERTS * (bm - 1)) // bm)
    padded = (counts + bm - 1) // bm * bm
    pad_ends = jnp.cumsum(padded)
    pad_starts = pad_ends - padded
    e_ids = jnp.arange(N_EXPERTS, dtype=I32)
    start_of = jnp.sum(jnp.where(idx[None] == e_ids[:, None, None], pad_starts[:, None, None], 0), axis=0)
    dest = (start_of + rank).reshape(-1)
    nact = (pad_ends[-1] // bm).astype(I32)
    blk_start = jnp.arange(n_blocks, dtype=I32) * bm
    last = jnp.sum(jnp.where(pad_ends <= pad_ends[-1] - 1, 1, 0)).astype(I32)
    be = jnp.sum(jnp.where(pad_ends[None, :] <= blk_start[:, None], 1, 0), axis=1).astype(I32)
    active = blk_start < pad_ends[-1]
    be = jnp.where(active, be, last)
    blk = jnp.arange(n_blocks, dtype=I32)
    first = active & ((blk == 0) | (be != jnp.roll(be, 1)))
    slot = (jnp.cumsum(first.astype(I32)) - 1) & 1
    later_first = first[None, :] & (blk[None, :] > blk[:, None])
    nxt_pos = jnp.min(jnp.where(later_first, blk[None, :], n_blocks), axis=1)
    nxt = jnp.sum(jnp.where(blk[None, :] == nxt_pos[:, None], be[None, :], 0), axis=1)
    nxt = jnp.where(nxt_pos < n_blocks, nxt, -1).astype(I32)
    mine = be[:, None] == e_ids[None, :]
    cnt_b = jnp.sum(jnp.where(mine, counts[None, :], 0), axis=1)
    start_b = jnp.sum(jnp.where(mine, pad_starts[None, :], 0), axis=1)
    nvalid = jnp.where(active, jnp.clip(cnt_b - (blk_start - start_b), 0, bm), 0).astype(I32)
    return be, first.astype(I32), nxt, slot.astype(I32), nvalid, nact.reshape(1), dest, n_blocks * bm


def kernel(x, c, w_ada, b_ada, mix_norm_g, ffn_norm_g, w_in, hg_lower_bound_logits, hg_out_norm_g, da_q_norm_g, da_k_norm_g, da_lambda_q1, da_lambda_k1, da_lambda_q2, da_lambda_k2, da_subln_g, w_out, w_router, b_router, w1, b1, w2, b2):
    bsz, seq, d = x.shape
    t = bsz * seq
    depth = w_ada.shape[0]
    out_dtype = x.dtype
    hw = HG_HEADS * HG_DV
    xcur = x.reshape(t, d)
    for l in range(depth):
        ada = _ada(c, w_ada[l], b_ada[l])
        ada3 = ada.reshape(bsz, 1, N_MOD * d)
        col_h = 0
        col_a = col_h + 4 * HG_HEADS
        col_g = (4 * hw + 3 * DA_HEADS * 2 * DA_DH) // (d // 2)
        proj = _inproj(xcur, mix_norm_g[l].reshape(1, d), ada3, w_in[l].astype(BF16), seq)

        o_a = _hgrn(proj, hg_lower_bound_logits, hg_out_norm_g[l].reshape(1, HG_DV), bsz, seq, col_h, l)
        lambda_init = 0.8 - 0.6 * math.exp(-0.3 * l)
        qg2 = jnp.tile(da_q_norm_g[l], 2).reshape(1, 2 * DA_DH)
        kg2 = jnp.tile(da_k_norm_g[l], 2).reshape(1, 2 * DA_DH)
        lam4 = jnp.stack([da_lambda_q1[l], da_lambda_k1[l], da_lambda_q2[l], da_lambda_k2[l]])
        o_d = _attn(proj, qg2, kg2, lam4, da_subln_g[l].reshape(1, 2 * DA_DH), bsz, seq, col_a, lambda_init)

        x1, h2, idx, rank, gcol, cnt = _mixout(
            xcur, o_a, o_d, proj, col_g, w_out[l], ada3, ffn_norm_g[l].reshape(1, d),
            w_router[l].T, b_router[l].reshape(N_EXPERTS, 1), seq)

        counts = cnt[:, 0].astype(I32)
        be, first, nxt, slot, nvalid, nact, dest, n_rows = _route_tables(idx, rank, counts, t)
        b1p = b1[l].reshape(N_EXPERTS, -1, LANES, 2).transpose(0, 1, 3, 2).reshape(N_EXPERTS, 1, -1)
        xb = _sc_dispatch(h2, dest, n_rows)
        yb = _ffn(be, first, nxt, slot, nvalid, nact, xb, w1[l], b1p, w2[l], b2[l].reshape(N_EXPERTS, 1, d))
        yall = _sc_undispatch(yb, dest)
        xcur = _combine(x1, yall, gcol, ada3, seq, out_dtype)
    return xcur.reshape(bsz, seq, d)
```

```python
import functools
import math

import jax
import jax.numpy as jnp
from jax import lax
from jax.experimental import pallas as pl
from jax.experimental.pallas import tpu as pltpu
from jax.experimental.pallas import tpu_sc as plsc

F32 = jnp.float32
BF16 = jnp.bfloat16
I32 = jnp.int32

HG_HEADS = 4
HG_DK = 128
HG_DV = 128
HG_CHUNK = 32
DA_HEADS = 4
DA_DH = 64
N_EXPERTS = 32
TOP_K = 4
SWIGLU_ALPHA = 1.702
SWIGLU_LIMIT = 7.0
NORM_EPS = 1e-6
LOG2E = math.log2(math.e)
N_MOD = 6

LANES = 128
VMEM_LIMIT = 56 * 1024 * 1024

HG_ROWS = 256
ATTN_TILE = 256
FFN_BLOCK = 512

SC_CORES = 2
SC_WORKERS = SC_CORES * 16
SC_CHUNK = 128


def _sigmoid(x):
    return 0.5 * jnp.tanh(0.5 * x) + 0.5


def _dot(a, b):
    return jnp.dot(a, b, preferred_element_type=F32)


def _dot_nt(a, b):
    return lax.dot_general(a, b, (((1,), (1,)), ((), ())), preferred_element_type=F32)


def _split_bf16(x):
    hi = x.astype(BF16)
    lo = (x - hi.astype(F32)).astype(BF16)
    return hi, lo


def _pack_halves(x):
    return _pack_rounded(x.astype(BF16).astype(F32))


def _pack_rounded(xr):
    half = xr.shape[1] // 2
    lo = lax.bitcast_convert_type(xr[:, :half], I32)
    hi = lax.bitcast_convert_type(xr[:, half:], I32)
    return lax.shift_right_logical(lo, 16) | hi


def _unpack_halves(w):
    lo = lax.bitcast_convert_type(lax.shift_left(w, 16), F32)
    hi = lax.bitcast_convert_type(w & jnp.int32(-65536), F32)
    return lo, hi


def _ada_body(c_ref, w_ref, b_ref, o_ref):
    c = c_ref[...].astype(F32)
    ch, cl = _split_bf16(c * _sigmoid(c))
    wh, wl = _split_bf16(w_ref[...])
    o_ref[...] = _dot(ch, wh) + _dot(cl, wh) + _dot(ch, wl) + b_ref[...]


def _ada(c, w_ada, b_ada):
    bsz, d = c.shape
    n = w_ada.shape[1]
    tn = 2 * d
    return pl.pallas_call(
        _ada_body,
        grid=(n // tn,),
        in_specs=[
            pl.BlockSpec((bsz, d), lambda j: (0, 0)),
            pl.BlockSpec((d, tn), lambda j: (0, j)),
            pl.BlockSpec((1, tn), lambda j: (0, j)),
        ],
        out_specs=pl.BlockSpec((bsz, tn), lambda j: (0, j)),
        out_shape=jax.ShapeDtypeStruct((bsz, n), F32),
        name="ada",
    )(c, w_ada, b_ada.reshape(1, n))


def _norm_mod(x, g, shift, scale):
    ms = jnp.mean(x * x, axis=-1, keepdims=True)
    return x * lax.rsqrt(ms + NORM_EPS) * (g * (1.0 + scale)) + shift


def _inproj_body(x_ref, g_ref, sh_ref, sc_ref, w_ref, o_ref):
    h = _norm_mod(x_ref[...], g_ref[...], sh_ref[0], sc_ref[0])
    o_ref[...] = _dot(h.astype(BF16), w_ref[...]).astype(BF16)


def _inproj(x2, g, ada3, w_bf16, seq):
    t, d = x2.shape
    n = w_bf16.shape[1]
    tm = min(512, seq)
    nj = 1
    tn = n // nj
    per_b = seq // tm
    return pl.pallas_call(
        _inproj_body,
        grid=(nj, t // tm),
        in_specs=[
            pl.BlockSpec((tm, d), lambda j, i: (i, 0)),
            pl.BlockSpec((1, d), lambda j, i: (0, 0)),
            pl.BlockSpec((1, 1, d), lambda j, i: (i // per_b, 0, 0)),
            pl.BlockSpec((1, 1, d), lambda j, i: (i // per_b, 0, 1)),
            pl.BlockSpec((d, tn), lambda j, i: (0, j)),
        ],
        out_specs=pl.BlockSpec((tm, tn), lambda j, i: (i, j)),
        out_shape=jax.ShapeDtypeStruct((t, n), BF16),
        compiler_params=pltpu.CompilerParams(
            dimension_semantics=("arbitrary", "arbitrary"), vmem_limit_bytes=VMEM_LIMIT),
        name="inproj",
    )(x2, g, ada3, ada3, w_bf16)


def _hgrn_body(q_ref, f_ref, i_ref, og_ref, lbl_ref, g_ref, o_ref, *, seq, layer):
    rows, chunk = HG_ROWS, HG_CHUNK
    nchunk = rows // chunk
    lbl = lbl_ref[...].astype(F32)
    e = jnp.exp(lbl - jnp.max(lbl, axis=0, keepdims=True))
    lb = jnp.sum(e[: layer + 1], axis=0, keepdims=True) / jnp.sum(e, axis=0, keepdims=True)
    r_i = lax.broadcasted_iota(I32, (rows, rows), 0)
    c_i = lax.broadcasted_iota(I32, (rows, rows), 1)
    tri = ((r_i // chunk) == (c_i // chunk)) & (r_i >= c_i)
    tri_b = jnp.where(tri, 1.0, 0.0).astype(BF16)
    row_chunk = lax.broadcasted_iota(I32, (rows, HG_DK), 0) // chunk
    g = g_ref[...].astype(F32)

    def block(r, st):
        sl = pl.ds(r * rows, rows)
        qr = q_ref[sl, :].astype(F32)
        fr = f_ref[sl, :].astype(F32)
        v = i_ref[sl, :].astype(F32)
        og = og_ref[sl, :].astype(F32)
        q = qr * _sigmoid(qr)
        f = lb + (1.0 - lb) * _sigmoid(fr)
        k = 1.0 - f
        logf = jnp.log(f)
        lhi, llo = _split_bf16(logf)
        bc2 = _dot(tri_b, jnp.concatenate([lhi, llo], axis=1))
        bcum = bc2[:, :HG_DK] + bc2[:, HG_DK:]
        b3 = bcum.reshape(nchunk, chunk, HG_DK)
        bl = b3[:, chunk - 1:chunk, :]
        dec = jnp.exp(bl)
        kt_f = k * jnp.exp(-bcum)
        qt_f = q * jnp.exp(bcum)
        kd = (kt_f.reshape(nchunk, chunk, HG_DK) * dec).reshape(rows, HG_DK)
        a = jnp.where(tri, _dot_nt(qt_f.astype(BF16), kt_f.astype(BF16)), 0.0).astype(BF16)
        vt_b = v.T.astype(BF16)
        kd_x = jnp.concatenate([jnp.where(row_chunk == c, kd, 0.0) for c in range(nchunk)], axis=1)
        kv_all = _dot(vt_b, kd_x.astype(BF16))
        starts = []
        for c in range(nchunk):
            starts.append(st.astype(BF16))
            st = st * dec[c] + kv_all[:, c * HG_DK:(c + 1) * HG_DK]
        q_x = [jnp.where(row_chunk == c, qt_f, 0.0).astype(BF16) for c in range(nchunk)]
        o = _dot_nt(jnp.concatenate([a] + q_x, axis=1), jnp.concatenate([vt_b] + starts, axis=1))
        ms = jnp.mean(o * o, axis=-1, keepdims=True)
        o = o * lax.rsqrt(ms + NORM_EPS) * g
        o_ref[sl, :] = (o * (og * _sigmoid(og))).astype(o_ref.dtype)
        return st

    st = jnp.zeros((HG_DV, HG_DK), F32)
    for r in range(seq // rows):
        st = block(r, st)


def _hgrn(proj, lb_logits, norm_g, bsz, seq, col0, layer):
    t = proj.shape[0]
    blk = lambda off: pl.BlockSpec((seq, LANES), lambda b, h, off=off: (b, col0 + off + h))
    return pl.pallas_call(
        functools.partial(_hgrn_body, seq=seq, layer=layer),
        grid=(bsz, HG_HEADS),
        in_specs=[
            blk(0), blk(HG_HEADS), blk(2 * HG_HEADS), blk(3 * HG_HEADS),
            pl.BlockSpec((lb_logits.shape[0], HG_DK), lambda b, h: (0, h)),
            pl.BlockSpec((1, HG_DV), lambda b, h: (0, 0)),
        ],
        out_specs=pl.BlockSpec((seq, HG_DV), lambda b, h: (b, h)),
        out_shape=jax.ShapeDtypeStruct((t, HG_HEADS * HG_DV), BF16),
        compiler_params=pltpu.CompilerParams(
            dimension_semantics=("arbitrary", "arbitrary"), vmem_limit_bytes=VMEM_LIMIT),
        name="hgrn",
    )(proj, proj, proj, proj, lb_logits, norm_g)


def _group_norm(x, gsum_b, gain):
    ss = _dot((x * x).astype(BF16), gsum_b)
    return x * lax.rsqrt(ss * (1.0 / DA_DH) + NORM_EPS) * gain


def _attn_body(q_ref, k_ref, v_ref, qg_ref, kg_ref, lam_ref, sg_ref, o_ref, kn_scr, v1_scr, *, seq, lambda_init):
    tq = ATTN_TILE
    width = 2 * DA_DH
    r_l = lax.broadcasted_iota(I32, (width, width), 0) // DA_DH
    c_l = lax.broadcasted_iota(I32, (width, width), 1) // DA_DH
    gsum_b = jnp.where(r_l == c_l, 1.0, 0.0).astype(BF16)
    lane = lax.broadcasted_iota(I32, (1, width), 1)
    kg = kg_ref[...].astype(F32)
    qg = qg_ref[...].astype(F32) * (DA_DH ** -0.5 * LOG2E)
    sg = sg_ref[...].astype(F32) * (1.0 - lambda_init)
    ones = jnp.ones((tq, width), BF16)
    row = lax.broadcasted_iota(I32, (tq, tq), 0)
    col = lax.broadcasted_iota(I32, (tq, tq), 1)
    keep = row >= col

    lam_v = lam_ref[...].astype(F32)
    lam = (jnp.exp(jnp.sum(lam_v[0:1] * lam_v[1:2], axis=-1, keepdims=True))
           - jnp.exp(jnp.sum(lam_v[2:3] * lam_v[3:4], axis=-1, keepdims=True)) + lambda_init)

    def softmax_v(qc, nk):
        s = _dot_nt(qc, kn_scr[0:nk, :])
        diag = jnp.where(keep, s[:, nk - tq:], -jnp.inf)
        s = diag if nk == tq else jnp.concatenate([s[:, :nk - tq], diag], axis=1)
        m = jnp.max(s, axis=-1, keepdims=True)
        return _dot(jnp.exp2(s - m).astype(BF16), v1_scr[0:nk, :])

    for i in range(seq // tq):
        sl = slice(i * tq, (i + 1) * tq)
        kn_scr[sl, :] = _group_norm(k_ref[sl, :].astype(F32), gsum_b, kg).astype(BF16)
        v1_scr[sl, :] = jnp.concatenate([v_ref[sl, :], ones], axis=1)
        qn = _group_norm(q_ref[sl, :].astype(F32), gsum_b, qg)
        nk = (i + 1) * tq
        a1 = softmax_v(jnp.where(lane < DA_DH, qn, 0.0).astype(BF16), nk)
        a2 = softmax_v(jnp.where(lane >= DA_DH, qn, 0.0).astype(BF16), nk)
        o = a1[:, :width] / a1[:, width:width + 1] - lam * (a2[:, :width] / a2[:, width:width + 1])
        ms = jnp.mean(o * o, axis=-1, keepdims=True)
        o_ref[sl, :] = (o * lax.rsqrt(ms + NORM_EPS) * sg).astype(o_ref.dtype)


def _attn(proj, qg2, kg2, lam4, subln_g, bsz, seq, col0, lambda_init):
    t = proj.shape[0]
    width = 2 * DA_DH
    return pl.pallas_call(
        functools.partial(_attn_body, seq=seq, lambda_init=lambda_init),
        grid=(bsz, DA_HEADS),
        in_specs=[
            pl.BlockSpec((seq, width), lambda b, h: (b, col0 + h)),
            pl.BlockSpec((seq, width), lambda b, h: (b, col0 + DA_HEADS + h)),
            pl.BlockSpec((seq, width), lambda b, h: (b, col0 + 2 * DA_HEADS + h)),
            pl.BlockSpec((1, width), lambda b, h: (0, 0)),
            pl.BlockSpec((1, width), lambda b, h: (0, 0)),
            pl.BlockSpec((4, DA_DH), lambda b, h: (0, 0)),
            pl.BlockSpec((1, width), lambda b, h: (0, 0)),
        ],
        out_specs=pl.BlockSpec((seq, width), lambda b, h: (b, h)),
        out_shape=jax.ShapeDtypeStruct((t, DA_HEADS * width), BF16),
        scratch_shapes=[pltpu.VMEM((seq, width), BF16), pltpu.VMEM((seq, 2 * width), BF16)],
        compiler_params=pltpu.CompilerParams(
            dimension_semantics=("arbitrary", "arbitrary"), vmem_limit_bytes=VMEM_LIMIT),
        name="attn",
    )(proj, proj, proj, qg2, kg2, lam4, subln_g)


def _mixout_body(x_ref, oa_ref, od_ref, ga0_ref, ga1_ref, gd0_ref, gd1_ref, wo_ref, g1_ref, g_ref, sh_ref, sc_ref,
                 wr_ref, br_ref,
                 x1_ref, h2_ref, idx_ref, rank_ref, gcol_ref, cnt_ref, carry_scr, wo_scr):
    i = pl.program_id(0)
    tm = x_ref.shape[0]
    hw = oa_ref.shape[1]

    @pl.when(i == 0)
    def _():
        carry_scr[...] = jnp.zeros_like(carry_scr)
        wo_scr[...] = wo_ref[...].astype(BF16)

    ya = _dot(oa_ref[...], wo_scr[0:hw, :])
    yd = _dot(od_ref[...], wo_scr[hw:, :])
    ga = jnp.concatenate([ga0_ref[...], ga1_ref[...]], axis=1).astype(F32)
    gd = jnp.concatenate([gd0_ref[...], gd1_ref[...]], axis=1).astype(F32)
    y = _sigmoid(ga) * ya + _sigmoid(gd) * yd
    x1 = x_ref[...] + g1_ref[0] * y
    x1_ref[...] = x1
    h2 = _norm_mod(x1, g_ref[...], sh_ref[0], sc_ref[0])
    hh = h2.astype(BF16)
    h2r = hh.astype(F32)
    h2_ref[...] = _pack_rounded(h2r)

    hl = (h2 - h2r).astype(BF16)
    wh, wl = _split_bf16(wr_ref[...])
    logits = _dot_nt(wh, hh) + _dot_nt(wl, hh) + _dot_nt(wh, hl) + br_ref[...]

    e_iota = lax.broadcasted_iota(I32, (N_EXPERTS, tm), 0).astype(F32)
    vals = logits
    tops, sels, idxs = [], [], []
    for _ in range(TOP_K):
        m = jnp.max(vals, axis=0, keepdims=True)
        idx = jnp.min(jnp.where(vals == m, e_iota, float(N_EXPERTS)), axis=0, keepdims=True)
        sel = e_iota == idx
        vals = jnp.where(sel, -jnp.inf, vals)
        tops.append(m)
        sels.append(sel)
        idxs.append(idx)
    ex = [jnp.exp(tv - tops[0]) for tv in tops]
    den = ex[0] + ex[1] + ex[2] + ex[3]
    gates = [v / den for v in ex]

    hot = jnp.where(sels[0] | sels[1] | sels[2] | sels[3], 1.0, 0.0)
    r_t = lax.broadcasted_iota(I32, (tm, tm), 0)
    c_t = lax.broadcasted_iota(I32, (tm, tm), 1)
    upper = jnp.where(r_t < c_t, 1.0, 0.0).astype(BF16)
    excl = _dot(hot.astype(BF16), upper) + carry_scr[:, 0:1]
    carry_scr[...] = carry_scr[...] + jnp.sum(hot, axis=1, keepdims=True)
    cnt_ref[...] = carry_scr[...]

    ranks = [jnp.sum(jnp.where(s, excl, 0.0), axis=0, keepdims=True) for s in sels]
    idx_ref[...] = jnp.concatenate(idxs, axis=0).astype(I32)
    rank_ref[...] = jnp.concatenate(ranks, axis=0).astype(I32)
    gpad = jnp.concatenate(gates + [jnp.zeros((LANES - TOP_K, tm), F32)], axis=0)
    gcol_ref[...] = gpad.T


def _mixout(x2, oa, od, proj, col_g, wo, ada3, ffn_g, wr_t, br_col, seq):
    t, d = x2.shape
    tm = min(1024, seq)
    per_b = seq // tm
    hw = oa.shape[1]
    row = lambda w: pl.BlockSpec((tm, w), lambda i: (i, 0))
    gate = lambda c: pl.BlockSpec((tm, d // 2), lambda i, c=c: (i, col_g + c))
    mod = lambda c: pl.BlockSpec((1, 1, d), lambda i, c=c: (i // per_b, 0, c))
    full = lambda a: pl.BlockSpec(a.shape, lambda i: (0,) * a.ndim)
    return pl.pallas_call(
        _mixout_body,
        grid=(t // tm,),
        in_specs=[
            row(d), row(hw), row(hw),
            gate(0), gate(1), gate(2), gate(3),
            full(wo),
            mod(2),
            full(ffn_g), mod(3), mod(4),
            full(wr_t), full(br_col),
        ],
        out_specs=[
            row(d), row(d // 2),
            pl.BlockSpec((TOP_K, tm), lambda i: (0, i)),
            pl.BlockSpec((TOP_K, tm), lambda i: (0, i)),
            pl.BlockSpec((tm, LANES), lambda i: (i, 0)),
            pl.BlockSpec((N_EXPERTS, LANES), lambda i: (0, 0)),
        ],
        out_shape=[
            jax.ShapeDtypeStruct((t, d), F32),
            jax.ShapeDtypeStruct((t, d // 2), I32),
            jax.ShapeDtypeStruct((TOP_K, t), I32),
            jax.ShapeDtypeStruct((TOP_K, t), I32),
            jax.ShapeDtypeStruct((t, LANES), F32),
            jax.ShapeDtypeStruct((N_EXPERTS, LANES), F32),
        ],
        scratch_shapes=[pltpu.VMEM((N_EXPERTS, LANES), F32), pltpu.VMEM(wo.shape, BF16)],
        compiler_params=pltpu.CompilerParams(
            dimension_semantics=("arbitrary",), vmem_limit_bytes=VMEM_LIMIT),
        name="mixout",
    )(x2, oa, od, proj, proj, proj, proj, wo, ada3, ffn_g, ada3, ada3, wr_t, br_col)


def _sc_mesh():
    return plsc.VectorSubcoreMesh(core_axis_name="c", subcore_axis_name="s")


def _sc_worker_id():
    return lax.axis_index("s") * SC_CORES + lax.axis_index("c")


def _sc_dispatch(h2, dest, n_rows):
    t, d = h2.shape
    assert t % (SC_WORKERS * SC_CHUNK) == 0
    cpw = t // (SC_WORKERS * SC_CHUNK)
    dest_w = dest.reshape(TOP_K, SC_WORKERS, cpw, SC_CHUNK).transpose(1, 0, 2, 3).reshape(-1, SC_CHUNK)
    ipw = TOP_K * cpw

    def body(h2_hbm, dest_hbm, xb_hbm, idx_v, rows_v):
        wid = _sc_worker_id()
        pltpu.sync_copy(dest_hbm.at[pl.ds(pl.multiple_of(wid * ipw, ipw), ipw)], idx_v)

        @pl.loop(0, cpw)
        def _(c):
            t0 = pl.multiple_of((wid * cpw + c) * SC_CHUNK, SC_CHUNK)
            pltpu.sync_copy(h2_hbm.at[pl.ds(t0, SC_CHUNK)], rows_v)
            for k in range(TOP_K):
                pltpu.sync_copy(rows_v, xb_hbm.at[idx_v.at[k * cpw + c]])

    return pl.kernel(
        body, out_type=jax.ShapeDtypeStruct((n_rows, d), h2.dtype), mesh=_sc_mesh(),
        scratch_types=[pltpu.VMEM((ipw, SC_CHUNK), I32), pltpu.VMEM((SC_CHUNK, d), h2.dtype)],
        name="dispatch",
    )(h2, dest_w)


def _sc_undispatch(y, dest):
    n_asg = dest.shape[0]
    d = y.shape[1]
    rows = SC_CHUNK // 2
    assert n_asg % (2 * SC_WORKERS * rows) == 0
    cpw = n_asg // (SC_WORKERS * rows)

    def body(y_hbm, dest_hbm, yt_hbm, idx_v, rows_v, gsem, wsem):
        wid = _sc_worker_id()
        pltpu.sync_copy(dest_hbm.at[pl.ds(pl.multiple_of(wid * cpw, cpw), cpw)], idx_v)

        def gather(c, b):
            return pltpu.make_async_copy(y_hbm.at[idx_v.at[c]], rows_v.at[b], gsem.at[b])

        def write(c, b):
            r0 = pl.multiple_of((wid * cpw + c) * rows, rows)
            return pltpu.make_async_copy(rows_v.at[b], yt_hbm.at[pl.ds(r0, rows)], wsem.at[b])

        gather(0, 0).start()

        @pl.loop(0, cpw, step=2)
        def _(c0):
            for b in range(2):
                c = c0 + b

                @pl.when(c + 1 < cpw)
                def _():
                    @pl.when(c >= 1)
                    def _():
                        write(c - 1, 1 - b).wait()
                    gather(c + 1, 1 - b).start()

                gather(c, b).wait()
                write(c, b).start()

        write(cpw - 2, 0).wait()
        write(cpw - 1, 1).wait()

    return pl.kernel(
        body, out_type=jax.ShapeDtypeStruct((n_asg, d), y.dtype), mesh=_sc_mesh(),
        scratch_types=[pltpu.VMEM((cpw, rows), I32), pltpu.VMEM((2, rows, d), y.dtype),
                       pltpu.SemaphoreType.DMA((2,)), pltpu.SemaphoreType.DMA((2,))],
        name="undispatch",
    )(y, dest.reshape(-1, rows))


def _ffn_body(be_ref, first_ref, nxt_ref, slot_ref, nv_ref, nact_ref, x_ref, w1_hbm, b1_ref, w2_hbm, b2_ref, y_ref,
              w1f, w2f, w1c, w2c, sem1, sem2):
    j = pl.program_id(0)
    ff = w2f.shape[1]
    pair = 2 * LANES
    ngroup = (2 * ff) // pair

    def weight_copies(e, slot):
        return (pltpu.make_async_copy(w1_hbm.at[e], w1f.at[slot], sem1.at[slot]),
                pltpu.make_async_copy(w2_hbm.at[e], w2f.at[slot], sem2.at[slot]))

    @pl.when(j == 0)
    def _():
        for cp in weight_copies(be_ref[0], 0):
            cp.start()

    @pl.when(first_ref[j] == 1)
    def _():
        slot = slot_ref[j]
        for cp in weight_copies(be_ref[j], slot):
            cp.wait()

        @pl.when(nxt_ref[j] >= 0)
        def _():
            for cp in weight_copies(nxt_ref[j], 1 - slot):
                cp.start(priority=1)

        r_p = lax.broadcasted_iota(I32, (pair, pair), 0)
        c_p = lax.broadcasted_iota(I32, (pair, pair), 1)
        src = jnp.where(c_p < LANES, 2 * c_p, 2 * (c_p - LANES) + 1)
        perm = jnp.where(r_p == src, 1.0, 0.0).astype(BF16)
        for g in range(ngroup):
            cols = slice(g * pair, (g + 1) * pair)
            w1c[:, cols] = _dot(w1f[slot, :, cols].astype(BF16), perm).astype(BF16)
        w2c[...] = w2f[slot].astype(BF16)

    def expert_rows(nrows):
        x_lo, x_hi = _unpack_halves(x_ref[0:nrows, :])
        xb = jnp.concatenate([x_lo.astype(BF16), x_hi.astype(BF16)], axis=1)
        u = _dot(xb, w1c[...]) + b1_ref[0]
        acts = []
        for g in range(ngroup):
            glu = jnp.minimum(u[:, g * pair:g * pair + LANES], SWIGLU_LIMIT)
            lin = jnp.clip(u[:, g * pair + LANES:(g + 1) * pair], -SWIGLU_LIMIT, SWIGLU_LIMIT)
            acts.append((glu * _sigmoid(SWIGLU_ALPHA * glu) * (lin + 1.0)).astype(BF16))
        act = jnp.concatenate(acts, axis=1)
        y_ref[0:nrows, :] = _pack_halves(_dot(act, w2c[...]) + b2_ref[0])

    active = j < nact_ref[0]
    half = x_ref.shape[0] // 2
    pl.when(active & (nv_ref[j] > half))(functools.partial(expert_rows, x_ref.shape[0]))
    pl.when(active & (nv_ref[j] <= half))(functools.partial(expert_rows, half))


def _ffn(block_expert, first, nxt, slot, nvalid, nact, xb, w1, b1p, w2, b2):
    bm = FFN_BLOCK
    n_rows, dw = xb.shape
    d = 2 * dw
    n_blocks = n_rows // bm
    ff2 = w1.shape[2]
    ff = w2.shape[1]
    row_blk = lambda j, be, fi, nx, sl, nv, na: (jnp.minimum(j, na[0] - 1), 0)
    bias_blk = lambda j, be, fi, nx, sl, nv, na: (be[j], 0, 0)
    grid_spec = pltpu.PrefetchScalarGridSpec(
        num_scalar_prefetch=6,
        grid=(n_blocks,),
        in_specs=[
            pl.BlockSpec((bm, dw), row_blk),
            pl.BlockSpec(memory_space=pl.ANY),
            pl.BlockSpec((1, 1, ff2), bias_blk),
            pl.BlockSpec(memory_space=pl.ANY),
            pl.BlockSpec((1, 1, d), bias_blk),
        ],
        out_specs=pl.BlockSpec((bm, dw), row_blk),
        scratch_shapes=[
            pltpu.VMEM((2, d, ff2), F32), pltpu.VMEM((2, ff, d), F32),
            pltpu.VMEM((d, ff2), BF16), pltpu.VMEM((ff, d), BF16),
            pltpu.SemaphoreType.DMA((2,)), pltpu.SemaphoreType.DMA((2,)),
        ],
    )
    return pl.pallas_call(
        _ffn_body,
        grid_spec=grid_spec,
        out_shape=jax.ShapeDtypeStruct((n_rows, dw), I32),
        compiler_params=pltpu.CompilerParams(
            dimension_semantics=("arbitrary",), vmem_limit_bytes=VMEM_LIMIT),
        name="ffn",
    )(block_expert, first, nxt, slot, nvalid, nact, xb, w1, b1p, w2, b2)


def _combine_body(x1_ref, y0_ref, y1_ref, y2_ref, y3_ref, gcol_ref, g2_ref, o_ref):
    gc = gcol_ref[...]
    m_lo = m_hi = None
    for k, y_ref in enumerate((y0_ref, y1_ref, y2_ref, y3_ref)):
        lo, hi = _unpack_halves(y_ref[...])
        gk = gc[:, k:k + 1]
        m_lo = gk * lo if m_lo is None else m_lo + gk * lo
        m_hi = gk * hi if m_hi is None else m_hi + gk * hi
    m = jnp.concatenate([m_lo, m_hi], axis=1)
    o_ref[...] = (x1_ref[...] + g2_ref[0] * m).astype(o_ref.dtype)


def _combine(x1, yall, gcol, ada3, seq, out_dtype):
    t, d = x1.shape
    tm = min(1024, seq)
    per_b = seq // tm
    nt = t // tm
    yk = lambda k: pl.BlockSpec((tm, d // 2), lambda i, k=k: (k * nt + i, 0))
    return pl.pallas_call(
        _combine_body,
        grid=(nt,),
        in_specs=[
            pl.BlockSpec((tm, d), lambda i: (i, 0)),
            yk(0), yk(1), yk(2), yk(3),
            pl.BlockSpec((tm, LANES), lambda i: (i, 0)),
            pl.BlockSpec((1, 1, d), lambda i: (i // per_b, 0, 5)),
        ],
        out_specs=pl.BlockSpec((tm, d), lambda i: (i, 0)),
        out_shape=jax.ShapeDtypeStruct((t, d), out_dtype),
        compiler_params=pltpu.CompilerParams(
            dimension_semantics=("arbitrary",), vmem_limit_bytes=VMEM_LIMIT),
        name="combine",
    )(x1, yall, yall, yall, yall, gcol, ada3)


def _route_tables(idx, rank, counts, n_tok):
    bm = FFN_BLOCK
    n_asg = TOP_K * n_tok
    n_blocks = -(-(n_asg + N_EXPERTS * (bm - 1)) // bm)
    padded = (counts + bm - 1) // bm * bm
    pad_ends = jnp.cumsum(padded)
    pad_starts = pad_ends - padded
    e_ids = jnp.arange(N_EXPERTS, dtype=I32)
    start_of = jnp.sum(jnp.where(idx[None] == e_ids[:, None, None], pad_starts[:, None, None], 0), axis=0)
    dest = (start_of + rank).reshape(-1)
    nact = (pad_ends[-1] // bm).astype(I32)
    blk_start = jnp.arange(n_blocks, dtype=I32) * bm
    last = jnp.sum(jnp.where(pad_ends <= pad_ends[-1] - 1, 1, 0)).astype(I32)
    be = jnp.sum(jnp.where(pad_ends[None, :] <= blk_start[:, None], 1, 0), axis=1).astype(I32)
    active = blk_start < pad_ends[-1]
    be = jnp.where(active, be, last)
    blk = jnp.arange(n_blocks, dtype=I32)
    first = active & ((blk == 0) | (be != jnp.roll(be, 1)))
    slot = (jnp.cumsum(first.astype(I32)) - 1) & 1
    later_first = first[None, :] & (blk[None, :] > blk[:, None])
    nxt_pos = jnp.min(jnp.where(later_first, blk[None, :], n_blocks), axis=1)
    nxt = jnp.sum(jnp.where(blk[None, :] == nxt_pos[:, None], be[None, :], 0), axis=1)
    nxt = jnp.where(nxt_pos < n_blocks, nxt, -1).astype(I32)
    mine = be[:, None] == e_ids[None, :]
    cnt_b = jnp.sum(jnp.where(mine, counts[None, :], 0), axis=1)
    start_b = jnp.sum(jnp.where(mine, pad_starts[None, :], 0), axis=1)
    nvalid = jnp.where(active, jnp.clip(cnt_b - (blk_start - start_b), 0, bm), 0).astype(I32)
    return be, first.astype(I32), nxt, slot.astype(I32), nvalid, nact.reshape(1), dest, n_blocks * bm


def kernel(x, c, w_ada, b_ada, mix_norm_g, ffn_norm_g, w_in, hg_lower_bound_logits, hg_out_norm_g, da_q_norm_g, da_k_norm_g, da_lambda_q1, da_lambda_k1, da_lambda_q2, da_lambda_k2, da_subln_g, w_out, w_router, b_router, w1, b1, w2, b2):
    bsz, seq, d = x.shape
    t = bsz * seq
    depth = w_ada.shape[0]
    out_dtype = x.dtype
    hw = HG_HEADS * HG_DV
    xcur = x.reshape(t, d)
    for l in range(depth):
        ada = _ada(c, w_ada[l], b_ada[l])
        ada3 = ada.reshape(bsz, 1, N_MOD * d)
        col_h = 0
        col_a = col_h + 4 * HG_HEADS
        col_g = (4 * hw + 3 * DA_HEADS * 2 * DA_DH) // (d // 2)
        proj = _inproj(xcur, mix_norm_g[l].reshape(1, d), ada3, w_in[l].astype(BF16), seq)

        o_a = _hgrn(proj, hg_lower_bound_logits, hg_out_norm_g[l].reshape(1, HG_DV), bsz, seq, col_h, l)
        lambda_init = 0.8 - 0.6 * math.exp(-0.3 * l)
        qg2 = jnp.tile(da_q_norm_g[l], 2).reshape(1, 2 * DA_DH)
        kg2 = jnp.tile(da_k_norm_g[l], 2).reshape(1, 2 * DA_DH)
        lam4 = jnp.stack([da_lambda_q1[l], da_lambda_k1[l], da_lambda_q2[l], da_lambda_k2[l]])
        o_d = _attn(proj, qg2, kg2, lam4, da_subln_g[l].reshape(1, 2 * DA_DH), bsz, seq, col_a, lambda_init)

        x1, h2, idx, rank, gcol, cnt = _mixout(
            xcur, o_a, o_d, proj, col_g, w_out[l], ada3, ffn_norm_g[l].reshape(1, d),
            w_router[l].T, b_router[l].reshape(N_EXPERTS, 1), seq)

        counts = cnt[:, 0].astype(I32)
        be, first, nxt, slot, nvalid, nact, dest, n_rows = _route_tables(idx, rank, counts, t)
        b1p = b1[l].reshape(N_EXPERTS, -1, LANES, 2).transpose(0, 1, 3, 2).reshape(N_EXPERTS, 1, -1)
        xb = _sc_dispatch(h2, dest, n_rows)
        yb = _ffn(be, first, nxt, slot, nvalid, nact, xb, w1[l], b1p, w2[l], b2[l].reshape(N_EXPERTS, 1, d))
        yall = _sc_undispatch(yb, dest)
        xcur = _combine(x1, yall, gcol, ada3, seq, out_dtype)
    return xcur.reshape(bsz, seq, d)
```

```python
import functools
import math

import jax
import jax.numpy as jnp
from jax import lax
from jax.experimental import pallas as pl
from jax.experimental.pallas import tpu as pltpu
from jax.experimental.pallas import tpu_sc as plsc

F32 = jnp.float32
BF16 = jnp.bfloat16
I32 = jnp.int32

HG_HEADS = 4
HG_DK = 128
HG_DV = 128
HG_CHUNK = 32
DA_HEADS = 4
DA_DH = 64
N_EXPERTS = 32
TOP_K = 4
SWIGLU_ALPHA = 1.702
SWIGLU_LIMIT = 7.0
NORM_EPS = 1e-6
LOG2E = math.log2(math.e)
N_MOD = 6

LANES = 128
VMEM_LIMIT = 56 * 1024 * 1024

HG_ROWS = 256
ATTN_TILE = 256
FFN_BLOCK = 512

SC_CORES = 2
SC_WORKERS = SC_CORES * 16
SC_CHUNK = 128


def _sigmoid(x):
    return 0.5 * jnp.tanh(0.5 * x) + 0.5


def _dot(a, b):
    return jnp.dot(a, b, preferred_element_type=F32)


def _dot_nt(a, b):
    return lax.dot_general(a, b, (((1,), (1,)), ((), ())), preferred_element_type=F32)


def _split_bf16(x):
    hi = x.astype(BF16)
    lo = (x - hi.astype(F32)).astype(BF16)
    return hi, lo


def _pack_halves(x):
    return _pack_rounded(x.astype(BF16).astype(F32))


def _pack_rounded(xr):
    half = xr.shape[1] // 2
    lo = lax.bitcast_convert_type(xr[:, :half], I32)
    hi = lax.bitcast_convert_type(xr[:, half:], I32)
    return lax.shift_right_logical(lo, 16) | hi


def _unpack_halves(w):
    lo = lax.bitcast_convert_type(lax.shift_left(w, 16), F32)
    hi = lax.bitcast_convert_type(w & jnp.int32(-65536), F32)
    return lo, hi


def _ada_body(c_ref, w_ref, b_ref, o_ref):
    c = c_ref[...].astype(F32)
    ch, cl = _split_bf16(c * _sigmoid(c))
    wh, wl = _split_bf16(w_ref[...])
    o_ref[...] = _dot(ch, wh) + _dot(cl, wh) + _dot(ch, wl) + b_ref[...]


def _ada(c, w_ada, b_ada):
    bsz, d = c.shape
    n = w_ada.shape[1]
    tn = 2 * d
    return pl.pallas_call(
        _ada_body,
        grid=(n // tn,),
        in_specs=[
            pl.BlockSpec((bsz, d), lambda j: (0, 0)),
            pl.BlockSpec((d, tn), lambda j: (0, j)),
            pl.BlockSpec((1, tn), lambda j: (0, j)),
        ],
        out_specs=pl.BlockSpec((bsz, tn), lambda j: (0, j)),
        out_shape=jax.ShapeDtypeStruct((bsz, n), F32),
        name="ada",
    )(c, w_ada, b_ada.reshape(1, n))


def _norm_mod(x, g, shift, scale):
    ms = jnp.mean(x * x, axis=-1, keepdims=True)
    return x * lax.rsqrt(ms + NORM_EPS) * (g * (1.0 + scale)) + shift


def _inproj_body(x_ref, g_ref, sh_ref, sc_ref, w_ref, o_ref):
    h = _norm_mod(x_ref[...], g_ref[...], sh_ref[0], sc_ref[0])
    o_ref[...] = _dot(h.astype(BF16), w_ref[...]).astype(BF16)


def _inproj(x2, g, ada3, w_bf16, seq):
    t, d = x2.shape
    n = w_bf16.shape[1]
    tm = min(512, seq)
    nj = 1
    tn = n // nj
    per_b = seq // tm
    return pl.pallas_call(
        _inproj_body,
        grid=(nj, t // tm),
        in_specs=[
            pl.BlockSpec((tm, d), lambda j, i: (i, 0)),
            pl.BlockSpec((1, d), lambda j, i: (0, 0)),
            pl.BlockSpec((1, 1, d), lambda j, i: (i // per_b, 0, 0)),
            pl.BlockSpec((1, 1, d), lambda j, i: (i // per_b, 0, 1)),
            pl.BlockSpec((d, tn), lambda j, i: (0, j)),
        ],
        out_specs=pl.BlockSpec((tm, tn), lambda j, i: (i, j)),
        out_shape=jax.ShapeDtypeStruct((t, n), BF16),
        compiler_params=pltpu.CompilerParams(
            dimension_semantics=("arbitrary", "arbitrary"), vmem_limit_bytes=VMEM_LIMIT),
        name="inproj",
    )(x2, g, ada3, ada3, w_bf16)


def _hgrn_body(q_ref, f_ref, i_ref, og_ref, lbl_ref, g_ref, o_ref, *, seq, layer):
    rows, chunk = HG_ROWS, HG_CHUNK
    nchunk = rows // chunk
    lbl = lbl_ref[...].astype(F32)
    e = jnp.exp(lbl - jnp.max(lbl, axis=0, keepdims=True))
    lb = jnp.sum(e[: layer + 1], axis=0, keepdims=True) / jnp.sum(e, axis=0, keepdims=True)
    r_i = lax.broadcasted_iota(I32, (rows, rows), 0)
    c_i = lax.broadcasted_iota(I32, (rows, rows), 1)
    tri = ((r_i // chunk) == (c_i // chunk)) & (r_i >= c_i)
    tri_b = jnp.where(tri, 1.0, 0.0).astype(BF16)
    row_chunk = lax.broadcasted_iota(I32, (rows, HG_DK), 0) // chunk
    g = g_ref[...].astype(F32)

    def block(r, st):
        sl = pl.ds(r * rows, rows)
        qr = q_ref[sl, :].astype(F32)
        fr = f_ref[sl, :].astype(F32)
        v = i_ref[sl, :].astype(F32)
        og = og_ref[sl, :].astype(F32)
        q = qr * _sigmoid(qr)
        f = lb + (1.0 - lb) * _sigmoid(fr)
        k = 1.0 - f
        logf = jnp.log(f)
        lhi, llo = _split_bf16(logf)
        bc2 = _dot(tri_b, jnp.concatenate([lhi, llo], axis=1))
        bcum = bc2[:, :HG_DK] + bc2[:, HG_DK:]
        b3 = bcum.reshape(nchunk, chunk, HG_DK)
        bl = b3[:, chunk - 1:chunk, :]
        dec = jnp.exp(bl)
        kt_f = k * jnp.exp(-bcum)
        qt_f = q * jnp.exp(bcum)
        kd = (kt_f.reshape(nchunk, chunk, HG_DK) * dec).reshape(rows, HG_DK)
        a = jnp.where(tri, _dot_nt(qt_f.astype(BF16), kt_f.astype(BF16)), 0.0).astype(BF16)
        vt_b = v.T.astype(BF16)
        kd_x = jnp.concatenate([jnp.where(row_chunk == c, kd, 0.0) for c in range(nchunk)], axis=1)
        kv_all = _dot(vt_b, kd_x.astype(BF16))
        starts = []
        for c in range(nchunk):
            starts.append(st.astype(BF16))
            st = st * dec[c] + kv_all[:, c * HG_DK:(c + 1) * HG_DK]
        q_x = [jnp.where(row_chunk == c, qt_f, 0.0).astype(BF16) for c in range(nchunk)]
        o = _dot_nt(jnp.concatenate([a] + q_x, axis=1), jnp.concatenate([vt_b] + starts, axis=1))
        ms = jnp.mean(o * o, axis=-1, keepdims=True)
        o = o * lax.rsqrt(ms + NORM_EPS) * g
        o_ref[sl, :] = (o * (og * _sigmoid(og))).astype(o_ref.dtype)
        return st

    st = jnp.zeros((HG_DV, HG_DK), F32)
    for r in range(seq // rows):
        st = block(r, st)


def _hgrn(proj, lb_logits, norm_g, bsz, seq, col0, layer):
    t = proj.shape[0]
    blk = lambda off: pl.BlockSpec((seq, LANES), lambda b, h, off=off: (b, col0 + off + h))
    return pl.pallas_call(
        functools.partial(_hgrn_body, seq=seq, layer=layer),
        grid=(bsz, HG_HEADS),
        in_specs=[
            blk(0), blk(HG_HEADS), blk(2 * HG_HEADS), blk(3 * HG_HEADS),
            pl.BlockSpec((lb_logits.shape[0], HG_DK), lambda b, h: (0, h)),
            pl.BlockSpec((1, HG_DV), lambda b, h: (0, 0)),
        ],
        out_specs=pl.BlockSpec((seq, HG_DV), lambda b, h: (b, h)),
        out_shape=jax.ShapeDtypeStruct((t, HG_HEADS * HG_DV), BF16),
        compiler_params=pltpu.CompilerParams(
            dimension_semantics=("arbitrary", "arbitrary"), vmem_limit_bytes=VMEM_LIMIT),
        name="hgrn",
    )(proj, proj, proj, proj, lb_logits, norm_g)


def _group_norm(x, gsum_b, gain):
    ss = _dot((x * x).astype(BF16), gsum_b)
    return x * lax.rsqrt(ss * (1.0 / DA_DH) + NORM_EPS) * gain


def _attn_body(q_ref, k_ref, v_ref, qg_ref, kg_ref, lam_ref, sg_ref, o_ref, kn_scr, v1_scr, *, seq, lambda_init):
    tq = ATTN_TILE
    width = 2 * DA_DH
    r_l = lax.broadcasted_iota(I32, (width, width), 0) // DA_DH
    c_l = lax.broadcasted_iota(I32, (width, width), 1) // DA_DH
    gsum_b = jnp.where(r_l == c_l, 1.0, 0.0).astype(BF16)
    lane = lax.broadcasted_iota(I32, (1, width), 1)
    kg = kg_ref[...].astype(F32)
    qg = qg_ref[...].astype(F32) * (DA_DH ** -0.5 * LOG2E)
    sg = sg_ref[...].astype(F32) * (1.0 - lambda_init)
    ones = jnp.ones((tq, width), BF16)
    row = lax.broadcasted_iota(I32, (tq, tq), 0)
    col = lax.broadcasted_iota(I32, (tq, tq), 1)
    keep = row >= col

    lam_v = lam_ref[...].astype(F32)
    lam = (jnp.exp(jnp.sum(lam_v[0:1] * lam_v[1:2], axis=-1, keepdims=True))
           - jnp.exp(jnp.sum(lam_v[2:3] * lam_v[3:4], axis=-1, keepdims=True)) + lambda_init)

    def softmax_v(qc, nk):
        s = _dot_nt(qc, kn_scr[0:nk, :])
        diag = jnp.where(keep, s[:, nk - tq:], -jnp.inf)
        s = diag if nk == tq else jnp.concatenate([s[:, :nk - tq], diag], axis=1)
        m = jnp.max(s, axis=-1, keepdims=True)
        return _dot(jnp.exp2(s - m).astype(BF16), v1_scr[0:nk, :])

    for i in range(seq // tq):
        sl = slice(i * tq, (i + 1) * tq)
        kn_scr[sl, :] = _group_norm(k_ref[sl, :].astype(F32), gsum_b, kg).astype(BF16)
        v1_scr[sl, :] = jnp.concatenate([v_ref[sl, :], ones], axis=1)
        qn = _group_norm(q_ref[sl, :].astype(F32), gsum_b, qg)
        nk = (i + 1) * tq
        a1 = softmax_v(jnp.where(lane < DA_DH, qn, 0.0).astype(BF16), nk)
        a2 = softmax_v(jnp.where(lane >= DA_DH, qn, 0.0).astype(BF16), nk)
        o = a1[:, :width] / a1[:, width:width + 1] - lam * (a2[:, :width] / a2[:, width:width + 1])
        ms = jnp.mean(o * o, axis=-1, keepdims=True)
        o_ref[sl, :] = (o * lax.rsqrt(ms + NORM_EPS) * sg).astype(o_ref.dtype)


def _attn(proj, qg2, kg2, lam4, subln_g, bsz, seq, col0, lambda_init):
    t = proj.shape[0]
    width = 2 * DA_DH
    return pl.pallas_call(
        functools.partial(_attn_body, seq=seq, lambda_init=lambda_init),
        grid=(bsz, DA_HEADS),
        in_specs=[
            pl.BlockSpec((seq, width), lambda b, h: (b, col0 + h)),
            pl.BlockSpec((seq, width), lambda b, h: (b, col0 + DA_HEADS + h)),
            pl.BlockSpec((seq, width), lambda b, h: (b, col0 + 2 * DA_HEADS + h)),
            pl.BlockSpec((1, width), lambda b, h: (0, 0)),
            pl.BlockSpec((1, width), lambda b, h: (0, 0)),
            pl.BlockSpec((4, DA_DH), lambda b, h: (0, 0)),
            pl.BlockSpec((1, width), lambda b, h: (0, 0)),
        ],
        out_specs=pl.BlockSpec((seq, width), lambda b, h: (b, h)),
        out_shape=jax.ShapeDtypeStruct((t, DA_HEADS * width), BF16),
        scratch_shapes=[pltpu.VMEM((seq, width), BF16), pltpu.VMEM((seq, 2 * width), BF16)],
        compiler_params=pltpu.CompilerParams(
            dimension_semantics=("arbitrary", "arbitrary"), vmem_limit_bytes=VMEM_LIMIT),
        name="attn",
    )(proj, proj, proj, qg2, kg2, lam4, subln_g)


def _mixout_body(x_ref, oa_ref, od_ref, ga0_ref, ga1_ref, gd0_ref, gd1_ref, wo_ref, g1_ref, g_ref, sh_ref, sc_ref,
                 wr_ref, br_ref,
                 x1_ref, h2_ref, idx_ref, rank_ref, gcol_ref, cnt_ref, carry_scr, wo_scr):
    i = pl.program_id(0)
    tm = x_ref.shape[0]
    hw = oa_ref.shape[1]

    @pl.when(i == 0)
    def _():
        carry_scr[...] = jnp.zeros_like(carry_scr)
        wo_scr[...] = wo_ref[...].astype(BF16)

    ya = _dot(oa_ref[...], wo_scr[0:hw, :])
    yd = _dot(od_ref[...], wo_scr[hw:, :])
    ga = jnp.concatenate([ga0_ref[...], ga1_ref[...]], axis=1).astype(F32)
    gd = jnp.concatenate([gd0_ref[...], gd1_ref[...]], axis=1).astype(F32)
    y = _sigmoid(ga) * ya + _sigmoid(gd) * yd
    x1 = x_ref[...] + g1_ref[0] * y
    x1_ref[...] = x1
    h2 = _norm_mod(x1, g_ref[...], sh_ref[0], sc_ref[0])
    hh = h2.astype(BF16)
    h2r = hh.astype(F32)
    h2_ref[...] = _pack_rounded(h2r)

    hl = (h2 - h2r).astype(BF16)
    wh, wl = _split_bf16(wr_ref[...])
    logits = _dot_nt(wh, hh) + _dot_nt(wl, hh) + _dot_nt(wh, hl) + br_ref[...]

    e_iota = lax.broadcasted_iota(I32, (N_EXPERTS, tm), 0).astype(F32)
    vals = logits
    tops, sels, idxs = [], [], []
    for _ in range(TOP_K):
        m = jnp.max(vals, axis=0, keepdims=True)
        idx = jnp.min(jnp.where(vals == m, e_iota, float(N_EXPERTS)), axis=0, keepdims=True)
        sel = e_iota == idx
        vals = jnp.where(sel, -jnp.inf, vals)
        tops.append(m)
        sels.append(sel)
        idxs.append(idx)
    ex = [jnp.exp(tv - tops[0]) for tv in tops]
    den = ex[0] + ex[1] + ex[2] + ex[3]
    gates = [v / den for v in ex]

    hot = jnp.where(sels[0] | sels[1] | sels[2] | sels[3], 1.0, 0.0)
    r_t = lax.broadcasted_iota(I32, (tm, tm), 0)
    c_t = lax.broadcasted_iota(I32, (tm, tm), 1)
    upper = jnp.where(r_t < c_t, 1.0, 0.0).astype(BF16)
    excl = _dot(hot.astype(BF16), upper) + carry_scr[:, 0:1]
    carry_scr[...] = carry_scr[...] + jnp.sum(hot, axis=1, keepdims=True)
    cnt_ref[...] = carry_scr[...]

    ranks = [jnp.sum(jnp.where(s, excl, 0.0), axis=0, keepdims=True) for s in sels]
    idx_ref[...] = jnp.concatenate(idxs, axis=0).astype(I32)
    rank_ref[...] = jnp.concatenate(ranks, axis=0).astype(I32)
    gpad = jnp.concatenate(gates + [jnp.zeros((LANES - TOP_K, tm), F32)], axis=0)
    gcol_ref[...] = gpad.T


def _mixout(x2, oa, od, proj, col_g, wo, ada3, ffn_g, wr_t, br_col, seq):
    t, d = x2.shape
    tm = min(1024, seq)
    per_b = seq // tm
    hw = oa.shape[1]
    row = lambda w: pl.BlockSpec((tm, w), lambda i: (i, 0))
    gate = lambda c: pl.BlockSpec((tm, d // 2), lambda i, c=c: (i, col_g + c))
    mod = lambda c: pl.BlockSpec((1, 1, d), lambda i, c=c: (i // per_b, 0, c))
    full = lambda a: pl.BlockSpec(a.shape, lambda i: (0,) * a.ndim)
    return pl.pallas_call(
        _mixout_body,
        grid=(t // tm,),
        in_specs=[
            row(d), row(hw), row(hw),
            gate(0), gate(1), gate(2), gate(3),
            full(wo),
            mod(2),
            full(ffn_g), mod(3), mod(4),
            full(wr_t), full(br_col),
        ],
        out_specs=[
            row(d), row(d // 2),
            pl.BlockSpec((TOP_K, tm), lambda i: (0, i)),
            pl.BlockSpec((TOP_K, tm), lambda i: (0, i)),
            pl.BlockSpec((tm, LANES), lambda i: (i, 0)),
            pl.BlockSpec((N_EXPERTS, LANES), lambda i: (0, 0)),
        ],
        out_shape=[
            jax.ShapeDtypeStruct((t, d), F32),
            jax.ShapeDtypeStruct((t, d // 2), I32),
            jax.ShapeDtypeStruct((TOP_K, t), I32),
            jax.ShapeDtypeStruct((TOP_K, t), I32),
            jax.ShapeDtypeStruct((t, LANES), F32),
            jax.ShapeDtypeStruct((N_EXPERTS, LANES), F32),
        ],
        scratch_shapes=[pltpu.VMEM((N_EXPERTS, LANES), F32), pltpu.VMEM(wo.shape, BF16)],
        compiler_params=pltpu.CompilerParams(
            dimension_semantics=("arbitrary",), vmem_limit_bytes=VMEM_LIMIT),
        name="mixout",
    )(x2, oa, od, proj, proj, proj, proj, wo, ada3, ffn_g, ada3, ada3, wr_t, br_col)


def _sc_mesh():
    return plsc.VectorSubcoreMesh(core_axis_name="c", subcore_axis_name="s")


def _sc_worker_id():
    return lax.axis_index("s") * SC_CORES + lax.axis_index("c")


def _sc_dispatch(h2, dest, n_rows):
    t, d = h2.shape
    assert t % (SC_WORKERS * SC_CHUNK) == 0
    cpw = t // (SC_WORKERS * SC_CHUNK)
    dest_w = dest.reshape(TOP_K, SC_WORKERS, cpw, SC_CHUNK).transpose(1, 0, 2, 3).reshape(-1, SC_CHUNK)
    ipw = TOP_K * cpw

    def body(h2_hbm, dest_hbm, xb_hbm, idx_v, rows_v):
        wid = _sc_worker_id()
        pltpu.sync_copy(dest_hbm.at[pl.ds(pl.multiple_of(wid * ipw, ipw), ipw)], idx_v)

        @pl.loop(0, cpw)
        def _(c):
            t0 = pl.multiple_of((wid * cpw + c) * SC_CHUNK, SC_CHUNK)
            pltpu.sync_copy(h2_hbm.at[pl.ds(t0, SC_CHUNK)], rows_v)
            for k in range(TOP_K):
                pltpu.sync_copy(rows_v, xb_hbm.at[idx_v.at[k * cpw + c]])

    return pl.kernel(
        body, out_type=jax.ShapeDtypeStruct((n_rows, d), h2.dtype), mesh=_sc_mesh(),
        scratch_types=[pltpu.VMEM((ipw, SC_CHUNK), I32), pltpu.VMEM((SC_CHUNK, d), h2.dtype)],
        name="dispatch",
    )(h2, dest_w)


def _sc_undispatch(y, dest):
    n_asg = dest.shape[0]
    d = y.shape[1]
    rows = SC_CHUNK // 2
    assert n_asg % (2 * SC_WORKERS * rows) == 0
    cpw = n_asg // (SC_WORKERS * rows)

    def body(y_hbm, dest_hbm, yt_hbm, idx_v, rows_v, gsem, wsem):
        wid = _sc_worker_id()
        pltpu.sync_copy(dest_hbm.at[pl.ds(pl.multiple_of(wid * cpw, cpw), cpw)], idx_v)

        def gather(c, b):
            return pltpu.make_async_copy(y_hbm.at[idx_v.at[c]], rows_v.at[b], gsem.at[b])

        def write(c, b):
            r0 = pl.multiple_of((wid * cpw + c) * rows, rows)
            return pltpu.make_async_copy(rows_v.at[b], yt_hbm.at[pl.ds(r0, rows)], wsem.at[b])

        gather(0, 0).start()

        @pl.loop(0, cpw, step=2)
        def _(c0):
            for b in range(2):
                c = c0 + b

                @pl.when(c + 1 < cpw)
                def _():
                    @pl.when(c >= 1)
                    def _():
                        write(c - 1, 1 - b).wait()
                    gather(c + 1, 1 - b).start()

                gather(c, b).wait()
                write(c, b).start()

        write(cpw - 2, 0).wait()
        write(cpw - 1, 1).wait()

    return pl.kernel(
        body, out_type=jax.ShapeDtypeStruct((n_asg, d), y.dtype), mesh=_sc_mesh(),
        scratch_types=[pltpu.VMEM((cpw, rows), I32), pltpu.VMEM((2, rows, d), y.dtype),
                       pltpu.SemaphoreType.DMA((2,)), pltpu.SemaphoreType.DMA((2,))],
        name="undispatch",
    )(y, dest.reshape(-1, rows))


def _ffn_body(be_ref, first_ref, nxt_ref, slot_ref, nv_ref, nact_ref, x_ref, w1_hbm, b1_ref, w2_hbm, b2_ref, y_ref,
              w1f, w2f, w1c, w2c, sem1, sem2):
    j = pl.program_id(0)
    ff = w2f.shape[1]
    pair = 2 * LANES
    ngroup = (2 * ff) // pair

    def weight_copies(e, slot):
        return (pltpu.make_async_copy(w1_hbm.at[e], w1f.at[slot], sem1.at[slot]),
                pltpu.make_async_copy(w2_hbm.at[e], w2f.at[slot], sem2.at[slot]))

    @pl.when(j == 0)
    def _():
        for cp in weight_copies(be_ref[0], 0):
            cp.start()

    @pl.when(first_ref[j] == 1)
    def _():
        slot = slot_ref[j]
        for cp in weight_copies(be_ref[j], slot):
            cp.wait()

        @pl.when(nxt_ref[j] >= 0)
        def _():
            for cp in weight_copies(nxt_ref[j], 1 - slot):
                cp.start(priority=1)

        r_p = lax.broadcasted_iota(I32, (pair, pair), 0)
        c_p = lax.broadcasted_iota(I32, (pair, pair), 1)
        src = jnp.where(c_p < LANES, 2 * c_p, 2 * (c_p - LANES) + 1)
        perm = jnp.where(r_p == src, 1.0, 0.0).astype(BF16)
        for g in range(ngroup):
            cols = slice(g * pair, (g + 1) * pair)
            w1c[:, cols] = _dot(w1f[slot, :, cols].astype(BF16), perm).astype(BF16)
        w2c[...] = w2f[slot].astype(BF16)

    def expert_rows(nrows):
        x_lo, x_hi = _unpack_halves(x_ref[0:nrows, :])
        xb = jnp.concatenate([x_lo.astype(BF16), x_hi.astype(BF16)], axis=1)
        u = _dot(xb, w1c[...]) + b1_ref[0]
        acts = []
        for g in range(ngroup):
            glu = jnp.minimum(u[:, g * pair:g * pair + LANES], SWIGLU_LIMIT)
            lin = lax.clamp(F32(-SWIGLU_LIMIT), u[:, g * pair + LANES:(g + 1) * pair], F32(SWIGLU_LIMIT))
            acts.append((glu * _sigmoid(SWIGLU_ALPHA * glu) * (lin + 1.0)).astype(BF16))
        act = jnp.concatenate(acts, axis=1)
        y = _dot(act, w2c[...]) + b2_ref[0]
        half = y.shape[1] // 2
        words = pltpu.pack_elementwise([y[:, :half], y[:, half:]], packed_dtype=BF16)
        y_ref[0:nrows, :] = lax.bitcast_convert_type(words, I32)

    active = j < nact_ref[0]
    half = x_ref.shape[0] // 2
    pl.when(active & (nv_ref[j] > half))(functools.partial(expert_rows, x_ref.shape[0]))
    pl.when(active & (nv_ref[j] <= half))(functools.partial(expert_rows, half))


def _ffn(block_expert, first, nxt, slot, nvalid, nact, xb, w1, b1p, w2, b2):
    bm = FFN_BLOCK
    n_rows, dw = xb.shape
    d = 2 * dw
    n_blocks = n_rows // bm
    ff2 = w1.shape[2]
    ff = w2.shape[1]
    row_blk = lambda j, be, fi, nx, sl, nv, na: (jnp.minimum(j, na[0] - 1), 0)
    bias_blk = lambda j, be, fi, nx, sl, nv, na: (be[j], 0, 0)
    grid_spec = pltpu.PrefetchScalarGridSpec(
        num_scalar_prefetch=6,
        grid=(n_blocks,),
        in_specs=[
            pl.BlockSpec((bm, dw), row_blk),
            pl.BlockSpec(memory_space=pl.ANY),
            pl.BlockSpec((1, 1, ff2), bias_blk),
            pl.BlockSpec(memory_space=pl.ANY),
            pl.BlockSpec((1, 1, d), bias_blk),
        ],
        out_specs=pl.BlockSpec((bm, dw), row_blk),
        scratch_shapes=[
            pltpu.VMEM((2, d, ff2), F32), pltpu.VMEM((2, ff, d), F32),
            pltpu.VMEM((d, ff2), BF16), pltpu.VMEM((ff, d), BF16),
            pltpu.SemaphoreType.DMA((2,)), pltpu.SemaphoreType.DMA((2,)),
        ],
    )
    return pl.pallas_call(
        _ffn_body,
        grid_spec=grid_spec,
        out_shape=jax.ShapeDtypeStruct((n_rows, dw), I32),
        compiler_params=pltpu.CompilerParams(
            dimension_semantics=("arbitrary",), vmem_limit_bytes=VMEM_LIMIT),
        name="ffn",
    )(block_expert, first, nxt, slot, nvalid, nact, xb, w1, b1p, w2, b2)


def _combine_body(x1_ref, y0_ref, y1_ref, y2_ref, y3_ref, gcol_ref, g2_ref, o_ref):
    gc = gcol_ref[...]
    m_lo = m_hi = None
    for k, y_ref in enumerate((y0_ref, y1_ref, y2_ref, y3_ref)):
        lo, hi = _unpack_halves(y_ref[...])
        gk = gc[:, k:k + 1]
        m_lo = gk * lo if m_lo is None else m_lo + gk * lo
        m_hi = gk * hi if m_hi is None else m_hi + gk * hi
    m = jnp.concatenate([m_lo, m_hi], axis=1)
    o_ref[...] = (x1_ref[...] + g2_ref[0] * m).astype(o_ref.dtype)


def _combine(x1, yall, gcol, ada3, seq, out_dtype):
    t, d = x1.shape
    tm = min(1024, seq)
    per_b = seq // tm
    nt = t // tm
    yk = lambda k: pl.BlockSpec((tm, d // 2), lambda i, k=k: (k * nt + i, 0))
    return pl.pallas_call(
        _combine_body,
        grid=(nt,),
        in_specs=[
            pl.BlockSpec((tm, d), lambda i: (i, 0)),
            yk(0), yk(1), yk(2), yk(3),
            pl.BlockSpec((tm, LANES), lambda i: (i, 0)),
            pl.BlockSpec((1, 1, d), lambda i: (i // per_b, 0, 5)),
        ],
        out_specs=pl.BlockSpec((tm, d), lambda i: (i, 0)),
        out_shape=jax.ShapeDtypeStruct((t, d), out_dtype),
        compiler_params=pltpu.CompilerParams(
            dimension_semantics=("arbitrary",), vmem_limit_bytes=VMEM_LIMIT),
        name="combine",
    )(x1, yall, yall, yall, yall, gcol, ada3)


def _route_tables(idx, rank, counts, n_tok):
    bm = FFN_BLOCK
    n_asg = TOP_K * n_tok
    n_blocks = -(-(n_asg + N_EXPERTS * (bm - 1)) // bm)
    padded = (counts + bm - 1) // bm * bm
    pad_ends = jnp.cumsum(padded)
    pad_starts = pad_ends - padded
    e_ids = jnp.arange(N_EXPERTS, dtype=I32)
    start_of = jnp.sum(jnp.where(idx[None] == e_ids[:, None, None], pad_starts[:, None, None], 0), axis=0)
    dest = (start_of + rank).reshape(-1)
    nact = (pad_ends[-1] // bm).astype(I32)
    blk_start = jnp.arange(n_blocks, dtype=I32) * bm
    last = jnp.sum(jnp.where(pad_ends <= pad_ends[-1] - 1, 1, 0)).astype(I32)
    be = jnp.sum(jnp.where(pad_ends[None, :] <= blk_start[:, None], 1, 0), axis=1).astype(I32)
    active = blk_start < pad_ends[-1]
    be = jnp.where(active, be, last)
    blk = jnp.arange(n_blocks, dtype=I32)
    first = active & ((blk == 0) | (be != jnp.roll(be, 1)))
    slot = (jnp.cumsum(first.astype(I32)) - 1) & 1
    later_first = first[None, :] & (blk[None, :] > blk[:, None])
    nxt_pos = jnp.min(jnp.where(later_first, blk[None, :], n_blocks), axis=1)
    nxt = jnp.sum(jnp.where(blk[None, :] == nxt_pos[:, None], be[None, :], 0), axis=1)
    nxt = jnp.where(nxt_pos < n_blocks, nxt, -1).astype(I32)
    mine = be[:, None] == e_ids[None, :]
    cnt_b = jnp.sum(jnp.where(mine, counts[None, :], 0), axis=1)
    start_b = jnp.sum(jnp.where(mine, pad_starts[None, :], 0), axis=1)
    nvalid = jnp.where(active, jnp.clip(cnt_b - (blk_start - start_b), 0, bm), 0).astype(I32)
    return be, first.astype(I32), nxt, slot.astype(I32), nvalid, nact.reshape(1), dest, n_blocks * bm


def kernel(x, c, w_ada, b_ada, mix_norm_g, ffn_norm_g, w_in, hg_lower_bound_logits, hg_out_norm_g, da_q_norm_g, da_k_norm_g, da_lambda_q1, da_lambda_k1, da_lambda_q2, da_lambda_k2, da_subln_g, w_out, w_router, b_router, w1, b1, w2, b2):
    bsz, seq, d = x.shape
    t = bsz * seq
    depth = w_ada.shape[0]
    out_dtype = x.dtype
    hw = HG_HEADS * HG_DV
    xcur = x.reshape(t, d)
    for l in range(depth):
        ada = _ada(c, w_ada[l], b_ada[l])
        ada3 = ada.reshape(bsz, 1, N_MOD * d)
        col_h = 0
        col_a = col_h + 4 * HG_HEADS
        col_g = (4 * hw + 3 * DA_HEADS * 2 * DA_DH) // (d // 2)
        proj = _inproj(xcur, mix_norm_g[l].reshape(1, d), ada3, w_in[l].astype(BF16), seq)

        o_a = _hgrn(proj, hg_lower_bound_logits, hg_out_norm_g[l].reshape(1, HG_DV), bsz, seq, col_h, l)
        lambda_init = 0.8 - 0.6 * math.exp(-0.3 * l)
        qg2 = jnp.tile(da_q_norm_g[l], 2).reshape(1, 2 * DA_DH)
        kg2 = jnp.tile(da_k_norm_g[l], 2).reshape(1, 2 * DA_DH)
        lam4 = jnp.stack([da_lambda_q1[l], da_lambda_k1[l], da_lambda_q2[l], da_lambda_k2[l]])
        o_d = _attn(proj, qg2, kg2, lam4, da_subln_g[l].reshape(1, 2 * DA_DH), bsz, seq, col_a, lambda_init)

        x1, h2, idx, rank, gcol, cnt = _mixout(
            xcur, o_a, o_d, proj, col_g, w_out[l], ada3, ffn_norm_g[l].reshape(1, d),
            w_router[l].T, b_router[l].reshape(N_EXPERTS, 1), seq)

        counts = cnt[:, 0].astype(I32)
        be, first, nxt, slot, nvalid, nact, dest, n_rows = _route_tables(idx, rank, counts, t)
        b1p = b1[l].reshape(N_EXPERTS, -1, LANES, 2).transpose(0, 1, 3, 2).reshape(N_EXPERTS, 1, -1)
        xb = _sc_dispatch(h2, dest, n_rows)
        yb = _ffn(be, first, nxt, slot, nvalid, nact, xb, w1[l], b1p, w2[l], b2[l].reshape(N_EXPERTS, 1, d))
        yall = _sc_undispatch(yb, dest)
        xcur = _combine(x1, yall, gcol, ada3, seq, out_dtype)
    return xcur.reshape(bsz, seq, d)
```

```python
import functools
import math

import jax
import jax.numpy as jnp
from jax import lax
from jax.experimental import pallas as pl
from jax.experimental.pallas import tpu as pltpu
from jax.experimental.pallas import tpu_sc as plsc

F32 = jnp.float32
BF16 = jnp.bfloat16
I32 = jnp.int32

HG_HEADS = 4
HG_DK = 128
HG_DV = 128
HG_CHUNK = 32
DA_HEADS = 4
DA_DH = 64
N_EXPERTS = 32
TOP_K = 4
SWIGLU_ALPHA = 1.702
SWIGLU_LIMIT = 7.0
NORM_EPS = 1e-6
LOG2E = math.log2(math.e)
N_MOD = 6

LANES = 128
VMEM_LIMIT = 56 * 1024 * 1024

HG_ROWS = 256
HG_HEADS_PER_STEP = 4
ATTN_TILE = 256
DA_HEADS_PER_STEP = 2
FFN_BLOCK = 512

SC_CORES = 2
SC_WORKERS = SC_CORES * 16
SC_CHUNK = 128


def _sigmoid(x):
    return 0.5 * jnp.tanh(0.5 * x) + 0.5


def _dot(a, b):
    return jnp.dot(a, b, preferred_element_type=F32)


def _dot_nt(a, b):
    return lax.dot_general(a, b, (((1,), (1,)), ((), ())), preferred_element_type=F32)


def _split_bf16(x):
    hi = x.astype(BF16)
    lo = (x - hi.astype(F32)).astype(BF16)
    return hi, lo


def _pack_halves(x):
    return _pack_rounded(x.astype(BF16).astype(F32))


def _pack_rounded(xr):
    half = xr.shape[1] // 2
    lo = lax.bitcast_convert_type(xr[:, :half], I32)
    hi = lax.bitcast_convert_type(xr[:, half:], I32)
    return lax.shift_right_logical(lo, 16) | hi


def _unpack_halves(w):
    lo = lax.bitcast_convert_type(lax.shift_left(w, 16), F32)
    hi = lax.bitcast_convert_type(w & jnp.int32(-65536), F32)
    return lo, hi


def _ada_body(c_ref, w_ref, b_ref, o_ref):
    c = c_ref[...].astype(F32)
    ch, cl = _split_bf16(c * _sigmoid(c))
    wh, wl = _split_bf16(w_ref[...])
    o_ref[...] = _dot(ch, wh) + _dot(cl, wh) + _dot(ch, wl) + b_ref[...]


def _ada(c, w_ada, b_ada):
    bsz, d = c.shape
    n = w_ada.shape[1]
    tn = 2 * d
    return pl.pallas_call(
        _ada_body,
        grid=(n // tn,),
        in_specs=[
            pl.BlockSpec((bsz, d), lambda j: (0, 0)),
            pl.BlockSpec((d, tn), lambda j: (0, j)),
            pl.BlockSpec((1, tn), lambda j: (0, j)),
        ],
        out_specs=pl.BlockSpec((bsz, tn), lambda j: (0, j)),
        out_shape=jax.ShapeDtypeStruct((bsz, n), F32),
        name="ada",
    )(c, w_ada, b_ada.reshape(1, n))


def _norm_mod(x, g, shift, scale):
    ms = jnp.mean(x * x, axis=-1, keepdims=True)
    return x * lax.rsqrt(ms + NORM_EPS) * (g * (1.0 + scale)) + shift


def _inproj_body(x_ref, g_ref, sh_ref, sc_ref, w_ref, o_ref):
    h = _norm_mod(x_ref[...], g_ref[...], sh_ref[0], sc_ref[0])
    o_ref[...] = _dot(h.astype(BF16), w_ref[...]).astype(BF16)


def _inproj(x2, g, ada3, w_bf16, seq):
    t, d = x2.shape
    n = w_bf16.shape[1]
    tm = min(512, seq)
    nj = 1
    tn = n // nj
    per_b = seq // tm
    return pl.pallas_call(
        _inproj_body,
        grid=(nj, t // tm),
        in_specs=[
            pl.BlockSpec((tm, d), lambda j, i: (i, 0)),
            pl.BlockSpec((1, d), lambda j, i: (0, 0)),
            pl.BlockSpec((1, 1, d), lambda j, i: (i // per_b, 0, 0)),
            pl.BlockSpec((1, 1, d), lambda j, i: (i // per_b, 0, 1)),
            pl.BlockSpec((d, tn), lambda j, i: (0, j)),
        ],
        out_specs=pl.BlockSpec((tm, tn), lambda j, i: (i, j)),
        out_shape=jax.ShapeDtypeStruct((t, n), BF16),
        compiler_params=pltpu.CompilerParams(
            dimension_semantics=("arbitrary", "arbitrary"), vmem_limit_bytes=VMEM_LIMIT),
        name="inproj",
    )(x2, g, ada3, ada3, w_bf16)


def _hgrn_body(q_ref, f_ref, i_ref, og_ref, lbl_ref, g_ref, o_ref, *, seq, layer):
    rows, chunk = HG_ROWS, HG_CHUNK
    nchunk = rows // chunk
    lbl = lbl_ref[...].astype(F32)
    e = jnp.exp(lbl - jnp.max(lbl, axis=0, keepdims=True))
    lb_all = jnp.sum(e[: layer + 1], axis=0, keepdims=True) / jnp.sum(e, axis=0, keepdims=True)
    heads = q_ref.shape[1] // HG_DK
    r_i = lax.broadcasted_iota(I32, (rows, rows), 0)
    c_i = lax.broadcasted_iota(I32, (rows, rows), 1)
    tri = ((r_i // chunk) == (c_i // chunk)) & (r_i >= c_i)
    tri_b = jnp.where(tri, 1.0, 0.0).astype(BF16)
    row_chunk = lax.broadcasted_iota(I32, (rows, HG_DK), 0) // chunk
    g = g_ref[...].astype(F32)

    def block(r, st, h):
        sl = pl.ds(r * rows, rows)
        cs = slice(h * HG_DK, (h + 1) * HG_DK)
        lb = lb_all[:, cs]
        qr = q_ref[sl, cs].astype(F32)
        fr = f_ref[sl, cs].astype(F32)
        v = i_ref[sl, cs].astype(F32)
        og = og_ref[sl, cs].astype(F32)
        q = qr * _sigmoid(qr)
        f = lb + (1.0 - lb) * _sigmoid(fr)
        k = 1.0 - f
        logf = jnp.log(f)
        lhi, llo = _split_bf16(logf)
        bc2 = _dot(tri_b, jnp.concatenate([lhi, llo], axis=1))
        bcum = bc2[:, :HG_DK] + bc2[:, HG_DK:]
        b3 = bcum.reshape(nchunk, chunk, HG_DK)
        bl = b3[:, chunk - 1:chunk, :]
        dec = jnp.exp(bl)
        kt_f = k * jnp.exp(-bcum)
        qt_f = q * jnp.exp(bcum)
        kd = (kt_f.reshape(nchunk, chunk, HG_DK) * dec).reshape(rows, HG_DK)
        a = jnp.where(tri, _dot_nt(qt_f.astype(BF16), kt_f.astype(BF16)), 0.0).astype(BF16)
        vt_b = v.T.astype(BF16)
        kd_x = jnp.concatenate([jnp.where(row_chunk == c, kd, 0.0) for c in range(nchunk)], axis=1)
        kv_all = _dot(vt_b, kd_x.astype(BF16))
        starts = []
        for c in range(nchunk):
            starts.append(st.astype(BF16))
            st = st * dec[c] + kv_all[:, c * HG_DK:(c + 1) * HG_DK]
        q_x = [jnp.where(row_chunk == c, qt_f, 0.0).astype(BF16) for c in range(nchunk)]
        o = _dot_nt(jnp.concatenate([a] + q_x, axis=1), jnp.concatenate([vt_b] + starts, axis=1))
        ms = jnp.mean(o * o, axis=-1, keepdims=True)
        o = o * lax.rsqrt(ms + NORM_EPS) * g
        o_ref[sl, cs] = (o * (og * _sigmoid(og))).astype(o_ref.dtype)
        return st

    sts = [jnp.zeros((HG_DV, HG_DK), F32) for _ in range(heads)]
    for r in range(seq // rows):
        for h in range(heads):
            sts[h] = block(r, sts[h], h)


def _hgrn(proj, lb_logits, norm_g, bsz, seq, col0, layer):
    t = proj.shape[0]
    hps = HG_HEADS_PER_STEP
    wid = hps * LANES
    assert col0 % hps == 0 and HG_HEADS % hps == 0
    blk = lambda off: pl.BlockSpec((seq, wid), lambda b, h, off=off: (b, (col0 + off) // hps + h))
    return pl.pallas_call(
        functools.partial(_hgrn_body, seq=seq, layer=layer),
        grid=(bsz, HG_HEADS // hps),
        in_specs=[
            blk(0), blk(HG_HEADS), blk(2 * HG_HEADS), blk(3 * HG_HEADS),
            pl.BlockSpec((lb_logits.shape[0], wid), lambda b, h: (0, h)),
            pl.BlockSpec((1, HG_DV), lambda b, h: (0, 0)),
        ],
        out_specs=pl.BlockSpec((seq, wid), lambda b, h: (b, h)),
        out_shape=jax.ShapeDtypeStruct((t, HG_HEADS * HG_DV), BF16),
        compiler_params=pltpu.CompilerParams(
            dimension_semantics=("arbitrary", "arbitrary"), vmem_limit_bytes=VMEM_LIMIT),
        name="hgrn",
    )(proj, proj, proj, proj, lb_logits, norm_g)


def _group_norm(x, gsum_b, gain):
    ss = _dot((x * x).astype(BF16), gsum_b)
    return x * lax.rsqrt(ss * (1.0 / DA_DH) + NORM_EPS) * gain


def _attn_body(q_ref, k_ref, v_ref, qg_ref, kg_ref, lam_ref, sg_ref, o_ref, kn_scr, v1_scr, *, seq, lambda_init):
    tq = ATTN_TILE
    width = 2 * DA_DH
    r_l = lax.broadcasted_iota(I32, (width, width), 0) // DA_DH
    c_l = lax.broadcasted_iota(I32, (width, width), 1) // DA_DH
    gsum_b = jnp.where(r_l == c_l, 1.0, 0.0).astype(BF16)
    lane = lax.broadcasted_iota(I32, (1, width), 1)
    kg = kg_ref[...].astype(F32)
    qg = qg_ref[...].astype(F32) * (DA_DH ** -0.5 * LOG2E)
    sg = sg_ref[...].astype(F32) * (1.0 - lambda_init)
    ones = jnp.ones((tq, width), BF16)
    row = lax.broadcasted_iota(I32, (tq, tq), 0)
    col = lax.broadcasted_iota(I32, (tq, tq), 1)
    keep = row >= col

    lam_v = lam_ref[...].astype(F32)
    lam = (jnp.exp(jnp.sum(lam_v[0:1] * lam_v[1:2], axis=-1, keepdims=True))
           - jnp.exp(jnp.sum(lam_v[2:3] * lam_v[3:4], axis=-1, keepdims=True)) + lambda_init)

    def softmax_v(qc, nk, h):
        s = _dot_nt(qc, kn_scr[h, 0:nk, :])
        diag = jnp.where(keep, s[:, nk - tq:], -jnp.inf)
        s = diag if nk == tq else jnp.concatenate([s[:, :nk - tq], diag], axis=1)
        m = jnp.max(s, axis=-1, keepdims=True)
        return _dot(jnp.exp2(s - m).astype(BF16), v1_scr[h, 0:nk, :])

    for i in range(seq // tq):
        sl = slice(i * tq, (i + 1) * tq)
        nk = (i + 1) * tq
        for h in range(q_ref.shape[1] // width):
            cs = slice(h * width, (h + 1) * width)
            kn_scr[h, sl, :] = _group_norm(k_ref[sl, cs].astype(F32), gsum_b, kg).astype(BF16)
            v1_scr[h, sl, :] = jnp.concatenate([v_ref[sl, cs], ones], axis=1)
            qn = _group_norm(q_ref[sl, cs].astype(F32), gsum_b, qg)
            a1 = softmax_v(jnp.where(lane < DA_DH, qn, 0.0).astype(BF16), nk, h)
            a2 = softmax_v(jnp.where(lane >= DA_DH, qn, 0.0).astype(BF16), nk, h)
            o = a1[:, :width] / a1[:, width:width + 1] - lam * (a2[:, :width] / a2[:, width:width + 1])
            ms = jnp.mean(o * o, axis=-1, keepdims=True)
            o_ref[sl, cs] = (o * lax.rsqrt(ms + NORM_EPS) * sg).astype(o_ref.dtype)


def _attn(proj, qg2, kg2, lam4, subln_g, bsz, seq, col0, lambda_init):
    t = proj.shape[0]
    width = 2 * DA_DH
    hps = DA_HEADS_PER_STEP
    assert col0 % hps == 0 and DA_HEADS % hps == 0
    blk = lambda off: pl.BlockSpec((seq, hps * width), lambda b, h, off=off: (b, (col0 + off) // hps + h))
    return pl.pallas_call(
        functools.partial(_attn_body, seq=seq, lambda_init=lambda_init),
        grid=(bsz, DA_HEADS // hps),
        in_specs=[
            blk(0), blk(DA_HEADS), blk(2 * DA_HEADS),
            pl.BlockSpec((1, width), lambda b, h: (0, 0)),
            pl.BlockSpec((1, width), lambda b, h: (0, 0)),
            pl.BlockSpec((4, DA_DH), lambda b, h: (0, 0)),
            pl.BlockSpec((1, width), lambda b, h: (0, 0)),
        ],
        out_specs=pl.BlockSpec((seq, hps * width), lambda b, h: (b, h)),
        out_shape=jax.ShapeDtypeStruct((t, DA_HEADS * width), BF16),
        scratch_shapes=[pltpu.VMEM((hps, seq, width), BF16), pltpu.VMEM((hps, seq, 2 * width), BF16)],
        compiler_params=pltpu.CompilerParams(
            dimension_semantics=("arbitrary", "arbitrary"), vmem_limit_bytes=VMEM_LIMIT),
        name="attn",
    )(proj, proj, proj, qg2, kg2, lam4, subln_g)


def _mixout_body(x_ref, oa_ref, od_ref, ga0_ref, ga1_ref, gd0_ref, gd1_ref, wo_ref, g1_ref, g_ref, sh_ref, sc_ref,
                 wr_ref, br_ref,
                 x1_ref, h2_ref, idx_ref, rank_ref, gcol_ref, cnt_ref, carry_scr, wo_scr):
    i = pl.program_id(0)
    tm = x_ref.shape[0]
    hw = oa_ref.shape[1]

    @pl.when(i == 0)
    def _():
        carry_scr[...] = jnp.zeros_like(carry_scr)
        wo_scr[...] = wo_ref[...].astype(BF16)

    ya = _dot(oa_ref[...], wo_scr[0:hw, :])
    yd = _dot(od_ref[...], wo_scr[hw:, :])
    ga = jnp.concatenate([ga0_ref[...], ga1_ref[...]], axis=1).astype(F32)
    gd = jnp.concatenate([gd0_ref[...], gd1_ref[...]], axis=1).astype(F32)
    y = _sigmoid(ga) * ya + _sigmoid(gd) * yd
    x1 = x_ref[...] + g1_ref[0] * y
    x1_ref[...] = x1
    h2 = _norm_mod(x1, g_ref[...], sh_ref[0], sc_ref[0])
    hh = h2.astype(BF16)
    h2r = hh.astype(F32)
    h2_ref[...] = _pack_rounded(h2r)

    hl = (h2 - h2r).astype(BF16)
    wh, wl = _split_bf16(wr_ref[...])
    logits = _dot_nt(wh, hh) + _dot_nt(wl, hh) + _dot_nt(wh, hl) + br_ref[...]

    e_iota = lax.broadcasted_iota(I32, (N_EXPERTS, tm), 0).astype(F32)
    vals = logits
    tops, sels, idxs = [], [], []
    for _ in range(TOP_K):
        m = jnp.max(vals, axis=0, keepdims=True)
        idx = jnp.min(jnp.where(vals == m, e_iota, float(N_EXPERTS)), axis=0, keepdims=True)
        sel = e_iota == idx
        vals = jnp.where(sel, -jnp.inf, vals)
        tops.append(m)
        sels.append(sel)
        idxs.append(idx)
    ex = [jnp.exp(tv - tops[0]) for tv in tops]
    den = ex[0] + ex[1] + ex[2] + ex[3]
    gates = [v / den for v in ex]

    hot = jnp.where(sels[0] | sels[1] | sels[2] | sels[3], 1.0, 0.0)
    r_t = lax.broadcasted_iota(I32, (tm, tm), 0)
    c_t = lax.broadcasted_iota(I32, (tm, tm), 1)
    upper = jnp.where(r_t < c_t, 1.0, 0.0).astype(BF16)
    excl = _dot(hot.astype(BF16), upper) + carry_scr[:, 0:1]
    carry_scr[...] = carry_scr[...] + jnp.sum(hot, axis=1, keepdims=True)
    cnt_ref[...] = carry_scr[...]

    ranks = [jnp.sum(jnp.where(s, excl, 0.0), axis=0, keepdims=True) for s in sels]
    idx_ref[...] = jnp.concatenate(idxs, axis=0).astype(I32)
    rank_ref[...] = jnp.concatenate(ranks, axis=0).astype(I32)
    gpad = jnp.concatenate(gates + [jnp.zeros((LANES - TOP_K, tm), F32)], axis=0)
    gcol_ref[...] = gpad.T


def _mixout(x2, oa, od, proj, col_g, wo, ada3, ffn_g, wr_t, br_col, seq):
    t, d = x2.shape
    tm = min(1024, seq)
    per_b = seq // tm
    hw = oa.shape[1]
    row = lambda w: pl.BlockSpec((tm, w), lambda i: (i, 0))
    gate = lambda c: pl.BlockSpec((tm, d // 2), lambda i, c=c: (i, col_g + c))
    mod = lambda c: pl.BlockSpec((1, 1, d), lambda i, c=c: (i // per_b, 0, c))
    full = lambda a: pl.BlockSpec(a.shape, lambda i: (0,) * a.ndim)
    return pl.pallas_call(
        _mixout_body,
        grid=(t // tm,),
        in_specs=[
            row(d), row(hw), row(hw),
            gate(0), gate(1), gate(2), gate(3),
            full(wo),
            mod(2),
            full(ffn_g), mod(3), mod(4),
            full(wr_t), full(br_col),
        ],
        out_specs=[
            row(d), row(d // 2),
            pl.BlockSpec((TOP_K, tm), lambda i: (0, i)),
            pl.BlockSpec((TOP_K, tm), lambda i: (0, i)),
            pl.BlockSpec((tm, LANES), lambda i: (i, 0)),
            pl.BlockSpec((N_EXPERTS, LANES), lambda i: (0, 0)),
        ],
        out_shape=[
            jax.ShapeDtypeStruct((t, d), F32),
            jax.ShapeDtypeStruct((t, d // 2), I32),
            jax.ShapeDtypeStruct((TOP_K, t), I32),
            jax.ShapeDtypeStruct((TOP_K, t), I32),
            jax.ShapeDtypeStruct((t, LANES), F32),
            jax.ShapeDtypeStruct((N_EXPERTS, LANES), F32),
        ],
        scratch_shapes=[pltpu.VMEM((N_EXPERTS, LANES), F32), pltpu.VMEM(wo.shape, BF16)],
        compiler_params=pltpu.CompilerParams(
            dimension_semantics=("arbitrary",), vmem_limit_bytes=VMEM_LIMIT),
        name="mixout",
    )(x2, oa, od, proj, proj, proj, proj, wo, ada3, ffn_g, ada3, ada3, wr_t, br_col)


def _sc_mesh():
    return plsc.VectorSubcoreMesh(core_axis_name="c", subcore_axis_name="s")


def _sc_worker_id():
    return lax.axis_index("s") * SC_CORES + lax.axis_index("c")


def _sc_dispatch(h2, dest, n_rows):
    t, d = h2.shape
    assert t % (SC_WORKERS * SC_CHUNK) == 0
    cpw = t // (SC_WORKERS * SC_CHUNK)
    dest_w = dest.reshape(TOP_K, SC_WORKERS, cpw, SC_CHUNK).transpose(1, 0, 2, 3).reshape(-1, SC_CHUNK)
    ipw = TOP_K * cpw

    def body(h2_hbm, dest_hbm, xb_hbm, idx_v, rows_v):
        wid = _sc_worker_id()
        pltpu.sync_copy(dest_hbm.at[pl.ds(pl.multiple_of(wid * ipw, ipw), ipw)], idx_v)

        @pl.loop(0, cpw)
        def _(c):
            t0 = pl.multiple_of((wid * cpw + c) * SC_CHUNK, SC_CHUNK)
            pltpu.sync_copy(h2_hbm.at[pl.ds(t0, SC_CHUNK)], rows_v)
            for k in range(TOP_K):
                pltpu.sync_copy(rows_v, xb_hbm.at[idx_v.at[k * cpw + c]])

    return pl.kernel(
        body, out_type=jax.ShapeDtypeStruct((n_rows, d), h2.dtype), mesh=_sc_mesh(),
        scratch_types=[pltpu.VMEM((ipw, SC_CHUNK), I32), pltpu.VMEM((SC_CHUNK, d), h2.dtype)],
        name="dispatch",
    )(h2, dest_w)


def _sc_undispatch(y, dest):
    n_asg = dest.shape[0]
    d = y.shape[1]
    rows = SC_CHUNK // 2
    assert n_asg % (2 * SC_WORKERS * rows) == 0
    cpw = n_asg // (SC_WORKERS * rows)

    def body(y_hbm, dest_hbm, yt_hbm, idx_v, rows_v, gsem, wsem):
        wid = _sc_worker_id()
        pltpu.sync_copy(dest_hbm.at[pl.ds(pl.multiple_of(wid * cpw, cpw), cpw)], idx_v)

        def gather(c, b):
            return pltpu.make_async_copy(y_hbm.at[idx_v.at[c]], rows_v.at[b], gsem.at[b])

        def write(c, b):
            r0 = pl.multiple_of((wid * cpw + c) * rows, rows)
            return pltpu.make_async_copy(rows_v.at[b], yt_hbm.at[pl.ds(r0, rows)], wsem.at[b])

        gather(0, 0).start()

        @pl.loop(0, cpw, step=2)
        def _(c0):
            for b in range(2):
                c = c0 + b

                @pl.when(c + 1 < cpw)
                def _():
                    @pl.when(c >= 1)
                    def _():
                        write(c - 1, 1 - b).wait()
                    gather(c + 1, 1 - b).start()

                gather(c, b).wait()
                write(c, b).start()

        write(cpw - 2, 0).wait()
        write(cpw - 1, 1).wait()

    return pl.kernel(
        body, out_type=jax.ShapeDtypeStruct((n_asg, d), y.dtype), mesh=_sc_mesh(),
        scratch_types=[pltpu.VMEM((cpw, rows), I32), pltpu.VMEM((2, rows, d), y.dtype),
                       pltpu.SemaphoreType.DMA((2,)), pltpu.SemaphoreType.DMA((2,))],
        name="undispatch",
    )(y, dest.reshape(-1, rows))


def _ffn_body(be_ref, first_ref, nxt_ref, slot_ref, nv_ref, nact_ref, x_ref, w1_hbm, b1_ref, w2_hbm, b2_ref, y_ref,
              w1f, w2f, w1c, w2c, sem1, sem2):
    j = pl.program_id(0)
    ff = w2f.shape[1]
    pair = 2 * LANES
    ngroup = (2 * ff) // pair

    def weight_copies(e, slot):
        return (pltpu.make_async_copy(w1_hbm.at[e], w1f.at[slot], sem1.at[slot]),
                pltpu.make_async_copy(w2_hbm.at[e], w2f.at[slot], sem2.at[slot]))

    @pl.when(j == 0)
    def _():
        for cp in weight_copies(be_ref[0], 0):
            cp.start()

    @pl.when(first_ref[j] == 1)
    def _():
        slot = slot_ref[j]
        for cp in weight_copies(be_ref[j], slot):
            cp.wait()

        @pl.when(nxt_ref[j] >= 0)
        def _():
            for cp in weight_copies(nxt_ref[j], 1 - slot):
                cp.start(priority=1)

        r_p = lax.broadcasted_iota(I32, (pair, pair), 0)
        c_p = lax.broadcasted_iota(I32, (pair, pair), 1)
        src = jnp.where(c_p < LANES, 2 * c_p, 2 * (c_p - LANES) + 1)
        perm = jnp.where(r_p == src, 1.0, 0.0).astype(BF16)
        for g in range(ngroup):
            cols = slice(g * pair, (g + 1) * pair)
            w1c[:, cols] = _dot(w1f[slot, :, cols].astype(BF16), perm).astype(BF16)
        w2c[...] = w2f[slot].astype(BF16)

    def expert_rows(nrows):
        x_lo, x_hi = _unpack_halves(x_ref[0:nrows, :])
        xb = jnp.concatenate([x_lo.astype(BF16), x_hi.astype(BF16)], axis=1)
        u = _dot(xb, w1c[...]) + b1_ref[0]
        acts = []
        for g in range(ngroup):
            glu = jnp.minimum(u[:, g * pair:g * pair + LANES], SWIGLU_LIMIT)
            lin = jnp.clip(u[:, g * pair + LANES:(g + 1) * pair], -SWIGLU_LIMIT, SWIGLU_LIMIT)
            acts.append((glu * _sigmoid(SWIGLU_ALPHA * glu) * (lin + 1.0)).astype(BF16))
        act = jnp.concatenate(acts, axis=1)
        y_ref[0:nrows, :] = _pack_halves(_dot(act, w2c[...]) + b2_ref[0])

    active = j < nact_ref[0]
    half = x_ref.shape[0] // 2
    pl.when(active & (nv_ref[j] > half))(functools.partial(expert_rows, x_ref.shape[0]))
    pl.when(active & (nv_ref[j] <= half))(functools.partial(expert_rows, half))


def _ffn(block_expert, first, nxt, slot, nvalid, nact, xb, w1, b1p, w2, b2):
    bm = FFN_BLOCK
    n_rows, dw = xb.shape
    d = 2 * dw
    n_blocks = n_rows // bm
    ff2 = w1.shape[2]
    ff = w2.shape[1]
    row_blk = lambda j, be, fi, nx, sl, nv, na: (jnp.minimum(j, na[0] - 1), 0)
    bias_blk = lambda j, be, fi, nx, sl, nv, na: (be[j], 0, 0)
    grid_spec = pltpu.PrefetchScalarGridSpec(
        num_scalar_prefetch=6,
        grid=(n_blocks,),
        in_specs=[
            pl.BlockSpec((bm, dw), row_blk),
            pl.BlockSpec(memory_space=pl.ANY),
            pl.BlockSpec((1, 1, ff2), bias_blk),
            pl.BlockSpec(memory_space=pl.ANY),
            pl.BlockSpec((1, 1, d), bias_blk),
        ],
        out_specs=pl.BlockSpec((bm, dw), row_blk),
        scratch_shapes=[
            pltpu.VMEM((2, d, ff2), F32), pltpu.VMEM((2, ff, d), F32),
            pltpu.VMEM((d, ff2), BF16), pltpu.VMEM((ff, d), BF16),
            pltpu.SemaphoreType.DMA((2,)), pltpu.SemaphoreType.DMA((2,)),
        ],
    )
    return pl.pallas_call(
        _ffn_body,
        grid_spec=grid_spec,
        out_shape=jax.ShapeDtypeStruct((n_rows, dw), I32),
        compiler_params=pltpu.CompilerParams(
            dimension_semantics=("arbitrary",), vmem_limit_bytes=VMEM_LIMIT),
        name="ffn",
    )(block_expert, first, nxt, slot, nvalid, nact, xb, w1, b1p, w2, b2)


def _combine_body(x1_ref, y0_ref, y1_ref, y2_ref, y3_ref, gcol_ref, g2_ref, o_ref):
    gc = gcol_ref[...]
    m_lo = m_hi = None
    for k, y_ref in enumerate((y0_ref, y1_ref, y2_ref, y3_ref)):
        lo, hi = _unpack_halves(y_ref[...])
        gk = gc[:, k:k + 1]
        m_lo = gk * lo if m_lo is None else m_lo + gk * lo
        m_hi = gk * hi if m_hi is None else m_hi + gk * hi
    m = jnp.concatenate([m_lo, m_hi], axis=1)
    o_ref[...] = (x1_ref[...] + g2_ref[0] * m).astype(o_ref.dtype)


def _combine(x1, yall, gcol, ada3, seq, out_dtype):
    t, d = x1.shape
    tm = min(1024, seq)
    per_b = seq // tm
    nt = t // tm
    yk = lambda k: pl.BlockSpec((tm, d // 2), lambda i, k=k: (k * nt + i, 0))
    return pl.pallas_call(
        _combine_body,
        grid=(nt,),
        in_specs=[
            pl.BlockSpec((tm, d), lambda i: (i, 0)),
            yk(0), yk(1), yk(2), yk(3),
            pl.BlockSpec((tm, LANES), lambda i: (i, 0)),
            pl.BlockSpec((1, 1, d), lambda i: (i // per_b, 0, 5)),
        ],
        out_specs=pl.BlockSpec((tm, d), lambda i: (i, 0)),
        out_shape=jax.ShapeDtypeStruct((t, d), out_dtype),
        compiler_params=pltpu.CompilerParams(
            dimension_semantics=("arbitrary",), vmem_limit_bytes=VMEM_LIMIT),
        name="combine",
    )(x1, yall, yall, yall, yall, gcol, ada3)


def _route_tables(idx, rank, counts, n_tok):
    bm = FFN_BLOCK
    n_asg = TOP_K * n_tok
    n_blocks = -(-(n_asg + N_EXPERTS * (bm - 1)) // bm)
    padded = (counts + bm - 1) // bm * bm
    pad_ends = jnp.cumsum(padded)
    pad_starts = pad_ends - padded
    e_ids = jnp.arange(N_EXPERTS, dtype=I32)
    start_of = jnp.sum(jnp.where(idx[None] == e_ids[:, None, None], pad_starts[:, None, None], 0), axis=0)
    dest = (start_of + rank).reshape(-1)
    nact = (pad_ends[-1] // bm).astype(I32)
    blk_start = jnp.arange(n_blocks, dtype=I32) * bm
    last = jnp.sum(jnp.where(pad_ends <= pad_ends[-1] - 1, 1, 0)).astype(I32)
    be = jnp.sum(jnp.where(pad_ends[None, :] <= blk_start[:, None], 1, 0), axis=1).astype(I32)
    active = blk_start < pad_ends[-1]
    be = jnp.where(active, be, last)
    blk = jnp.arange(n_blocks, dtype=I32)
    first = active & ((blk == 0) | (be != jnp.roll(be, 1)))
    slot = (jnp.cumsum(first.astype(I32)) - 1) & 1
    later_first = first[None, :] & (blk[None, :] > blk[:, None])
    nxt_pos = jnp.min(jnp.where(later_first, blk[None, :], n_blocks), axis=1)
    nxt = jnp.sum(jnp.where(blk[None, :] == nxt_pos[:, None], be[None, :], 0), axis=1)
    nxt = jnp.where(nxt_pos < n_blocks, nxt, -1).astype(I32)
    mine = be[:, None] == e_ids[None, :]
    cnt_b = jnp.sum(jnp.where(mine, counts[None, :], 0), axis=1)
    start_b = jnp.sum(jnp.where(mine, pad_starts[None, :], 0), axis=1)
    nvalid = jnp.where(active, jnp.clip(cnt_b - (blk_start - start_b), 0, bm), 0).astype(I32)
    return be, first.astype(I32), nxt, slot.astype(I32), nvalid, nact.reshape(1), dest, n_blocks * bm


def kernel(x, c, w_ada, b_ada, mix_norm_g, ffn_norm_g, w_in, hg_lower_bound_logits, hg_out_norm_g, da_q_norm_g, da_k_norm_g, da_lambda_q1, da_lambda_k1, da_lambda_q2, da_lambda_k2, da_subln_g, w_out, w_router, b_router, w1, b1, w2, b2):
    bsz, seq, d = x.shape
    t = bsz * seq
    depth = w_ada.shape[0]
    out_dtype = x.dtype
    hw = HG_HEADS * HG_DV
    xcur = x.reshape(t, d)
    for l in range(depth):
        ada = _ada(c, w_ada[l], b_ada[l])
        ada3 = ada.reshape(bsz, 1, N_MOD * d)
        col_h = 0
        col_a = col_h + 4 * HG_HEADS
        col_g = (4 * hw + 3 * DA_HEADS * 2 * DA_DH) // (d // 2)
        proj = _inproj(xcur, mix_norm_g[l].reshape(1, d), ada3, w_in[l].astype(BF16), seq)

        o_a = _hgrn(proj, hg_lower_bound_logits, hg_out_norm_g[l].reshape(1, HG_DV), bsz, seq, col_h, l)
        lambda_init = 0.8 - 0.6 * math.exp(-0.3 * l)
        qg2 = jnp.tile(da_q_norm_g[l], 2).reshape(1, 2 * DA_DH)
        kg2 = jnp.tile(da_k_norm_g[l], 2).reshape(1, 2 * DA_DH)
        lam4 = jnp.stack([da_lambda_q1[l], da_lambda_k1[l], da_lambda_q2[l], da_lambda_k2[l]])
        o_d = _attn(proj, qg2, kg2, lam4, da_subln_g[l].reshape(1, 2 * DA_DH), bsz, seq, col_a, lambda_init)

        x1, h2, idx, rank, gcol, cnt = _mixout(
            xcur, o_a, o_d, proj, col_g, w_out[l], ada3, ffn_norm_g[l].reshape(1, d),
            w_router[l].T, b_router[l].reshape(N_EXPERTS, 1), seq)

        counts = cnt[:, 0].astype(I32)
        be, first, nxt, slot, nvalid, nact, dest, n_rows = _route_tables(idx, rank, counts, t)
        b1p = b1[l].reshape(N_EXPERTS, -1, LANES, 2).transpose(0, 1, 3, 2).reshape(N_EXPERTS, 1, -1)
        xb = _sc_dispatch(h2, dest, n_rows)
        yb = _ffn(be, first, nxt, slot, nvalid, nact, xb, w1[l], b1p, w2[l], b2[l].reshape(N_EXPERTS, 1, d))
        yall = _sc_undispatch(yb, dest)
        xcur = _combine(x1, yall, gcol, ada3, seq, out_dtype)
    return xcur.reshape(bsz, seq, d)
```

```python
import functools
import math

import jax
import jax.numpy as jnp
from jax import lax
from jax.experimental import pallas as pl
from jax.experimental.pallas import tpu as pltpu
from jax.experimental.pallas import tpu_sc as plsc

F32 = jnp.float32
BF16 = jnp.bfloat16
I32 = jnp.int32

HG_HEADS = 4
HG_DK = 128
HG_DV = 128
HG_CHUNK = 32
DA_HEADS = 4
DA_DH = 64
N_EXPERTS = 32
TOP_K = 4
SWIGLU_ALPHA = 1.702
SWIGLU_LIMIT = 7.0
NORM_EPS = 1e-6
LOG2E = math.log2(math.e)
N_MOD = 6

LANES = 128
VMEM_LIMIT = 56 * 1024 * 1024

HG_ROWS = 256
HG_HEADS_PER_STEP = 4
ATTN_TILE = 256
DA_HEADS_PER_STEP = 2
FFN_BLOCK = 512

SC_CORES = 2
SC_WORKERS = SC_CORES * 16
SC_CHUNK = 128


def _sigmoid(x):
    return 0.5 * jnp.tanh(0.5 * x) + 0.5


def _dot(a, b):
    return jnp.dot(a, b, preferred_element_type=F32)


def _dot_nt(a, b):
    return lax.dot_general(a, b, (((1,), (1,)), ((), ())), preferred_element_type=F32)


def _split_bf16(x):
    hi = x.astype(BF16)
    lo = (x - hi.astype(F32)).astype(BF16)
    return hi, lo


def _pack_halves(x):
    return _pack_rounded(x.astype(BF16).astype(F32))


def _pack_rounded(xr):
    half = xr.shape[1] // 2
    lo = lax.bitcast_convert_type(xr[:, :half], I32)
    hi = lax.bitcast_convert_type(xr[:, half:], I32)
    return lax.shift_right_logical(lo, 16) | hi


def _unpack_halves(w):
    lo = lax.bitcast_convert_type(lax.shift_left(w, 16), F32)
    hi = lax.bitcast_convert_type(w & jnp.int32(-65536), F32)
    return lo, hi


def _ada_body(c_ref, w_ref, b_ref, o_ref):
    c = c_ref[...].astype(F32)
    ch, cl = _split_bf16(c * _sigmoid(c))
    wh, wl = _split_bf16(w_ref[...])
    o_ref[...] = _dot(ch, wh) + _dot(cl, wh) + _dot(ch, wl) + b_ref[...]


def _ada(c, w_ada, b_ada):
    bsz, d = c.shape
    n = w_ada.shape[1]
    tn = 2 * d
    return pl.pallas_call(
        _ada_body,
        grid=(n // tn,),
        in_specs=[
            pl.BlockSpec((bsz, d), lambda j: (0, 0)),
            pl.BlockSpec((d, tn), lambda j: (0, j)),
            pl.BlockSpec((1, tn), lambda j: (0, j)),
        ],
        out_specs=pl.BlockSpec((bsz, tn), lambda j: (0, j)),
        out_shape=jax.ShapeDtypeStruct((bsz, n), F32),
        name="ada",
    )(c, w_ada, b_ada.reshape(1, n))


def _norm_mod(x, g, shift, scale):
    ms = jnp.mean(x * x, axis=-1, keepdims=True)
    return x * lax.rsqrt(ms + NORM_EPS) * (g * (1.0 + scale)) + shift


def _inproj_body(x_ref, g_ref, sh_ref, sc_ref, w_ref, o_ref):
    h = _norm_mod(x_ref[...], g_ref[...], sh_ref[0], sc_ref[0])
    o_ref[...] = _dot(h.astype(BF16), w_ref[...]).astype(BF16)


def _inproj(x2, g, ada3, w_bf16, seq):
    t, d = x2.shape
    n = w_bf16.shape[1]
    tm = min(512, seq)
    nj = 1
    tn = n // nj
    per_b = seq // tm
    return pl.pallas_call(
        _inproj_body,
        grid=(nj, t // tm),
        in_specs=[
            pl.BlockSpec((tm, d), lambda j, i: (i, 0)),
            pl.BlockSpec((1, d), lambda j, i: (0, 0)),
            pl.BlockSpec((1, 1, d), lambda j, i: (i // per_b, 0, 0)),
            pl.BlockSpec((1, 1, d), lambda j, i: (i // per_b, 0, 1)),
            pl.BlockSpec((d, tn), lambda j, i: (0, j)),
        ],
        out_specs=pl.BlockSpec((tm, tn), lambda j, i: (i, j)),
        out_shape=jax.ShapeDtypeStruct((t, n), BF16),
        compiler_params=pltpu.CompilerParams(
            dimension_semantics=("arbitrary", "arbitrary"), vmem_limit_bytes=VMEM_LIMIT),
        name="inproj",
    )(x2, g, ada3, ada3, w_bf16)


def _hgrn_body(q_ref, f_ref, i_ref, og_ref, lbl_ref, g_ref, o_ref, *, seq, layer):
    rows, chunk = HG_ROWS, HG_CHUNK
    nchunk = rows // chunk
    lbl = lbl_ref[...].astype(F32)
    e = jnp.exp(lbl - jnp.max(lbl, axis=0, keepdims=True))
    lb_all = jnp.sum(e[: layer + 1], axis=0, keepdims=True) / jnp.sum(e, axis=0, keepdims=True)
    heads = q_ref.shape[1] // HG_DK
    r_i = lax.broadcasted_iota(I32, (rows, rows), 0)
    c_i = lax.broadcasted_iota(I32, (rows, rows), 1)
    tri = ((r_i // chunk) == (c_i // chunk)) & (r_i >= c_i)
    tri_b = jnp.where(tri, 1.0, 0.0).astype(BF16)
    row_chunk = lax.broadcasted_iota(I32, (rows, HG_DK), 0) // chunk
    g = g_ref[...].astype(F32)

    def block(r, st, h):
        sl = pl.ds(r * rows, rows)
        cs = slice(h * HG_DK, (h + 1) * HG_DK)
        lb = lb_all[:, cs]
        qr = q_ref[sl, cs].astype(F32)
        fr = f_ref[sl, cs].astype(F32)
        v = i_ref[sl, cs].astype(F32)
        og = og_ref[sl, cs].astype(F32)
        q = qr * _sigmoid(qr)
        f = lb + (1.0 - lb) * _sigmoid(fr)
        k = 1.0 - f
        logf = jnp.log(f)
        lhi, llo = _split_bf16(logf)
        bc2 = _dot(tri_b, jnp.concatenate([lhi, llo], axis=1))
        bcum = bc2[:, :HG_DK] + bc2[:, HG_DK:]
        b3 = bcum.reshape(nchunk, chunk, HG_DK)
        bl = b3[:, chunk - 1:chunk, :]
        dec = jnp.exp(bl)
        kt_f = k * jnp.exp(-bcum)
        qt_f = q * jnp.exp(bcum)
        kd = (kt_f.reshape(nchunk, chunk, HG_DK) * dec).reshape(rows, HG_DK)
        a = jnp.where(tri, _dot_nt(qt_f.astype(BF16), kt_f.astype(BF16)), 0.0).astype(BF16)
        vt_b = v.T.astype(BF16)
        kd_x = jnp.concatenate([jnp.where(row_chunk == c, kd, 0.0) for c in range(nchunk)], axis=1)
        kv_all = _dot(vt_b, kd_x.astype(BF16))
        starts = []
        for c in range(nchunk):
            starts.append(st.astype(BF16))
            st = st * dec[c] + kv_all[:, c * HG_DK:(c + 1) * HG_DK]
        q_x = [jnp.where(row_chunk == c, qt_f, 0.0).astype(BF16) for c in range(nchunk)]
        o = _dot_nt(jnp.concatenate([a] + q_x, axis=1), jnp.concatenate([vt_b] + starts, axis=1))
        ms = jnp.mean(o * o, axis=-1, keepdims=True)
        o = o * lax.rsqrt(ms + NORM_EPS) * g
        o_ref[sl, cs] = (o * (og * _sigmoid(og))).astype(o_ref.dtype)
        return st

    sts = [jnp.zeros((HG_DV, HG_DK), F32) for _ in range(heads)]
    for r in range(seq // rows):
        for h in range(heads):
            sts[h] = block(r, sts[h], h)


def _hgrn(proj, lb_logits, norm_g, bsz, seq, col0, layer):
    t = proj.shape[0]
    hps = HG_HEADS_PER_STEP
    wid = hps * LANES
    assert col0 % hps == 0 and HG_HEADS % hps == 0
    blk = lambda off: pl.BlockSpec((seq, wid), lambda b, h, off=off: (b, (col0 + off) // hps + h))
    return pl.pallas_call(
        functools.partial(_hgrn_body, seq=seq, layer=layer),
        grid=(bsz, HG_HEADS // hps),
        in_specs=[
            blk(0), blk(HG_HEADS), blk(2 * HG_HEADS), blk(3 * HG_HEADS),
            pl.BlockSpec((lb_logits.shape[0], wid), lambda b, h: (0, h)),
            pl.BlockSpec((1, HG_DV), lambda b, h: (0, 0)),
        ],
        out_specs=pl.BlockSpec((seq, wid), lambda b, h: (b, h)),
        out_shape=jax.ShapeDtypeStruct((t, HG_HEADS * HG_DV), BF16),
        compiler_params=pltpu.CompilerParams(
            dimension_semantics=("arbitrary", "arbitrary"), vmem_limit_bytes=VMEM_LIMIT),
        name="hgrn",
    )(proj, proj, proj, proj, lb_logits, norm_g)


def _group_norm(x, gsum_b, gain):
    ss = _dot((x * x).astype(BF16), gsum_b)
    return x * lax.rsqrt(ss * (1.0 / DA_DH) + NORM_EPS) * gain


def _attn_body(q_ref, k_ref, v_ref, qg_ref, kg_ref, lam_ref, sg_ref, o_ref, kn_scr, v1_scr, *, seq, lambda_init):
    tq = ATTN_TILE
    width = 2 * DA_DH
    r_l = lax.broadcasted_iota(I32, (width, width), 0) // DA_DH
    c_l = lax.broadcasted_iota(I32, (width, width), 1) // DA_DH
    gsum_b = jnp.where(r_l == c_l, 1.0, 0.0).astype(BF16)
    lane = lax.broadcasted_iota(I32, (1, width), 1)
    kg = kg_ref[...].astype(F32)
    qg = qg_ref[...].astype(F32) * (DA_DH ** -0.5 * LOG2E)
    sg = sg_ref[...].astype(F32) * (1.0 - lambda_init)
    ones = jnp.ones((tq, width), BF16)
    row = lax.broadcasted_iota(I32, (tq, tq), 0)
    col = lax.broadcasted_iota(I32, (tq, tq), 1)
    keep = row >= col

    lam_v = lam_ref[...].astype(F32)
    lam = (jnp.exp(jnp.sum(lam_v[0:1] * lam_v[1:2], axis=-1, keepdims=True))
           - jnp.exp(jnp.sum(lam_v[2:3] * lam_v[3:4], axis=-1, keepdims=True)) + lambda_init)

    def softmax_v(qc, nk, h):
        s = _dot_nt(qc, kn_scr[h, 0:nk, :])
        diag = jnp.where(keep, s[:, nk - tq:], -jnp.inf)
        s = diag if nk == tq else jnp.concatenate([s[:, :nk - tq], diag], axis=1)
        m = jnp.max(s, axis=-1, keepdims=True)
        return _dot(jnp.exp2(s - m).astype(BF16), v1_scr[h, 0:nk, :])

    for i in range(seq // tq):
        sl = slice(i * tq, (i + 1) * tq)
        nk = (i + 1) * tq
        for h in range(q_ref.shape[1] // width):
            cs = slice(h * width, (h + 1) * width)
            kn_scr[h, sl, :] = _group_norm(k_ref[sl, cs].astype(F32), gsum_b, kg).astype(BF16)
            v1_scr[h, sl, :] = jnp.concatenate([v_ref[sl, cs], ones], axis=1)
            qn = _group_norm(q_ref[sl, cs].astype(F32), gsum_b, qg)
            a1 = softmax_v(jnp.where(lane < DA_DH, qn, 0.0).astype(BF16), nk, h)
            a2 = softmax_v(jnp.where(lane >= DA_DH, qn, 0.0).astype(BF16), nk, h)
            o = a1[:, :width] / a1[:, width:width + 1] - lam * (a2[:, :width] / a2[:, width:width + 1])
            ms = jnp.mean(o * o, axis=-1, keepdims=True)
            o_ref[sl, cs] = (o * lax.rsqrt(ms + NORM_EPS) * sg).astype(o_ref.dtype)


def _attn(proj, qg2, kg2, lam4, subln_g, bsz, seq, col0, lambda_init):
    t = proj.shape[0]
    width = 2 * DA_DH
    hps = DA_HEADS_PER_STEP
    assert col0 % hps == 0 and DA_HEADS % hps == 0
    blk = lambda off: pl.BlockSpec((seq, hps * width), lambda b, h, off=off: (b, (col0 + off) // hps + h))
    return pl.pallas_call(
        functools.partial(_attn_body, seq=seq, lambda_init=lambda_init),
        grid=(bsz, DA_HEADS // hps),
        in_specs=[
            blk(0), blk(DA_HEADS), blk(2 * DA_HEADS),
            pl.BlockSpec((1, width), lambda b, h: (0, 0)),
            pl.BlockSpec((1, width), lambda b, h: (0, 0)),
            pl.BlockSpec((4, DA_DH), lambda b, h: (0, 0)),
            pl.BlockSpec((1, width), lambda b, h: (0, 0)),
        ],
        out_specs=pl.BlockSpec((seq, hps * width), lambda b, h: (b, h)),
        out_shape=jax.ShapeDtypeStruct((t, DA_HEADS * width), BF16),
        scratch_shapes=[pltpu.VMEM((hps, seq, width), BF16), pltpu.VMEM((hps, seq, 2 * width), BF16)],
        compiler_params=pltpu.CompilerParams(
            dimension_semantics=("arbitrary", "arbitrary"), vmem_limit_bytes=VMEM_LIMIT),
        name="attn",
    )(proj, proj, proj, qg2, kg2, lam4, subln_g)


def _mixout_body(x_ref, oa_ref, od_ref, ga0_ref, ga1_ref, gd0_ref, gd1_ref, wo_ref, g1_ref, g_ref, sh_ref, sc_ref,
                 wr_ref, br_ref,
                 x1_ref, h2_ref, idx_ref, rank_ref, gcol_ref, cnt_ref, carry_scr, wo_scr):
    i = pl.program_id(0)
    tm = x_ref.shape[0]
    hw = oa_ref.shape[1]

    @pl.when(i == 0)
    def _():
        carry_scr[...] = jnp.zeros_like(carry_scr)
        wo_scr[...] = wo_ref[...].astype(BF16)

    ya = _dot(oa_ref[...], wo_scr[0:hw, :])
    yd = _dot(od_ref[...], wo_scr[hw:, :])
    ga = jnp.concatenate([ga0_ref[...], ga1_ref[...]], axis=1).astype(F32)
    gd = jnp.concatenate([gd0_ref[...], gd1_ref[...]], axis=1).astype(F32)
    y = _sigmoid(ga) * ya + _sigmoid(gd) * yd
    x1 = x_ref[...] + g1_ref[0] * y
    x1_ref[...] = x1
    h2 = _norm_mod(x1, g_ref[...], sh_ref[0], sc_ref[0])
    hh = h2.astype(BF16)
    h2r = hh.astype(F32)
    h2_ref[...] = _pack_rounded(h2r)

    hl = (h2 - h2r).astype(BF16)
    wh, wl = _split_bf16(wr_ref[...])
    logits = _dot_nt(wh, hh) + _dot_nt(wl, hh) + _dot_nt(wh, hl) + br_ref[...]

    e_iota = lax.broadcasted_iota(I32, (N_EXPERTS, tm), 0).astype(F32)
    vals = logits
    tops, sels, idxs = [], [], []
    for _ in range(TOP_K):
        m = jnp.max(vals, axis=0, keepdims=True)
        idx = jnp.min(jnp.where(vals == m, e_iota, float(N_EXPERTS)), axis=0, keepdims=True)
        sel = e_iota == idx
        vals = jnp.where(sel, -jnp.inf, vals)
        tops.append(m)
        sels.append(sel)
        idxs.append(idx)
    ex = [jnp.exp(tv - tops[0]) for tv in tops]
    den = ex[0] + ex[1] + ex[2] + ex[3]
    gates = [v / den for v in ex]

    hot = jnp.where(sels[0] | sels[1] | sels[2] | sels[3], 1.0, 0.0)
    r_t = lax.broadcasted_iota(I32, (tm, tm), 0)
    c_t = lax.broadcasted_iota(I32, (tm, tm), 1)
    upper = jnp.where(r_t < c_t, 1.0, 0.0).astype(BF16)
    excl = _dot(hot.astype(BF16), upper) + carry_scr[:, 0:1]
    carry_scr[...] = carry_scr[...] + jnp.sum(hot, axis=1, keepdims=True)
    cnt_ref[...] = carry_scr[...]

    ranks = [jnp.sum(jnp.where(s, excl, 0.0), axis=0, keepdims=True) for s in sels]
    idx_ref[...] = jnp.concatenate(idxs, axis=0).astype(I32)
    rank_ref[...] = jnp.concatenate(ranks, axis=0).astype(I32)
    gpad = jnp.concatenate(gates + [jnp.zeros((LANES - TOP_K, tm), F32)], axis=0)
    gcol_ref[...] = gpad.T


def _mixout(x2, oa, od, proj, col_g, wo, ada3, ffn_g, wr_t, br_col, seq):
    t, d = x2.shape
    tm = min(1024, seq)
    per_b = seq // tm
    hw = oa.shape[1]
    row = lambda w: pl.BlockSpec((tm, w), lambda i: (i, 0))
    gate = lambda c: pl.BlockSpec((tm, d // 2), lambda i, c=c: (i, col_g + c))
    mod = lambda c: pl.BlockSpec((1, 1, d), lambda i, c=c: (i // per_b, 0, c))
    full = lambda a: pl.BlockSpec(a.shape, lambda i: (0,) * a.ndim)
    return pl.pallas_call(
        _mixout_body,
        grid=(t // tm,),
        in_specs=[
            row(d), row(hw), row(hw),
            gate(0), gate(1), gate(2), gate(3),
            full(wo),
            mod(2),
            full(ffn_g), mod(3), mod(4),
            full(wr_t), full(br_col),
        ],
        out_specs=[
            row(d), row(d // 2),
            pl.BlockSpec((TOP_K, tm), lambda i: (0, i)),
            pl.BlockSpec((TOP_K, tm), lambda i: (0, i)),
            pl.BlockSpec((tm, LANES), lambda i: (i, 0)),
            pl.BlockSpec((N_EXPERTS, LANES), lambda i: (0, 0)),
        ],
        out_shape=[
            jax.ShapeDtypeStruct((t, d), F32),
            jax.ShapeDtypeStruct((t, d // 2), I32),
            jax.ShapeDtypeStruct((TOP_K, t), I32),
            jax.ShapeDtypeStruct((TOP_K, t), I32),
            jax.ShapeDtypeStruct((t, LANES), F32),
            jax.ShapeDtypeStruct((N_EXPERTS, LANES), F32),
        ],
        scratch_shapes=[pltpu.VMEM((N_EXPERTS, LANES), F32), pltpu.VMEM(wo.shape, BF16)],
        compiler_params=pltpu.CompilerParams(
            dimension_semantics=("arbitrary",), vmem_limit_bytes=VMEM_LIMIT),
        name="mixout",
    )(x2, oa, od, proj, proj, proj, proj, wo, ada3, ffn_g, ada3, ada3, wr_t, br_col)


def _sc_mesh():
    return plsc.VectorSubcoreMesh(core_axis_name="c", subcore_axis_name="s")


def _sc_worker_id():
    return lax.axis_index("s") * SC_CORES + lax.axis_index("c")


def _sc_dispatch(h2, dest, n_rows):
    t, d = h2.shape
    assert t % (SC_WORKERS * SC_CHUNK) == 0
    cpw = t // (SC_WORKERS * SC_CHUNK)
    dest_w = dest.reshape(TOP_K, SC_WORKERS, cpw, SC_CHUNK).transpose(1, 0, 2, 3).reshape(-1, SC_CHUNK)
    ipw = TOP_K * cpw

    def body(h2_hbm, dest_hbm, xb_hbm, idx_v, rows_v):
        wid = _sc_worker_id()
        pltpu.sync_copy(dest_hbm.at[pl.ds(pl.multiple_of(wid * ipw, ipw), ipw)], idx_v)

        @pl.loop(0, cpw)
        def _(c):
            t0 = pl.multiple_of((wid * cpw + c) * SC_CHUNK, SC_CHUNK)
            pltpu.sync_copy(h2_hbm.at[pl.ds(t0, SC_CHUNK)], rows_v)
            for k in range(TOP_K):
                pltpu.sync_copy(rows_v, xb_hbm.at[idx_v.at[k * cpw + c]])

    return pl.kernel(
        body, out_type=jax.ShapeDtypeStruct((n_rows, d), h2.dtype), mesh=_sc_mesh(),
        scratch_types=[pltpu.VMEM((ipw, SC_CHUNK), I32), pltpu.VMEM((SC_CHUNK, d), h2.dtype)],
        name="dispatch",
    )(h2, dest_w)


def _sc_undispatch(y, dest):
    n_asg = dest.shape[0]
    d = y.shape[1]
    rows = SC_CHUNK // 2
    assert n_asg % (2 * SC_WORKERS * rows) == 0
    cpw = n_asg // (SC_WORKERS * rows)

    def body(y_hbm, dest_hbm, yt_hbm, idx_v, rows_v, gsem, wsem):
        wid = _sc_worker_id()
        pltpu.sync_copy(dest_hbm.at[pl.ds(pl.multiple_of(wid * cpw, cpw), cpw)], idx_v)

        def gather(c, b):
            return pltpu.make_async_copy(y_hbm.at[idx_v.at[c]], rows_v.at[b], gsem.at[b])

        def write(c, b):
            r0 = pl.multiple_of((wid * cpw + c) * rows, rows)
            return pltpu.make_async_copy(rows_v.at[b], yt_hbm.at[pl.ds(r0, rows)], wsem.at[b])

        gather(0, 0).start()

        @pl.loop(0, cpw, step=2)
        def _(c0):
            for b in range(2):
                c = c0 + b

                @pl.when(c + 1 < cpw)
                def _():
                    @pl.when(c >= 1)
                    def _():
                        write(c - 1, 1 - b).wait()
                    gather(c + 1, 1 - b).start()

                gather(c, b).wait()
                write(c, b).start()

        write(cpw - 2, 0).wait()
        write(cpw - 1, 1).wait()

    return pl.kernel(
        body, out_type=jax.ShapeDtypeStruct((n_asg, d), y.dtype), mesh=_sc_mesh(),
        scratch_types=[pltpu.VMEM((cpw, rows), I32), pltpu.VMEM((2, rows, d), y.dtype),
                       pltpu.SemaphoreType.DMA((2,)), pltpu.SemaphoreType.DMA((2,))],
        name="undispatch",
    )(y, dest.reshape(-1, rows))


def _ffn_body(be_ref, first_ref, nxt_ref, slot_ref, nv_ref, nact_ref, x_ref, w1_hbm, b1_ref, w2_hbm, b2_ref, y_ref,
              w1f, w2f, w1c, w2c, sem1, sem2):
    j = pl.program_id(0)
    ff = w2f.shape[1]
    pair = 2 * LANES
    ngroup = (2 * ff) // pair

    def weight_copies(e, slot):
        return (pltpu.make_async_copy(w1_hbm.at[e], w1f.at[slot], sem1.at[slot]),
                pltpu.make_async_copy(w2_hbm.at[e], w2f.at[slot], sem2.at[slot]))

    @pl.when(j == 0)
    def _():
        for cp in weight_copies(be_ref[0], 0):
            cp.start()

    @pl.when(first_ref[j] == 1)
    def _():
        slot = slot_ref[j]
        for cp in weight_copies(be_ref[j], slot):
            cp.wait()

        @pl.when(nxt_ref[j] >= 0)
        def _():
            for cp in weight_copies(nxt_ref[j], 1 - slot):
                cp.start()

        r_p = lax.broadcasted_iota(I32, (pair, pair), 0)
        c_p = lax.broadcasted_iota(I32, (pair, pair), 1)
        src = jnp.where(c_p < LANES, 2 * c_p, 2 * (c_p - LANES) + 1)
        perm = jnp.where(r_p == src, 1.0, 0.0).astype(BF16)
        for g in range(ngroup):
            cols = slice(g * pair, (g + 1) * pair)
            w1c[:, cols] = _dot(w1f[slot, :, cols].astype(BF16), perm).astype(BF16)
        w2c[...] = w2f[slot].astype(BF16)

    def expert_rows(nrows):
        x_lo, x_hi = _unpack_halves(x_ref[0:nrows, :])
        xb = jnp.concatenate([x_lo.astype(BF16), x_hi.astype(BF16)], axis=1)
        u = _dot(xb, w1c[...]) + b1_ref[0]
        acts = []
        for g in range(ngroup):
            glu = jnp.minimum(u[:, g * pair:g * pair + LANES], SWIGLU_LIMIT)
            lin = jnp.clip(u[:, g * pair + LANES:(g + 1) * pair], -SWIGLU_LIMIT, SWIGLU_LIMIT)
            acts.append((glu * _sigmoid(SWIGLU_ALPHA * glu) * (lin + 1.0)).astype(BF16))
        act = jnp.concatenate(acts, axis=1)
        y_ref[0:nrows, :] = _pack_halves(_dot(act, w2c[...]) + b2_ref[0])

    active = j < nact_ref[0]
    half = x_ref.shape[0] // 2
    pl.when(active & (nv_ref[j] > half))(functools.partial(expert_rows, x_ref.shape[0]))
    pl.when(active & (nv_ref[j] <= half))(functools.partial(expert_rows, half))


def _ffn(block_expert, first, nxt, slot, nvalid, nact, xb, w1, b1p, w2, b2):
    bm = FFN_BLOCK
    n_rows, dw = xb.shape
    d = 2 * dw
    n_blocks = n_rows // bm
    ff2 = w1.shape[2]
    ff = w2.shape[1]
    row_blk = lambda j, be, fi, nx, sl, nv, na: (jnp.minimum(j, na[0] - 1), 0)
    bias_blk = lambda j, be, fi, nx, sl, nv, na: (be[j], 0, 0)
    grid_spec = pltpu.PrefetchScalarGridSpec(
        num_scalar_prefetch=6,
        grid=(n_blocks,),
        in_specs=[
            pl.BlockSpec((bm, dw), row_blk),
            pl.BlockSpec(memory_space=pl.ANY),
            pl.BlockSpec((1, 1, ff2), bias_blk),
            pl.BlockSpec(memory_space=pl.ANY),
            pl.BlockSpec((1, 1, d), bias_blk),
        ],
        out_specs=pl.BlockSpec((bm, dw), row_blk),
        scratch_shapes=[
            pltpu.VMEM((2, d, ff2), F32), pltpu.VMEM((2, ff, d), F32),
            pltpu.VMEM((d, ff2), BF16), pltpu.VMEM((ff, d), BF16),
            pltpu.SemaphoreType.DMA((2,)), pltpu.SemaphoreType.DMA((2,)),
        ],
    )
    return pl.pallas_call(
        _ffn_body,
        grid_spec=grid_spec,
        out_shape=jax.ShapeDtypeStruct((n_rows, dw), I32),
        compiler_params=pltpu.CompilerParams(
            dimension_semantics=("arbitrary",), vmem_limit_bytes=VMEM_LIMIT),
        name="ffn",
    )(block_expert, first, nxt, slot, nvalid, nact, xb, w1, b1p, w2, b2)


def _combine_body(x1_ref, y0_ref, y1_ref, y2_ref, y3_ref, gcol_ref, g2_ref, o_ref):
    gc = gcol_ref[...]
    m_lo = m_hi = None
    for k, y_ref in enumerate((y0_ref, y1_ref, y2_ref, y3_ref)):
        lo, hi = _unpack_halves(y_ref[...])
        gk = gc[:, k:k + 1]
        m_lo = gk * lo if m_lo is None else m_lo + gk * lo
        m_hi = gk * hi if m_hi is None else m_hi + gk * hi
    m = jnp.concatenate([m_lo, m_hi], axis=1)
    o_ref[...] = (x1_ref[...] + g2_ref[0] * m).astype(o_ref.dtype)


def _combine(x1, yall, gcol, ada3, seq, out_dtype):
    t, d = x1.shape
    tm = min(1024, seq)
    per_b = seq // tm
    nt = t // tm
    yk = lambda k: pl.BlockSpec((tm, d // 2), lambda i, k=k: (k * nt + i, 0))
    return pl.pallas_call(
        _combine_body,
        grid=(nt,),
        in_specs=[
            pl.BlockSpec((tm, d), lambda i: (i, 0)),
            yk(0), yk(1), yk(2), yk(3),
            pl.BlockSpec((tm, LANES), lambda i: (i, 0)),
            pl.BlockSpec((1, 1, d), lambda i: (i // per_b, 0, 5)),
        ],
        out_specs=pl.BlockSpec((tm, d), lambda i: (i, 0)),
        out_shape=jax.ShapeDtypeStruct((t, d), out_dtype),
        compiler_params=pltpu.CompilerParams(
            dimension_semantics=("arbitrary",), vmem_limit_bytes=VMEM_LIMIT),
        name="combine",
    )(x1, yall, yall, yall, yall, gcol, ada3)


def _route_tables(idx, rank, counts, n_tok):
    bm = FFN_BLOCK
    n_asg = TOP_K * n_tok
    n_blocks = -(-(n_asg + N_EXPERTS * (bm - 1)) // bm)
    padded = (counts + bm - 1) // bm * bm
    pad_ends = jnp.cumsum(padded)
    pad_starts = pad_ends - padded
    e_ids = jnp.arange(N_EXPERTS, dtype=I32)
    start_of = jnp.sum(jnp.where(idx[None] == e_ids[:, None, None], pad_starts[:, None, None], 0), axis=0)
    dest = (start_of + rank).reshape(-1)
    nact = (pad_ends[-1] // bm).astype(I32)
    blk_start = jnp.arange(n_blocks, dtype=I32) * bm
    last = jnp.sum(jnp.where(pad_ends <= pad_ends[-1] - 1, 1, 0)).astype(I32)
    be = jnp.sum(jnp.where(pad_ends[None, :] <= blk_start[:, None], 1, 0), axis=1).astype(I32)
    active = blk_start < pad_ends[-1]
    be = jnp.where(active, be, last)
    blk = jnp.arange(n_blocks, dtype=I32)
    first = active & ((blk == 0) | (be != jnp.roll(be, 1)))
    slot = (jnp.cumsum(first.astype(I32)) - 1) & 1
    later_first = first[None, :] & (blk[None, :] > blk[:, None])
    nxt_pos = jnp.min(jnp.where(later_first, blk[None, :], n_blocks), axis=1)
    nxt = jnp.sum(jnp.where(blk[None, :] == nxt_pos[:, None], be[None, :], 0), axis=1)
    nxt = jnp.where(nxt_pos < n_blocks, nxt, -1).astype(I32)
    mine = be[:, None] == e_ids[None, :]
    cnt_b = jnp.sum(jnp.where(mine, counts[None, :], 0), axis=1)
    start_b = jnp.sum(jnp.where(mine, pad_starts[None, :], 0), axis=1)
    nvalid = jnp.where(active, jnp.clip(cnt_b - (blk_start - start_b), 0, bm), 0).astype(I32)
    return be, first.astype(I32), nxt, slot.astype(I32), nvalid, nact.reshape(1), dest, n_blocks * bm


def kernel(x, c, w_ada, b_ada, mix_norm_g, ffn_norm_g, w_in, hg_lower_bound_logits, hg_out_norm_g, da_q_norm_g, da_k_norm_g, da_lambda_q1, da_lambda_k1, da_lambda_q2, da_lambda_k2, da_subln_g, w_out, w_router, b_router, w1, b1, w2, b2):
    bsz, seq, d = x.shape
    t = bsz * seq
    depth = w_ada.shape[0]
    out_dtype = x.dtype
    hw = HG_HEADS * HG_DV
    xcur = x.reshape(t, d)
    for l in range(depth):
        ada = _ada(c, w_ada[l], b_ada[l])
        ada3 = ada.reshape(bsz, 1, N_MOD * d)
        col_h = 0
        col_a = col_h + 4 * HG_HEADS
        col_g = (4 * hw + 3 * DA_HEADS * 2 * DA_DH) // (d // 2)
        proj = _inproj(xcur, mix_norm_g[l].reshape(1, d), ada3, w_in[l].astype(BF16), seq)

        o_a = _hgrn(proj, hg_lower_bound_logits, hg_out_norm_g[l].reshape(1, HG_DV), bsz, seq, col_h, l)
        lambda_init = 0.8 - 0.6 * math.exp(-0.3 * l)
        qg2 = jnp.tile(da_q_norm_g[l], 2).reshape(1, 2 * DA_DH)
        kg2 = jnp.tile(da_k_norm_g[l], 2).reshape(1, 2 * DA_DH)
        lam4 = jnp.stack([da_lambda_q1[l], da_lambda_k1[l], da_lambda_q2[l], da_lambda_k2[l]])
        o_d = _attn(proj, qg2, kg2, lam4, da_subln_g[l].reshape(1, 2 * DA_DH), bsz, seq, col_a, lambda_init)

        x1, h2, idx, rank, gcol, cnt = _mixout(
            xcur, o_a, o_d, proj, col_g, w_out[l], ada3, ffn_norm_g[l].reshape(1, d),
            w_router[l].T, b_router[l].reshape(N_EXPERTS, 1), seq)

        counts = cnt[:, 0].astype(I32)
        be, first, nxt, slot, nvalid, nact, dest, n_rows = _route_tables(idx, rank, counts, t)
        b1p = b1[l].reshape(N_EXPERTS, -1, LANES, 2).transpose(0, 1, 3, 2).reshape(N_EXPERTS, 1, -1)
        xb = _sc_dispatch(h2, dest, n_rows)
        yb = _ffn(be, first, nxt, slot, nvalid, nact, xb, w1[l], b1p, w2[l], b2[l].reshape(N_EXPERTS, 1, d))
        yall = _sc_undispatch(yb, dest)
        xcur = _combine(x1, yall, gcol, ada3, seq, out_dtype)
    return xcur.reshape(bsz, seq, d)
```

```python
import functools
import math

import jax
import jax.numpy as jnp
from jax import lax
from jax.experimental import pallas as pl
from jax.experimental.pallas import tpu as pltpu
from jax.experimental.pallas import tpu_sc as plsc

F32 = jnp.float32
BF16 = jnp.bfloat16
I32 = jnp.int32

HG_HEADS = 4
HG_DK = 128
HG_DV = 128
HG_CHUNK = 32
DA_HEADS = 4
DA_DH = 64
N_EXPERTS = 32
TOP_K = 4
SWIGLU_ALPHA = 1.702
SWIGLU_LIMIT = 7.0
NORM_EPS = 1e-6
LOG2E = math.log2(math.e)
N_MOD = 6

LANES = 128
VMEM_LIMIT = 56 * 1024 * 1024

HG_ROWS = 256
HG_HEADS_PER_STEP = 4
ATTN_TILE = 256
DA_HEADS_PER_STEP = 2
FFN_BLOCK = 512

SC_CORES = 2
SC_WORKERS = SC_CORES * 16
SC_CHUNK = 128


def _sigmoid(x):
    return 0.5 * jnp.tanh(0.5 * x) + 0.5


def _dot(a, b):
    return jnp.dot(a, b, preferred_element_type=F32)


def _dot_nt(a, b):
    return lax.dot_general(a, b, (((1,), (1,)), ((), ())), preferred_element_type=F32)


def _split_bf16(x):
    hi = x.astype(BF16)
    lo = (x - hi.astype(F32)).astype(BF16)
    return hi, lo


def _pack_halves(x):
    return _pack_rounded(x.astype(BF16).astype(F32))


def _pack_rounded(xr):
    half = xr.shape[1] // 2
    lo = lax.bitcast_convert_type(xr[:, :half], I32)
    hi = lax.bitcast_convert_type(xr[:, half:], I32)
    return lax.shift_right_logical(lo, 16) | hi


def _unpack_halves(w):
    lo = lax.bitcast_convert_type(lax.shift_left(w, 16), F32)
    hi = lax.bitcast_convert_type(w & jnp.int32(-65536), F32)
    return lo, hi


def _ada_body(c_ref, w_ref, b_ref, o_ref):
    c = c_ref[...].astype(F32)
    ch, cl = _split_bf16(c * _sigmoid(c))
    wh, wl = _split_bf16(w_ref[...])
    o_ref[...] = _dot(ch, wh) + _dot(cl, wh) + _dot(ch, wl) + b_ref[...]


def _ada(c, w_ada, b_ada):
    bsz, d = c.shape
    n = w_ada.shape[1]
    tn = 2 * d
    return pl.pallas_call(
        _ada_body,
        grid=(n // tn,),
        in_specs=[
            pl.BlockSpec((bsz, d), lambda j: (0, 0)),
            pl.BlockSpec((d, tn), lambda j: (0, j)),
            pl.BlockSpec((1, tn), lambda j: (0, j)),
        ],
        out_specs=pl.BlockSpec((bsz, tn), lambda j: (0, j)),
        out_shape=jax.ShapeDtypeStruct((bsz, n), F32),
        name="ada",
    )(c, w_ada, b_ada.reshape(1, n))


def _norm_mod(x, g, shift, scale):
    ms = jnp.mean(x * x, axis=-1, keepdims=True)
    return x * lax.rsqrt(ms + NORM_EPS) * (g * (1.0 + scale)) + shift


def _inproj_body(x_ref, g_ref, sh_ref, sc_ref, w_ref, o_ref):
    h = _norm_mod(x_ref[...], g_ref[...], sh_ref[0], sc_ref[0])
    o_ref[...] = _dot(h.astype(BF16), w_ref[...]).astype(BF16)


def _inproj(x2, g, ada3, w_bf16, seq):
    t, d = x2.shape
    n = w_bf16.shape[1]
    tm = min(512, seq)
    nj = 1
    tn = n // nj
    per_b = seq // tm
    return pl.pallas_call(
        _inproj_body,
        grid=(nj, t // tm),
        in_specs=[
            pl.BlockSpec((tm, d), lambda j, i: (i, 0)),
            pl.BlockSpec((1, d), lambda j, i: (0, 0)),
            pl.BlockSpec((1, 1, d), lambda j, i: (i // per_b, 0, 0)),
            pl.BlockSpec((1, 1, d), lambda j, i: (i // per_b, 0, 1)),
            pl.BlockSpec((d, tn), lambda j, i: (0, j)),
        ],
        out_specs=pl.BlockSpec((tm, tn), lambda j, i: (i, j)),
        out_shape=jax.ShapeDtypeStruct((t, n), BF16),
        compiler_params=pltpu.CompilerParams(
            dimension_semantics=("arbitrary", "arbitrary"), vmem_limit_bytes=VMEM_LIMIT),
        name="inproj",
    )(x2, g, ada3, ada3, w_bf16)


def _hgrn_body(q_ref, f_ref, i_ref, og_ref, lbl_ref, g_ref, o_ref, *, seq, layer):
    rows, chunk = HG_ROWS, HG_CHUNK
    nchunk = rows // chunk
    lbl = lbl_ref[...].astype(F32)
    e = jnp.exp(lbl - jnp.max(lbl, axis=0, keepdims=True))
    lb_all = jnp.sum(e[: layer + 1], axis=0, keepdims=True) / jnp.sum(e, axis=0, keepdims=True)
    heads = q_ref.shape[1] // HG_DK
    r_i = lax.broadcasted_iota(I32, (rows, rows), 0)
    c_i = lax.broadcasted_iota(I32, (rows, rows), 1)
    tri = ((r_i // chunk) == (c_i // chunk)) & (r_i >= c_i)
    tri_b = jnp.where(tri, 1.0, 0.0).astype(BF16)
    row_chunk = lax.broadcasted_iota(I32, (rows, HG_DK), 0) // chunk
    g = g_ref[...].astype(F32)

    def block(r, st, h):
        sl = pl.ds(r * rows, rows)
        cs = slice(h * HG_DK, (h + 1) * HG_DK)
        lb = lb_all[:, cs]
        qr = q_ref[sl, cs].astype(F32)
        fr = f_ref[sl, cs].astype(F32)
        v = i_ref[sl, cs].astype(F32)
        og = og_ref[sl, cs].astype(F32)
        q = qr * _sigmoid(qr)
        f = lb + (1.0 - lb) * _sigmoid(fr)
        k = 1.0 - f
        logf = jnp.log(f)
        lhi, llo = _split_bf16(logf)
        bc2 = _dot(tri_b, jnp.concatenate([lhi, llo], axis=1))
        bcum = bc2[:, :HG_DK] + bc2[:, HG_DK:]
        b3 = bcum.reshape(nchunk, chunk, HG_DK)
        bl = b3[:, chunk - 1:chunk, :]
        dec = jnp.exp(bl)
        kt_f = k * jnp.exp(-bcum)
        qt_f = q * jnp.exp(bcum)
        kd = (kt_f.reshape(nchunk, chunk, HG_DK) * dec).reshape(rows, HG_DK)
        a = jnp.where(tri, _dot_nt(qt_f.astype(BF16), kt_f.astype(BF16)), 0.0).astype(BF16)
        vt_b = v.T.astype(BF16)
        kd_x = jnp.concatenate([jnp.where(row_chunk == c, kd, 0.0) for c in range(nchunk)], axis=1)
        kv_all = _dot(vt_b, kd_x.astype(BF16))
        starts = []
        for c in range(nchunk):
            starts.append(st.astype(BF16))
            st = st * dec[c] + kv_all[:, c * HG_DK:(c + 1) * HG_DK]
        q_x = [jnp.where(row_chunk == c, qt_f, 0.0).astype(BF16) for c in range(nchunk)]
        o = _dot_nt(jnp.concatenate([a] + q_x, axis=1), jnp.concatenate([vt_b] + starts, axis=1))
        ms = jnp.mean(o * o, axis=-1, keepdims=True)
        o = o * lax.rsqrt(ms + NORM_EPS) * g
        o_ref[sl, cs] = (o * (og * _sigmoid(og))).astype(o_ref.dtype)
        return st

    sts = [jnp.zeros((HG_DV, HG_DK), F32) for _ in range(heads)]
    for r in range(seq // rows):
        for h in range(heads):
            sts[h] = block(r, sts[h], h)


def _hgrn(proj, lb_logits, norm_g, bsz, seq, col0, layer):
    t = proj.shape[0]
    hps = HG_HEADS_PER_STEP
    wid = hps * LANES
    assert col0 % hps == 0 and HG_HEADS % hps == 0
    blk = lambda off: pl.BlockSpec((seq, wid), lambda b, h, off=off: (b, (col0 + off) // hps + h))
    return pl.pallas_call(
        functools.partial(_hgrn_body, seq=seq, layer=layer),
        grid=(bsz, HG_HEADS // hps),
        in_specs=[
            blk(0), blk(HG_HEADS), blk(2 * HG_HEADS), blk(3 * HG_HEADS),
            pl.BlockSpec((lb_logits.shape[0], wid), lambda b, h: (0, h)),
            pl.BlockSpec((1, HG_DV), lambda b, h: (0, 0)),
        ],
        out_specs=pl.BlockSpec((seq, wid), lambda b, h: (b, h)),
        out_shape=jax.ShapeDtypeStruct((t, HG_HEADS * HG_DV), BF16),
        compiler_params=pltpu.CompilerParams(
            dimension_semantics=("arbitrary", "arbitrary"), vmem_limit_bytes=VMEM_LIMIT),
        name="hgrn",
    )(proj, proj, proj, proj, lb_logits, norm_g)


def _group_norm(x, gsum_b, gain):
    ss = _dot((x * x).astype(BF16), gsum_b)
    return x * lax.rsqrt(ss * (1.0 / DA_DH) + NORM_EPS) * gain


def _attn_body(q_ref, k_ref, v_ref, qg_ref, kg_ref, lam_ref, sg_ref, o_ref, kn_scr, v1_scr, *, seq, lambda_init):
    tq = ATTN_TILE
    width = 2 * DA_DH
    r_l = lax.broadcasted_iota(I32, (width, width), 0) // DA_DH
    c_l = lax.broadcasted_iota(I32, (width, width), 1) // DA_DH
    gsum_b = jnp.where(r_l == c_l, 1.0, 0.0).astype(BF16)
    lane = lax.broadcasted_iota(I32, (1, width), 1)
    kg = kg_ref[...].astype(F32)
    qg = qg_ref[...].astype(F32) * (DA_DH ** -0.5 * LOG2E)
    sg = sg_ref[...].astype(F32) * (1.0 - lambda_init)
    ones = jnp.ones((tq, width), BF16)
    row = lax.broadcasted_iota(I32, (tq, tq), 0)
    col = lax.broadcasted_iota(I32, (tq, tq), 1)
    keep = row >= col

    lam_v = lam_ref[...].astype(F32)
    lam = (jnp.exp(jnp.sum(lam_v[0:1] * lam_v[1:2], axis=-1, keepdims=True))
           - jnp.exp(jnp.sum(lam_v[2:3] * lam_v[3:4], axis=-1, keepdims=True)) + lambda_init)

    def softmax_v(qc, nk, h):
        s = _dot_nt(qc, kn_scr[h, 0:nk, :])
        diag = jnp.where(keep, s[:, nk - tq:], -jnp.inf)
        s = diag if nk == tq else jnp.concatenate([s[:, :nk - tq], diag], axis=1)
        m = jnp.max(s, axis=-1, keepdims=True)
        return _dot(jnp.exp2(s - m).astype(BF16), v1_scr[h, 0:nk, :])

    for i in range(seq // tq):
        sl = slice(i * tq, (i + 1) * tq)
        nk = (i + 1) * tq
        for h in range(q_ref.shape[1] // width):
            cs = slice(h * width, (h + 1) * width)
            kn_scr[h, sl, :] = _group_norm(k_ref[sl, cs].astype(F32), gsum_b, kg).astype(BF16)
            v1_scr[h, sl, :] = jnp.concatenate([v_ref[sl, cs], ones], axis=1)
            qn = _group_norm(q_ref[sl, cs].astype(F32), gsum_b, qg)
            a1 = softmax_v(jnp.where(lane < DA_DH, qn, 0.0).astype(BF16), nk, h)
            a2 = softmax_v(jnp.where(lane >= DA_DH, qn, 0.0).astype(BF16), nk, h)
            o = a1[:, :width] / a1[:, width:width + 1] - lam * (a2[:, :width] / a2[:, width:width + 1])
            ms = jnp.mean(o * o, axis=-1, keepdims=True)
            o_ref[sl, cs] = (o * lax.rsqrt(ms + NORM_EPS) * sg).astype(o_ref.dtype)


def _attn(proj, qg2, kg2, lam4, subln_g, bsz, seq, col0, lambda_init):
    t = proj.shape[0]
    width = 2 * DA_DH
    hps = DA_HEADS_PER_STEP
    assert col0 % hps == 0 and DA_HEADS % hps == 0
    blk = lambda off: pl.BlockSpec((seq, hps * width), lambda b, h, off=off: (b, (col0 + off) // hps + h))
    return pl.pallas_call(
        functools.partial(_attn_body, seq=seq, lambda_init=lambda_init),
        grid=(bsz, DA_HEADS // hps),
        in_specs=[
            blk(0), blk(DA_HEADS), blk(2 * DA_HEADS),
            pl.BlockSpec((1, width), lambda b, h: (0, 0)),
            pl.BlockSpec((1, width), lambda b, h: (0, 0)),
            pl.BlockSpec((4, DA_DH), lambda b, h: (0, 0)),
            pl.BlockSpec((1, width), lambda b, h: (0, 0)),
        ],
        out_specs=pl.BlockSpec((seq, hps * width), lambda b, h: (b, h)),
        out_shape=jax.ShapeDtypeStruct((t, DA_HEADS * width), BF16),
        scratch_shapes=[pltpu.VMEM((hps, seq, width), BF16), pltpu.VMEM((hps, seq, 2 * width), BF16)],
        compiler_params=pltpu.CompilerParams(
            dimension_semantics=("arbitrary", "arbitrary"), vmem_limit_bytes=VMEM_LIMIT),
        name="attn",
    )(proj, proj, proj, qg2, kg2, lam4, subln_g)


def _mixout_body(x_ref, oa_ref, od_ref, ga0_ref, ga1_ref, gd0_ref, gd1_ref, wo_ref, g1_ref, g_ref, sh_ref, sc_ref,
                 wr_ref, br_ref,
                 x1_ref, h2_ref, idx_ref, rank_ref, gcol_ref, cnt_ref, carry_scr, wo_scr):
    i = pl.program_id(0)
    tm = x_ref.shape[0]
    hw = oa_ref.shape[1]

    @pl.when(i == 0)
    def _():
        carry_scr[...] = jnp.zeros_like(carry_scr)
        wo_scr[...] = wo_ref[...].astype(BF16)

    ya = _dot(oa_ref[...], wo_scr[0:hw, :])
    yd = _dot(od_ref[...], wo_scr[hw:, :])
    ga = jnp.concatenate([ga0_ref[...], ga1_ref[...]], axis=1).astype(F32)
    gd = jnp.concatenate([gd0_ref[...], gd1_ref[...]], axis=1).astype(F32)
    y = _sigmoid(ga) * ya + _sigmoid(gd) * yd
    x1 = x_ref[...] + g1_ref[0] * y
    x1_ref[...] = x1
    h2 = _norm_mod(x1, g_ref[...], sh_ref[0], sc_ref[0])
    hh = h2.astype(BF16)
    h2r = hh.astype(F32)
    h2_ref[...] = _pack_rounded(h2r)

    hl = (h2 - h2r).astype(BF16)
    wh, wl = _split_bf16(wr_ref[...])
    logits = _dot_nt(wh, hh) + _dot_nt(wl, hh) + _dot_nt(wh, hl) + br_ref[...]

    e_iota = lax.broadcasted_iota(I32, (N_EXPERTS, tm), 0).astype(F32)
    vals = logits
    tops, sels, idxs = [], [], []
    for _ in range(TOP_K):
        m = jnp.max(vals, axis=0, keepdims=True)
        idx = jnp.min(jnp.where(vals == m, e_iota, float(N_EXPERTS)), axis=0, keepdims=True)
        sel = e_iota == idx
        vals = jnp.where(sel, -jnp.inf, vals)
        tops.append(m)
        sels.append(sel)
        idxs.append(idx)
    ex = [jnp.exp(tv - tops[0]) for tv in tops]
    den = ex[0] + ex[1] + ex[2] + ex[3]
    gates = [v / den for v in ex]

    hot = jnp.where(sels[0] | sels[1] | sels[2] | sels[3], 1.0, 0.0)
    r_t = lax.broadcasted_iota(I32, (tm, tm), 0)
    c_t = lax.broadcasted_iota(I32, (tm, tm), 1)
    upper = jnp.where(r_t < c_t, 1.0, 0.0).astype(BF16)
    excl = _dot(hot.astype(BF16), upper) + carry_scr[:, 0:1]
    carry_scr[...] = carry_scr[...] + jnp.sum(hot, axis=1, keepdims=True)
    cnt_ref[...] = carry_scr[...]

    ranks = [jnp.sum(jnp.where(s, excl, 0.0), axis=0, keepdims=True) for s in sels]
    idx_ref[...] = jnp.concatenate(idxs, axis=0).astype(I32)
    rank_ref[...] = jnp.concatenate(ranks, axis=0).astype(I32)
    gpad = jnp.concatenate(gates + [jnp.zeros((LANES - TOP_K, tm), F32)], axis=0)
    gcol_ref[...] = gpad.T


def _mixout(x2, oa, od, proj, col_g, wo, ada3, ffn_g, wr_t, br_col, seq):
    t, d = x2.shape
    tm = min(1024, seq)
    per_b = seq // tm
    hw = oa.shape[1]
    row = lambda w: pl.BlockSpec((tm, w), lambda i: (i, 0))
    gate = lambda c: pl.BlockSpec((tm, d // 2), lambda i, c=c: (i, col_g + c))
    mod = lambda c: pl.BlockSpec((1, 1, d), lambda i, c=c: (i // per_b, 0, c))
    full = lambda a: pl.BlockSpec(a.shape, lambda i: (0,) * a.ndim)
    return pl.pallas_call(
        _mixout_body,
        grid=(t // tm,),
        in_specs=[
            row(d), row(hw), row(hw),
            gate(0), gate(1), gate(2), gate(3),
            full(wo),
            mod(2),
            full(ffn_g), mod(3), mod(4),
            full(wr_t), full(br_col),
        ],
        out_specs=[
            row(d), row(d // 2),
            pl.BlockSpec((TOP_K, tm), lambda i: (0, i)),
            pl.BlockSpec((TOP_K, tm), lambda i: (0, i)),
            pl.BlockSpec((tm, LANES), lambda i: (i, 0)),
            pl.BlockSpec((N_EXPERTS, LANES), lambda i: (0, 0)),
        ],
        out_shape=[
            jax.ShapeDtypeStruct((t, d), F32),
            jax.ShapeDtypeStruct((t, d // 2), I32),
            jax.ShapeDtypeStruct((TOP_K, t), I32),
            jax.ShapeDtypeStruct((TOP_K, t), I32),
            jax.ShapeDtypeStruct((t, LANES), F32),
            jax.ShapeDtypeStruct((N_EXPERTS, LANES), F32),
        ],
        scratch_shapes=[pltpu.VMEM((N_EXPERTS, LANES), F32), pltpu.VMEM(wo.shape, BF16)],
        compiler_params=pltpu.CompilerParams(
            dimension_semantics=("arbitrary",), vmem_limit_bytes=VMEM_LIMIT),
        name="mixout",
    )(x2, oa, od, proj, proj, proj, proj, wo, ada3, ffn_g, ada3, ada3, wr_t, br_col)


def _sc_mesh():
    return plsc.VectorSubcoreMesh(core_axis_name="c", subcore_axis_name="s")


def _sc_worker_id():
    return lax.axis_index("s") * SC_CORES + lax.axis_index("c")


def _sc_dispatch(h2, dest, n_rows):
    t, d = h2.shape
    rows = SC_CHUNK // 2
    assert t % (2 * SC_WORKERS * rows) == 0
    cpw = t // (SC_WORKERS * rows)
    dest_w = dest.reshape(TOP_K, SC_WORKERS, cpw, rows).transpose(1, 0, 2, 3).reshape(-1, rows)
    ipw = TOP_K * cpw

    def body(h2_hbm, dest_hbm, xb_hbm, idx_v, rows_v, rsem, ssem):
        wid = _sc_worker_id()
        pltpu.sync_copy(dest_hbm.at[pl.ds(pl.multiple_of(wid * ipw, ipw), ipw)], idx_v)

        def read(c, b):
            t0 = pl.multiple_of((wid * cpw + c) * rows, rows)
            return pltpu.make_async_copy(h2_hbm.at[pl.ds(t0, rows)], rows_v.at[b], rsem.at[b])

        def scatters(c, b):
            return [pltpu.make_async_copy(rows_v.at[b], xb_hbm.at[idx_v.at[k * cpw + c]], ssem.at[b])
                    for k in range(TOP_K)]

        read(0, 0).start()

        @pl.loop(0, cpw, step=2)
        def _(c0):
            for b in range(2):
                c = c0 + b

                @pl.when(c + 1 < cpw)
                def _():
                    @pl.when(c >= 1)
                    def _():
                        for cp in scatters(c - 1, 1 - b):
                            cp.wait()
                    read(c + 1, 1 - b).start()

                read(c, b).wait()
                for cp in scatters(c, b):
                    cp.start()

        for cp in scatters(cpw - 2, 0) + scatters(cpw - 1, 1):
            cp.wait()

    return pl.kernel(
        body, out_type=jax.ShapeDtypeStruct((n_rows, d), h2.dtype), mesh=_sc_mesh(),
        scratch_types=[pltpu.VMEM((ipw, rows), I32), pltpu.VMEM((2, rows, d), h2.dtype),
                       pltpu.SemaphoreType.DMA((2,)), pltpu.SemaphoreType.DMA((2,))],
        name="dispatch",
    )(h2, dest_w)


def _sc_undispatch(y, dest):
    n_asg = dest.shape[0]
    d = y.shape[1]
    rows = SC_CHUNK // 2
    assert n_asg % (2 * SC_WORKERS * rows) == 0
    cpw = n_asg // (SC_WORKERS * rows)

    def body(y_hbm, dest_hbm, yt_hbm, idx_v, rows_v, gsem, wsem):
        wid = _sc_worker_id()
        pltpu.sync_copy(dest_hbm.at[pl.ds(pl.multiple_of(wid * cpw, cpw), cpw)], idx_v)

        def gather(c, b):
            return pltpu.make_async_copy(y_hbm.at[idx_v.at[c]], rows_v.at[b], gsem.at[b])

        def write(c, b):
            r0 = pl.multiple_of((wid * cpw + c) * rows, rows)
            return pltpu.make_async_copy(rows_v.at[b], yt_hbm.at[pl.ds(r0, rows)], wsem.at[b])

        gather(0, 0).start()

        @pl.loop(0, cpw, step=2)
        def _(c0):
            for b in range(2):
                c = c0 + b

                @pl.when(c + 1 < cpw)
                def _():
                    @pl.when(c >= 1)
                    def _():
                        write(c - 1, 1 - b).wait()
                    gather(c + 1, 1 - b).start()

                gather(c, b).wait()
                write(c, b).start()

        write(cpw - 2, 0).wait()
        write(cpw - 1, 1).wait()

    return pl.kernel(
        body, out_type=jax.ShapeDtypeStruct((n_asg, d), y.dtype), mesh=_sc_mesh(),
        scratch_types=[pltpu.VMEM((cpw, rows), I32), pltpu.VMEM((2, rows, d), y.dtype),
                       pltpu.SemaphoreType.DMA((2,)), pltpu.SemaphoreType.DMA((2,))],
        name="undispatch",
    )(y, dest.reshape(-1, rows))


def _ffn_body(be_ref, first_ref, nxt_ref, slot_ref, nv_ref, nact_ref, x_ref, w1_hbm, b1_ref, w2_hbm, b2_ref, y_ref,
              w1f, w2f, w1c, w2c, sem1, sem2):
    j = pl.program_id(0)
    ff = w2f.shape[1]
    pair = 2 * LANES
    ngroup = (2 * ff) // pair

    def weight_copies(e, slot):
        return (pltpu.make_async_copy(w1_hbm.at[e], w1f.at[slot], sem1.at[slot]),
                pltpu.make_async_copy(w2_hbm.at[e], w2f.at[slot], sem2.at[slot]))

    @pl.when(j == 0)
    def _():
        for cp in weight_copies(be_ref[0], 0):
            cp.start()

    @pl.when(first_ref[j] == 1)
    def _():
        slot = slot_ref[j]
        for cp in weight_copies(be_ref[j], slot):
            cp.wait()

        @pl.when(nxt_ref[j] >= 0)
        def _():
            for cp in weight_copies(nxt_ref[j], 1 - slot):
                cp.start()

        r_p = lax.broadcasted_iota(I32, (pair, pair), 0)
        c_p = lax.broadcasted_iota(I32, (pair, pair), 1)
        src = jnp.where(c_p < LANES, 2 * c_p, 2 * (c_p - LANES) + 1)
        perm = jnp.where(r_p == src, 1.0, 0.0).astype(BF16)
        for g in range(ngroup):
            cols = slice(g * pair, (g + 1) * pair)
            w1c[:, cols] = _dot(w1f[slot, :, cols].astype(BF16), perm).astype(BF16)
        w2c[...] = w2f[slot].astype(BF16)

    def expert_rows(nrows):
        x_lo, x_hi = _unpack_halves(x_ref[0:nrows, :])
        xb = jnp.concatenate([x_lo.astype(BF16), x_hi.astype(BF16)], axis=1)
        u = _dot(xb, w1c[...]) + b1_ref[0]
        acts = []
        for g in range(ngroup):
            glu = jnp.minimum(u[:, g * pair:g * pair + LANES], SWIGLU_LIMIT)
            lin = jnp.clip(u[:, g * pair + LANES:(g + 1) * pair], -SWIGLU_LIMIT, SWIGLU_LIMIT)
            acts.append((glu * _sigmoid(SWIGLU_ALPHA * glu) * (lin + 1.0)).astype(BF16))
        act = jnp.concatenate(acts, axis=1)
        y_ref[0:nrows, :] = _pack_halves(_dot(act, w2c[...]) + b2_ref[0])

    active = j < nact_ref[0]
    half = x_ref.shape[0] // 2
    pl.when(active & (nv_ref[j] > half))(functools.partial(expert_rows, x_ref.shape[0]))
    pl.when(active & (nv_ref[j] <= half))(functools.partial(expert_rows, half))


def _ffn(block_expert, first, nxt, slot, nvalid, nact, xb, w1, b1p, w2, b2):
    bm = FFN_BLOCK
    n_rows, dw = xb.shape
    d = 2 * dw
    n_blocks = n_rows // bm
    ff2 = w1.shape[2]
    ff = w2.shape[1]
    row_blk = lambda j, be, fi, nx, sl, nv, na: (jnp.minimum(j, na[0] - 1), 0)
    bias_blk = lambda j, be, fi, nx, sl, nv, na: (be[j], 0, 0)
    grid_spec = pltpu.PrefetchScalarGridSpec(
        num_scalar_prefetch=6,
        grid=(n_blocks,),
        in_specs=[
            pl.BlockSpec((bm, dw), row_blk),
            pl.BlockSpec(memory_space=pl.ANY),
            pl.BlockSpec((1, 1, ff2), bias_blk),
            pl.BlockSpec(memory_space=pl.ANY),
            pl.BlockSpec((1, 1, d), bias_blk),
        ],
        out_specs=pl.BlockSpec((bm, dw), row_blk),
        scratch_shapes=[
            pltpu.VMEM((2, d, ff2), F32), pltpu.VMEM((2, ff, d), F32),
            pltpu.VMEM((d, ff2), BF16), pltpu.VMEM((ff, d), BF16),
            pltpu.SemaphoreType.DMA((2,)), pltpu.SemaphoreType.DMA((2,)),
        ],
    )
    return pl.pallas_call(
        _ffn_body,
        grid_spec=grid_spec,
        out_shape=jax.ShapeDtypeStruct((n_rows, dw), I32),
        compiler_params=pltpu.CompilerParams(
            dimension_semantics=("arbitrary",), vmem_limit_bytes=VMEM_LIMIT),
        name="ffn",
    )(block_expert, first, nxt, slot, nvalid, nact, xb, w1, b1p, w2, b2)


def _combine_body(x1_ref, y0_ref, y1_ref, y2_ref, y3_ref, gcol_ref, g2_ref, o_ref):
    gc = gcol_ref[...]
    m_lo = m_hi = None
    for k, y_ref in enumerate((y0_ref, y1_ref, y2_ref, y3_ref)):
        lo, hi = _unpack_halves(y_ref[...])
        gk = gc[:, k:k + 1]
        m_lo = gk * lo if m_lo is None else m_lo + gk * lo
        m_hi = gk * hi if m_hi is None else m_hi + gk * hi
    m = jnp.concatenate([m_lo, m_hi], axis=1)
    o_ref[...] = (x1_ref[...] + g2_ref[0] * m).astype(o_ref.dtype)


def _combine(x1, yall, gcol, ada3, seq, out_dtype):
    t, d = x1.shape
    tm = min(1024, seq)
    per_b = seq // tm
    nt = t // tm
    yk = lambda k: pl.BlockSpec((tm, d // 2), lambda i, k=k: (k * nt + i, 0))
    return pl.pallas_call(
        _combine_body,
        grid=(nt,),
        in_specs=[
            pl.BlockSpec((tm, d), lambda i: (i, 0)),
            yk(0), yk(1), yk(2), yk(3),
            pl.BlockSpec((tm, LANES), lambda i: (i, 0)),
            pl.BlockSpec((1, 1, d), lambda i: (i // per_b, 0, 5)),
        ],
        out_specs=pl.BlockSpec((tm, d), lambda i: (i, 0)),
        out_shape=jax.ShapeDtypeStruct((t, d), out_dtype),
        compiler_params=pltpu.CompilerParams(
            dimension_semantics=("arbitrary",), vmem_limit_bytes=VMEM_LIMIT),
        name="combine",
    )(x1, yall, yall, yall, yall, gcol, ada3)


def _route_tables(idx, rank, counts, n_tok):
    bm = FFN_BLOCK
    n_asg = TOP_K * n_tok
    n_blocks = -(-(n_asg + N_EXPERTS * (bm - 1)) // bm)
    padded = (counts + bm - 1) // bm * bm
    pad_ends = jnp.cumsum(padded)
    pad_starts = pad_ends - padded
    e_ids = jnp.arange(N_EXPERTS, dtype=I32)
    start_of = jnp.sum(jnp.where(idx[None] == e_ids[:, None, None], pad_starts[:, None, None], 0), axis=0)
    dest = (start_of + rank).reshape(-1)
    nact = (pad_ends[-1] // bm).astype(I32)
    blk_start = jnp.arange(n_blocks, dtype=I32) * bm
    last = jnp.sum(jnp.where(pad_ends <= pad_ends[-1] - 1, 1, 0)).astype(I32)
    be = jnp.sum(jnp.where(pad_ends[None, :] <= blk_start[:, None], 1, 0), axis=1).astype(I32)
    active = blk_start < pad_ends[-1]
    be = jnp.where(active, be, last)
    blk = jnp.arange(n_blocks, dtype=I32)
    first = active & ((blk == 0) | (be != jnp.roll(be, 1)))
    slot = (jnp.cumsum(first.astype(I32)) - 1) & 1
    later_first = first[None, :] & (blk[None, :] > blk[:, None])
    nxt_pos = jnp.min(jnp.where(later_first, blk[None, :], n_blocks), axis=1)
    nxt = jnp.sum(jnp.where(blk[None, :] == nxt_pos[:, None], be[None, :], 0), axis=1)
    nxt = jnp.where(nxt_pos < n_blocks, nxt, -1).astype(I32)
    mine = be[:, None] == e_ids[None, :]
    cnt_b = jnp.sum(jnp.where(mine, counts[None, :], 0), axis=1)
    start_b = jnp.sum(jnp.where(mine, pad_starts[None, :], 0), axis=1)
    nvalid = jnp.where(active, jnp.clip(cnt_b - (blk_start - start_b), 0, bm), 0).astype(I32)
    return be, first.astype(I32), nxt, slot.astype(I32), nvalid, nact.reshape(1), dest, n_blocks * bm


def kernel(x, c, w_ada, b_ada, mix_norm_g, ffn_norm_g, w_in, hg_lower_bound_logits, hg_out_norm_g, da_q_norm_g, da_k_norm_g, da_lambda_q1, da_lambda_k1, da_lambda_q2, da_lambda_k2, da_subln_g, w_out, w_router, b_router, w1, b1, w2, b2):
    bsz, seq, d = x.shape
    t = bsz * seq
    depth = w_ada.shape[0]
    out_dtype = x.dtype
    hw = HG_HEADS * HG_DV
    xcur = x.reshape(t, d)
    for l in range(depth):
        ada = _ada(c, w_ada[l], b_ada[l])
        ada3 = ada.reshape(bsz, 1, N_MOD * d)
        col_h = 0
        col_a = col_h + 4 * HG_HEADS
        col_g = (4 * hw + 3 * DA_HEADS * 2 * DA_DH) // (d // 2)
        proj = _inproj(xcur, mix_norm_g[l].reshape(1, d), ada3, w_in[l].astype(BF16), seq)

        o_a = _hgrn(proj, hg_lower_bound_logits, hg_out_norm_g[l].reshape(1, HG_DV), bsz, seq, col_h, l)
        lambda_init = 0.8 - 0.6 * math.exp(-0.3 * l)
        qg2 = jnp.tile(da_q_norm_g[l], 2).reshape(1, 2 * DA_DH)
        kg2 = jnp.tile(da_k_norm_g[l], 2).reshape(1, 2 * DA_DH)
        lam4 = jnp.stack([da_lambda_q1[l], da_lambda_k1[l], da_lambda_q2[l], da_lambda_k2[l]])
        o_d = _attn(proj, qg2, kg2, lam4, da_subln_g[l].reshape(1, 2 * DA_DH), bsz, seq, col_a, lambda_init)

        x1, h2, idx, rank, gcol, cnt = _mixout(
            xcur, o_a, o_d, proj, col_g, w_out[l], ada3, ffn_norm_g[l].reshape(1, d),
            w_router[l].T, b_router[l].reshape(N_EXPERTS, 1), seq)

        counts = cnt[:, 0].astype(I32)
        be, first, nxt, slot, nvalid, nact, dest, n_rows = _route_tables(idx, rank, counts, t)
        b1p = b1[l].reshape(N_EXPERTS, -1, LANES, 2).transpose(0, 1, 3, 2).reshape(N_EXPERTS, 1, -1)
        xb = _sc_dispatch(h2, dest, n_rows)
        yb = _ffn(be, first, nxt, slot, nvalid, nact, xb, w1[l], b1p, w2[l], b2[l].reshape(N_EXPERTS, 1, d))
        yall = _sc_undispatch(yb, dest)
        xcur = _combine(x1, yall, gcol, ada3, seq, out_dtype)
    return xcur.reshape(bsz, seq, d)
```

```python
import functools
import math

import jax
import jax.numpy as jnp
from jax import lax
from jax.experimental import pallas as pl
from jax.experimental.pallas import tpu as pltpu
from jax.experimental.pallas import tpu_sc as plsc

F32 = jnp.float32
BF16 = jnp.bfloat16
I32 = jnp.int32

HG_HEADS = 4
HG_DK = 128
HG_DV = 128
HG_CHUNK = 32
DA_HEADS = 4
DA_DH = 64
N_EXPERTS = 32
TOP_K = 4
SWIGLU_ALPHA = 1.702
SWIGLU_LIMIT = 7.0
NORM_EPS = 1e-6
LOG2E = math.log2(math.e)
N_MOD = 6

LANES = 128
VMEM_LIMIT = 56 * 1024 * 1024

HG_ROWS = 256
HG_HEADS_PER_STEP = 4
ATTN_TILE = 256
DA_HEADS_PER_STEP = 2
FFN_BLOCK = 512

SC_CORES = 2
SC_WORKERS = SC_CORES * 16
SC_CHUNK = 128


def _sigmoid(x):
    return 0.5 * jnp.tanh(0.5 * x) + 0.5


def _dot(a, b):
    return jnp.dot(a, b, preferred_element_type=F32)


def _dot_nt(a, b):
    return lax.dot_general(a, b, (((1,), (1,)), ((), ())), preferred_element_type=F32)


def _split_bf16(x):
    hi = x.astype(BF16)
    lo = (x - hi.astype(F32)).astype(BF16)
    return hi, lo


def _pack_halves(x):
    return _pack_rounded(x.astype(BF16).astype(F32))


def _pack_rounded(xr):
    half = xr.shape[1] // 2
    lo = lax.bitcast_convert_type(xr[:, :half], I32)
    hi = lax.bitcast_convert_type(xr[:, half:], I32)
    return lax.shift_right_logical(lo, 16) | hi


def _unpack_halves(w):
    lo = lax.bitcast_convert_type(lax.shift_left(w, 16), F32)
    hi = lax.bitcast_convert_type(w & jnp.int32(-65536), F32)
    return lo, hi


def _ada_body(c_ref, w_ref, b_ref, o_ref):
    c = c_ref[...].astype(F32)
    ch, cl = _split_bf16(c * _sigmoid(c))
    wh, wl = _split_bf16(w_ref[...])
    o_ref[...] = _dot(ch, wh) + _dot(cl, wh) + _dot(ch, wl) + b_ref[...]


def _ada(c, w_ada, b_ada):
    bsz, d = c.shape
    n = w_ada.shape[1]
    tn = 2 * d
    return pl.pallas_call(
        _ada_body,
        grid=(n // tn,),
        in_specs=[
            pl.BlockSpec((bsz, d), lambda j: (0, 0)),
            pl.BlockSpec((d, tn), lambda j: (0, j)),
            pl.BlockSpec((1, tn), lambda j: (0, j)),
        ],
        out_specs=pl.BlockSpec((bsz, tn), lambda j: (0, j)),
        out_shape=jax.ShapeDtypeStruct((bsz, n), F32),
        name="ada",
    )(c, w_ada, b_ada.reshape(1, n))


def _norm_mod(x, g, shift, scale):
    ms = jnp.mean(x * x, axis=-1, keepdims=True)
    return x * lax.rsqrt(ms + NORM_EPS) * (g * (1.0 + scale)) + shift


def _inproj_body(x_ref, g_ref, sh_ref, sc_ref, w_ref, o_ref):
    h = _norm_mod(x_ref[...], g_ref[...], sh_ref[0], sc_ref[0])
    o_ref[...] = _dot(h.astype(BF16), w_ref[...]).astype(BF16)


def _inproj(x2, g, ada3, w_bf16, seq):
    t, d = x2.shape
    n = w_bf16.shape[1]
    tm = min(512, seq)
    nj = 1
    tn = n // nj
    per_b = seq // tm
    return pl.pallas_call(
        _inproj_body,
        grid=(nj, t // tm),
        in_specs=[
            pl.BlockSpec((tm, d), lambda j, i: (i, 0)),
            pl.BlockSpec((1, d), lambda j, i: (0, 0)),
            pl.BlockSpec((1, 1, d), lambda j, i: (i // per_b, 0, 0)),
            pl.BlockSpec((1, 1, d), lambda j, i: (i // per_b, 0, 1)),
            pl.BlockSpec((d, tn), lambda j, i: (0, j)),
        ],
        out_specs=pl.BlockSpec((tm, tn), lambda j, i: (i, j)),
        out_shape=jax.ShapeDtypeStruct((t, n), BF16),
        compiler_params=pltpu.CompilerParams(
            dimension_semantics=("arbitrary", "arbitrary"), vmem_limit_bytes=VMEM_LIMIT),
        name="inproj",
    )(x2, g, ada3, ada3, w_bf16)


def _hgrn_body(q_ref, f_ref, i_ref, og_ref, lbl_ref, g_ref, o_ref, *, seq, layer):
    rows, chunk = HG_ROWS, HG_CHUNK
    nchunk = rows // chunk
    lbl = lbl_ref[...].astype(F32)
    e = jnp.exp(lbl - jnp.max(lbl, axis=0, keepdims=True))
    lb_all = jnp.sum(e[: layer + 1], axis=0, keepdims=True) / jnp.sum(e, axis=0, keepdims=True)
    heads = q_ref.shape[1] // HG_DK
    r_i = lax.broadcasted_iota(I32, (rows, rows), 0)
    c_i = lax.broadcasted_iota(I32, (rows, rows), 1)
    tri = ((r_i // chunk) == (c_i // chunk)) & (r_i >= c_i)
    tri_b = jnp.where(tri, 1.0, 0.0).astype(BF16)
    row_chunk = lax.broadcasted_iota(I32, (rows, HG_DK), 0) // chunk
    g = g_ref[...].astype(F32)

    def block(r, st, h):
        sl = pl.ds(r * rows, rows)
        cs = slice(h * HG_DK, (h + 1) * HG_DK)
        lb = lb_all[:, cs]
        qr = q_ref[sl, cs].astype(F32)
        fr = f_ref[sl, cs].astype(F32)
        v = i_ref[sl, cs].astype(F32)
        og = og_ref[sl, cs].astype(F32)
        q = qr * _sigmoid(qr)
        f = lb + (1.0 - lb) * _sigmoid(fr)
        k = 1.0 - f
        logf = jnp.log(f)
        lhi, llo = _split_bf16(logf)
        bc2 = _dot(tri_b, jnp.concatenate([lhi, llo], axis=1))
        bcum = bc2[:, :HG_DK] + bc2[:, HG_DK:]
        b3 = bcum.reshape(nchunk, chunk, HG_DK)
        bl = b3[:, chunk - 1:chunk, :]
        dec = jnp.exp(bl)
        kt_f = k * jnp.exp(-bcum)
        qt_f = q * jnp.exp(bcum)
        kd = (kt_f.reshape(nchunk, chunk, HG_DK) * dec).reshape(rows, HG_DK)
        a = jnp.where(tri, _dot_nt(qt_f.astype(BF16), kt_f.astype(BF16)), 0.0).astype(BF16)
        vt_b = v.T.astype(BF16)
        kd_x = jnp.concatenate([jnp.where(row_chunk == c, kd, 0.0) for c in range(nchunk)], axis=1)
        kv_all = _dot(vt_b, kd_x.astype(BF16))
        starts = []
        for c in range(nchunk):
            starts.append(st.astype(BF16))
            st = st * dec[c] + kv_all[:, c * HG_DK:(c + 1) * HG_DK]
        q_x = [jnp.where(row_chunk == c, qt_f, 0.0).astype(BF16) for c in range(nchunk)]
        o = _dot_nt(jnp.concatenate([a] + q_x, axis=1), jnp.concatenate([vt_b] + starts, axis=1))
        ms = jnp.mean(o * o, axis=-1, keepdims=True)
        o = o * lax.rsqrt(ms + NORM_EPS) * g
        o_ref[sl, cs] = (o * (og * _sigmoid(og))).astype(o_ref.dtype)
        return st

    sts = [jnp.zeros((HG_DV, HG_DK), F32) for _ in range(heads)]
    for r in range(seq // rows):
        for h in range(heads):
            sts[h] = block(r, sts[h], h)


def _hgrn(proj, lb_logits, norm_g, bsz, seq, col0, layer):
    t = proj.shape[0]
    hps = HG_HEADS_PER_STEP
    wid = hps * LANES
    assert col0 % hps == 0 and HG_HEADS % hps == 0
    blk = lambda off: pl.BlockSpec((seq, wid), lambda b, h, off=off: (b, (col0 + off) // hps + h))
    return pl.pallas_call(
        functools.partial(_hgrn_body, seq=seq, layer=layer),
        grid=(bsz, HG_HEADS // hps),
        in_specs=[
            blk(0), blk(HG_HEADS), blk(2 * HG_HEADS), blk(3 * HG_HEADS),
            pl.BlockSpec((lb_logits.shape[0], wid), lambda b, h: (0, h)),
            pl.BlockSpec((1, HG_DV), lambda b, h: (0, 0)),
        ],
        out_specs=pl.BlockSpec((seq, wid), lambda b, h: (b, h)),
        out_shape=jax.ShapeDtypeStruct((t, HG_HEADS * HG_DV), BF16),
        compiler_params=pltpu.CompilerParams(
            dimension_semantics=("arbitrary", "arbitrary"), vmem_limit_bytes=VMEM_LIMIT),
        name="hgrn",
    )(proj, proj, proj, proj, lb_logits, norm_g)


def _group_norm(x, gsum_b, gain):
    ss = _dot((x * x).astype(BF16), gsum_b)
    return x * lax.rsqrt(ss * (1.0 / DA_DH) + NORM_EPS) * gain


def _attn_body(q_ref, k_ref, v_ref, qg_ref, kg_ref, lam_ref, sg_ref, o_ref, kn_scr, *, seq, lambda_init):
    tq = ATTN_TILE
    width = 2 * DA_DH
    r_l = lax.broadcasted_iota(I32, (width, width), 0) // DA_DH
    c_l = lax.broadcasted_iota(I32, (width, width), 1) // DA_DH
    gsum_b = jnp.where(r_l == c_l, 1.0, 0.0).astype(BF16)
    lane = lax.broadcasted_iota(I32, (1, width), 1)
    kg = kg_ref[...].astype(F32)
    qg = qg_ref[...].astype(F32) * (DA_DH ** -0.5 * LOG2E)
    sg = sg_ref[...].astype(F32) * (1.0 - lambda_init)
    row = lax.broadcasted_iota(I32, (tq, tq), 0)
    col = lax.broadcasted_iota(I32, (tq, tq), 1)
    keep = row >= col

    lam_v = lam_ref[...].astype(F32)
    lam = (jnp.exp(jnp.sum(lam_v[0:1] * lam_v[1:2], axis=-1, keepdims=True))
           - jnp.exp(jnp.sum(lam_v[2:3] * lam_v[3:4], axis=-1, keepdims=True)) + lambda_init)

    def softmax_parts(qc, nk, h):
        s = _dot_nt(qc, kn_scr[h, 0:nk, :])
        diag = jnp.where(keep, s[:, nk - tq:], -jnp.inf)
        s = diag if nk == tq else jnp.concatenate([s[:, :nk - tq], diag], axis=1)
        e = jnp.exp2(s - jnp.max(s, axis=-1, keepdims=True))
        return e, jnp.sum(e, axis=-1, keepdims=True)

    for i in range(seq // tq):
        sl = slice(i * tq, (i + 1) * tq)
        nk = (i + 1) * tq
        for h in range(q_ref.shape[1] // width):
            cs = slice(h * width, (h + 1) * width)
            kn_scr[h, sl, :] = _group_norm(k_ref[sl, cs].astype(F32), gsum_b, kg).astype(BF16)
            qn = _group_norm(q_ref[sl, cs].astype(F32), gsum_b, qg)
            e1, l1 = softmax_parts(jnp.where(lane < DA_DH, qn, 0.0).astype(BF16), nk, h)
            e2, l2 = softmax_parts(jnp.where(lane >= DA_DH, qn, 0.0).astype(BF16), nk, h)
            o = _dot((e1 * (1.0 / l1) - e2 * (lam / l2)).astype(BF16), v_ref[0:nk, cs])
            ms = jnp.mean(o * o, axis=-1, keepdims=True)
            o_ref[sl, cs] = (o * lax.rsqrt(ms + NORM_EPS) * sg).astype(o_ref.dtype)


def _attn(proj, qg2, kg2, lam4, subln_g, bsz, seq, col0, lambda_init):
    t = proj.shape[0]
    width = 2 * DA_DH
    hps = DA_HEADS_PER_STEP
    assert col0 % hps == 0 and DA_HEADS % hps == 0
    blk = lambda off: pl.BlockSpec((seq, hps * width), lambda b, h, off=off: (b, (col0 + off) // hps + h))
    return pl.pallas_call(
        functools.partial(_attn_body, seq=seq, lambda_init=lambda_init),
        grid=(bsz, DA_HEADS // hps),
        in_specs=[
            blk(0), blk(DA_HEADS), blk(2 * DA_HEADS),
            pl.BlockSpec((1, width), lambda b, h: (0, 0)),
            pl.BlockSpec((1, width), lambda b, h: (0, 0)),
            pl.BlockSpec((4, DA_DH), lambda b, h: (0, 0)),
            pl.BlockSpec((1, width), lambda b, h: (0, 0)),
        ],
        out_specs=pl.BlockSpec((seq, hps * width), lambda b, h: (b, h)),
        out_shape=jax.ShapeDtypeStruct((t, DA_HEADS * width), BF16),
        scratch_shapes=[pltpu.VMEM((hps, seq, width), BF16)],
        compiler_params=pltpu.CompilerParams(
            dimension_semantics=("arbitrary", "arbitrary"), vmem_limit_bytes=VMEM_LIMIT),
        name="attn",
    )(proj, proj, proj, qg2, kg2, lam4, subln_g)


def _mixout_body(x_ref, oa_ref, od_ref, ga0_ref, ga1_ref, gd0_ref, gd1_ref, wo_ref, g1_ref, g_ref, sh_ref, sc_ref,
                 wr_ref, br_ref,
                 x1_ref, h2_ref, idx_ref, rank_ref, gcol_ref, cnt_ref, carry_scr, wo_scr):
    i = pl.program_id(0)
    tm = x_ref.shape[0]
    hw = oa_ref.shape[1]

    @pl.when(i == 0)
    def _():
        carry_scr[...] = jnp.zeros_like(carry_scr)
        wo_scr[...] = wo_ref[...].astype(BF16)

    ya = _dot(oa_ref[...], wo_scr[0:hw, :])
    yd = _dot(od_ref[...], wo_scr[hw:, :])
    ga = jnp.concatenate([ga0_ref[...], ga1_ref[...]], axis=1).astype(F32)
    gd = jnp.concatenate([gd0_ref[...], gd1_ref[...]], axis=1).astype(F32)
    y = _sigmoid(ga) * ya + _sigmoid(gd) * yd
    x1 = x_ref[...] + g1_ref[0] * y
    x1_ref[...] = x1
    h2 = _norm_mod(x1, g_ref[...], sh_ref[0], sc_ref[0])
    hh = h2.astype(BF16)
    h2r = hh.astype(F32)
    h2_ref[...] = _pack_rounded(h2r)

    hl = (h2 - h2r).astype(BF16)
    wh, wl = _split_bf16(wr_ref[...])
    logits = _dot_nt(wh, hh) + _dot_nt(wl, hh) + _dot_nt(wh, hl) + br_ref[...]

    e_iota = lax.broadcasted_iota(I32, (N_EXPERTS, tm), 0).astype(F32)
    vals = logits
    tops, sels, idxs = [], [], []
    for _ in range(TOP_K):
        m = jnp.max(vals, axis=0, keepdims=True)
        idx = jnp.min(jnp.where(vals == m, e_iota, float(N_EXPERTS)), axis=0, keepdims=True)
        sel = e_iota == idx
        vals = jnp.where(sel, -jnp.inf, vals)
        tops.append(m)
        sels.append(sel)
        idxs.append(idx)
    ex = [jnp.exp(tv - tops[0]) for tv in tops]
    den = ex[0] + ex[1] + ex[2] + ex[3]
    gates = [v / den for v in ex]

    hot = jnp.where(sels[0] | sels[1] | sels[2] | sels[3], 1.0, 0.0)
    r_t = lax.broadcasted_iota(I32, (tm, tm), 0)
    c_t = lax.broadcasted_iota(I32, (tm, tm), 1)
    upper = jnp.where(r_t < c_t, 1.0, 0.0).astype(BF16)
    excl = _dot(hot.astype(BF16), upper) + carry_scr[:, 0:1]
    carry_scr[...] = carry_scr[...] + jnp.sum(hot, axis=1, keepdims=True)
    cnt_ref[...] = carry_scr[...]

    ranks = [jnp.sum(jnp.where(s, excl, 0.0), axis=0, keepdims=True) for s in sels]
    idx_ref[...] = jnp.concatenate(idxs, axis=0).astype(I32)
    rank_ref[...] = jnp.concatenate(ranks, axis=0).astype(I32)
    gpad = jnp.concatenate(gates + [jnp.zeros((LANES - TOP_K, tm), F32)], axis=0)
    gcol_ref[...] = gpad.T


def _mixout(x2, oa, od, proj, col_g, wo, ada3, ffn_g, wr_t, br_col, seq):
    t, d = x2.shape
    tm = min(1024, seq)
    per_b = seq // tm
    hw = oa.shape[1]
    row = lambda w: pl.BlockSpec((tm, w), lambda i: (i, 0))
    gate = lambda c: pl.BlockSpec((tm, d // 2), lambda i, c=c: (i, col_g + c))
    mod = lambda c: pl.BlockSpec((1, 1, d), lambda i, c=c: (i // per_b, 0, c))
    full = lambda a: pl.BlockSpec(a.shape, lambda i: (0,) * a.ndim)
    return pl.pallas_call(
        _mixout_body,
        grid=(t // tm,),
        in_specs=[
            row(d), row(hw), row(hw),
            gate(0), gate(1), gate(2), gate(3),
            full(wo),
            mod(2),
            full(ffn_g), mod(3), mod(4),
            full(wr_t), full(br_col),
        ],
        out_specs=[
            row(d), row(d // 2),
            pl.BlockSpec((TOP_K, tm), lambda i: (0, i)),
            pl.BlockSpec((TOP_K, tm), lambda i: (0, i)),
            pl.BlockSpec((tm, LANES), lambda i: (i, 0)),
            pl.BlockSpec((N_EXPERTS, LANES), lambda i: (0, 0)),
        ],
        out_shape=[
            jax.ShapeDtypeStruct((t, d), F32),
            jax.ShapeDtypeStruct((t, d // 2), I32),
            jax.ShapeDtypeStruct((TOP_K, t), I32),
            jax.ShapeDtypeStruct((TOP_K, t), I32),
            jax.ShapeDtypeStruct((t, LANES), F32),
            jax.ShapeDtypeStruct((N_EXPERTS, LANES), F32),
        ],
        scratch_shapes=[pltpu.VMEM((N_EXPERTS, LANES), F32), pltpu.VMEM(wo.shape, BF16)],
        compiler_params=pltpu.CompilerParams(
            dimension_semantics=("arbitrary",), vmem_limit_bytes=VMEM_LIMIT),
        name="mixout",
    )(x2, oa, od, proj, proj, proj, proj, wo, ada3, ffn_g, ada3, ada3, wr_t, br_col)


def _sc_mesh():
    return plsc.VectorSubcoreMesh(core_axis_name="c", subcore_axis_name="s")


def _sc_worker_id():
    return lax.axis_index("s") * SC_CORES + lax.axis_index("c")


def _sc_dispatch(h2, dest, n_rows):
    t, d = h2.shape
    assert t % (SC_WORKERS * SC_CHUNK) == 0
    cpw = t // (SC_WORKERS * SC_CHUNK)
    dest_w = dest.reshape(TOP_K, SC_WORKERS, cpw, SC_CHUNK).transpose(1, 0, 2, 3).reshape(-1, SC_CHUNK)
    ipw = TOP_K * cpw

    def body(h2_hbm, dest_hbm, xb_hbm, idx_v, rows_v):
        wid = _sc_worker_id()
        pltpu.sync_copy(dest_hbm.at[pl.ds(pl.multiple_of(wid * ipw, ipw), ipw)], idx_v)

        @pl.loop(0, cpw)
        def _(c):
            t0 = pl.multiple_of((wid * cpw + c) * SC_CHUNK, SC_CHUNK)
            pltpu.sync_copy(h2_hbm.at[pl.ds(t0, SC_CHUNK)], rows_v)
            for k in range(TOP_K):
                pltpu.sync_copy(rows_v, xb_hbm.at[idx_v.at[k * cpw + c]])

    return pl.kernel(
        body, out_type=jax.ShapeDtypeStruct((n_rows, d), h2.dtype), mesh=_sc_mesh(),
        scratch_types=[pltpu.VMEM((ipw, SC_CHUNK), I32), pltpu.VMEM((SC_CHUNK, d), h2.dtype)],
        name="dispatch",
    )(h2, dest_w)


def _sc_undispatch(y, dest):
    n_asg = dest.shape[0]
    d = y.shape[1]
    rows = SC_CHUNK // 2
    assert n_asg % (2 * SC_WORKERS * rows) == 0
    cpw = n_asg // (SC_WORKERS * rows)

    def body(y_hbm, dest_hbm, yt_hbm, idx_v, rows_v, gsem, wsem):
        wid = _sc_worker_id()
        pltpu.sync_copy(dest_hbm.at[pl.ds(pl.multiple_of(wid * cpw, cpw), cpw)], idx_v)

        def gather(c, b):
            return pltpu.make_async_copy(y_hbm.at[idx_v.at[c]], rows_v.at[b], gsem.at[b])

        def write(c, b):
            r0 = pl.multiple_of((wid * cpw + c) * rows, rows)
            return pltpu.make_async_copy(rows_v.at[b], yt_hbm.at[pl.ds(r0, rows)], wsem.at[b])

        gather(0, 0).start()

        @pl.loop(0, cpw, step=2)
        def _(c0):
            for b in range(2):
                c = c0 + b

                @pl.when(c + 1 < cpw)
                def _():
                    @pl.when(c >= 1)
                    def _():
                        write(c - 1, 1 - b).wait()
                    gather(c + 1, 1 - b).start()

                gather(c, b).wait()
                write(c, b).start()

        write(cpw - 2, 0).wait()
        write(cpw - 1, 1).wait()

    return pl.kernel(
        body, out_type=jax.ShapeDtypeStruct((n_asg, d), y.dtype), mesh=_sc_mesh(),
        scratch_types=[pltpu.VMEM((cpw, rows), I32), pltpu.VMEM((2, rows, d), y.dtype),
                       pltpu.SemaphoreType.DMA((2,)), pltpu.SemaphoreType.DMA((2,))],
        name="undispatch",
    )(y, dest.reshape(-1, rows))


def _ffn_body(be_ref, first_ref, nxt_ref, slot_ref, nv_ref, nact_ref, x_ref, w1_hbm, b1_ref, w2_hbm, b2_ref, y_ref,
              w1f, w2f, w1c, w2c, sem1, sem2):
    j = pl.program_id(0)
    ff = w2f.shape[1]
    pair = 2 * LANES
    ngroup = (2 * ff) // pair

    def weight_copies(e, slot):
        return (pltpu.make_async_copy(w1_hbm.at[e], w1f.at[slot], sem1.at[slot]),
                pltpu.make_async_copy(w2_hbm.at[e], w2f.at[slot], sem2.at[slot]))

    @pl.when(j == 0)
    def _():
        for cp in weight_copies(be_ref[0], 0):
            cp.start()

    @pl.when(first_ref[j] == 1)
    def _():
        slot = slot_ref[j]
        for cp in weight_copies(be_ref[j], slot):
            cp.wait()

        @pl.when(nxt_ref[j] >= 0)
        def _():
            for cp in weight_copies(nxt_ref[j], 1 - slot):
                cp.start()

        r_p = lax.broadcasted_iota(I32, (pair, pair), 0)
        c_p = lax.broadcasted_iota(I32, (pair, pair), 1)
        src = jnp.where(c_p < LANES, 2 * c_p, 2 * (c_p - LANES) + 1)
        perm = jnp.where(r_p == src, 1.0, 0.0).astype(BF16)
        for g in range(ngroup):
            cols = slice(g * pair, (g + 1) * pair)
            w1c[:, cols] = _dot(w1f[slot, :, cols].astype(BF16), perm).astype(BF16)
        w2c[...] = w2f[slot].astype(BF16)

    def expert_rows(nrows):
        x_lo, x_hi = _unpack_halves(x_ref[0:nrows, :])
        xb = jnp.concatenate([x_lo.astype(BF16), x_hi.astype(BF16)], axis=1)
        u = _dot(xb, w1c[...]) + b1_ref[0]
        acts = []
        for g in range(ngroup):
            glu = jnp.minimum(u[:, g * pair:g * pair + LANES], SWIGLU_LIMIT)
            lin = jnp.clip(u[:, g * pair + LANES:(g + 1) * pair], -SWIGLU_LIMIT, SWIGLU_LIMIT)
            acts.append((glu * _sigmoid(SWIGLU_ALPHA * glu) * (lin + 1.0)).astype(BF16))
        act = jnp.concatenate(acts, axis=1)
        y_ref[0:nrows, :] = _pack_halves(_dot(act, w2c[...]) + b2_ref[0])

    active = j < nact_ref[0]
    half = x_ref.shape[0] // 2
    pl.when(active & (nv_ref[j] > half))(functools.partial(expert_rows, x_ref.shape[0]))
    pl.when(active & (nv_ref[j] <= half))(functools.partial(expert_rows, half))


def _ffn(block_expert, first, nxt, slot, nvalid, nact, xb, w1, b1p, w2, b2):
    bm = FFN_BLOCK
    n_rows, dw = xb.shape
    d = 2 * dw
    n_blocks = n_rows // bm
    ff2 = w1.shape[2]
    ff = w2.shape[1]
    row_blk = lambda j, be, fi, nx, sl, nv, na: (jnp.minimum(j, na[0] - 1), 0)
    bias_blk = lambda j, be, fi, nx, sl, nv, na: (be[j], 0, 0)
    grid_spec = pltpu.PrefetchScalarGridSpec(
        num_scalar_prefetch=6,
        grid=(n_blocks,),
        in_specs=[
            pl.BlockSpec((bm, dw), row_blk),
            pl.BlockSpec(memory_space=pl.ANY),
            pl.BlockSpec((1, 1, ff2), bias_blk),
            pl.BlockSpec(memory_space=pl.ANY),
            pl.BlockSpec((1, 1, d), bias_blk),
        ],
        out_specs=pl.BlockSpec((bm, dw), row_blk),
        scratch_shapes=[
            pltpu.VMEM((2, d, ff2), F32), pltpu.VMEM((2, ff, d), F32),
            pltpu.VMEM((d, ff2), BF16), pltpu.VMEM((ff, d), BF16),
            pltpu.SemaphoreType.DMA((2,)), pltpu.SemaphoreType.DMA((2,)),
        ],
    )
    return pl.pallas_call(
        _ffn_body,
        grid_spec=grid_spec,
        out_shape=jax.ShapeDtypeStruct((n_rows, dw), I32),
        compiler_params=pltpu.CompilerParams(
            dimension_semantics=("arbitrary",), vmem_limit_bytes=VMEM_LIMIT),
        name="ffn",
    )(block_expert, first, nxt, slot, nvalid, nact, xb, w1, b1p, w2, b2)


def _combine_body(x1_ref, y0_ref, y1_ref, y2_ref, y3_ref, gcol_ref, g2_ref, o_ref):
    gc = gcol_ref[...]
    m_lo = m_hi = None
    for k, y_ref in enumerate((y0_ref, y1_ref, y2_ref, y3_ref)):
        lo, hi = _unpack_halves(y_ref[...])
        gk = gc[:, k:k + 1]
        m_lo = gk * lo if m_lo is None else m_lo + gk * lo
        m_hi = gk * hi if m_hi is None else m_hi + gk * hi
    m = jnp.concatenate([m_lo, m_hi], axis=1)
    o_ref[...] = (x1_ref[...] + g2_ref[0] * m).astype(o_ref.dtype)


def _combine(x1, yall, gcol, ada3, seq, out_dtype):
    t, d = x1.shape
    tm = min(1024, seq)
    per_b = seq // tm
    nt = t // tm
    yk = lambda k: pl.BlockSpec((tm, d // 2), lambda i, k=k: (k * nt + i, 0))
    return pl.pallas_call(
        _combine_body,
        grid=(nt,),
        in_specs=[
            pl.BlockSpec((tm, d), lambda i: (i, 0)),
            yk(0), yk(1), yk(2), yk(3),
            pl.BlockSpec((tm, LANES), lambda i: (i, 0)),
            pl.BlockSpec((1, 1, d), lambda i: (i // per_b, 0, 5)),
        ],
        out_specs=pl.BlockSpec((tm, d), lambda i: (i, 0)),
        out_shape=jax.ShapeDtypeStruct((t, d), out_dtype),
        compiler_params=pltpu.CompilerParams(
            dimension_semantics=("arbitrary",), vmem_limit_bytes=VMEM_LIMIT),
        name="combine",
    )(x1, yall, yall, yall, yall, gcol, ada3)


def _route_tables(idx, rank, counts, n_tok):
    bm = FFN_BLOCK
    n_asg = TOP_K * n_tok
    n_blocks = -(-(n_asg + N_EXPERTS * (bm - 1)) // bm)
    padded = (counts + bm - 1) // bm * bm
    pad_ends = jnp.cumsum(padded)
    pad_starts = pad_ends - padded
    e_ids = jnp.arange(N_EXPERTS, dtype=I32)
    start_of = jnp.sum(jnp.where(idx[None] == e_ids[:, None, None], pad_starts[:, None, None], 0), axis=0)
    dest = (start_of + rank).reshape(-1)
    nact = (pad_ends[-1] // bm).astype(I32)
    blk_start = jnp.arange(n_blocks, dtype=I32) * bm
    last = jnp.sum(jnp.where(pad_ends <= pad_ends[-1] - 1, 1, 0)).astype(I32)
    be = jnp.sum(jnp.where(pad_ends[None, :] <= blk_start[:, None], 1, 0), axis=1).astype(I32)
    active = blk_start < pad_ends[-1]
    be = jnp.where(active, be, last)
    blk = jnp.arange(n_blocks, dtype=I32)
    first = active & ((blk == 0) | (be != jnp.roll(be, 1)))
    slot = (jnp.cumsum(first.astype(I32)) - 1) & 1
    later_first = first[None, :] & (blk[None, :] > blk[:, None])
    nxt_pos = jnp.min(jnp.where(later_first, blk[None, :], n_blocks), axis=1)
    nxt = jnp.sum(jnp.where(blk[None, :] == nxt_pos[:, None], be[None, :], 0), axis=1)
    nxt = jnp.where(nxt_pos < n_blocks, nxt, -1).astype(I32)
    mine = be[:, None] == e_ids[None, :]
    cnt_b = jnp.sum(jnp.where(mine, counts[None, :], 0), axis=1)
    start_b = jnp.sum(jnp.where(mine, pad_starts[None, :], 0), axis=1)
    nvalid = jnp.where(active, jnp.clip(cnt_b - (blk_start - start_b), 0, bm), 0).astype(I32)
    return be, first.astype(I32), nxt, slot.astype(I32), nvalid, nact.reshape(1), dest, n_blocks * bm


def kernel(x, c, w_ada, b_ada, mix_norm_g, ffn_norm_g, w_in, hg_lower_bound_logits, hg_out_norm_g, da_q_norm_g, da_k_norm_g, da_lambda_q1, da_lambda_k1, da_lambda_q2, da_lambda_k2, da_subln_g, w_out, w_router, b_router, w1, b1, w2, b2):
    bsz, seq, d = x.shape
    t = bsz * seq
    depth = w_ada.shape[0]
    out_dtype = x.dtype
    hw = HG_HEADS * HG_DV
    xcur = x.reshape(t, d)
    for l in range(depth):
        ada = _ada(c, w_ada[l], b_ada[l])
        ada3 = ada.reshape(bsz, 1, N_MOD * d)
        col_h = 0
        col_a = col_h + 4 * HG_HEADS
        col_g = (4 * hw + 3 * DA_HEADS * 2 * DA_DH) // (d // 2)
        proj = _inproj(xcur, mix_norm_g[l].reshape(1, d), ada3, w_in[l].astype(BF16), seq)

        o_a = _hgrn(proj, hg_lower_bound_logits, hg_out_norm_g[l].reshape(1, HG_DV), bsz, seq, col_h, l)
        lambda_init = 0.8 - 0.6 * math.exp(-0.3 * l)
        qg2 = jnp.tile(da_q_norm_g[l], 2).reshape(1, 2 * DA_DH)
        kg2 = jnp.tile(da_k_norm_g[l], 2).reshape(1, 2 * DA_DH)
        lam4 = jnp.stack([da_lambda_q1[l], da_lambda_k1[l], da_lambda_q2[l], da_lambda_k2[l]])
        o_d = _attn(proj, qg2, kg2, lam4, da_subln_g[l].reshape(1, 2 * DA_DH), bsz, seq, col_a, lambda_init)

        x1, h2, idx, rank, gcol, cnt = _mixout(
            xcur, o_a, o_d, proj, col_g, w_out[l], ada3, ffn_norm_g[l].reshape(1, d),
            w_router[l].T, b_router[l].reshape(N_EXPERTS, 1), seq)

        counts = cnt[:, 0].astype(I32)
        be, first, nxt, slot, nvalid, nact, dest, n_rows = _route_tables(idx, rank, counts, t)
        b1p = b1[l].reshape(N_EXPERTS, -1, LANES, 2).transpose(0, 1, 3, 2).reshape(N_EXPERTS, 1, -1)
        xb = _sc_dispatch(h2, dest, n_rows)
        yb = _ffn(be, first, nxt, slot, nvalid, nact, xb, w1[l], b1p, w2[l], b2[l].reshape(N_EXPERTS, 1, d))
        yall = _sc_undispatch(yb, dest)
        xcur = _combine(x1, yall, gcol, ada3, seq, out_dtype)
    return xcur.reshape(bsz, seq, d)
```

```python
import functools
import math

import jax
import jax.numpy as jnp
from jax import lax
from jax.experimental import pallas as pl
from jax.experimental.pallas import tpu as pltpu
from jax.experimental.pallas import tpu_sc as plsc

F32 = jnp.float32
BF16 = jnp.bfloat16
I32 = jnp.int32

HG_HEADS = 4
HG_DK = 128
HG_DV = 128
HG_CHUNK = 32
DA_HEADS = 4
DA_DH = 64
N_EXPERTS = 32
TOP_K = 4
SWIGLU_ALPHA = 1.702
SWIGLU_LIMIT = 7.0
NORM_EPS = 1e-6
LOG2E = math.log2(math.e)
N_MOD = 6

LANES = 128
VMEM_LIMIT = 56 * 1024 * 1024

HG_ROWS = 256
HG_HEADS_PER_STEP = 4
ATTN_TILE = 256
DA_HEADS_PER_STEP = 2
FFN_BLOCK = 512

SC_CORES = 2
SC_WORKERS = SC_CORES * 16
SC_CHUNK = 128


def _sigmoid(x):
    return 0.5 * jnp.tanh(0.5 * x) + 0.5


def _dot(a, b):
    return jnp.dot(a, b, preferred_element_type=F32)


def _dot_nt(a, b):
    return lax.dot_general(a, b, (((1,), (1,)), ((), ())), preferred_element_type=F32)


def _split_bf16(x):
    hi = x.astype(BF16)
    lo = (x - hi.astype(F32)).astype(BF16)
    return hi, lo


def _pack_halves(x):
    return _pack_rounded(x.astype(BF16).astype(F32))


def _pack_rounded(xr):
    half = xr.shape[1] // 2
    lo = lax.bitcast_convert_type(xr[:, :half], I32)
    hi = lax.bitcast_convert_type(xr[:, half:], I32)
    return lax.shift_right_logical(lo, 16) | hi


def _unpack_halves(w):
    lo = lax.bitcast_convert_type(lax.shift_left(w, 16), F32)
    hi = lax.bitcast_convert_type(w & jnp.int32(-65536), F32)
    return lo, hi


def _ada_body(c_ref, w_ref, b_ref, o_ref):
    c = c_ref[...].astype(F32)
    ch, cl = _split_bf16(c * _sigmoid(c))
    wh, wl = _split_bf16(w_ref[...])
    o_ref[...] = _dot(ch, wh) + _dot(cl, wh) + _dot(ch, wl) + b_ref[...]


def _ada(c, w_ada, b_ada):
    bsz, d = c.shape
    n = w_ada.shape[1]
    tn = 2 * d
    return pl.pallas_call(
        _ada_body,
        grid=(n // tn,),
        in_specs=[
            pl.BlockSpec((bsz, d), lambda j: (0, 0)),
            pl.BlockSpec((d, tn), lambda j: (0, j)),
            pl.BlockSpec((1, tn), lambda j: (0, j)),
        ],
        out_specs=pl.BlockSpec((bsz, tn), lambda j: (0, j)),
        out_shape=jax.ShapeDtypeStruct((bsz, n), F32),
        name="ada",
    )(c, w_ada, b_ada.reshape(1, n))


def _norm_mod(x, g, shift, scale):
    ms = jnp.mean(x * x, axis=-1, keepdims=True)
    return x * lax.rsqrt(ms + NORM_EPS) * (g * (1.0 + scale)) + shift


def _inproj_body(x_ref, g_ref, sh_ref, sc_ref, w_ref, o_ref):
    h = _norm_mod(x_ref[...], g_ref[...], sh_ref[0], sc_ref[0])
    o_ref[...] = _dot(h.astype(BF16), w_ref[...]).astype(BF16)


def _inproj(x2, g, ada3, w_bf16, seq):
    t, d = x2.shape
    n = w_bf16.shape[1]
    tm = min(512, seq)
    nj = 1
    tn = n // nj
    per_b = seq // tm
    return pl.pallas_call(
        _inproj_body,
        grid=(nj, t // tm),
        in_specs=[
            pl.BlockSpec((tm, d), lambda j, i: (i, 0)),
            pl.BlockSpec((1, d), lambda j, i: (0, 0)),
            pl.BlockSpec((1, 1, d), lambda j, i: (i // per_b, 0, 0)),
            pl.BlockSpec((1, 1, d), lambda j, i: (i // per_b, 0, 1)),
            pl.BlockSpec((d, tn), lambda j, i: (0, j)),
        ],
        out_specs=pl.BlockSpec((tm, tn), lambda j, i: (i, j)),
        out_shape=jax.ShapeDtypeStruct((t, n), BF16),
        compiler_params=pltpu.CompilerParams(
            dimension_semantics=("arbitrary", "arbitrary"), vmem_limit_bytes=VMEM_LIMIT),
        name="inproj",
    )(x2, g, ada3, ada3, w_bf16)


def _hgrn_body(q_ref, f_ref, i_ref, og_ref, lbl_ref, g_ref, o_ref, *, seq, layer):
    rows, chunk = HG_ROWS, HG_CHUNK
    nchunk = rows // chunk
    lbl = lbl_ref[...].astype(F32)
    e = jnp.exp(lbl - jnp.max(lbl, axis=0, keepdims=True))
    lb_all = jnp.sum(e[: layer + 1], axis=0, keepdims=True) / jnp.sum(e, axis=0, keepdims=True)
    heads = q_ref.shape[1] // HG_DK
    r_i = lax.broadcasted_iota(I32, (rows, rows), 0)
    c_i = lax.broadcasted_iota(I32, (rows, rows), 1)
    tri = ((r_i // chunk) == (c_i // chunk)) & (r_i >= c_i)
    tri_b = jnp.where(tri, 1.0, 0.0).astype(BF16)
    row_chunk = lax.broadcasted_iota(I32, (rows, HG_DK), 0) // chunk
    g = g_ref[...].astype(F32)

    def block(r, st, h):
        sl = pl.ds(r * rows, rows)
        cs = slice(h * HG_DK, (h + 1) * HG_DK)
        lb = lb_all[:, cs]
        qr = q_ref[sl, cs].astype(F32)
        fr = f_ref[sl, cs].astype(F32)
        v = i_ref[sl, cs].astype(F32)
        og = og_ref[sl, cs].astype(F32)
        q = qr * _sigmoid(qr)
        f = lb + (1.0 - lb) * _sigmoid(fr)
        k = 1.0 - f
        logf = jnp.log(f)
        lhi, llo = _split_bf16(logf)
        bc2 = _dot(tri_b, jnp.concatenate([lhi, llo], axis=1))
        bcum = bc2[:, :HG_DK] + bc2[:, HG_DK:]
        b3 = bcum.reshape(nchunk, chunk, HG_DK)
        bl = b3[:, chunk - 1:chunk, :]
        dec = jnp.exp(bl)
        kt_f = k * jnp.exp(-bcum)
        qt_f = q * jnp.exp(bcum)
        kd = (kt_f.reshape(nchunk, chunk, HG_DK) * dec).reshape(rows, HG_DK)
        a = jnp.where(tri, _dot_nt(qt_f.astype(BF16), kt_f.astype(BF16)), 0.0).astype(BF16)
        vt_b = v.T.astype(BF16)
        kd_x = jnp.concatenate([jnp.where(row_chunk == c, kd, 0.0) for c in range(nchunk)], axis=1)
        kv_all = _dot(vt_b, kd_x.astype(BF16))
        starts = []
        for c in range(nchunk):
            starts.append(st.astype(BF16))
            st = st * dec[c] + kv_all[:, c * HG_DK:(c + 1) * HG_DK]
        q_x = [jnp.where(row_chunk == c, qt_f, 0.0).astype(BF16) for c in range(nchunk)]
        o = _dot_nt(jnp.concatenate([a] + q_x, axis=1), jnp.concatenate([vt_b] + starts, axis=1))
        ms = jnp.mean(o * o, axis=-1, keepdims=True)
        o = o * lax.rsqrt(ms + NORM_EPS) * g
        o_ref[sl, cs] = (o * (og * _sigmoid(og))).astype(o_ref.dtype)
        return st

    sts = [jnp.zeros((HG_DV, HG_DK), F32) for _ in range(heads)]
    for r in range(seq // rows):
        for h in range(heads):
            sts[h] = block(r, sts[h], h)


def _hgrn(proj, lb_logits, norm_g, bsz, seq, col0, layer):
    t = proj.shape[0]
    hps = HG_HEADS_PER_STEP
    wid = hps * LANES
    assert col0 % hps == 0 and HG_HEADS % hps == 0
    blk = lambda off: pl.BlockSpec((seq, wid), lambda b, h, off=off: (b, (col0 + off) // hps + h))
    return pl.pallas_call(
        functools.partial(_hgrn_body, seq=seq, layer=layer),
        grid=(bsz, HG_HEADS // hps),
        in_specs=[
            blk(0), blk(HG_HEADS), blk(2 * HG_HEADS), blk(3 * HG_HEADS),
            pl.BlockSpec((lb_logits.shape[0], wid), lambda b, h: (0, h)),
            pl.BlockSpec((1, HG_DV), lambda b, h: (0, 0)),
        ],
        out_specs=pl.BlockSpec((seq, wid), lambda b, h: (b, h)),
        out_shape=jax.ShapeDtypeStruct((t, HG_HEADS * HG_DV), BF16),
        compiler_params=pltpu.CompilerParams(
            dimension_semantics=("arbitrary", "arbitrary"), vmem_limit_bytes=VMEM_LIMIT),
        name="hgrn",
    )(proj, proj, proj, proj, lb_logits, norm_g)


def _group_norm(x, gsum_b, gain):
    ss = _dot((x * x).astype(BF16), gsum_b)
    return x * lax.rsqrt(ss * (1.0 / DA_DH) + NORM_EPS) * gain


def _attn_body(q_ref, k_ref, v_ref, qg_ref, kg_ref, lam_ref, sg_ref, o_ref, kn_scr, v1_scr, *, seq, lambda_init):
    tq = ATTN_TILE
    width = 2 * DA_DH
    r_l = lax.broadcasted_iota(I32, (width, width), 0) // DA_DH
    c_l = lax.broadcasted_iota(I32, (width, width), 1) // DA_DH
    gsum_b = jnp.where(r_l == c_l, 1.0, 0.0).astype(BF16)
    lane = lax.broadcasted_iota(I32, (1, width), 1)
    kg = kg_ref[...].astype(F32)
    qg = qg_ref[...].astype(F32) * (DA_DH ** -0.5 * LOG2E)
    sg = sg_ref[...].astype(F32) * (1.0 - lambda_init)
    ones = jnp.ones((tq, width), BF16)
    row = lax.broadcasted_iota(I32, (tq, tq), 0)
    col = lax.broadcasted_iota(I32, (tq, tq), 1)
    keep = row >= col

    lam_v = lam_ref[...].astype(F32)
    lam = (jnp.exp(jnp.sum(lam_v[0:1] * lam_v[1:2], axis=-1, keepdims=True))
           - jnp.exp(jnp.sum(lam_v[2:3] * lam_v[3:4], axis=-1, keepdims=True)) + lambda_init)

    def softmax_v(qc, nk, h):
        s = _dot_nt(qc, kn_scr[h, 0:nk, :])
        diag = jnp.where(keep, s[:, nk - tq:], -jnp.inf)
        s = diag if nk == tq else jnp.concatenate([s[:, :nk - tq], diag], axis=1)
        m = jnp.max(s, axis=-1, keepdims=True)
        return _dot(jnp.exp2(s - m).astype(BF16), v1_scr[h, 0:nk, :])

    for i in range(seq // tq):
        sl = slice(i * tq, (i + 1) * tq)
        nk = (i + 1) * tq
        for h in range(q_ref.shape[1] // width):
            cs = slice(h * width, (h + 1) * width)
            kn_scr[h, sl, :] = _group_norm(k_ref[sl, cs].astype(F32), gsum_b, kg).astype(BF16)
            v1_scr[h, sl, :] = jnp.concatenate([v_ref[sl, cs], ones], axis=1)
            qn = _group_norm(q_ref[sl, cs].astype(F32), gsum_b, qg)
            a1 = softmax_v(jnp.where(lane < DA_DH, qn, 0.0).astype(BF16), nk, h)
            a2 = softmax_v(jnp.where(lane >= DA_DH, qn, 0.0).astype(BF16), nk, h)
            o = a1[:, :width] / a1[:, width:width + 1] - lam * (a2[:, :width] / a2[:, width:width + 1])
            ms = jnp.mean(o * o, axis=-1, keepdims=True)
            o_ref[sl, cs] = (o * lax.rsqrt(ms + NORM_EPS) * sg).astype(o_ref.dtype)


def _attn(proj, qg2, kg2, lam4, subln_g, bsz, seq, col0, lambda_init):
    t = proj.shape[0]
    width = 2 * DA_DH
    hps = DA_HEADS_PER_STEP
    assert col0 % hps == 0 and DA_HEADS % hps == 0
    blk = lambda off: pl.BlockSpec((seq, hps * width), lambda b, h, off=off: (b, (col0 + off) // hps + h))
    return pl.pallas_call(
        functools.partial(_attn_body, seq=seq, lambda_init=lambda_init),
        grid=(bsz, DA_HEADS // hps),
        in_specs=[
            blk(0), blk(DA_HEADS), blk(2 * DA_HEADS),
            pl.BlockSpec((1, width), lambda b, h: (0, 0)),
            pl.BlockSpec((1, width), lambda b, h: (0, 0)),
            pl.BlockSpec((4, DA_DH), lambda b, h: (0, 0)),
            pl.BlockSpec((1, width), lambda b, h: (0, 0)),
        ],
        out_specs=pl.BlockSpec((seq, hps * width), lambda b, h: (b, h)),
        out_shape=jax.ShapeDtypeStruct((t, DA_HEADS * width), BF16),
        scratch_shapes=[pltpu.VMEM((hps, seq, width), BF16), pltpu.VMEM((hps, seq, 2 * width), BF16)],
        compiler_params=pltpu.CompilerParams(
            dimension_semantics=("arbitrary", "arbitrary"), vmem_limit_bytes=VMEM_LIMIT),
        name="attn",
    )(proj, proj, proj, qg2, kg2, lam4, subln_g)


def _mixout_body(x_ref, oa_ref, od_ref, ga0_ref, ga1_ref, gd0_ref, gd1_ref, wo_ref, g1_ref, g_ref, sh_ref, sc_ref,
                 wr_ref, br_ref,
                 x1_ref, h2_ref, idx_ref, rank_ref, gcol_ref, cnt_ref, carry_scr, wo_scr):
    i = pl.program_id(0)
    tm = x_ref.shape[0]
    hw = oa_ref.shape[1]

    @pl.when(i == 0)
    def _():
        carry_scr[...] = jnp.zeros_like(carry_scr)
        wo_scr[...] = wo_ref[...].astype(BF16)

    ya = _dot(oa_ref[...], wo_scr[0:hw, :])
    yd = _dot(od_ref[...], wo_scr[hw:, :])
    ga = jnp.concatenate([ga0_ref[...], ga1_ref[...]], axis=1).astype(F32)
    gd = jnp.concatenate([gd0_ref[...], gd1_ref[...]], axis=1).astype(F32)
    y = _sigmoid(ga) * ya + _sigmoid(gd) * yd
    x1 = x_ref[...] + g1_ref[0] * y
    x1_ref[...] = x1
    h2 = _norm_mod(x1, g_ref[...], sh_ref[0], sc_ref[0])
    hh = h2.astype(BF16)
    h2r = hh.astype(F32)
    h2_ref[...] = _pack_rounded(h2r)

    hl = (h2 - h2r).astype(BF16)
    wh, wl = _split_bf16(wr_ref[...])
    logits = _dot_nt(wh, hh) + _dot_nt(wl, hh) + _dot_nt(wh, hl) + br_ref[...]

    e_iota = lax.broadcasted_iota(I32, (N_EXPERTS, tm), 0).astype(F32)
    vals = logits
    tops, sels, idxs = [], [], []
    for _ in range(TOP_K):
        m = jnp.max(vals, axis=0, keepdims=True)
        idx = jnp.min(jnp.where(vals == m, e_iota, float(N_EXPERTS)), axis=0, keepdims=True)
        sel = e_iota == idx
        vals = jnp.where(sel, -jnp.inf, vals)
        tops.append(m)
        sels.append(sel)
        idxs.append(idx)
    ex = [jnp.exp(tv - tops[0]) for tv in tops]
    den = ex[0] + ex[1] + ex[2] + ex[3]
    gates = [v / den for v in ex]

    hot = jnp.where(sels[0] | sels[1] | sels[2] | sels[3], 1.0, 0.0)
    r_t = lax.broadcasted_iota(I32, (tm, tm), 0)
    c_t = lax.broadcasted_iota(I32, (tm, tm), 1)
    upper = jnp.where(r_t < c_t, 1.0, 0.0).astype(BF16)
    excl = _dot(hot.astype(BF16), upper) + carry_scr[:, 0:1]
    carry_scr[...] = carry_scr[...] + jnp.sum(hot, axis=1, keepdims=True)
    cnt_ref[...] = carry_scr[...]

    ranks = [jnp.sum(jnp.where(s, excl, 0.0), axis=0, keepdims=True) for s in sels]
    idx_ref[...] = jnp.concatenate(idxs, axis=0).astype(I32)
    rank_ref[...] = jnp.concatenate(ranks, axis=0).astype(I32)
    gpad = jnp.concatenate(gates + [jnp.zeros((LANES - TOP_K, tm), F32)], axis=0)
    gcol_ref[...] = gpad.T


def _mixout(x2, oa, od, proj, col_g, wo, ada3, ffn_g, wr_t, br_col, seq):
    t, d = x2.shape
    tm = min(1024, seq)
    per_b = seq // tm
    hw = oa.shape[1]
    row = lambda w: pl.BlockSpec((tm, w), lambda i: (i, 0))
    gate = lambda c: pl.BlockSpec((tm, d // 2), lambda i, c=c: (i, col_g + c))
    mod = lambda c: pl.BlockSpec((1, 1, d), lambda i, c=c: (i // per_b, 0, c))
    full = lambda a: pl.BlockSpec(a.shape, lambda i: (0,) * a.ndim)
    return pl.pallas_call(
        _mixout_body,
        grid=(t // tm,),
        in_specs=[
            row(d), row(hw), row(hw),
            gate(0), gate(1), gate(2), gate(3),
            full(wo),
            mod(2),
            full(ffn_g), mod(3), mod(4),
            full(wr_t), full(br_col),
        ],
        out_specs=[
            row(d), row(d // 2),
            pl.BlockSpec((TOP_K, tm), lambda i: (0, i)),
            pl.BlockSpec((TOP_K, tm), lambda i: (0, i)),
            pl.BlockSpec((tm, LANES), lambda i: (i, 0)),
            pl.BlockSpec((N_EXPERTS, LANES), lambda i: (0, 0)),
        ],
        out_shape=[
            jax.ShapeDtypeStruct((t, d), F32),
            jax.ShapeDtypeStruct((t, d // 2), I32),
            jax.ShapeDtypeStruct((TOP_K, t), I32),
            jax.ShapeDtypeStruct((TOP_K, t), I32),
            jax.ShapeDtypeStruct((t, LANES), F32),
            jax.ShapeDtypeStruct((N_EXPERTS, LANES), F32),
        ],
        scratch_shapes=[pltpu.VMEM((N_EXPERTS, LANES), F32), pltpu.VMEM(wo.shape, BF16)],
        compiler_params=pltpu.CompilerParams(
            dimension_semantics=("arbitrary",), vmem_limit_bytes=VMEM_LIMIT),
        name="mixout",
    )(x2, oa, od, proj, proj, proj, proj, wo, ada3, ffn_g, ada3, ada3, wr_t, br_col)


def _sc_mesh():
    return plsc.VectorSubcoreMesh(core_axis_name="c", subcore_axis_name="s")


def _sc_worker_id():
    return lax.axis_index("s") * SC_CORES + lax.axis_index("c")


def _sc_dispatch(h2, dest, n_rows):
    t, d = h2.shape
    assert t % (SC_WORKERS * SC_CHUNK) == 0
    cpw = t // (SC_WORKERS * SC_CHUNK)
    dest_w = dest.reshape(TOP_K, SC_WORKERS, cpw, SC_CHUNK).transpose(1, 0, 2, 3).reshape(-1, SC_CHUNK)
    ipw = TOP_K * cpw

    def body(h2_hbm, dest_hbm, xb_hbm, idx_v, rows_v):
        wid = _sc_worker_id()
        pltpu.sync_copy(dest_hbm.at[pl.ds(pl.multiple_of(wid * ipw, ipw), ipw)], idx_v)

        @pl.loop(0, cpw)
        def _(c):
            t0 = pl.multiple_of((wid * cpw + c) * SC_CHUNK, SC_CHUNK)
            pltpu.sync_copy(h2_hbm.at[pl.ds(t0, SC_CHUNK)], rows_v)
            for k in range(TOP_K):
                pltpu.sync_copy(rows_v, xb_hbm.at[idx_v.at[k * cpw + c]])

    return pl.kernel(
        body, out_type=jax.ShapeDtypeStruct((n_rows, d), h2.dtype), mesh=_sc_mesh(),
        scratch_types=[pltpu.VMEM((ipw, SC_CHUNK), I32), pltpu.VMEM((SC_CHUNK, d), h2.dtype)],
        name="dispatch",
    )(h2, dest_w)


def _sc_undispatch(y, dest):
    n_asg = dest.shape[0]
    d = y.shape[1]
    rows = SC_CHUNK // 2
    assert n_asg % (2 * SC_WORKERS * rows) == 0
    cpw = n_asg // (SC_WORKERS * rows)

    def body(y_hbm, dest_hbm, yt_hbm, idx_v, rows_v, gsem, wsem):
        wid = _sc_worker_id()
        pltpu.sync_copy(dest_hbm.at[pl.ds(pl.multiple_of(wid * cpw, cpw), cpw)], idx_v)

        def gather(c, b):
            return pltpu.make_async_copy(y_hbm.at[idx_v.at[c]], rows_v.at[b], gsem.at[b])

        def write(c, b):
            r0 = pl.multiple_of((wid * cpw + c) * rows, rows)
            return pltpu.make_async_copy(rows_v.at[b], yt_hbm.at[pl.ds(r0, rows)], wsem.at[b])

        gather(0, 0).start()

        @pl.loop(0, cpw, step=2)
        def _(c0):
            for b in range(2):
                c = c0 + b

                @pl.when(c + 1 < cpw)
                def _():
                    @pl.when(c >= 1)
                    def _():
                        write(c - 1, 1 - b).wait()
                    gather(c + 1, 1 - b).start()

                gather(c, b).wait()
                write(c, b).start()

        write(cpw - 2, 0).wait()
        write(cpw - 1, 1).wait()

    return pl.kernel(
        body, out_type=jax.ShapeDtypeStruct((n_asg, d), y.dtype), mesh=_sc_mesh(),
        scratch_types=[pltpu.VMEM((cpw, rows), I32), pltpu.VMEM((2, rows, d), y.dtype),
                       pltpu.SemaphoreType.DMA((2,)), pltpu.SemaphoreType.DMA((2,))],
        name="undispatch",
    )(y, dest.reshape(-1, rows))


def _ffn_body(be_ref, first_ref, nxt_ref, slot_ref, nv_ref, nact_ref, x_ref, w1_hbm, b1_ref, w2_hbm, b2_ref, y_ref,
              w1f, w2f, w1c, sem1, sem2):
    j = pl.program_id(0)
    ff = w2f.shape[1]
    pair = 2 * LANES
    ngroup = (2 * ff) // pair

    def weight_copies(e, slot):
        return (pltpu.make_async_copy(w1_hbm.at[e], w1f.at[slot], sem1.at[slot]),
                pltpu.make_async_copy(w2_hbm.at[e], w2f.at[slot], sem2.at[slot]))

    @pl.when(j == 0)
    def _():
        for cp in weight_copies(be_ref[0], 0):
            cp.start()

    @pl.when(first_ref[j] == 1)
    def _():
        slot = slot_ref[j]
        for cp in weight_copies(be_ref[j], slot):
            cp.wait()

        @pl.when(nxt_ref[j] >= 0)
        def _():
            for cp in weight_copies(nxt_ref[j], 1 - slot):
                cp.start()

        r_p = lax.broadcasted_iota(I32, (pair, pair), 0)
        c_p = lax.broadcasted_iota(I32, (pair, pair), 1)
        src = jnp.where(c_p < LANES, 2 * c_p, 2 * (c_p - LANES) + 1)
        perm = jnp.where(r_p == src, 1.0, 0.0).astype(BF16)
        for g in range(ngroup):
            cols = slice(g * pair, (g + 1) * pair)
            w1c[:, cols] = _dot(w1f[slot, :, cols].astype(BF16), perm).astype(BF16)

    def expert_rows(nrows):
        x_lo, x_hi = _unpack_halves(x_ref[0:nrows, :])
        xb = jnp.concatenate([x_lo.astype(BF16), x_hi.astype(BF16)], axis=1)
        u = _dot(xb, w1c[...]) + b1_ref[0]
        acts = []
        for g in range(ngroup):
            glu = jnp.minimum(u[:, g * pair:g * pair + LANES], SWIGLU_LIMIT)
            lin = jnp.clip(u[:, g * pair + LANES:(g + 1) * pair], -SWIGLU_LIMIT, SWIGLU_LIMIT)
            acts.append((glu * _sigmoid(SWIGLU_ALPHA * glu) * (lin + 1.0)).astype(BF16))
        act = jnp.concatenate(acts, axis=1)
        y_ref[0:nrows, :] = _pack_halves(_dot(act, w2f[slot_ref[j]].astype(BF16)) + b2_ref[0])

    active = j < nact_ref[0]
    half = x_ref.shape[0] // 2
    pl.when(active & (nv_ref[j] > half))(functools.partial(expert_rows, x_ref.shape[0]))
    pl.when(active & (nv_ref[j] <= half))(functools.partial(expert_rows, half))


def _ffn(block_expert, first, nxt, slot, nvalid, nact, xb, w1, b1p, w2, b2):
    bm = FFN_BLOCK
    n_rows, dw = xb.shape
    d = 2 * dw
    n_blocks = n_rows // bm
    ff2 = w1.shape[2]
    ff = w2.shape[1]
    row_blk = lambda j, be, fi, nx, sl, nv, na: (jnp.minimum(j, na[0] - 1), 0)
    bias_blk = lambda j, be, fi, nx, sl, nv, na: (be[j], 0, 0)
    grid_spec = pltpu.PrefetchScalarGridSpec(
        num_scalar_prefetch=6,
        grid=(n_blocks,),
        in_specs=[
            pl.BlockSpec((bm, dw), row_blk),
            pl.BlockSpec(memory_space=pl.ANY),
            pl.BlockSpec((1, 1, ff2), bias_blk),
            pl.BlockSpec(memory_space=pl.ANY),
            pl.BlockSpec((1, 1, d), bias_blk),
        ],
        out_specs=pl.BlockSpec((bm, dw), row_blk),
        scratch_shapes=[
            pltpu.VMEM((2, d, ff2), F32), pltpu.VMEM((2, ff, d), F32),
            pltpu.VMEM((d, ff2), BF16),
            pltpu.SemaphoreType.DMA((2,)), pltpu.SemaphoreType.DMA((2,)),
        ],
    )
    return pl.pallas_call(
        _ffn_body,
        grid_spec=grid_spec,
        out_shape=jax.ShapeDtypeStruct((n_rows, dw), I32),
        compiler_params=pltpu.CompilerParams(
            dimension_semantics=("arbitrary",), vmem_limit_bytes=VMEM_LIMIT),
        name="ffn",
    )(block_expert, first, nxt, slot, nvalid, nact, xb, w1, b1p, w2, b2)


def _combine_body(x1_ref, y0_ref, y1_ref, y2_ref, y3_ref, gcol_ref, g2_ref, o_ref):
    gc = gcol_ref[...]
    m_lo = m_hi = None
    for k, y_ref in enumerate((y0_ref, y1_ref, y2_ref, y3_ref)):
        lo, hi = _unpack_halves(y_ref[...])
        gk = gc[:, k:k + 1]
        m_lo = gk * lo if m_lo is None else m_lo + gk * lo
        m_hi = gk * hi if m_hi is None else m_hi + gk * hi
    m = jnp.concatenate([m_lo, m_hi], axis=1)
    o_ref[...] = (x1_ref[...] + g2_ref[0] * m).astype(o_ref.dtype)


def _combine(x1, yall, gcol, ada3, seq, out_dtype):
    t, d = x1.shape
    tm = min(1024, seq)
    per_b = seq // tm
    nt = t // tm
    yk = lambda k: pl.BlockSpec((tm, d // 2), lambda i, k=k: (k * nt + i, 0))
    return pl.pallas_call(
        _combine_body,
        grid=(nt,),
        in_specs=[
            pl.BlockSpec((tm, d), lambda i: (i, 0)),
            yk(0), yk(1), yk(2), yk(3),
            pl.BlockSpec((tm, LANES), lambda i: (i, 0)),
            pl.BlockSpec((1, 1, d), lambda i: (i // per_b, 0, 5)),
        ],
        out_specs=pl.BlockSpec((tm, d), lambda i: (i, 0)),
        out_shape=jax.ShapeDtypeStruct((t, d), out_dtype),
        compiler_params=pltpu.CompilerParams(
            dimension_semantics=("arbitrary",), vmem_limit_bytes=VMEM_LIMIT),
        name="combine",
    )(x1, yall, yall, yall, yall, gcol, ada3)


def _route_tables(idx, rank, counts, n_tok):
    bm = FFN_BLOCK
    n_asg = TOP_K * n_tok
    n_blocks = -(-(n_asg + N_EXPERTS * (bm - 1)) // bm)
    padded = (counts + bm - 1) // bm * bm
    pad_ends = jnp.cumsum(padded)
    pad_starts = pad_ends - padded
    e_ids = jnp.arange(N_EXPERTS, dtype=I32)
    start_of = jnp.sum(jnp.where(idx[None] == e_ids[:, None, None], pad_starts[:, None, None], 0), axis=0)
    dest = (start_of + rank).reshape(-1)
    nact = (pad_ends[-1] // bm).astype(I32)
    blk_start = jnp.arange(n_blocks, dtype=I32) * bm
    last = jnp.sum(jnp.where(pad_ends <= pad_ends[-1] - 1, 1, 0)).astype(I32)
    be = jnp.sum(jnp.where(pad_ends[None, :] <= blk_start[:, None], 1, 0), axis=1).astype(I32)
    active = blk_start < pad_ends[-1]
    be = jnp.where(active, be, last)
    blk = jnp.arange(n_blocks, dtype=I32)
    first = active & ((blk == 0) | (be != jnp.roll(be, 1)))
    slot = (jnp.cumsum(first.astype(I32)) - 1) & 1
    later_first = first[None, :] & (blk[None, :] > blk[:, None])
    nxt_pos = jnp.min(jnp.where(later_first, blk[None, :], n_blocks), axis=1)
    nxt = jnp.sum(jnp.where(blk[None, :] == nxt_pos[:, None], be[None, :], 0), axis=1)
    nxt = jnp.where(nxt_pos < n_blocks, nxt, -1).astype(I32)
    mine = be[:, None] == e_ids[None, :]
    cnt_b = jnp.sum(jnp.where(mine, counts[None, :], 0), axis=1)
    start_b = jnp.sum(jnp.where(mine, pad_starts[None, :], 0), axis=1)
    nvalid = jnp.where(active, jnp.clip(cnt_b - (blk_start - start_b), 0, bm), 0).astype(I32)
    return be, first.astype(I32), nxt, slot.astype(I32), nvalid, nact.reshape(1), dest, n_blocks * bm


def kernel(x, c, w_ada, b_ada, mix_norm_g, ffn_norm_g, w_in, hg_lower_bound_logits, hg_out_norm_g, da_q_norm_g, da_k_norm_g, da_lambda_q1, da_lambda_k1, da_lambda_q2, da_lambda_k2, da_subln_g, w_out, w_router, b_router, w1, b1, w2, b2):
    bsz, seq, d = x.shape
    t = bsz * seq
    depth = w_ada.shape[0]
    out_dtype = x.dtype
    hw = HG_HEADS * HG_DV
    xcur = x.reshape(t, d)
    for l in range(depth):
        ada = _ada(c, w_ada[l], b_ada[l])
        ada3 = ada.reshape(bsz, 1, N_MOD * d)
        col_h = 0
        col_a = col_h + 4 * HG_HEADS
        col_g = (4 * hw + 3 * DA_HEADS * 2 * DA_DH) // (d // 2)
        proj = _inproj(xcur, mix_norm_g[l].reshape(1, d), ada3, w_in[l].astype(BF16), seq)

        o_a = _hgrn(proj, hg_lower_bound_logits, hg_out_norm_g[l].reshape(1, HG_DV), bsz, seq, col_h, l)
        lambda_init = 0.8 - 0.6 * math.exp(-0.3 * l)
        qg2 = jnp.tile(da_q_norm_g[l], 2).reshape(1, 2 * DA_DH)
        kg2 = jnp.tile(da_k_norm_g[l], 2).reshape(1, 2 * DA_DH)
        lam4 = jnp.stack([da_lambda_q1[l], da_lambda_k1[l], da_lambda_q2[l], da_lambda_k2[l]])
        o_d = _attn(proj, qg2, kg2, lam4, da_subln_g[l].reshape(1, 2 * DA_DH), bsz, seq, col_a, lambda_init)

        x1, h2, idx, rank, gcol, cnt = _mixout(
            xcur, o_a, o_d, proj, col_g, w_out[l], ada3, ffn_norm_g[l].reshape(1, d),
            w_router[l].T, b_router[l].reshape(N_EXPERTS, 1), seq)

        counts = cnt[:, 0].astype(I32)
        be, first, nxt, slot, nvalid, nact, dest, n_rows = _route_tables(idx, rank, counts, t)
        b1p = b1[l].reshape(N_EXPERTS, -1, LANES, 2).transpose(0, 1, 3, 2).reshape(N_EXPERTS, 1, -1)
        xb = _sc_dispatch(h2, dest, n_rows)
        yb = _ffn(be, first, nxt, slot, nvalid, nact, xb, w1[l], b1p, w2[l], b2[l].reshape(N_EXPERTS, 1, d))
        yall = _sc_undispatch(yb, dest)
        xcur = _combine(x1, yall, gcol, ada3, seq, out_dtype)
    return xcur.reshape(bsz, seq, d)
```
